```python
import math
import jax, jax.numpy as jnp
from jax import lax
import numpy as np

D_MODEL = 1024
BATCH = 4
SEQ = 8192
DEPTH = 2

CHUNK = 64
D_MIX = D_MODEL
HEAD_DIM = 64
SGU_WIDTH = D_MIX // 4
SGU_HEADS = SGU_WIDTH // HEAD_DIM
SGU_BLOCK = 128
S5_WIDTH = D_MIX // 4
S5_GROUP = 16
S5_N_GROUPS = S5_WIDTH // S5_GROUP
S5_STATE = 64
S5_DT_MIN = 0.001
S5_DT_MAX = 0.1
ATTN_WIDTH = D_MIX // 2
ATTN_HEADS = ATTN_WIDTH // HEAD_DIM
BAND_CHUNKS = 9
BAND = BAND_CHUNKS * CHUNK
MAX_REL = 256
IN_COLS = 2 * SGU_WIDTH + S5_WIDTH + 3 * ATTN_WIDTH
OUT_NORM_GROUP = 64
N_EXPERT_GROUPS = 4
EXPERTS_PER_GROUP = 4
N_EXPERTS = N_EXPERT_GROUPS * EXPERTS_PER_GROUP
TOP_K = 2
D_EXPERT = D_MODEL // 4
EPS = 1e-6
NEG_INF = -1e30

kernel_name = "hybrid_chunk_causal_parallel_heads_hmoe"


def rms_norm(x, g):
    xf = x.astype(jnp.float32)
    y = xf * lax.rsqrt(jnp.mean(xf * xf, axis=-1, keepdims=True) + EPS)
    return (y * g.astype(jnp.float32)).astype(x.dtype)


def sgu_mixer(z, norm_g, w_s, b_s):
    b_, s_, _ = z.shape
    u, v = jnp.split(z, 2, axis=-1)
    v = rms_norm(v, norm_g)
    v = v.reshape(b_, s_ // SGU_BLOCK, SGU_BLOCK, SGU_HEADS, HEAD_DIM)
    chunk_of = jnp.arange(SGU_BLOCK) // CHUNK
    mask = chunk_of[None, :] <= chunk_of[:, None]
    w = jnp.where(mask[None], w_s, 0).astype(v.dtype)
    v = jnp.einsum('hij,bnjhc->bnihc', w, v) + b_s.T[None, None, :, :, None].astype(v.dtype)
    return u * v.reshape(b_, s_, SGU_WIDTH)


def s5_mixer(u, lam_re, lam_im, log_dt, b_re, b_im, c_re, c_im, d, glu_w, glu_b):
    f32 = jnp.float32
    b_, s_, _ = u.shape
    lam = lax.complex(lam_re.astype(f32), lam_im.astype(f32))
    dt = jnp.exp(log_dt.astype(f32))[:, None]
    a_bar = jnp.exp(lam * dt)
    b_mat = lax.complex(b_re.astype(f32), b_im.astype(f32))
    b_bar = ((a_bar - 1) / lam)[..., None] * b_mat
    c_mat = lax.complex(c_re.astype(f32), c_im.astype(f32))
    uf = u.astype(f32)
    ug = uf.reshape(b_, s_, S5_N_GROUPS, S5_GROUP)
    bu = jnp.einsum('gph,blgh->blgp', b_bar, ug.astype(jnp.complex64))
    a = jnp.broadcast_to(a_bar, bu.shape)

    def combine(e1, e2):
        a1, x1 = e1
        a2, x2 = e2
        return a1 * a2, a2 * x1 + x2

    _, states = lax.associative_scan(combine, (a, bu), axis=1)
    y = jnp.einsum('ghp,blgp->blgh', c_mat, states).real.reshape(b_, s_, S5_WIDTH)
    y = jax.nn.gelu(y + d.astype(f32) * uf)
    y = y * jax.nn.sigmoid(y @ glu_w.astype(f32) + glu_b.astype(f32))
    return y.astype(u.dtype)


def band_attention(q, k, v, q_g, k_g, rel_bias):
    f32 = jnp.float32
    b_, s_, n_h, hd = q.shape
    n_c = s_ // CHUNK
    q = rms_norm(q, q_g)
    k = rms_norm(k, k_g)
    q_pos = jnp.arange(CHUNK) + (BAND_CHUNKS - 1) * CHUNK
    k_pos = jnp.arange(BAND)
    rel = jnp.clip(q_pos[:, None] - k_pos[None, :], -MAX_REL, MAX_REL) + MAX_REL
    bias = rel_bias.astype(f32)[:, rel]
    band_idx = jnp.arange(n_c)[:, None] + jnp.arange(BAND_CHUNKS)[None, :]
    valid = jnp.repeat(band_idx >= BAND_CHUNKS - 1, CHUNK, axis=1)
    scale = HEAD_DIM ** -0.5

    def one_seq(args):
        qs, ks, vs = args
        qc = qs.reshape(n_c, CHUNK, n_h, hd)
        pad = jnp.zeros(((BAND_CHUNKS - 1) * CHUNK, n_h, hd), ks.dtype)
        kc = jnp.concatenate([pad, ks], axis=0).reshape(n_c + BAND_CHUNKS - 1, CHUNK, n_h, hd)[band_idx]
        vc = jnp.concatenate([pad, vs], axis=0).reshape(n_c + BAND_CHUNKS - 1, CHUNK, n_h, hd)[band_idx]
        kc = kc.reshape(n_c, BAND, n_h, hd)
        vc = vc.reshape(n_c, BAND, n_h, hd)
        sc = jnp.einsum('cqhd,ckhd->chqk', qc, kc).astype(f32) * scale + bias[None]
        sc = jnp.where(valid[:, None, None, :], sc, NEG_INF)
        p = jax.nn.softmax(sc, axis=-1).astype(vs.dtype)
        o = jnp.einsum('chqk,ckhd->cqhd', p, vc)
        return o.reshape(s_, n_h * hd)

    return lax.map(one_seq, (q, k, v))


def hier_moe(x, wg, bg, we, be, w_gate, w_up, w_down):
    f32 = jnp.float32
    xf = x.astype(f32)
    g_prob = jax.nn.softmax(xf @ wg.astype(f32) + bg.astype(f32), axis=-1)
    g_top = jnp.argmax(g_prob, axis=-1)
    p_g = jnp.max(g_prob, axis=-1)
    e_logits = jnp.einsum('sd,gde->sge', xf, we.astype(f32)) + be.astype(f32)
    e_logits = jnp.take_along_axis(e_logits, g_top[:, None, None], axis=1)[:, 0]
    e_prob = jax.nn.softmax(e_logits, axis=-1)
    w2, i2 = lax.top_k(e_prob, TOP_K)
    w2 = w2 / jnp.sum(w2, axis=-1, keepdims=True)
    expert_id = g_top[:, None] * EXPERTS_PER_GROUP + i2
    gate = jnp.sum(jax.nn.one_hot(expert_id, N_EXPERTS, dtype=f32) * (p_g[:, None] * w2)[..., None], axis=1)
    h = jax.nn.silu(jnp.einsum('sd,edf->sef', x, w_gate)) * jnp.einsum('sd,edf->sef', x, w_up)
    h = h * gate[:, :, None].astype(h.dtype)
    return jnp.einsum('sef,efd->sd', h, w_down)


def setup_inputs(seed: int = 0) -> dict:
    key = jax.random.key(seed)
    ks = jax.random.split(key, 32)
    f32 = jnp.float32
    L = DEPTH
    nrm = lambda k, shape, s: jax.random.normal(k, shape, f32) * s
    lam_im0 = jnp.pi * jnp.arange(S5_STATE, dtype=f32)
    return {
        "x": nrm(ks[0], (BATCH, SEQ, D_MODEL), 1.0),
        "norm_mix": 1.0 + nrm(ks[1], (L, D_MODEL), 0.02),
        "w_in": nrm(ks[2], (L, D_MODEL, IN_COLS), D_MODEL ** -0.5),
        "sgu_norm": 1.0 + nrm(ks[3], (L, SGU_WIDTH), 0.02),
        "sgu_w": nrm(ks[4], (L, SGU_HEADS, SGU_BLOCK, SGU_BLOCK), 0.5 * SGU_BLOCK ** -0.5),
        "sgu_b": 1.0 + nrm(ks[5], (L, SGU_HEADS, SGU_BLOCK), 0.02),
        "s5_lambda_re": -0.5 + nrm(ks[6], (L, S5_N_GROUPS, S5_STATE), 0.01),
        "s5_lambda_im": lam_im0 + nrm(ks[7], (L, S5_N_GROUPS, S5_STATE), 0.01),
        "s5_log_dt": jax.random.uniform(ks[8], (L, S5_N_GROUPS), f32, math.log(S5_DT_MIN), math.log(S5_DT_MAX)),
        "s5_b_re": nrm(ks[9], (L, S5_N_GROUPS, S5_STATE, S5_GROUP), (2.0 * S5_GROUP) ** -0.5),
        "s5_b_im": nrm(ks[10], (L, S5_N_GROUPS, S5_STATE, S5_GROUP), (2.0 * S5_GROUP) ** -0.5),
        "s5_c_re": nrm(ks[11], (L, S5_N_GROUPS, S5_GROUP, S5_STATE), (2.0 * S5_STATE) ** -0.5),
        "s5_c_im": nrm(ks[12], (L, S5_N_GROUPS, S5_GROUP, S5_STATE), (2.0 * S5_STATE) ** -0.5),
        "s5_d": nrm(ks[13], (L, S5_WIDTH), 1.0),
        "s5_glu_w": nrm(ks[14], (L, S5_WIDTH, S5_WIDTH), S5_WIDTH ** -0.5),
        "s5_glu_b": nrm(ks[15], (L, S5_WIDTH), 0.02),
        "q_norm": 1.0 + nrm(ks[16], (L, HEAD_DIM), 0.02),
        "k_norm": 1.0 + nrm(ks[17], (L, HEAD_DIM), 0.02),
        "rel_bias": nrm(ks[18], (L, ATTN_HEADS, 2 * MAX_REL + 1), 0.1),
        "out_norm": 1.0 + nrm(ks[19], (L, D_MIX), 0.02),
        "w_out": nrm(ks[20], (L, D_MIX, D_MODEL), D_MIX ** -0.5),
        "norm_ffn": 1.0 + nrm(ks[21], (L, D_MODEL), 0.02),
        "router_group_w": nrm(ks[22], (L, D_MODEL, N_EXPERT_GROUPS), D_MODEL ** -0.5),
        "router_group_b": nrm(ks[23], (L, N_EXPERT_GROUPS), 0.01),
        "router_expert_w": nrm(ks[24], (L, N_EXPERT_GROUPS, D_MODEL, EXPERTS_PER_GROUP), D_MODEL ** -0.5),
        "router_expert_b": nrm(ks[25], (L, N_EXPERT_GROUPS, EXPERTS_PER_GROUP), 0.01),
        "w_gate": nrm(ks[26], (L, N_EXPERTS, D_MODEL, D_EXPERT), D_MODEL ** -0.5),
        "w_up": nrm(ks[27], (L, N_EXPERTS, D_MODEL, D_EXPERT), D_MODEL ** -0.5),
        "w_down": nrm(ks[28], (L, N_EXPERTS, D_EXPERT, D_MODEL), D_EXPERT ** -0.5),
    }


def reference(x, norm_mix, w_in, sgu_norm, sgu_w, sgu_b, s5_lambda_re, s5_lambda_im, s5_log_dt,
              s5_b_re, s5_b_im, s5_c_re, s5_c_im, s5_d, s5_glu_w, s5_glu_b, q_norm, k_norm, rel_bias,
              out_norm, w_out, norm_ffn, router_group_w, router_group_b, router_expert_w,
              router_expert_b, w_gate, w_up, w_down):
    b_, s_, _ = x.shape
    for l in range(DEPTH):
        h = rms_norm(x, norm_mix[l])
        proj = h @ w_in[l]
        z_sgu, z_s5, z_qkv = jnp.split(proj, [2 * SGU_WIDTH, 2 * SGU_WIDTH + S5_WIDTH], axis=-1)
        o_a = sgu_mixer(jax.nn.gelu(z_sgu), sgu_norm[l], sgu_w[l], sgu_b[l])
        o_b = s5_mixer(z_s5, s5_lambda_re[l], s5_lambda_im[l], s5_log_dt[l], s5_b_re[l], s5_b_im[l],
                       s5_c_re[l], s5_c_im[l], s5_d[l], s5_glu_w[l], s5_glu_b[l])
        q, k, v = jnp.split(z_qkv.reshape(b_, s_, 3, ATTN_HEADS, HEAD_DIM), 3, axis=2)
        o_c = band_attention(q[:, :, 0], k[:, :, 0], v[:, :, 0], q_norm[l], k_norm[l], rel_bias[l])
        o = jnp.concatenate([o_a, o_b, o_c], axis=-1)
        o = rms_norm(o.reshape(b_, s_, D_MIX // OUT_NORM_GROUP, OUT_NORM_GROUP),
                     out_norm[l].reshape(D_MIX // OUT_NORM_GROUP, OUT_NORM_GROUP)).reshape(b_, s_, D_MIX)
        x = x + o @ w_out[l]
        h = rms_norm(x, norm_ffn[l])
        moe = lambda hs, l=l: hier_moe(hs, router_group_w[l], router_group_b[l], router_expert_w[l],
                                       router_expert_b[l], w_gate[l], w_up[l], w_down[l])
        x = x + lax.map(moe, h)
    return x
```

```python
import functools

import jax
import jax.numpy as jnp
from jax import lax
from jax.experimental import pallas as pl
from jax.experimental.pallas import tpu as pltpu

F32 = jnp.float32
BF16 = jnp.bfloat16

D_MODEL = 1024
CHUNK = 64
HEAD_DIM = 64
SGU_WIDTH = 256
SGU_HEADS = 4
SGU_BLOCK = 128
S5_WIDTH = 256
S5_GROUP = 16
S5_N_GROUPS = 16
S5_STATE = 64
ATTN_WIDTH = 512
ATTN_HEADS = 8
BAND_CHUNKS = 9
MAX_REL = 256
IN_COLS = 2 * SGU_WIDTH + S5_WIDTH + 3 * ATTN_WIDTH
OUT_NORM_GROUP = 64
N_EXPERT_GROUPS = 4
EXPERTS_PER_GROUP = 4
N_EXPERTS = 16
D_EXPERT = 256
EPS = 1e-6
NEG_INF = -1e30

LANES = 128
SUBLANES = 8
VMEM_LIMIT_BYTES = 56 * 1024 * 1024

S5_T = 16
S5_ROW = S5_T * S5_WIDTH
S5_NSTATE = S5_N_GROUPS * S5_STATE
ATTN_TQ = 256
ATTN_PREV = (BAND_CHUNKS - 1) * CHUNK
ATTN_TK = ATTN_TQ + ATTN_PREV
ROUTER_SLAB = SUBLANES


def _tiles(n_tokens, seq):
    def pick(pref, total):
        t = min(pref, total)
        assert total % t == 0
        return t
    return dict(
        tm_in=pick(512, seq),
        tm_out=pick(512, seq),
        tm_moe=pick(1024, n_tokens),
        s5_rows=pick(256, seq // S5_T),
    )


def _dot(a, b):
    return jnp.dot(a, b, preferred_element_type=F32)


def _dot_nt(a, b):
    return lax.dot_general(a, b, (((1,), (1,)), ((), ())), preferred_element_type=F32)


def _group_sumsq(x, bd_ref):
    x2 = (x * x).astype(BF16)
    parts = [_dot(x2[:, t * LANES:(t + 1) * LANES], bd_ref[...]) for t in range(x.shape[1] // LANES)]
    return jnp.concatenate(parts, axis=-1)


def _in_kernel(x_ref, g_ref, w_ref, sgug_ref, sguw_ref, sgub_ref, qg_ref, kg_ref, bd_ref,
               oa_ref, u_ref, ub_ref, q_ref, k_ref, v_ref):
    tm = x_ref.shape[0]
    x = x_ref[...]
    ms = jnp.mean(x * x, axis=-1, keepdims=True)
    hn = (x * lax.rsqrt(ms + EPS) * g_ref[...]).astype(BF16)

    z = jax.nn.gelu(_dot(hn, w_ref[:, 0:2 * SGU_WIDTH]))
    u = z[:, :SGU_WIDTH]
    v = z[:, SGU_WIDTH:]
    v = v * lax.rsqrt(jnp.mean(v * v, axis=-1, keepdims=True) + EPS) * sgug_ref[...]
    vb = v.astype(BF16)
    first_head = lax.broadcasted_iota(jnp.int32, (SGU_BLOCK, LANES), 1) < HEAD_DIM
    for r in range(tm // SGU_BLOCK):
        rows = slice(r * SGU_BLOCK, (r + 1) * SGU_BLOCK)
        for p in range(SGU_WIDTH // LANES):
            cols = slice(p * LANES, (p + 1) * LANES)
            vp = vb[rows, cols]
            mixed = jnp.where(first_head, _dot(sguw_ref[2 * p], vp), _dot(sguw_ref[2 * p + 1], vp))
            oa_ref[rows, cols] = u[rows, cols] * (mixed + sgub_ref[:, cols])

    c0 = 2 * SGU_WIDTH
    us = _dot(hn, w_ref[:, c0:c0 + S5_WIDTH])
    u_ref[...] = us
    ub_ref[...] = us.astype(BF16)

    c0 += S5_WIDTH
    q = _dot(hn, w_ref[:, c0:c0 + ATTN_WIDTH])
    q_ref[...] = (q * lax.rsqrt(_group_sumsq(q, bd_ref) * (1.0 / HEAD_DIM) + EPS) * qg_ref[...]).astype(BF16)
    c0 += ATTN_WIDTH
    k = _dot(hn, w_ref[:, c0:c0 + ATTN_WIDTH])
    k_ref[...] = (k * lax.rsqrt(_group_sumsq(k, bd_ref) * (1.0 / HEAD_DIM) + EPS) * kg_ref[...]).astype(BF16)
    c0 += ATTN_WIDTH
    v_ref[...] = _dot(hn, w_ref[:, c0:c0 + ATTN_WIDTH]).astype(BF16)


def _in_call(x, g, w, sgug, sguw, sgub, qg, kg, bd, tm):
    n = x.shape[0]
    row = lambda c: pl.BlockSpec((tm, c), lambda i: (i, 0))
    full = lambda a: pl.BlockSpec(a.shape, lambda i: (0,) * a.ndim)
    return pl.pallas_call(
        _in_kernel,
        grid=(n // tm,),
        in_specs=[row(D_MODEL), full(g), full(w), full(sgug), full(sguw), full(sgub), full(qg), full(kg), full(bd)],
        out_specs=[row(SGU_WIDTH), row(S5_WIDTH), row(S5_WIDTH), row(ATTN_WIDTH), row(ATTN_WIDTH), row(ATTN_WIDTH)],
        out_shape=[jax.ShapeDtypeStruct((n, SGU_WIDTH), F32), jax.ShapeDtypeStruct((n, S5_WIDTH), F32),
                   jax.ShapeDtypeStruct((n, S5_WIDTH), BF16), jax.ShapeDtypeStruct((n, ATTN_WIDTH), BF16),
                   jax.ShapeDtypeStruct((n, ATTN_WIDTH), BF16), jax.ShapeDtypeStruct((n, ATTN_WIDTH), BF16)],
        compiler_params=pltpu.CompilerParams(dimension_semantics=("parallel",), vmem_limit_bytes=VMEM_LIMIT_BYTES),
        name="in_proj",
    )(x, g, w, sgug, sguw, sgub, qg, kg, bd)


def _s5_kernel(u_ref, wv_ref, krev_ref, wc_ref, apow_ref, y_ref, s_ref, sp_ref, carry_ref):
    rows = u_ref.shape[0]
    n_steps = apow_ref.shape[0]

    @pl.when(pl.program_id(1) == 0)
    def _():
        carry_ref[...] = jnp.zeros_like(carry_ref)

    s_ref[...] = _dot(u_ref[...], wv_ref[...])
    row_id = lax.broadcasted_iota(jnp.int32, (rows, LANES), 0)
    for cb in range(S5_NSTATE // LANES):
        cre = slice(cb * LANES, (cb + 1) * LANES)
        cim = slice(S5_NSTATE + cb * LANES, S5_NSTATE + (cb + 1) * LANES)
        re = s_ref[:, cre]
        im = s_ref[:, cim]
        c_re = carry_ref[0:1, cre]
        c_im = carry_ref[0:1, cim]
        a_re = apow_ref[0, 0:1, cre]
        a_im = apow_ref[0, 1:2, cre]
        re = re + jnp.where(row_id == 0, a_re * c_re - a_im * c_im, 0.0)
        im = im + jnp.where(row_id == 0, a_re * c_im + a_im * c_re, 0.0)
        for k in range(n_steps):
            shift = 1 << k
            a_re = apow_ref[k, 0:1, cre]
            a_im = apow_ref[k, 1:2, cre]
            re_s = jnp.where(row_id >= shift, pltpu.roll(re, shift, 0), 0.0)
            im_s = jnp.where(row_id >= shift, pltpu.roll(im, shift, 0), 0.0)
            re, im = re + a_re * re_s - a_im * im_s, im + a_re * im_s + a_im * re_s
        sp_ref[:, cre] = jnp.where(row_id >= 1, pltpu.roll(re, 1, 0), c_re).astype(BF16)
        sp_ref[:, cim] = jnp.where(row_id >= 1, pltpu.roll(im, 1, 0), c_im).astype(BF16)
        carry_ref[0:1, cre] = re[rows - 1:rows, :]
        carry_ref[0:1, cim] = im[rows - 1:rows, :]

    for i in range(S5_T):
        cols = slice(i * S5_WIDTH, (i + 1) * S5_WIDTH)
        intra = _dot(u_ref[:, 0:(i + 1) * S5_WIDTH], krev_ref[(S5_T - 1 - i) * S5_WIDTH:, :])
        y_ref[:, cols] = intra + _dot(sp_ref[...], wc_ref[:, cols])


def _s5_call(u_rows, wv, krev, wc, apow, batch, rows):
    n_rows = u_rows.shape[0]
    tiles_per_seq = n_rows // batch // rows
    const = lambda a: pl.BlockSpec(a.shape, lambda b, t: (0,) * a.ndim, pipeline_mode=pl.Buffered(1))
    blk = pl.BlockSpec((rows, S5_ROW), lambda b, t: (b * tiles_per_seq + t, 0))
    return pl.pallas_call(
        _s5_kernel,
        grid=(batch, tiles_per_seq),
        in_specs=[blk, const(wv), const(krev), const(wc), const(apow)],
        out_specs=blk,
        out_shape=jax.ShapeDtypeStruct((n_rows, S5_ROW), F32),
        scratch_shapes=[pltpu.VMEM((rows, 2 * S5_NSTATE), F32), pltpu.VMEM((rows, 2 * S5_NSTATE), BF16),
                        pltpu.VMEM((SUBLANES, 2 * S5_NSTATE), F32)],
        compiler_params=pltpu.CompilerParams(dimension_semantics=("arbitrary", "arbitrary"),
                                             vmem_limit_bytes=VMEM_LIMIT_BYTES),
        name="s5_scan",
    )(u_rows, wv, krev, wc, apow)


def _s5_tables(lam_re, lam_im, log_dt, b_re, b_im, c_re, c_im, rows):
    g_, p_, h_ = S5_N_GROUPS, S5_STATE, S5_GROUP
    lam = lax.complex(lam_re.astype(F32), lam_im.astype(F32))
    dt = jnp.exp(log_dt.astype(F32))[:, None]
    a_bar = jnp.exp(lam * dt)
    b_bar = ((a_bar - 1) / lam)[..., None] * lax.complex(b_re.astype(F32), b_im.astype(F32))
    c_mat = lax.complex(c_re.astype(F32), c_im.astype(F32))
    tau = jnp.arange(S5_T + 1, dtype=F32)
    a_pow = jnp.exp((lam * dt)[None] * tau[:, None, None])
    eye_g = jnp.eye(g_, dtype=F32)

    k_tau = jnp.einsum('gop,tgp,gpi->tgio', c_mat, a_pow[:S5_T], b_bar).real
    kbd = jnp.einsum('tgio,gk->tgiko', k_tau, eye_g).reshape(S5_T, S5_WIDTH, S5_WIDTH)
    krev = kbd[::-1].reshape(S5_T * S5_WIDTH, S5_WIDTH)

    wv_c = jnp.einsum('tgp,gpi->tgip', a_pow[:S5_T][::-1], b_bar)
    wv_re = jnp.einsum('tgip,gk->tgikp', wv_c.real, eye_g).reshape(S5_ROW, S5_NSTATE)
    wv_im = jnp.einsum('tgip,gk->tgikp', wv_c.imag, eye_g).reshape(S5_ROW, S5_NSTATE)
    wv = jnp.concatenate([wv_re, wv_im], axis=1)

    wc_c = jnp.einsum('gop,tgp->gpto', c_mat, a_pow[1:])
    wc_c = jnp.einsum('gpto,gk->gptko', wc_c, eye_g).reshape(S5_NSTATE, S5_ROW)
    wc = jnp.concatenate([wc_c.real, -wc_c.imag], axis=0)

    n_steps = max(1, (rows - 1).bit_length())
    a_chunk = a_pow[S5_T].reshape(S5_NSTATE)
    pows = [a_chunk]
    for _ in range(n_steps - 1):
        pows.append(pows[-1] * pows[-1])
    apow = jnp.stack([jnp.stack([p.real, p.imag]) for p in pows])
    return wv.astype(BF16), krev.astype(BF16), wc.astype(BF16), apow.astype(F32)


def _attn_kernel(q_ref, k0_ref, k1_ref, k2_ref, v0_ref, v1_ref, v2_ref, bias_ref, o_ref):
    lane = lax.broadcasted_iota(jnp.int32, (1, LANES), 1)
    for p in range(ATTN_WIDTH // LANES):
        cols = slice(p * LANES, (p + 1) * LANES)
        qp = q_ref[:, cols]
        kcat = jnp.concatenate([k0_ref[:, cols], k1_ref[:, cols], k2_ref[:, cols]], axis=0)
        vcat = jnp.concatenate([v0_ref[:, cols], v1_ref[:, cols], v2_ref[:, cols]], axis=0)
        acc = jnp.zeros((ATTN_TQ, LANES), F32)
        for hh in range(LANES // HEAD_DIM):
            in_head = (lane >= hh * HEAD_DIM) & (lane < (hh + 1) * HEAD_DIM)
            qm = jnp.where(in_head, qp, jnp.zeros_like(qp))
            s = _dot_nt(qm, kcat) + bias_ref[0, 2 * p + hh]
            m = jnp.max(s, axis=-1, keepdims=True)
            e = jnp.exp(s - m)
            denom = jnp.sum(e, axis=-1, keepdims=True)
            o = _dot(e.astype(BF16), vcat)
            acc = acc + jnp.where(in_head, o / denom, 0.0)
        o_ref[:, cols] = acc


def _attn_call(q, k, v, bias, batch, seq):
    n = q.shape[0]
    tiles = seq // ATTN_TQ
    n_prev = ATTN_PREV // ATTN_TQ
    assert n_prev == 2 and bias.shape[0] == n_prev + 1
    cur = pl.BlockSpec((ATTN_TQ, ATTN_WIDTH), lambda b, t: (b * tiles + t, 0))
    prev = lambda d: pl.BlockSpec((ATTN_TQ, ATTN_WIDTH), lambda b, t: (b * tiles + jnp.maximum(t - d, 0), 0))
    bias_spec = pl.BlockSpec((1, ATTN_HEADS, ATTN_TQ, ATTN_TK), lambda b, t: (jnp.minimum(t, n_prev), 0, 0, 0))
    return pl.pallas_call(
        _attn_kernel,
        grid=(batch, tiles),
        in_specs=[cur, prev(2), prev(1), cur, prev(2), prev(1), cur, bias_spec],
        out_specs=cur,
        out_shape=jax.ShapeDtypeStruct((n, ATTN_WIDTH), F32),
        compiler_params=pltpu.CompilerParams(dimension_semantics=("parallel", "arbitrary"),
                                             vmem_limit_bytes=VMEM_LIMIT_BYTES),
        name="band_attn",
    )(q, k, k, k, v, v, v, bias)


def _attn_bias(rel_bias):
    qi = jnp.arange(ATTN_TQ)
    kj = jnp.arange(ATTN_TK)
    rel = jnp.clip((qi[:, None] + ATTN_PREV) - kj[None, :], -MAX_REL, MAX_REL) + MAX_REL
    base = rel_bias.astype(F32)[:, rel]
    q_chunk = qi[:, None] // CHUNK + (BAND_CHUNKS - 1)
    k_chunk = kj[None, :] // CHUNK
    in_band = (k_chunk <= q_chunk) & (k_chunk >= q_chunk - (BAND_CHUNKS - 1))
    tabs = []
    for t in range(ATTN_PREV // ATTN_TQ + 1):
        exists = kj[None, :] >= ATTN_PREV - t * ATTN_TQ
        tabs.append(jnp.where((in_band & exists)[None], base, NEG_INF))
    return jnp.stack(tabs)


def _router_gates(lt):
    tm = lt.shape[1]
    row = lax.broadcasted_iota(jnp.int32, (ROUTER_SLAB, tm), 0)
    gl = lt[0:ROUTER_SLAB]
    gmax = jnp.max(gl, axis=0, keepdims=True)
    p_g = 1.0 / jnp.sum(jnp.exp(gl - gmax), axis=0, keepdims=True)
    g_top = jnp.min(jnp.where(gl == gmax, row, ROUTER_SLAB), axis=0, keepdims=True)
    el = jnp.zeros((ROUTER_SLAB, tm), F32)
    for g in range(N_EXPERT_GROUPS):
        el = el + jnp.where(g_top == g, lt[(g + 1) * ROUTER_SLAB:(g + 2) * ROUTER_SLAB], 0.0)
    ee = jnp.exp(el - jnp.max(el, axis=0, keepdims=True))
    ep = ee / jnp.sum(ee, axis=0, keepdims=True)
    p1 = jnp.max(ep, axis=0, keepdims=True)
    i1 = jnp.min(jnp.where(ep == p1, row, ROUTER_SLAB), axis=0, keepdims=True)
    rest = jnp.where(row == i1, -1.0, ep)
    p2 = jnp.max(rest, axis=0, keepdims=True)
    i2 = jnp.min(jnp.where(rest == p2, row, ROUTER_SLAB), axis=0, keepdims=True)
    tot = p1 + p2
    w = (jnp.where(row == i1, p1 / tot, 0.0) + jnp.where(row == i2, p2 / tot, 0.0)) * p_g
    return [jnp.where(g_top == g, w, 0.0) for g in range(N_EXPERT_GROUPS)]


def _out_kernel(oa_ref, y_ref, u_ref, oc_ref, x_ref, d_ref, gluw_ref, glub_ref, og_ref, wout_ref,
                fg_ref, wr_ref, br_ref, bd_ref, x1_ref, h_ref, gate_ref):
    tm = x_ref.shape[0]
    y = jax.nn.gelu(y_ref[...] + d_ref[...] * u_ref[...])
    ob = y * jax.nn.sigmoid(_dot(y.astype(BF16), gluw_ref[...]) + glub_ref[...])
    o = jnp.concatenate([oa_ref[...], ob, oc_ref[...]], axis=-1)
    on = o * lax.rsqrt(_group_sumsq(o, bd_ref) * (1.0 / OUT_NORM_GROUP) + EPS) * og_ref[...]
    x1 = x_ref[...] + _dot(on.astype(BF16), wout_ref[...])
    x1_ref[...] = x1
    ms = jnp.mean(x1 * x1, axis=-1, keepdims=True)
    hb = (x1 * lax.rsqrt(ms + EPS) * fg_ref[...]).astype(BF16)
    h_ref[...] = hb
    lt = _dot(hb, wr_ref[...]).T + br_ref[...]
    slabs = _router_gates(lt)
    pad = jnp.zeros((LANES - N_EXPERT_GROUPS * ROUTER_SLAB, tm), F32)
    gate_ref[...] = jnp.concatenate(slabs + [pad], axis=0).T


def _out_call(oa, y, u, oc, x, d, gluw, glub, og, wout, fg, wr, br, bd, tm):
    n = x.shape[0]
    row = lambda c: pl.BlockSpec((tm, c), lambda i: (i, 0))
    full = lambda a: pl.BlockSpec(a.shape, lambda i: (0,) * a.ndim)
    return pl.pallas_call(
        _out_kernel,
        grid=(n // tm,),
        in_specs=[row(SGU_WIDTH), row(S5_WIDTH), row(S5_WIDTH), row(ATTN_WIDTH), row(D_MODEL), full(d), full(gluw),
                  full(glub), full(og), full(wout), full(fg), full(wr), full(br), full(bd)],
        out_specs=[row(D_MODEL), row(D_MODEL), row(LANES)],
        out_shape=[jax.ShapeDtypeStruct((n, D_MODEL), F32), jax.ShapeDtypeStruct((n, D_MODEL), BF16),
                   jax.ShapeDtypeStruct((n, LANES), F32)],
        compiler_params=pltpu.CompilerParams(dimension_semantics=("parallel",), vmem_limit_bytes=VMEM_LIMIT_BYTES),
        name="out_proj",
    )(oa, y, u, oc, x, d, gluw, glub, og, wout, fg, wr, br, bd)


def _router_tables(wg, bg, we, be):
    w = jnp.zeros((D_MODEL, LANES), F32)
    b = jnp.full((LANES,), NEG_INF, F32)
    w = w.at[:, 0:N_EXPERT_GROUPS].set(wg.astype(F32))
    b = b.at[0:N_EXPERT_GROUPS].set(bg.astype(F32))
    for g in range(N_EXPERT_GROUPS):
        c0 = (g + 1) * ROUTER_SLAB
        w = w.at[:, c0:c0 + EXPERTS_PER_GROUP].set(we[g].astype(F32))
        b = b.at[c0:c0 + EXPERTS_PER_GROUP].set(be[g].astype(F32))
    return w.astype(BF16), b[:, None]


def _moe_kernel(h_ref, x_ref, gate_ref, wg_ref, wu_ref, wd_ref, o_ref, acc_ref):
    e = pl.program_id(1)

    @pl.when(e == 0)
    def _():
        acc_ref[...] = x_ref[...]

    h = h_ref[...]
    g = _dot(h, wg_ref[0])
    u = _dot(h, wu_ref[0])
    gate_lane = (e // EXPERTS_PER_GROUP) * ROUTER_SLAB + e % EXPERTS_PER_GROUP
    lane = lax.broadcasted_iota(jnp.int32, (1, LANES), 1)
    gate = jnp.sum(jnp.where(lane == gate_lane, gate_ref[...], 0.0), axis=-1, keepdims=True)
    a = (g * jax.nn.sigmoid(g)) * u * gate
    acc_ref[...] += _dot(a.astype(BF16), wd_ref[0])

    @pl.when(e == N_EXPERTS - 1)
    def _():
        o_ref[...] = acc_ref[...]


def _moe_call(h, x1, gates, wg, wu, wd, tm):
    n = h.shape[0]
    row = lambda c: pl.BlockSpec((tm, c), lambda i, e: (i, 0))
    return pl.pallas_call(
        _moe_kernel,
        grid=(n // tm, N_EXPERTS),
        in_specs=[row(D_MODEL), row(D_MODEL), row(LANES),
                  pl.BlockSpec((1, D_MODEL, D_EXPERT), lambda i, e: (e, 0, 0)),
                  pl.BlockSpec((1, D_MODEL, D_EXPERT), lambda i, e: (e, 0, 0)),
                  pl.BlockSpec((1, D_EXPERT, D_MODEL), lambda i, e: (e, 0, 0))],
        out_specs=row(D_MODEL),
        out_shape=jax.ShapeDtypeStruct((n, D_MODEL), F32),
        scratch_shapes=[pltpu.VMEM((tm, D_MODEL), F32)],
        compiler_params=pltpu.CompilerParams(dimension_semantics=("parallel", "arbitrary"),
                                             vmem_limit_bytes=VMEM_LIMIT_BYTES),
        name="moe_experts",
    )(h, x1, gates, wg, wu, wd)


def kernel(x, norm_mix, w_in, sgu_norm, sgu_w, sgu_b, s5_lambda_re, s5_lambda_im, s5_log_dt, s5_b_re, s5_b_im,
           s5_c_re, s5_c_im, s5_d, s5_glu_w, s5_glu_b, q_norm, k_norm, rel_bias, out_norm, w_out, norm_ffn,
           router_group_w, router_group_b, router_expert_w, router_expert_b, w_gate, w_up, w_down):
    batch, seq, _ = x.shape
    n = batch * seq
    depth = w_in.shape[0]
    t = _tiles(n, seq)
    assert seq % ATTN_TQ == 0 and seq % (S5_T * t["s5_rows"]) == 0
    row_vec = lambda a: a.astype(F32)[None, :]

    lane_group = jnp.arange(LANES) // HEAD_DIM
    bd = (lane_group[:, None] == lane_group[None, :]).astype(BF16)
    block_chunk = jnp.arange(SGU_BLOCK) // CHUNK
    sgu_mask = block_chunk[None, :] <= block_chunk[:, None]

    xf = x.reshape(n, D_MODEL)
    for l in range(depth):
        sguw = jnp.where(sgu_mask[None], sgu_w[l], 0).astype(BF16)
        sgub = jnp.repeat(sgu_b[l].astype(F32).T, HEAD_DIM, axis=1)
        qg = row_vec(jnp.tile(q_norm[l], ATTN_HEADS)) * (HEAD_DIM ** -0.5)
        kg = row_vec(jnp.tile(k_norm[l], ATTN_HEADS))
        oa, u, ub, q, k, v = _in_call(xf, row_vec(norm_mix[l]), w_in[l].astype(BF16), row_vec(sgu_norm[l]), sguw, sgub,
                                      qg, kg, bd, t["tm_in"])

        wv, krev, wc, apow = _s5_tables(s5_lambda_re[l], s5_lambda_im[l], s5_log_dt[l], s5_b_re[l], s5_b_im[l],
                                        s5_c_re[l], s5_c_im[l], t["s5_rows"])
        y = _s5_call(ub.reshape(n // S5_T, S5_ROW), wv, krev, wc, apow, batch, t["s5_rows"]).reshape(n, S5_WIDTH)

        oc = _attn_call(q, k, v, _attn_bias(rel_bias[l]), batch, seq)

        wr, br = _router_tables(router_group_w[l], router_group_b[l], router_expert_w[l], router_expert_b[l])
        x1, h, gates = _out_call(oa, y, u, oc, xf, row_vec(s5_d[l]), s5_glu_w[l].astype(BF16), row_vec(s5_glu_b[l]),
                                 row_vec(out_norm[l]), w_out[l].astype(BF16), row_vec(norm_ffn[l]), wr, br, bd,
                                 t["tm_out"])

        xf = _moe_call(h, x1, gates, w_gate[l].astype(BF16), w_up[l].astype(BF16), w_down[l].astype(BF16),
                       t["tm_moe"])
    return xf.reshape(batch, seq, D_MODEL)
```

```python
import functools

import jax
import jax.numpy as jnp
from jax import lax
from jax.experimental import pallas as pl
from jax.experimental.pallas import tpu as pltpu

F32 = jnp.float32
BF16 = jnp.bfloat16

D_MODEL = 1024
CHUNK = 64
HEAD_DIM = 64
SGU_WIDTH = 256
SGU_HEADS = 4
SGU_BLOCK = 128
S5_WIDTH = 256
S5_GROUP = 16
S5_N_GROUPS = 16
S5_STATE = 64
ATTN_WIDTH = 512
ATTN_HEADS = 8
BAND_CHUNKS = 9
MAX_REL = 256
IN_COLS = 2 * SGU_WIDTH + S5_WIDTH + 3 * ATTN_WIDTH
OUT_NORM_GROUP = 64
N_EXPERT_GROUPS = 4
EXPERTS_PER_GROUP = 4
N_EXPERTS = 16
D_EXPERT = 256
EPS = 1e-6
NEG_INF = -1e30

LANES = 128
SUBLANES = 8
VMEM_LIMIT_BYTES = 56 * 1024 * 1024

S5_T = 16
S5_ROW = S5_T * S5_WIDTH
S5_NSTATE = S5_N_GROUPS * S5_STATE
ATTN_TQ = 256
ATTN_PREV = (BAND_CHUNKS - 1) * CHUNK
ATTN_TK = ATTN_TQ + ATTN_PREV
ROUTER_SLAB = SUBLANES


def _tiles(n_tokens, seq):
    def pick(pref, total):
        t = min(pref, total)
        assert total % t == 0
        return t
    return dict(
        tm_in=pick(512, seq),
        tm_out=pick(512, seq),
        tm_moe=pick(1024, n_tokens),
        s5_rows=pick(128, seq // S5_T),
    )


def _dot(a, b):
    return jnp.dot(a, b, preferred_element_type=F32)


def _dot_nt(a, b):
    return lax.dot_general(a, b, (((1,), (1,)), ((), ())), preferred_element_type=F32)


def _group_sumsq(x, bd_ref):
    x2 = (x * x).astype(BF16)
    parts = [_dot(x2[:, t * LANES:(t + 1) * LANES], bd_ref[...]) for t in range(x.shape[1] // LANES)]
    return jnp.concatenate(parts, axis=-1)


def _in_kernel(x_ref, g_ref, w_ref, sgug_ref, sguw_ref, sgub_ref, qg_ref, kg_ref, bd_ref,
               oa_ref, u0_ref, u1_ref, q_ref, k_ref, v_ref):
    tm = x_ref.shape[0]
    x = x_ref[...]
    ms = jnp.mean(x * x, axis=-1, keepdims=True)
    hn = (x * lax.rsqrt(ms + EPS) * g_ref[...]).astype(BF16)

    z = jax.nn.gelu(_dot(hn, w_ref[:, 0:2 * SGU_WIDTH]))
    u = z[:, :SGU_WIDTH]
    v = z[:, SGU_WIDTH:]
    v = v * lax.rsqrt(jnp.mean(v * v, axis=-1, keepdims=True) + EPS) * sgug_ref[...]
    vb = v.astype(BF16)
    first_head = lax.broadcasted_iota(jnp.int32, (SGU_BLOCK, LANES), 1) < HEAD_DIM
    for r in range(tm // SGU_BLOCK):
        rows = slice(r * SGU_BLOCK, (r + 1) * SGU_BLOCK)
        for p in range(SGU_WIDTH // LANES):
            cols = slice(p * LANES, (p + 1) * LANES)
            vp = vb[rows, cols]
            mixed = jnp.where(first_head, _dot(sguw_ref[2 * p], vp), _dot(sguw_ref[2 * p + 1], vp))
            oa_ref[rows, cols] = u[rows, cols] * (mixed + sgub_ref[:, cols])

    c0 = 2 * SGU_WIDTH
    us = _dot(hn, w_ref[:, c0:c0 + S5_WIDTH])
    u0_ref[...] = us[:, :LANES]
    u1_ref[...] = us[:, LANES:]

    c0 += S5_WIDTH
    q = _dot(hn, w_ref[:, c0:c0 + ATTN_WIDTH])
    q_ref[...] = (q * lax.rsqrt(_group_sumsq(q, bd_ref) * (1.0 / HEAD_DIM) + EPS) * qg_ref[...]).astype(BF16)
    c0 += ATTN_WIDTH
    k = _dot(hn, w_ref[:, c0:c0 + ATTN_WIDTH])
    k_ref[...] = (k * lax.rsqrt(_group_sumsq(k, bd_ref) * (1.0 / HEAD_DIM) + EPS) * kg_ref[...]).astype(BF16)
    c0 += ATTN_WIDTH
    v_ref[...] = _dot(hn, w_ref[:, c0:c0 + ATTN_WIDTH]).astype(BF16)


def _in_call(x, g, w, sgug, sguw, sgub, qg, kg, bd, tm):
    n = x.shape[0]
    row = lambda c: pl.BlockSpec((tm, c), lambda i: (i, 0))
    full = lambda a: pl.BlockSpec(a.shape, lambda i: (0,) * a.ndim)
    return pl.pallas_call(
        _in_kernel,
        grid=(n // tm,),
        in_specs=[row(D_MODEL), full(g), full(w), full(sgug), full(sguw), full(sgub), full(qg), full(kg), full(bd)],
        out_specs=[row(SGU_WIDTH), row(LANES), row(LANES), row(ATTN_WIDTH), row(ATTN_WIDTH), row(ATTN_WIDTH)],
        out_shape=[jax.ShapeDtypeStruct((n, SGU_WIDTH), F32), jax.ShapeDtypeStruct((n, LANES), F32),
                   jax.ShapeDtypeStruct((n, LANES), F32), jax.ShapeDtypeStruct((n, ATTN_WIDTH), BF16),
                   jax.ShapeDtypeStruct((n, ATTN_WIDTH), BF16), jax.ShapeDtypeStruct((n, ATTN_WIDTH), BF16)],
        compiler_params=pltpu.CompilerParams(dimension_semantics=("parallel",), vmem_limit_bytes=VMEM_LIMIT_BYTES),
        name="in_proj",
    )(x, g, w, sgug, sguw, sgub, qg, kg, bd)


def _s5_kernel(u0_ref, u1_ref, wv_ref, krev_ref, wc_ref, apow_ref, y0_ref, y1_ref, ur_ref, s_ref, sp_ref, carry_ref):
    rows = ur_ref.shape[0]
    n_steps = apow_ref.shape[0]

    @pl.when(pl.program_id(1) == 0)
    def _():
        carry_ref[...] = jnp.zeros_like(carry_ref)

    for t in range(S5_T):
        ur_ref[:, t * S5_WIDTH:t * S5_WIDTH + LANES] = u0_ref[pl.ds(t, rows, stride=S5_T), :].astype(BF16)
        ur_ref[:, t * S5_WIDTH + LANES:(t + 1) * S5_WIDTH] = u1_ref[pl.ds(t, rows, stride=S5_T), :].astype(BF16)

    s_ref[...] = _dot(ur_ref[...], wv_ref[...])
    row_id = lax.broadcasted_iota(jnp.int32, (rows, LANES), 0)
    for cb in range(S5_NSTATE // LANES):
        cre = slice(cb * LANES, (cb + 1) * LANES)
        cim = slice(S5_NSTATE + cb * LANES, S5_NSTATE + (cb + 1) * LANES)
        re = s_ref[:, cre]
        im = s_ref[:, cim]
        c_re = carry_ref[0:1, cre]
        c_im = carry_ref[0:1, cim]
        a_re = apow_ref[0, 0:1, cre]
        a_im = apow_ref[0, 1:2, cre]
        re = re + jnp.where(row_id == 0, a_re * c_re - a_im * c_im, 0.0)
        im = im + jnp.where(row_id == 0, a_re * c_im + a_im * c_re, 0.0)
        for k in range(n_steps):
            shift = 1 << k
            a_re = apow_ref[k, 0:1, cre]
            a_im = apow_ref[k, 1:2, cre]
            re_s = jnp.where(row_id >= shift, pltpu.roll(re, shift, 0), 0.0)
            im_s = jnp.where(row_id >= shift, pltpu.roll(im, shift, 0), 0.0)
            re, im = re + a_re * re_s - a_im * im_s, im + a_re * im_s + a_im * re_s
        sp_ref[:, cre] = jnp.where(row_id >= 1, pltpu.roll(re, 1, 0), c_re).astype(BF16)
        sp_ref[:, cim] = jnp.where(row_id >= 1, pltpu.roll(im, 1, 0), c_im).astype(BF16)
        carry_ref[0:1, cre] = re[rows - 1:rows, :]
        carry_ref[0:1, cim] = im[rows - 1:rows, :]

    for i in range(S5_T):
        cols = slice(i * S5_WIDTH, (i + 1) * S5_WIDTH)
        intra = _dot(ur_ref[:, 0:(i + 1) * S5_WIDTH], krev_ref[(S5_T - 1 - i) * S5_WIDTH:, :])
        y = intra + _dot(sp_ref[...], wc_ref[:, cols])
        y0_ref[pl.ds(i, rows, stride=S5_T), :] = y[:, :LANES]
        y1_ref[pl.ds(i, rows, stride=S5_T), :] = y[:, LANES:]


def _s5_call(u0, u1, wv, krev, wc, apow, batch, rows):
    n = u0.shape[0]
    tok = rows * S5_T
    tiles_per_seq = n // batch // tok
    const = lambda a: pl.BlockSpec(a.shape, lambda b, t: (0,) * a.ndim, pipeline_mode=pl.Buffered(1))
    blk = pl.BlockSpec((tok, LANES), lambda b, t: (b * tiles_per_seq + t, 0))
    half = jax.ShapeDtypeStruct((n, LANES), F32)
    return pl.pallas_call(
        _s5_kernel,
        grid=(batch, tiles_per_seq),
        in_specs=[blk, blk, const(wv), const(krev), const(wc), const(apow)],
        out_specs=[blk, blk],
        out_shape=[half, half],
        scratch_shapes=[pltpu.VMEM((rows, S5_ROW), BF16), pltpu.VMEM((rows, 2 * S5_NSTATE), F32),
                        pltpu.VMEM((rows, 2 * S5_NSTATE), BF16), pltpu.VMEM((SUBLANES, 2 * S5_NSTATE), F32)],
        compiler_params=pltpu.CompilerParams(dimension_semantics=("arbitrary", "arbitrary"),
                                             vmem_limit_bytes=VMEM_LIMIT_BYTES),
        name="s5_scan",
    )(u0, u1, wv, krev, wc, apow)


def _s5_tables(lam_re, lam_im, log_dt, b_re, b_im, c_re, c_im, rows):
    g_, p_, h_ = S5_N_GROUPS, S5_STATE, S5_GROUP
    hi = lax.Precision.HIGHEST
    lam_re, lam_im = lam_re.astype(F32), lam_im.astype(F32)
    dt = jnp.exp(log_dt.astype(F32))[:, None]
    tau = jnp.arange(S5_T + 1, dtype=F32)[:, None, None]
    mag = jnp.exp(tau * (lam_re * dt)[None])
    ap_re = mag * jnp.cos(tau * (lam_im * dt)[None])
    ap_im = mag * jnp.sin(tau * (lam_im * dt)[None])
    n_re, n_im = ap_re[1] - 1.0, ap_im[1]
    den = lam_re * lam_re + lam_im * lam_im
    k_re = ((n_re * lam_re + n_im * lam_im) / den)[..., None]
    k_im = ((n_im * lam_re - n_re * lam_im) / den)[..., None]
    b_re, b_im = b_re.astype(F32), b_im.astype(F32)
    bb_re = k_re * b_re - k_im * b_im
    bb_im = k_re * b_im + k_im * b_re
    c_re, c_im = c_re.astype(F32), c_im.astype(F32)
    ab_re = ap_re[:S5_T, :, :, None] * bb_re[None] - ap_im[:S5_T, :, :, None] * bb_im[None]
    ab_im = ap_re[:S5_T, :, :, None] * bb_im[None] + ap_im[:S5_T, :, :, None] * bb_re[None]

    col_group = jnp.arange(S5_WIDTH) // h_
    state_group = jnp.arange(S5_NSTATE) // p_
    chan_group = jnp.arange(S5_ROW) // h_ % g_

    k_tau = (jnp.einsum('gop,tgpi->tgio', c_re, ab_re, precision=hi)
             - jnp.einsum('gop,tgpi->tgio', c_im, ab_im, precision=hi))
    krev = jnp.tile(k_tau[::-1].reshape(S5_ROW, h_), (1, g_))
    krev = jnp.where(chan_group[:, None] == col_group[None, :], krev, 0.0)

    def wv_half(ab):
        w = jnp.tile(ab[::-1].transpose(0, 1, 3, 2).reshape(S5_ROW, p_), (1, g_))
        return jnp.where(chan_group[:, None] == state_group[None, :], w, 0.0)
    wv = jnp.concatenate([wv_half(ab_re), wv_half(ab_im)], axis=1)

    cp = lambda c: c.transpose(0, 2, 1)[:, :, None, :]
    w_re = cp(c_re) * ap_re[1:].transpose(1, 2, 0)[..., None] - cp(c_im) * ap_im[1:].transpose(1, 2, 0)[..., None]
    w_im = cp(c_re) * ap_im[1:].transpose(1, 2, 0)[..., None] + cp(c_im) * ap_re[1:].transpose(1, 2, 0)[..., None]
    def wc_half(w):
        w = jnp.broadcast_to(w.reshape(S5_NSTATE, S5_T, 1, h_), (S5_NSTATE, S5_T, g_, h_)).reshape(S5_NSTATE, S5_ROW)
        return jnp.where(state_group[:, None] == chan_group[None, :], w, 0.0)
    wc = jnp.concatenate([wc_half(w_re), wc_half(-w_im)], axis=0)

    n_steps = max(1, (rows - 1).bit_length())
    pows = [(ap_re[S5_T].reshape(S5_NSTATE), ap_im[S5_T].reshape(S5_NSTATE))]
    for _ in range(n_steps - 1):
        r, i = pows[-1]
        pows.append((r * r - i * i, 2.0 * r * i))
    apow = jnp.stack([jnp.stack(p) for p in pows])
    return wv.astype(BF16), krev.astype(BF16), wc.astype(BF16), apow


def _attn_kernel(q_ref, k0_ref, k1_ref, k2_ref, v0_ref, v1_ref, v2_ref, bias_ref, o_ref):
    lane = lax.broadcasted_iota(jnp.int32, (1, LANES), 1)
    for p in range(ATTN_WIDTH // LANES):
        cols = slice(p * LANES, (p + 1) * LANES)
        qp = q_ref[:, cols]
        kcat = jnp.concatenate([k0_ref[:, cols], k1_ref[:, cols], k2_ref[:, cols]], axis=0)
        vcat = jnp.concatenate([v0_ref[:, cols], v1_ref[:, cols], v2_ref[:, cols]], axis=0)
        acc = jnp.zeros((ATTN_TQ, LANES), F32)
        for hh in range(LANES // HEAD_DIM):
            in_head = (lane >= hh * HEAD_DIM) & (lane < (hh + 1) * HEAD_DIM)
            qm = jnp.where(in_head, qp, jnp.zeros_like(qp))
            s = _dot_nt(qm, kcat) + bias_ref[0, 2 * p + hh]
            m = jnp.max(s, axis=-1, keepdims=True)
            e = jnp.exp(s - m)
            denom = jnp.sum(e, axis=-1, keepdims=True)
            o = _dot(e.astype(BF16), vcat)
            acc = acc + jnp.where(in_head, o / denom, 0.0)
        o_ref[:, cols] = acc


def _attn_call(q, k, v, bias, batch, seq):
    n = q.shape[0]
    tiles = seq // ATTN_TQ
    n_prev = ATTN_PREV // ATTN_TQ
    assert n_prev == 2 and bias.shape[0] == n_prev + 1
    cur = pl.BlockSpec((ATTN_TQ, ATTN_WIDTH), lambda b, t: (b * tiles + t, 0))
    prev = lambda d: pl.BlockSpec((ATTN_TQ, ATTN_WIDTH), lambda b, t: (b * tiles + jnp.maximum(t - d, 0), 0))
    bias_spec = pl.BlockSpec((1, ATTN_HEADS, ATTN_TQ, ATTN_TK), lambda b, t: (jnp.minimum(t, n_prev), 0, 0, 0))
    return pl.pallas_call(
        _attn_kernel,
        grid=(batch, tiles),
        in_specs=[cur, prev(2), prev(1), cur, prev(2), prev(1), cur, bias_spec],
        out_specs=cur,
        out_shape=jax.ShapeDtypeStruct((n, ATTN_WIDTH), F32),
        compiler_params=pltpu.CompilerParams(dimension_semantics=("parallel", "arbitrary"),
                                             vmem_limit_bytes=VMEM_LIMIT_BYTES),
        name="band_attn",
    )(q, k, k, k, v, v, v, bias)


def _attn_bias(rel_bias):
    n_diag = ATTN_TQ + ATTN_TK - 1
    d = (ATTN_TQ - 1 + ATTN_PREV) - jnp.arange(n_diag)
    vec = jnp.take(rel_bias.astype(F32), jnp.clip(d, -MAX_REL, MAX_REL) + MAX_REL, axis=1)
    skew = jnp.tile(jnp.pad(vec, ((0, 0), (0, 1))), (1, ATTN_TQ))[:, :ATTN_TQ * n_diag]
    skew = skew.reshape(ATTN_HEADS, ATTN_TQ, n_diag)
    base = skew[:, :, ATTN_TQ - 1:ATTN_TQ - 1 + ATTN_TK]
    qi = jnp.arange(ATTN_TQ)
    kj = jnp.arange(ATTN_TK)
    q_chunk = qi[:, None] // CHUNK + (BAND_CHUNKS - 1)
    k_chunk = kj[None, :] // CHUNK
    in_band = (k_chunk <= q_chunk) & (k_chunk >= q_chunk - (BAND_CHUNKS - 1))
    tabs = []
    for t in range(ATTN_PREV // ATTN_TQ + 1):
        exists = kj[None, :] >= ATTN_PREV - t * ATTN_TQ
        tabs.append(jnp.where((in_band & exists)[None], base, NEG_INF))
    return jnp.stack(tabs)


def _router_gates(lt):
    tm = lt.shape[1]
    row = lax.broadcasted_iota(jnp.int32, (ROUTER_SLAB, tm), 0)
    gl = lt[0:ROUTER_SLAB]
    gmax = jnp.max(gl, axis=0, keepdims=True)
    p_g = 1.0 / jnp.sum(jnp.exp(gl - gmax), axis=0, keepdims=True)
    g_top = jnp.min(jnp.where(gl == gmax, row, ROUTER_SLAB), axis=0, keepdims=True)
    el = jnp.zeros((ROUTER_SLAB, tm), F32)
    for g in range(N_EXPERT_GROUPS):
        el = el + jnp.where(g_top == g, lt[(g + 1) * ROUTER_SLAB:(g + 2) * ROUTER_SLAB], 0.0)
    ee = jnp.exp(el - jnp.max(el, axis=0, keepdims=True))
    ep = ee / jnp.sum(ee, axis=0, keepdims=True)
    p1 = jnp.max(ep, axis=0, keepdims=True)
    i1 = jnp.min(jnp.where(ep == p1, row, ROUTER_SLAB), axis=0, keepdims=True)
    rest = jnp.where(row == i1, -1.0, ep)
    p2 = jnp.max(rest, axis=0, keepdims=True)
    i2 = jnp.min(jnp.where(rest == p2, row, ROUTER_SLAB), axis=0, keepdims=True)
    tot = p1 + p2
    w = (jnp.where(row == i1, p1 / tot, 0.0) + jnp.where(row == i2, p2 / tot, 0.0)) * p_g
    return [jnp.where(g_top == g, w, 0.0) for g in range(N_EXPERT_GROUPS)]


def _out_kernel(oa_ref, y0_ref, y1_ref, u0_ref, u1_ref, oc_ref, x_ref, d_ref, gluw_ref, glub_ref, og_ref, wout_ref,
                fg_ref, wr_ref, br_ref, bd_ref, x1_ref, h_ref, gate_ref):
    tm = x_ref.shape[0]
    y = jnp.concatenate([y0_ref[...], y1_ref[...]], axis=-1)
    u = jnp.concatenate([u0_ref[...], u1_ref[...]], axis=-1)
    y = jax.nn.gelu(y + d_ref[...] * u)
    ob = y * jax.nn.sigmoid(_dot(y.astype(BF16), gluw_ref[...]) + glub_ref[...])
    o = jnp.concatenate([oa_ref[...], ob, oc_ref[...]], axis=-1)
    on = o * lax.rsqrt(_group_sumsq(o, bd_ref) * (1.0 / OUT_NORM_GROUP) + EPS) * og_ref[...]
    x1 = x_ref[...] + _dot(on.astype(BF16), wout_ref[...])
    x1_ref[...] = x1
    ms = jnp.mean(x1 * x1, axis=-1, keepdims=True)
    hb = (x1 * lax.rsqrt(ms + EPS) * fg_ref[...]).astype(BF16)
    h_ref[...] = hb
    lt = _dot(hb, wr_ref[...]).T + br_ref[...]
    slabs = _router_gates(lt)
    pad = jnp.zeros((LANES - N_EXPERT_GROUPS * ROUTER_SLAB, tm), F32)
    gate_ref[...] = jnp.concatenate(slabs + [pad], axis=0).T


def _out_call(oa, y0, y1, u0, u1, oc, x, d, gluw, glub, og, wout, fg, wr, br, bd, tm):
    n = x.shape[0]
    row = lambda c: pl.BlockSpec((tm, c), lambda i: (i, 0))
    full = lambda a: pl.BlockSpec(a.shape, lambda i: (0,) * a.ndim)
    return pl.pallas_call(
        _out_kernel,
        grid=(n // tm,),
        in_specs=[row(SGU_WIDTH), row(LANES), row(LANES), row(LANES), row(LANES), row(ATTN_WIDTH), row(D_MODEL),
                  full(d), full(gluw),
                  full(glub), full(og), full(wout), full(fg), full(wr), full(br), full(bd)],
        out_specs=[row(D_MODEL), row(D_MODEL), row(LANES)],
        out_shape=[jax.ShapeDtypeStruct((n, D_MODEL), F32), jax.ShapeDtypeStruct((n, D_MODEL), BF16),
                   jax.ShapeDtypeStruct((n, LANES), F32)],
        compiler_params=pltpu.CompilerParams(dimension_semantics=("parallel",), vmem_limit_bytes=VMEM_LIMIT_BYTES),
        name="out_proj",
    )(oa, y0, y1, u0, u1, oc, x, d, gluw, glub, og, wout, fg, wr, br, bd)


def _router_tables(wg, bg, we, be):
    w = jnp.zeros((D_MODEL, LANES), F32)
    b = jnp.full((LANES,), NEG_INF, F32)
    w = w.at[:, 0:N_EXPERT_GROUPS].set(wg.astype(F32))
    b = b.at[0:N_EXPERT_GROUPS].set(bg.astype(F32))
    for g in range(N_EXPERT_GROUPS):
        c0 = (g + 1) * ROUTER_SLAB
        w = w.at[:, c0:c0 + EXPERTS_PER_GROUP].set(we[g].astype(F32))
        b = b.at[c0:c0 + EXPERTS_PER_GROUP].set(be[g].astype(F32))
    return w.astype(BF16), b[:, None]


def _moe_kernel(h_ref, x_ref, gate_ref, wg_ref, wu_ref, wd_ref, o_ref, acc_ref):
    e = pl.program_id(1)

    @pl.when(e == 0)
    def _():
        acc_ref[...] = x_ref[...]

    h = h_ref[...]
    g = _dot(h, wg_ref[0])
    u = _dot(h, wu_ref[0])
    gate_lane = (e // EXPERTS_PER_GROUP) * ROUTER_SLAB + e % EXPERTS_PER_GROUP
    lane = lax.broadcasted_iota(jnp.int32, (1, LANES), 1)
    gate = jnp.sum(jnp.where(lane == gate_lane, gate_ref[...], 0.0), axis=-1, keepdims=True)
    a = (g * jax.nn.sigmoid(g)) * u * gate
    acc_ref[...] += _dot(a.astype(BF16), wd_ref[0])

    @pl.when(e == N_EXPERTS - 1)
    def _():
        o_ref[...] = acc_ref[...]


def _moe_call(h, x1, gates, wg, wu, wd, tm):
    n = h.shape[0]
    row = lambda c: pl.BlockSpec((tm, c), lambda i, e: (i, 0))
    return pl.pallas_call(
        _moe_kernel,
        grid=(n // tm, N_EXPERTS),
        in_specs=[row(D_MODEL), row(D_MODEL), row(LANES),
                  pl.BlockSpec((1, D_MODEL, D_EXPERT), lambda i, e: (e, 0, 0)),
                  pl.BlockSpec((1, D_MODEL, D_EXPERT), lambda i, e: (e, 0, 0)),
                  pl.BlockSpec((1, D_EXPERT, D_MODEL), lambda i, e: (e, 0, 0))],
        out_specs=row(D_MODEL),
        out_shape=jax.ShapeDtypeStruct((n, D_MODEL), F32),
        scratch_shapes=[pltpu.VMEM((tm, D_MODEL), F32)],
        compiler_params=pltpu.CompilerParams(dimension_semantics=("parallel", "arbitrary"),
                                             vmem_limit_bytes=VMEM_LIMIT_BYTES),
        name="moe_experts",
    )(h, x1, gates, wg, wu, wd)


def kernel(x, norm_mix, w_in, sgu_norm, sgu_w, sgu_b, s5_lambda_re, s5_lambda_im, s5_log_dt, s5_b_re, s5_b_im,
           s5_c_re, s5_c_im, s5_d, s5_glu_w, s5_glu_b, q_norm, k_norm, rel_bias, out_norm, w_out, norm_ffn,
           router_group_w, router_group_b, router_expert_w, router_expert_b, w_gate, w_up, w_down):
    batch, seq, _ = x.shape
    n = batch * seq
    depth = w_in.shape[0]
    t = _tiles(n, seq)
    assert seq % ATTN_TQ == 0 and seq % (S5_T * t["s5_rows"]) == 0
    row_vec = lambda a: a.astype(F32)[None, :]

    lane_group = jnp.arange(LANES) // HEAD_DIM
    bd = (lane_group[:, None] == lane_group[None, :]).astype(BF16)
    block_chunk = jnp.arange(SGU_BLOCK) // CHUNK
    sgu_mask = block_chunk[None, :] <= block_chunk[:, None]

    xf = x.reshape(n, D_MODEL)
    for l in range(depth):
        sguw = jnp.where(sgu_mask[None], sgu_w[l], 0).astype(BF16)
        sgub = jnp.repeat(sgu_b[l].astype(F32).T, HEAD_DIM, axis=1)
        qg = row_vec(jnp.tile(q_norm[l], ATTN_HEADS)) * (HEAD_DIM ** -0.5)
        kg = row_vec(jnp.tile(k_norm[l], ATTN_HEADS))
        oa, u0, u1, q, k, v = _in_call(xf, row_vec(norm_mix[l]), w_in[l].astype(BF16), row_vec(sgu_norm[l]), sguw, sgub,
                                      qg, kg, bd, t["tm_in"])

        wv, krev, wc, apow = _s5_tables(s5_lambda_re[l], s5_lambda_im[l], s5_log_dt[l], s5_b_re[l], s5_b_im[l],
                                        s5_c_re[l], s5_c_im[l], t["s5_rows"])
        y0, y1 = _s5_call(u0, u1, wv, krev, wc, apow, batch, t["s5_rows"])

        oc = _attn_call(q, k, v, _attn_bias(rel_bias[l]), batch, seq)

        wr, br = _router_tables(router_group_w[l], router_group_b[l], router_expert_w[l], router_expert_b[l])
        x1, h, gates = _out_call(oa, y0, y1, u0, u1, oc, xf, row_vec(s5_d[l]), s5_glu_w[l].astype(BF16), row_vec(s5_glu_b[l]),
                                 row_vec(out_norm[l]), w_out[l].astype(BF16), row_vec(norm_ffn[l]), wr, br, bd,
                                 t["tm_out"])

        xf = _moe_call(h, x1, gates, w_gate[l].astype(BF16), w_up[l].astype(BF16), w_down[l].astype(BF16),
                       t["tm_moe"])
    return xf.reshape(batch, seq, D_MODEL)
```

```python
import functools

import jax
import jax.numpy as jnp
from jax import lax
from jax.experimental import pallas as pl
from jax.experimental.pallas import tpu as pltpu

F32 = jnp.float32
BF16 = jnp.bfloat16

D_MODEL = 1024
CHUNK = 64
HEAD_DIM = 64
SGU_WIDTH = 256
SGU_HEADS = 4
SGU_BLOCK = 128
S5_WIDTH = 256
S5_GROUP = 16
S5_N_GROUPS = 16
S5_STATE = 64
ATTN_WIDTH = 512
ATTN_HEADS = 8
BAND_CHUNKS = 9
MAX_REL = 256
IN_COLS = 2 * SGU_WIDTH + S5_WIDTH + 3 * ATTN_WIDTH
OUT_NORM_GROUP = 64
N_EXPERT_GROUPS = 4
EXPERTS_PER_GROUP = 4
N_EXPERTS = 16
D_EXPERT = 256
EPS = 1e-6
NEG_INF = -1e30

LANES = 128
SUBLANES = 8
VMEM_LIMIT_BYTES = 56 * 1024 * 1024

S5_T = 16
S5_ROW = S5_T * S5_WIDTH
S5_NSTATE = S5_N_GROUPS * S5_STATE
ATTN_TQ = 256
ATTN_PREV = (BAND_CHUNKS - 1) * CHUNK
ATTN_TK = ATTN_TQ + ATTN_PREV
ROUTER_SLAB = SUBLANES
MOE_ROW = D_MODEL + LANES


def _tiles(n_tokens, seq):
    def pick(pref, total):
        t = min(pref, total)
        assert total % t == 0
        return t
    return dict(
        tm_in=pick(512, seq),
        tm_out=pick(512, seq),
        tm_moe=pick(512, n_tokens),
        tm_perm=pick(512, n_tokens),
        s5_rows=pick(128, seq // S5_T),
    )


def _dot(a, b):
    return jnp.dot(a, b, preferred_element_type=F32)


def _dot_nt(a, b):
    return lax.dot_general(a, b, (((1,), (1,)), ((), ())), preferred_element_type=F32)


def _group_sumsq(x, bd_ref):
    x2 = (x * x).astype(BF16)
    parts = [_dot(x2[:, t * LANES:(t + 1) * LANES], bd_ref[...]) for t in range(x.shape[1] // LANES)]
    return jnp.concatenate(parts, axis=-1)


def _in_kernel(x_ref, g_ref, w_ref, sgug_ref, sguw_ref, sgub_ref, qg_ref, kg_ref, bd_ref,
               oa_ref, u0_ref, u1_ref, q_ref, k_ref, v_ref):
    tm = x_ref.shape[0]
    x = x_ref[...]
    ms = jnp.mean(x * x, axis=-1, keepdims=True)
    hn = (x * lax.rsqrt(ms + EPS) * g_ref[...]).astype(BF16)

    z = jax.nn.gelu(_dot(hn, w_ref[:, 0:2 * SGU_WIDTH]))
    u = z[:, :SGU_WIDTH]
    v = z[:, SGU_WIDTH:]
    v = v * lax.rsqrt(jnp.mean(v * v, axis=-1, keepdims=True) + EPS) * sgug_ref[...]
    vb = v.astype(BF16)
    first_head = lax.broadcasted_iota(jnp.int32, (SGU_BLOCK, LANES), 1) < HEAD_DIM
    for r in range(tm // SGU_BLOCK):
        rows = slice(r * SGU_BLOCK, (r + 1) * SGU_BLOCK)
        for p in range(SGU_WIDTH // LANES):
            cols = slice(p * LANES, (p + 1) * LANES)
            vp = vb[rows, cols]
            mixed = jnp.where(first_head, _dot(sguw_ref[2 * p], vp), _dot(sguw_ref[2 * p + 1], vp))
            oa_ref[rows, cols] = u[rows, cols] * (mixed + sgub_ref[:, cols])

    c0 = 2 * SGU_WIDTH
    us = _dot(hn, w_ref[:, c0:c0 + S5_WIDTH])
    u0_ref[...] = us[:, :LANES]
    u1_ref[...] = us[:, LANES:]

    c0 += S5_WIDTH
    q = _dot(hn, w_ref[:, c0:c0 + ATTN_WIDTH])
    q_ref[...] = (q * lax.rsqrt(_group_sumsq(q, bd_ref) * (1.0 / HEAD_DIM) + EPS) * qg_ref[...]).astype(BF16)
    c0 += ATTN_WIDTH
    k = _dot(hn, w_ref[:, c0:c0 + ATTN_WIDTH])
    k_ref[...] = (k * lax.rsqrt(_group_sumsq(k, bd_ref) * (1.0 / HEAD_DIM) + EPS) * kg_ref[...]).astype(BF16)
    c0 += ATTN_WIDTH
    v_ref[...] = _dot(hn, w_ref[:, c0:c0 + ATTN_WIDTH]).astype(BF16)


def _in_call(x, g, w, sgug, sguw, sgub, qg, kg, bd, tm):
    n = x.shape[0]
    row = lambda c: pl.BlockSpec((tm, c), lambda i: (i, 0))
    full = lambda a: pl.BlockSpec(a.shape, lambda i: (0,) * a.ndim)
    return pl.pallas_call(
        _in_kernel,
        grid=(n // tm,),
        in_specs=[row(D_MODEL), full(g), full(w), full(sgug), full(sguw), full(sgub), full(qg), full(kg), full(bd)],
        out_specs=[row(SGU_WIDTH), row(LANES), row(LANES), row(ATTN_WIDTH), row(ATTN_WIDTH), row(ATTN_WIDTH)],
        out_shape=[jax.ShapeDtypeStruct((n, SGU_WIDTH), F32), jax.ShapeDtypeStruct((n, LANES), F32),
                   jax.ShapeDtypeStruct((n, LANES), F32), jax.ShapeDtypeStruct((n, ATTN_WIDTH), BF16),
                   jax.ShapeDtypeStruct((n, ATTN_WIDTH), BF16), jax.ShapeDtypeStruct((n, ATTN_WIDTH), BF16)],
        compiler_params=pltpu.CompilerParams(dimension_semantics=("parallel",), vmem_limit_bytes=VMEM_LIMIT_BYTES),
        name="in_proj",
    )(x, g, w, sgug, sguw, sgub, qg, kg, bd)


def _s5_kernel(u0_ref, u1_ref, wv_ref, krev_ref, wc_ref, apow_ref, y0_ref, y1_ref, ur_ref, s_ref, sp_ref, carry_ref):
    rows = ur_ref.shape[0]
    n_steps = apow_ref.shape[0]

    @pl.when(pl.program_id(1) == 0)
    def _():
        carry_ref[...] = jnp.zeros_like(carry_ref)

    for t in range(S5_T):
        ur_ref[:, t * S5_WIDTH:t * S5_WIDTH + LANES] = u0_ref[pl.ds(t, rows, stride=S5_T), :].astype(BF16)
        ur_ref[:, t * S5_WIDTH + LANES:(t + 1) * S5_WIDTH] = u1_ref[pl.ds(t, rows, stride=S5_T), :].astype(BF16)

    s_ref[...] = _dot(ur_ref[...], wv_ref[...])
    row_id = lax.broadcasted_iota(jnp.int32, (rows, LANES), 0)
    for cb in range(S5_NSTATE // LANES):
        cre = slice(cb * LANES, (cb + 1) * LANES)
        cim = slice(S5_NSTATE + cb * LANES, S5_NSTATE + (cb + 1) * LANES)
        re = s_ref[:, cre]
        im = s_ref[:, cim]
        c_re = carry_ref[0:1, cre]
        c_im = carry_ref[0:1, cim]
        a_re = apow_ref[0, 0:1, cre]
        a_im = apow_ref[0, 1:2, cre]
        re = re + jnp.where(row_id == 0, a_re * c_re - a_im * c_im, 0.0)
        im = im + jnp.where(row_id == 0, a_re * c_im + a_im * c_re, 0.0)
        for k in range(n_steps):
            shift = 1 << k
            a_re = apow_ref[k, 0:1, cre]
            a_im = apow_ref[k, 1:2, cre]
            re_s = jnp.where(row_id >= shift, pltpu.roll(re, shift, 0), 0.0)
            im_s = jnp.where(row_id >= shift, pltpu.roll(im, shift, 0), 0.0)
            re, im = re + a_re * re_s - a_im * im_s, im + a_re * im_s + a_im * re_s
        sp_ref[:, cre] = jnp.where(row_id >= 1, pltpu.roll(re, 1, 0), c_re).astype(BF16)
        sp_ref[:, cim] = jnp.where(row_id >= 1, pltpu.roll(im, 1, 0), c_im).astype(BF16)
        carry_ref[0:1, cre] = re[rows - 1:rows, :]
        carry_ref[0:1, cim] = im[rows - 1:rows, :]

    for i in range(S5_T):
        cols = slice(i * S5_WIDTH, (i + 1) * S5_WIDTH)
        intra = _dot(ur_ref[:, 0:(i + 1) * S5_WIDTH], krev_ref[(S5_T - 1 - i) * S5_WIDTH:, :])
        y = intra + _dot(sp_ref[...], wc_ref[:, cols])
        y0_ref[pl.ds(i, rows, stride=S5_T), :] = y[:, :LANES]
        y1_ref[pl.ds(i, rows, stride=S5_T), :] = y[:, LANES:]


def _s5_call(u0, u1, wv, krev, wc, apow, batch, rows):
    n = u0.shape[0]
    tok = rows * S5_T
    tiles_per_seq = n // batch // tok
    const = lambda a: pl.BlockSpec(a.shape, lambda b, t: (0,) * a.ndim, pipeline_mode=pl.Buffered(1))
    blk = pl.BlockSpec((tok, LANES), lambda b, t: (b * tiles_per_seq + t, 0))
    half = jax.ShapeDtypeStruct((n, LANES), F32)
    return pl.pallas_call(
        _s5_kernel,
        grid=(batch, tiles_per_seq),
        in_specs=[blk, blk, const(wv), const(krev), const(wc), const(apow)],
        out_specs=[blk, blk],
        out_shape=[half, half],
        scratch_shapes=[pltpu.VMEM((rows, S5_ROW), BF16), pltpu.VMEM((rows, 2 * S5_NSTATE), F32),
                        pltpu.VMEM((rows, 2 * S5_NSTATE), BF16), pltpu.VMEM((SUBLANES, 2 * S5_NSTATE), F32)],
        compiler_params=pltpu.CompilerParams(dimension_semantics=("arbitrary", "arbitrary"),
                                             vmem_limit_bytes=VMEM_LIMIT_BYTES),
        name="s5_scan",
    )(u0, u1, wv, krev, wc, apow)


def _s5_tables(lam_re, lam_im, log_dt, b_re, b_im, c_re, c_im, rows):
    g_, p_, h_ = S5_N_GROUPS, S5_STATE, S5_GROUP
    hi = lax.Precision.HIGHEST
    lam_re, lam_im = lam_re.astype(F32), lam_im.astype(F32)
    dt = jnp.exp(log_dt.astype(F32))[:, None]
    tau = jnp.arange(S5_T + 1, dtype=F32)[:, None, None]
    mag = jnp.exp(tau * (lam_re * dt)[None])
    ap_re = mag * jnp.cos(tau * (lam_im * dt)[None])
    ap_im = mag * jnp.sin(tau * (lam_im * dt)[None])
    n_re, n_im = ap_re[1] - 1.0, ap_im[1]
    den = lam_re * lam_re + lam_im * lam_im
    k_re = ((n_re * lam_re + n_im * lam_im) / den)[..., None]
    k_im = ((n_im * lam_re - n_re * lam_im) / den)[..., None]
    b_re, b_im = b_re.astype(F32), b_im.astype(F32)
    bb_re = k_re * b_re - k_im * b_im
    bb_im = k_re * b_im + k_im * b_re
    c_re, c_im = c_re.astype(F32), c_im.astype(F32)
    ab_re = ap_re[:S5_T, :, :, None] * bb_re[None] - ap_im[:S5_T, :, :, None] * bb_im[None]
    ab_im = ap_re[:S5_T, :, :, None] * bb_im[None] + ap_im[:S5_T, :, :, None] * bb_re[None]

    col_group = jnp.arange(S5_WIDTH) // h_
    state_group = jnp.arange(S5_NSTATE) // p_
    chan_group = jnp.arange(S5_ROW) // h_ % g_

    k_tau = (jnp.einsum('gop,tgpi->tgio', c_re, ab_re, precision=hi)
             - jnp.einsum('gop,tgpi->tgio', c_im, ab_im, precision=hi))
    krev = jnp.tile(k_tau[::-1].reshape(S5_ROW, h_), (1, g_))
    krev = jnp.where(chan_group[:, None] == col_group[None, :], krev, 0.0)

    def wv_half(ab):
        w = jnp.tile(ab[::-1].transpose(0, 1, 3, 2).reshape(S5_ROW, p_), (1, g_))
        return jnp.where(chan_group[:, None] == state_group[None, :], w, 0.0)
    wv = jnp.concatenate([wv_half(ab_re), wv_half(ab_im)], axis=1)

    cp = lambda c: c.transpose(0, 2, 1)[:, :, None, :]
    w_re = cp(c_re) * ap_re[1:].transpose(1, 2, 0)[..., None] - cp(c_im) * ap_im[1:].transpose(1, 2, 0)[..., None]
    w_im = cp(c_re) * ap_im[1:].transpose(1, 2, 0)[..., None] + cp(c_im) * ap_re[1:].transpose(1, 2, 0)[..., None]
    def wc_half(w):
        w = jnp.broadcast_to(w.reshape(S5_NSTATE, S5_T, 1, h_), (S5_NSTATE, S5_T, g_, h_)).reshape(S5_NSTATE, S5_ROW)
        return jnp.where(state_group[:, None] == chan_group[None, :], w, 0.0)
    wc = jnp.concatenate([wc_half(w_re), wc_half(-w_im)], axis=0)

    n_steps = max(1, (rows - 1).bit_length())
    pows = [(ap_re[S5_T].reshape(S5_NSTATE), ap_im[S5_T].reshape(S5_NSTATE))]
    for _ in range(n_steps - 1):
        r, i = pows[-1]
        pows.append((r * r - i * i, 2.0 * r * i))
    apow = jnp.stack([jnp.stack(p) for p in pows])
    return wv.astype(BF16), krev.astype(BF16), wc.astype(BF16), apow


def _attn_kernel(q_ref, k0_ref, k1_ref, k2_ref, v0_ref, v1_ref, v2_ref, bias_ref, o_ref):
    lane = lax.broadcasted_iota(jnp.int32, (1, LANES), 1)
    for p in range(ATTN_WIDTH // LANES):
        cols = slice(p * LANES, (p + 1) * LANES)
        qp = q_ref[:, cols]
        kcat = jnp.concatenate([k0_ref[:, cols], k1_ref[:, cols], k2_ref[:, cols]], axis=0)
        vcat = jnp.concatenate([v0_ref[:, cols], v1_ref[:, cols], v2_ref[:, cols]], axis=0)
        acc = jnp.zeros((ATTN_TQ, LANES), F32)
        for hh in range(LANES // HEAD_DIM):
            in_head = (lane >= hh * HEAD_DIM) & (lane < (hh + 1) * HEAD_DIM)
            qm = jnp.where(in_head, qp, jnp.zeros_like(qp))
            s = _dot_nt(qm, kcat) + bias_ref[0, 2 * p + hh]
            m = jnp.max(s, axis=-1, keepdims=True)
            e = jnp.exp(s - m)
            denom = jnp.sum(e, axis=-1, keepdims=True)
            o = _dot(e.astype(BF16), vcat)
            acc = acc + jnp.where(in_head, o / denom, 0.0)
        o_ref[:, cols] = acc


def _attn_call(q, k, v, bias, batch, seq):
    n = q.shape[0]
    tiles = seq // ATTN_TQ
    n_prev = ATTN_PREV // ATTN_TQ
    assert n_prev == 2 and bias.shape[0] == n_prev + 1
    cur = pl.BlockSpec((ATTN_TQ, ATTN_WIDTH), lambda b, t: (b * tiles + t, 0))
    prev = lambda d: pl.BlockSpec((ATTN_TQ, ATTN_WIDTH), lambda b, t: (b * tiles + jnp.maximum(t - d, 0), 0))
    bias_spec = pl.BlockSpec((1, ATTN_HEADS, ATTN_TQ, ATTN_TK), lambda b, t: (jnp.minimum(t, n_prev), 0, 0, 0))
    return pl.pallas_call(
        _attn_kernel,
        grid=(batch, tiles),
        in_specs=[cur, prev(2), prev(1), cur, prev(2), prev(1), cur, bias_spec],
        out_specs=cur,
        out_shape=jax.ShapeDtypeStruct((n, ATTN_WIDTH), F32),
        compiler_params=pltpu.CompilerParams(dimension_semantics=("parallel", "arbitrary"),
                                             vmem_limit_bytes=VMEM_LIMIT_BYTES),
        name="band_attn",
    )(q, k, k, k, v, v, v, bias)


def _attn_bias(rel_bias):
    n_diag = ATTN_TQ + ATTN_TK - 1
    d = (ATTN_TQ - 1 + ATTN_PREV) - jnp.arange(n_diag)
    vec = jnp.take(rel_bias.astype(F32), jnp.clip(d, -MAX_REL, MAX_REL) + MAX_REL, axis=1)
    skew = jnp.tile(jnp.pad(vec, ((0, 0), (0, 1))), (1, ATTN_TQ))[:, :ATTN_TQ * n_diag]
    skew = skew.reshape(ATTN_HEADS, ATTN_TQ, n_diag)
    base = skew[:, :, ATTN_TQ - 1:ATTN_TQ - 1 + ATTN_TK]
    qi = jnp.arange(ATTN_TQ)
    kj = jnp.arange(ATTN_TK)
    q_chunk = qi[:, None] // CHUNK + (BAND_CHUNKS - 1)
    k_chunk = kj[None, :] // CHUNK
    in_band = (k_chunk <= q_chunk) & (k_chunk >= q_chunk - (BAND_CHUNKS - 1))
    tabs = []
    for t in range(ATTN_PREV // ATTN_TQ + 1):
        exists = kj[None, :] >= ATTN_PREV - t * ATTN_TQ
        tabs.append(jnp.where((in_band & exists)[None], base, NEG_INF))
    return jnp.stack(tabs)


def _router_gates(lt):
    tm = lt.shape[1]
    row = lax.broadcasted_iota(jnp.int32, (ROUTER_SLAB, tm), 0)
    gl = lt[0:ROUTER_SLAB]
    gmax = jnp.max(gl, axis=0, keepdims=True)
    p_g = 1.0 / jnp.sum(jnp.exp(gl - gmax), axis=0, keepdims=True)
    g_top = jnp.min(jnp.where(gl == gmax, row, ROUTER_SLAB), axis=0, keepdims=True)
    el = jnp.zeros((ROUTER_SLAB, tm), F32)
    for g in range(N_EXPERT_GROUPS):
        el = el + jnp.where(g_top == g, lt[(g + 1) * ROUTER_SLAB:(g + 2) * ROUTER_SLAB], 0.0)
    ee = jnp.exp(el - jnp.max(el, axis=0, keepdims=True))
    ep = ee / jnp.sum(ee, axis=0, keepdims=True)
    p1 = jnp.max(ep, axis=0, keepdims=True)
    i1 = jnp.min(jnp.where(ep == p1, row, ROUTER_SLAB), axis=0, keepdims=True)
    rest = jnp.where(row == i1, -1.0, ep)
    p2 = jnp.max(rest, axis=0, keepdims=True)
    i2 = jnp.min(jnp.where(rest == p2, row, ROUTER_SLAB), axis=0, keepdims=True)
    tot = p1 + p2
    w = (jnp.where(row == i1, p1 / tot, 0.0) + jnp.where(row == i2, p2 / tot, 0.0)) * p_g
    return [jnp.where(g_top == g, w, 0.0) for g in range(N_EXPERT_GROUPS)], g_top


def _out_kernel(oa_ref, y0_ref, y1_ref, u0_ref, u1_ref, oc_ref, x_ref, d_ref, gluw_ref, glub_ref, og_ref, wout_ref,
                fg_ref, wr_ref, br_ref, bd_ref, x1_ref, hg_ref, route_ref, cnt_ref, run_ref):
    tm = x_ref.shape[0]

    @pl.when(pl.program_id(0) == 0)
    def _():
        run_ref[...] = jnp.zeros_like(run_ref)

    y = jnp.concatenate([y0_ref[...], y1_ref[...]], axis=-1)
    u = jnp.concatenate([u0_ref[...], u1_ref[...]], axis=-1)
    y = jax.nn.gelu(y + d_ref[...] * u)
    ob = y * jax.nn.sigmoid(_dot(y.astype(BF16), gluw_ref[...]) + glub_ref[...])
    o = jnp.concatenate([oa_ref[...], ob, oc_ref[...]], axis=-1)
    on = o * lax.rsqrt(_group_sumsq(o, bd_ref) * (1.0 / OUT_NORM_GROUP) + EPS) * og_ref[...]
    x1 = x_ref[...] + _dot(on.astype(BF16), wout_ref[...])
    x1_ref[...] = x1
    ms = jnp.mean(x1 * x1, axis=-1, keepdims=True)
    hn = x1 * lax.rsqrt(ms + EPS) * fg_ref[...]
    hg_ref[:, :D_MODEL] = hn
    lt = _dot(hn.astype(BF16), wr_ref[...]).T + br_ref[...]
    slabs, g_top = _router_gates(lt)
    pad = jnp.zeros((LANES - N_EXPERT_GROUPS * ROUTER_SLAB, tm), F32)
    hg_ref[:, D_MODEL:] = jnp.concatenate(slabs + [pad], axis=0).T

    row = lax.broadcasted_iota(jnp.int32, (ROUTER_SLAB, tm), 0)
    lane = lax.broadcasted_iota(jnp.int32, (ROUTER_SLAB, tm), 1)
    member = jnp.where(row == g_top, 1.0, 0.0)
    cum = member
    shift = 1
    while shift < tm:
        cum = cum + jnp.where(lane >= shift, pltpu.roll(cum, shift, 1), 0.0)
        shift *= 2
    run = run_ref[:, 0:1]
    rank = jnp.sum(member * (cum - 1.0 + run), axis=0, keepdims=True)
    total = run + cum[:, tm - 1:tm]
    run_ref[...] = jnp.broadcast_to(total, run_ref.shape)
    cnt_ref[...] = jnp.broadcast_to(total, cnt_ref.shape)
    route_ref[...] = jnp.where(row == 0, g_top, jnp.where(row == 1, rank.astype(jnp.int32), 0))


def _out_call(oa, y0, y1, u0, u1, oc, x, d, gluw, glub, og, wout, fg, wr, br, bd, tm):
    n = x.shape[0]
    row = lambda c: pl.BlockSpec((tm, c), lambda i: (i, 0))
    full = lambda a: pl.BlockSpec(a.shape, lambda i: (0,) * a.ndim)
    return pl.pallas_call(
        _out_kernel,
        grid=(n // tm,),
        in_specs=[row(SGU_WIDTH), row(LANES), row(LANES), row(LANES), row(LANES), row(ATTN_WIDTH), row(D_MODEL),
                  full(d), full(gluw),
                  full(glub), full(og), full(wout), full(fg), full(wr), full(br), full(bd)],
        out_specs=[row(D_MODEL), row(MOE_ROW), pl.BlockSpec((ROUTER_SLAB, tm), lambda i: (i, 0)),
                   pl.BlockSpec((ROUTER_SLAB, LANES), lambda i: (0, 0))],
        out_shape=[jax.ShapeDtypeStruct((n, D_MODEL), F32), jax.ShapeDtypeStruct((n, MOE_ROW), F32),
                   jax.ShapeDtypeStruct((n // tm * ROUTER_SLAB, tm), jnp.int32),
                   jax.ShapeDtypeStruct((ROUTER_SLAB, LANES), F32)],
        scratch_shapes=[pltpu.VMEM((ROUTER_SLAB, LANES), F32)],
        compiler_params=pltpu.CompilerParams(dimension_semantics=("arbitrary",), vmem_limit_bytes=VMEM_LIMIT_BYTES),
        name="out_proj",
    )(oa, y0, y1, u0, u1, oc, x, d, gluw, glub, og, wout, fg, wr, br, bd)


def _router_tables(wg, bg, we, be):
    w = jnp.zeros((D_MODEL, LANES), F32)
    b = jnp.full((LANES,), NEG_INF, F32)
    w = w.at[:, 0:N_EXPERT_GROUPS].set(wg.astype(F32))
    b = b.at[0:N_EXPERT_GROUPS].set(bg.astype(F32))
    for g in range(N_EXPERT_GROUPS):
        c0 = (g + 1) * ROUTER_SLAB
        w = w.at[:, c0:c0 + EXPERTS_PER_GROUP].set(we[g].astype(F32))
        b = b.at[c0:c0 + EXPERTS_PER_GROUP].set(be[g].astype(F32))
    return w.astype(BF16), b[:, None]


def _route_plan(route, counts, n, tm_out, tm_moe):
    r3 = route.reshape(n // tm_out, ROUTER_SLAB, tm_out)
    group = r3[:, 0, :].reshape(n)
    rank = r3[:, 1, :].reshape(n)
    cnt = counts[:N_EXPERT_GROUPS, 0].astype(jnp.int32)
    tiles_g = (cnt + tm_moe - 1) // tm_moe
    tile_end = jnp.cumsum(tiles_g)
    row_start = (tile_end - tiles_g) * tm_moe
    pos = rank
    for g in range(N_EXPERT_GROUPS):
        pos = pos + jnp.where(group == g, row_start[g], 0)
    n_tiles = n // tm_moe + N_EXPERT_GROUPS - 1
    tile_group = jnp.sum(jnp.arange(n_tiles)[:, None] >= tile_end[None, :], axis=1)
    meta = jnp.concatenate([jnp.minimum(tile_group, N_EXPERT_GROUPS - 1), tile_end[-1:]]).astype(jnp.int32)
    return pos.astype(jnp.int32), meta, n_tiles


def _dispatch_kernel(pos_ref, hg_ref, hs_in_ref, hs_ref, sem):
    del hs_in_ref
    td = hg_ref.shape[0]

    def row_copy(r):
        return pltpu.make_async_copy(hg_ref.at[pl.ds(r, 1)], hs_ref.at[pl.ds(pos_ref[r], 1)], sem)

    def start(r, c):
        row_copy(r).start()
        return c

    def wait(r, c):
        row_copy(r).wait()
        return c

    lax.fori_loop(0, td, start, 0, unroll=8)
    lax.fori_loop(0, td, wait, 0, unroll=8)


def _dispatch_call(pos, hg, n_rows, td):
    n = hg.shape[0]
    hs0 = jnp.zeros((n_rows, MOE_ROW), F32)
    return pl.pallas_call(
        _dispatch_kernel,
        grid=(n // td,),
        in_specs=[pl.BlockSpec((td,), lambda i: (i,), memory_space=pltpu.SMEM),
                  pl.BlockSpec((td, MOE_ROW), lambda i: (i, 0)),
                  pl.BlockSpec(memory_space=pl.ANY)],
        out_specs=pl.BlockSpec(memory_space=pl.ANY),
        out_shape=jax.ShapeDtypeStruct((n_rows, MOE_ROW), F32),
        scratch_shapes=[pltpu.SemaphoreType.DMA],
        input_output_aliases={2: 0},
        compiler_params=pltpu.CompilerParams(dimension_semantics=("arbitrary",), vmem_limit_bytes=VMEM_LIMIT_BYTES),
        name="moe_dispatch",
    )(pos, hg, hs0)


def _moe_group_kernel(meta_ref, hs_ref, wgu_ref, wd_ref, o_ref):
    j = pl.program_id(0)
    n_used = meta_ref[pl.num_programs(0)]
    width = EXPERTS_PER_GROUP * D_EXPERT

    @pl.when(j < n_used)
    def _():
        tm = hs_ref.shape[0]
        h = hs_ref[:, :D_MODEL].astype(BF16)
        gates = hs_ref[:, D_MODEL:]
        gu = _dot(h, wgu_ref[0])
        gt, up = gu[:, :width], gu[:, width:]
        lane = lax.broadcasted_iota(jnp.int32, (1, LANES), 1)
        first_lane = meta_ref[j] * ROUTER_SLAB
        cols = [jnp.sum(jnp.where(lane == first_lane + e, gates, 0.0), axis=-1, keepdims=True)
                for e in range(EXPERTS_PER_GROUP)]
        gate = jnp.concatenate([jnp.broadcast_to(c, (tm, D_EXPERT)) for c in cols], axis=-1)
        a = (gt * jax.nn.sigmoid(gt)) * up * gate
        o_ref[...] = _dot(a.astype(BF16), wd_ref[0])

    @pl.when(j >= n_used)
    def _():
        o_ref[...] = jnp.zeros_like(o_ref)


def _moe_group_call(meta, hs, wgu, wd, n_tiles, tm):
    width = EXPERTS_PER_GROUP * D_EXPERT
    return pl.pallas_call(
        _moe_group_kernel,
        grid_spec=pltpu.PrefetchScalarGridSpec(
            num_scalar_prefetch=1,
            grid=(n_tiles,),
            in_specs=[pl.BlockSpec((tm, MOE_ROW), lambda j, meta: (j, 0)),
                      pl.BlockSpec((1, D_MODEL, 2 * width), lambda j, meta: (meta[j], 0, 0)),
                      pl.BlockSpec((1, width, D_MODEL), lambda j, meta: (meta[j], 0, 0))],
            out_specs=pl.BlockSpec((tm, D_MODEL), lambda j, meta: (j, 0)),
        ),
        out_shape=jax.ShapeDtypeStruct((n_tiles * tm, D_MODEL), F32),
        compiler_params=pltpu.CompilerParams(dimension_semantics=("arbitrary",), vmem_limit_bytes=VMEM_LIMIT_BYTES),
        name="moe_experts",
    )(meta, hs, wgu, wd)


def _combine_kernel(pos_ref, posn_ref, x1_ref, src_ref, o_ref, buf_ref, sem):
    i = pl.program_id(0)
    tc = x1_ref.shape[0]
    slot = i % 2

    def row_copy(p_ref, s, r):
        return pltpu.make_async_copy(src_ref.at[pl.ds(p_ref[r], 1)], buf_ref.at[s, pl.ds(r, 1)], sem.at[s])

    def gather(p_ref, s):
        def start(r, c):
            row_copy(p_ref, s, r).start()
            return c
        lax.fori_loop(0, tc, start, 0, unroll=8)

    @pl.when(i == 0)
    def _():
        gather(pos_ref, 0)

    @pl.when(i + 1 < pl.num_programs(0))
    def _():
        gather(posn_ref, 1 - slot)

    def wait(r, c):
        row_copy(pos_ref, slot, r).wait()
        return c
    lax.fori_loop(0, tc, wait, 0, unroll=8)
    o_ref[...] = x1_ref[...] + buf_ref[slot]


def _combine_call(pos, x1, src, tc):
    n = x1.shape[0]
    last = n // tc - 1
    return pl.pallas_call(
        _combine_kernel,
        grid=(n // tc,),
        in_specs=[pl.BlockSpec((tc,), lambda i: (i,), memory_space=pltpu.SMEM),
                  pl.BlockSpec((tc,), lambda i: (jnp.minimum(i + 1, last),), memory_space=pltpu.SMEM),
                  pl.BlockSpec((tc, D_MODEL), lambda i: (i, 0)),
                  pl.BlockSpec(memory_space=pl.ANY)],
        out_specs=pl.BlockSpec((tc, D_MODEL), lambda i: (i, 0)),
        out_shape=jax.ShapeDtypeStruct((n, D_MODEL), F32),
        scratch_shapes=[pltpu.VMEM((2, tc, D_MODEL), F32), pltpu.SemaphoreType.DMA((2,))],
        compiler_params=pltpu.CompilerParams(dimension_semantics=("arbitrary",), vmem_limit_bytes=VMEM_LIMIT_BYTES),
        name="moe_combine",
    )(pos, pos, x1, src)


def kernel(x, norm_mix, w_in, sgu_norm, sgu_w, sgu_b, s5_lambda_re, s5_lambda_im, s5_log_dt, s5_b_re, s5_b_im,
           s5_c_re, s5_c_im, s5_d, s5_glu_w, s5_glu_b, q_norm, k_norm, rel_bias, out_norm, w_out, norm_ffn,
           router_group_w, router_group_b, router_expert_w, router_expert_b, w_gate, w_up, w_down):
    batch, seq, _ = x.shape
    n = batch * seq
    depth = w_in.shape[0]
    t = _tiles(n, seq)
    assert seq % ATTN_TQ == 0 and seq % (S5_T * t["s5_rows"]) == 0
    row_vec = lambda a: a.astype(F32)[None, :]

    lane_group = jnp.arange(LANES) // HEAD_DIM
    bd = (lane_group[:, None] == lane_group[None, :]).astype(BF16)
    block_chunk = jnp.arange(SGU_BLOCK) // CHUNK
    sgu_mask = block_chunk[None, :] <= block_chunk[:, None]

    xf = x.reshape(n, D_MODEL)
    for l in range(depth):
        sguw = jnp.where(sgu_mask[None], sgu_w[l], 0).astype(BF16)
        sgub = jnp.repeat(sgu_b[l].astype(F32).T, HEAD_DIM, axis=1)
        qg = row_vec(jnp.tile(q_norm[l], ATTN_HEADS)) * (HEAD_DIM ** -0.5)
        kg = row_vec(jnp.tile(k_norm[l], ATTN_HEADS))
        oa, u0, u1, q, k, v = _in_call(xf, row_vec(norm_mix[l]), w_in[l].astype(BF16), row_vec(sgu_norm[l]), sguw, sgub,
                                      qg, kg, bd, t["tm_in"])

        wv, krev, wc, apow = _s5_tables(s5_lambda_re[l], s5_lambda_im[l], s5_log_dt[l], s5_b_re[l], s5_b_im[l],
                                        s5_c_re[l], s5_c_im[l], t["s5_rows"])
        y0, y1 = _s5_call(u0, u1, wv, krev, wc, apow, batch, t["s5_rows"])

        oc = _attn_call(q, k, v, _attn_bias(rel_bias[l]), batch, seq)

        wr, br = _router_tables(router_group_w[l], router_group_b[l], router_expert_w[l], router_expert_b[l])
        x1, hg, route, counts = _out_call(oa, y0, y1, u0, u1, oc, xf, row_vec(s5_d[l]), s5_glu_w[l].astype(BF16), row_vec(s5_glu_b[l]),
                                 row_vec(out_norm[l]), w_out[l].astype(BF16), row_vec(norm_ffn[l]), wr, br, bd,
                                 t["tm_out"])

        pos, meta, n_tiles = _route_plan(route, counts, n, t["tm_out"], t["tm_moe"])
        hs = _dispatch_call(pos, hg, n_tiles * t["tm_moe"], t["tm_perm"])
        by_group = lambda w: (w.reshape(N_EXPERT_GROUPS, EXPERTS_PER_GROUP, D_MODEL, D_EXPERT).transpose(0, 2, 1, 3)
                              .reshape(N_EXPERT_GROUPS, D_MODEL, EXPERTS_PER_GROUP * D_EXPERT))
        wgu = jnp.concatenate([by_group(w_gate[l]), by_group(w_up[l])], axis=-1).astype(BF16)
        wd = w_down[l].reshape(N_EXPERT_GROUPS, EXPERTS_PER_GROUP * D_EXPERT, D_MODEL).astype(BF16)
        ys = _moe_group_call(meta, hs, wgu, wd, n_tiles, t["tm_moe"])
        xf = _combine_call(pos, x1, ys, t["tm_perm"])
    return xf.reshape(batch, seq, D_MODEL)
```

```python
import functools

import jax
import jax.numpy as jnp
from jax import lax
from jax.experimental import pallas as pl
from jax.experimental.pallas import tpu as pltpu

F32 = jnp.float32
BF16 = jnp.bfloat16

D_MODEL = 1024
CHUNK = 64
HEAD_DIM = 64
SGU_WIDTH = 256
SGU_HEADS = 4
SGU_BLOCK = 128
S5_WIDTH = 256
S5_GROUP = 16
S5_N_GROUPS = 16
S5_STATE = 64
ATTN_WIDTH = 512
ATTN_HEADS = 8
BAND_CHUNKS = 9
MAX_REL = 256
IN_COLS = 2 * SGU_WIDTH + S5_WIDTH + 3 * ATTN_WIDTH
OUT_NORM_GROUP = 64
N_EXPERT_GROUPS = 4
EXPERTS_PER_GROUP = 4
N_EXPERTS = 16
D_EXPERT = 256
EPS = 1e-6
NEG_INF = -1e30

LANES = 128
SUBLANES = 8
VMEM_LIMIT_BYTES = 56 * 1024 * 1024

S5_T = 16
S5_ROW = S5_T * S5_WIDTH
S5_NSTATE = S5_N_GROUPS * S5_STATE
ATTN_TQ = 256
ATTN_PREV = (BAND_CHUNKS - 1) * CHUNK
ATTN_TK = ATTN_TQ + ATTN_PREV
ROUTER_SLAB = SUBLANES
MOE_ROW = D_MODEL + LANES


def _tiles(n_tokens, seq):
    def pick(pref, total):
        t = min(pref, total)
        assert total % t == 0
        return t
    return dict(
        tm_in=pick(512, seq),
        tm_out=pick(512, seq),
        tm_moe=pick(512, n_tokens),
        tm_perm=pick(512, n_tokens),
        s5_rows=pick(128, seq // S5_T),
    )


def _dot(a, b):
    return jnp.dot(a, b, preferred_element_type=F32)


def _dot_nt(a, b):
    return lax.dot_general(a, b, (((1,), (1,)), ((), ())), preferred_element_type=F32)


def _group_sumsq(x, bd_ref):
    x2 = (x * x).astype(BF16)
    parts = [_dot(x2[:, t * LANES:(t + 1) * LANES], bd_ref[...]) for t in range(x.shape[1] // LANES)]
    return jnp.concatenate(parts, axis=-1)


def _in_kernel(x_ref, g_ref, w_ref, sgug_ref, sguw_ref, sgub_ref, qg_ref, kg_ref, bd_ref,
               oa_ref, u0_ref, u1_ref, q_ref, k_ref, v_ref):
    tm = x_ref.shape[0]
    x = x_ref[...]
    ms = jnp.mean(x * x, axis=-1, keepdims=True)
    hn = (x * lax.rsqrt(ms + EPS) * g_ref[...]).astype(BF16)

    z = jax.nn.gelu(_dot(hn, w_ref[:, 0:2 * SGU_WIDTH]))
    u = z[:, :SGU_WIDTH]
    v = z[:, SGU_WIDTH:]
    v = v * lax.rsqrt(jnp.mean(v * v, axis=-1, keepdims=True) + EPS) * sgug_ref[...]
    vb = v.astype(BF16)
    first_head = lax.broadcasted_iota(jnp.int32, (SGU_BLOCK, LANES), 1) < HEAD_DIM
    for r in range(tm // SGU_BLOCK):
        rows = slice(r * SGU_BLOCK, (r + 1) * SGU_BLOCK)
        for p in range(SGU_WIDTH // LANES):
            cols = slice(p * LANES, (p + 1) * LANES)
            vp = vb[rows, cols]
            mixed = jnp.where(first_head, _dot(sguw_ref[2 * p], vp), _dot(sguw_ref[2 * p + 1], vp))
            oa_ref[rows, cols] = u[rows, cols] * (mixed + sgub_ref[:, cols])

    c0 = 2 * SGU_WIDTH
    us = _dot(hn, w_ref[:, c0:c0 + S5_WIDTH])
    u0_ref[...] = us[:, :LANES]
    u1_ref[...] = us[:, LANES:]

    c0 += S5_WIDTH
    q = _dot(hn, w_ref[:, c0:c0 + ATTN_WIDTH])
    q_ref[...] = (q * lax.rsqrt(_group_sumsq(q, bd_ref) * (1.0 / HEAD_DIM) + EPS) * qg_ref[...]).astype(BF16)
    c0 += ATTN_WIDTH
    k = _dot(hn, w_ref[:, c0:c0 + ATTN_WIDTH])
    k_ref[...] = (k * lax.rsqrt(_group_sumsq(k, bd_ref) * (1.0 / HEAD_DIM) + EPS) * kg_ref[...]).astype(BF16)
    c0 += ATTN_WIDTH
    v_ref[...] = _dot(hn, w_ref[:, c0:c0 + ATTN_WIDTH]).astype(BF16)


def _in_call(x, g, w, sgug, sguw, sgub, qg, kg, bd, tm):
    n = x.shape[0]
    row = lambda c: pl.BlockSpec((tm, c), lambda i: (i, 0))
    full = lambda a: pl.BlockSpec(a.shape, lambda i: (0,) * a.ndim)
    return pl.pallas_call(
        _in_kernel,
        grid=(n // tm,),
        in_specs=[row(D_MODEL), full(g), full(w), full(sgug), full(sguw), full(sgub), full(qg), full(kg), full(bd)],
        out_specs=[row(SGU_WIDTH), row(LANES), row(LANES), row(ATTN_WIDTH), row(ATTN_WIDTH), row(ATTN_WIDTH)],
        out_shape=[jax.ShapeDtypeStruct((n, SGU_WIDTH), F32), jax.ShapeDtypeStruct((n, LANES), F32),
                   jax.ShapeDtypeStruct((n, LANES), F32), jax.ShapeDtypeStruct((n, ATTN_WIDTH), BF16),
                   jax.ShapeDtypeStruct((n, ATTN_WIDTH), BF16), jax.ShapeDtypeStruct((n, ATTN_WIDTH), BF16)],
        compiler_params=pltpu.CompilerParams(dimension_semantics=("parallel",), vmem_limit_bytes=VMEM_LIMIT_BYTES),
        name="in_proj",
    )(x, g, w, sgug, sguw, sgub, qg, kg, bd)


def _s5_kernel(u0_ref, u1_ref, wv_ref, krev_ref, wc_ref, apow_ref, y0_ref, y1_ref, ur_ref, s_ref, sp_ref, carry_ref):
    rows = ur_ref.shape[0]
    n_steps = apow_ref.shape[0]

    @pl.when(pl.program_id(1) == 0)
    def _():
        carry_ref[...] = jnp.zeros_like(carry_ref)

    for t in range(S5_T):
        ur_ref[:, t * S5_WIDTH:t * S5_WIDTH + LANES] = u0_ref[pl.ds(t, rows, stride=S5_T), :].astype(BF16)
        ur_ref[:, t * S5_WIDTH + LANES:(t + 1) * S5_WIDTH] = u1_ref[pl.ds(t, rows, stride=S5_T), :].astype(BF16)

    s_ref[...] = _dot(ur_ref[...], wv_ref[...])
    row_id = lax.broadcasted_iota(jnp.int32, (rows, LANES), 0)
    for cb in range(S5_NSTATE // LANES):
        cre = slice(cb * LANES, (cb + 1) * LANES)
        cim = slice(S5_NSTATE + cb * LANES, S5_NSTATE + (cb + 1) * LANES)
        re = s_ref[:, cre]
        im = s_ref[:, cim]
        c_re = carry_ref[0:1, cre]
        c_im = carry_ref[0:1, cim]
        a_re = apow_ref[0, 0:1, cre]
        a_im = apow_ref[0, 1:2, cre]
        re = re + jnp.where(row_id == 0, a_re * c_re - a_im * c_im, 0.0)
        im = im + jnp.where(row_id == 0, a_re * c_im + a_im * c_re, 0.0)
        for k in range(n_steps):
            shift = 1 << k
            a_re = apow_ref[k, 0:1, cre]
            a_im = apow_ref[k, 1:2, cre]
            re_s = jnp.where(row_id >= shift, pltpu.roll(re, shift, 0), 0.0)
            im_s = jnp.where(row_id >= shift, pltpu.roll(im, shift, 0), 0.0)
            re, im = re + a_re * re_s - a_im * im_s, im + a_re * im_s + a_im * re_s
        sp_ref[:, cre] = jnp.where(row_id >= 1, pltpu.roll(re, 1, 0), c_re).astype(BF16)
        sp_ref[:, cim] = jnp.where(row_id >= 1, pltpu.roll(im, 1, 0), c_im).astype(BF16)
        carry_ref[0:1, cre] = re[rows - 1:rows, :]
        carry_ref[0:1, cim] = im[rows - 1:rows, :]

    for i in range(S5_T):
        cols = slice(i * S5_WIDTH, (i + 1) * S5_WIDTH)
        intra = _dot(ur_ref[:, 0:(i + 1) * S5_WIDTH], krev_ref[(S5_T - 1 - i) * S5_WIDTH:, :])
        y = intra + _dot(sp_ref[...], wc_ref[:, cols])
        y0_ref[pl.ds(i, rows, stride=S5_T), :] = y[:, :LANES]
        y1_ref[pl.ds(i, rows, stride=S5_T), :] = y[:, LANES:]


def _s5_call(u0, u1, wv, krev, wc, apow, batch, rows):
    n = u0.shape[0]
    tok = rows * S5_T
    tiles_per_seq = n // batch // tok
    const = lambda a: pl.BlockSpec(a.shape, lambda b, t: (0,) * a.ndim, pipeline_mode=pl.Buffered(1))
    blk = pl.BlockSpec((tok, LANES), lambda b, t: (b * tiles_per_seq + t, 0))
    half = jax.ShapeDtypeStruct((n, LANES), F32)
    return pl.pallas_call(
        _s5_kernel,
        grid=(batch, tiles_per_seq),
        in_specs=[blk, blk, const(wv), const(krev), const(wc), const(apow)],
        out_specs=[blk, blk],
        out_shape=[half, half],
        scratch_shapes=[pltpu.VMEM((rows, S5_ROW), BF16), pltpu.VMEM((rows, 2 * S5_NSTATE), F32),
                        pltpu.VMEM((rows, 2 * S5_NSTATE), BF16), pltpu.VMEM((SUBLANES, 2 * S5_NSTATE), F32)],
        compiler_params=pltpu.CompilerParams(dimension_semantics=("arbitrary", "arbitrary"),
                                             vmem_limit_bytes=VMEM_LIMIT_BYTES),
        name="s5_scan",
    )(u0, u1, wv, krev, wc, apow)


def _s5_tables(lam_re, lam_im, log_dt, b_re, b_im, c_re, c_im, rows):
    g_, p_, h_ = S5_N_GROUPS, S5_STATE, S5_GROUP
    hi = lax.Precision.HIGHEST
    lam_re, lam_im = lam_re.astype(F32), lam_im.astype(F32)
    dt = jnp.exp(log_dt.astype(F32))[:, None]
    tau = jnp.arange(S5_T + 1, dtype=F32)[:, None, None]
    mag = jnp.exp(tau * (lam_re * dt)[None])
    ap_re = mag * jnp.cos(tau * (lam_im * dt)[None])
    ap_im = mag * jnp.sin(tau * (lam_im * dt)[None])
    n_re, n_im = ap_re[1] - 1.0, ap_im[1]
    den = lam_re * lam_re + lam_im * lam_im
    k_re = ((n_re * lam_re + n_im * lam_im) / den)[..., None]
    k_im = ((n_im * lam_re - n_re * lam_im) / den)[..., None]
    b_re, b_im = b_re.astype(F32), b_im.astype(F32)
    bb_re = k_re * b_re - k_im * b_im
    bb_im = k_re * b_im + k_im * b_re
    c_re, c_im = c_re.astype(F32), c_im.astype(F32)
    ab_re = ap_re[:S5_T, :, :, None] * bb_re[None] - ap_im[:S5_T, :, :, None] * bb_im[None]
    ab_im = ap_re[:S5_T, :, :, None] * bb_im[None] + ap_im[:S5_T, :, :, None] * bb_re[None]

    col_group = jnp.arange(S5_WIDTH) // h_
    state_group = jnp.arange(S5_NSTATE) // p_
    chan_group = jnp.arange(S5_ROW) // h_ % g_

    k_tau = (jnp.einsum('gop,tgpi->tgio', c_re, ab_re, precision=hi)
             - jnp.einsum('gop,tgpi->tgio', c_im, ab_im, precision=hi))
    krev = jnp.tile(k_tau[::-1].reshape(S5_ROW, h_), (1, g_))
    krev = jnp.where(chan_group[:, None] == col_group[None, :], krev, 0.0)

    def wv_half(ab):
        w = jnp.tile(ab[::-1].transpose(0, 1, 3, 2).reshape(S5_ROW, p_), (1, g_))
        return jnp.where(chan_group[:, None] == state_group[None, :], w, 0.0)
    wv = jnp.concatenate([wv_half(ab_re), wv_half(ab_im)], axis=1)

    cp = lambda c: c.transpose(0, 2, 1)[:, :, None, :]
    w_re = cp(c_re) * ap_re[1:].transpose(1, 2, 0)[..., None] - cp(c_im) * ap_im[1:].transpose(1, 2, 0)[..., None]
    w_im = cp(c_re) * ap_im[1:].transpose(1, 2, 0)[..., None] + cp(c_im) * ap_re[1:].transpose(1, 2, 0)[..., None]
    def wc_half(w):
        w = jnp.broadcast_to(w.reshape(S5_NSTATE, S5_T, 1, h_), (S5_NSTATE, S5_T, g_, h_)).reshape(S5_NSTATE, S5_ROW)
        return jnp.where(state_group[:, None] == chan_group[None, :], w, 0.0)
    wc = jnp.concatenate([wc_half(w_re), wc_half(-w_im)], axis=0)

    n_steps = max(1, (rows - 1).bit_length())
    pows = [(ap_re[S5_T].reshape(S5_NSTATE), ap_im[S5_T].reshape(S5_NSTATE))]
    for _ in range(n_steps - 1):
        r, i = pows[-1]
        pows.append((r * r - i * i, 2.0 * r * i))
    apow = jnp.stack([jnp.stack(p) for p in pows])
    return wv.astype(BF16), krev.astype(BF16), wc.astype(BF16), apow


def _attn_kernel(q_ref, k0_ref, k1_ref, k2_ref, v0_ref, v1_ref, v2_ref, bias_ref, o_ref):
    lane = lax.broadcasted_iota(jnp.int32, (1, LANES), 1)
    for p in range(ATTN_WIDTH // LANES):
        cols = slice(p * LANES, (p + 1) * LANES)
        qp = q_ref[:, cols]
        kcat = jnp.concatenate([k0_ref[:, cols], k1_ref[:, cols], k2_ref[:, cols]], axis=0)
        vcat = jnp.concatenate([v0_ref[:, cols], v1_ref[:, cols], v2_ref[:, cols]], axis=0)
        acc = jnp.zeros((ATTN_TQ, LANES), F32)
        for hh in range(LANES // HEAD_DIM):
            in_head = (lane >= hh * HEAD_DIM) & (lane < (hh + 1) * HEAD_DIM)
            qm = jnp.where(in_head, qp, jnp.zeros_like(qp))
            s = _dot_nt(qm, kcat) + bias_ref[0, 2 * p + hh]
            m = jnp.max(s, axis=-1, keepdims=True)
            e = jnp.exp(s - m)
            denom = jnp.sum(e, axis=-1, keepdims=True)
            o = _dot(e.astype(BF16), vcat)
            acc = acc + jnp.where(in_head, o / denom, 0.0)
        o_ref[:, cols] = acc


def _attn_call(q, k, v, bias, batch, seq):
    n = q.shape[0]
    tiles = seq // ATTN_TQ
    n_prev = ATTN_PREV // ATTN_TQ
    assert n_prev == 2 and bias.shape[0] == n_prev + 1
    cur = pl.BlockSpec((ATTN_TQ, ATTN_WIDTH), lambda b, t: (b * tiles + t, 0))
    prev = lambda d: pl.BlockSpec((ATTN_TQ, ATTN_WIDTH), lambda b, t: (b * tiles + jnp.maximum(t - d, 0), 0))
    bias_spec = pl.BlockSpec((1, ATTN_HEADS, ATTN_TQ, ATTN_TK), lambda b, t: (jnp.minimum(t, n_prev), 0, 0, 0))
    return pl.pallas_call(
        _attn_kernel,
        grid=(batch, tiles),
        in_specs=[cur, prev(2), prev(1), cur, prev(2), prev(1), cur, bias_spec],
        out_specs=cur,
        out_shape=jax.ShapeDtypeStruct((n, ATTN_WIDTH), F32),
        compiler_params=pltpu.CompilerParams(dimension_semantics=("parallel", "arbitrary"),
                                             vmem_limit_bytes=VMEM_LIMIT_BYTES),
        name="band_attn",
    )(q, k, k, k, v, v, v, bias)


def _attn_bias(rel_bias):
    n_diag = ATTN_TQ + ATTN_TK - 1
    d = (ATTN_TQ - 1 + ATTN_PREV) - jnp.arange(n_diag)
    vec = jnp.take(rel_bias.astype(F32), jnp.clip(d, -MAX_REL, MAX_REL) + MAX_REL, axis=1)
    skew = jnp.tile(jnp.pad(vec, ((0, 0), (0, 1))), (1, ATTN_TQ))[:, :ATTN_TQ * n_diag]
    skew = skew.reshape(ATTN_HEADS, ATTN_TQ, n_diag)
    base = skew[:, :, ATTN_TQ - 1:ATTN_TQ - 1 + ATTN_TK]
    qi = jnp.arange(ATTN_TQ)
    kj = jnp.arange(ATTN_TK)
    q_chunk = qi[:, None] // CHUNK + (BAND_CHUNKS - 1)
    k_chunk = kj[None, :] // CHUNK
    in_band = (k_chunk <= q_chunk) & (k_chunk >= q_chunk - (BAND_CHUNKS - 1))
    tabs = []
    for t in range(ATTN_PREV // ATTN_TQ + 1):
        exists = kj[None, :] >= ATTN_PREV - t * ATTN_TQ
        tabs.append(jnp.where((in_band & exists)[None], base, NEG_INF))
    return jnp.stack(tabs)


def _router_gates(lt):
    tm = lt.shape[1]
    row = lax.broadcasted_iota(jnp.int32, (ROUTER_SLAB, tm), 0)
    gl = lt[0:ROUTER_SLAB]
    gmax = jnp.max(gl, axis=0, keepdims=True)
    p_g = 1.0 / jnp.sum(jnp.exp(gl - gmax), axis=0, keepdims=True)
    g_top = jnp.min(jnp.where(gl == gmax, row, ROUTER_SLAB), axis=0, keepdims=True)
    el = jnp.zeros((ROUTER_SLAB, tm), F32)
    for g in range(N_EXPERT_GROUPS):
        el = el + jnp.where(g_top == g, lt[(g + 1) * ROUTER_SLAB:(g + 2) * ROUTER_SLAB], 0.0)
    ee = jnp.exp(el - jnp.max(el, axis=0, keepdims=True))
    ep = ee / jnp.sum(ee, axis=0, keepdims=True)
    p1 = jnp.max(ep, axis=0, keepdims=True)
    i1 = jnp.min(jnp.where(ep == p1, row, ROUTER_SLAB), axis=0, keepdims=True)
    rest = jnp.where(row == i1, -1.0, ep)
    p2 = jnp.max(rest, axis=0, keepdims=True)
    i2 = jnp.min(jnp.where(rest == p2, row, ROUTER_SLAB), axis=0, keepdims=True)
    tot = p1 + p2
    w = (jnp.where(row == i1, p1 / tot, 0.0) + jnp.where(row == i2, p2 / tot, 0.0)) * p_g
    return [jnp.where(g_top == g, w, 0.0) for g in range(N_EXPERT_GROUPS)], g_top


def _out_kernel(oa_ref, y0_ref, y1_ref, u0_ref, u1_ref, oc_ref, x_ref, d_ref, gluw_ref, glub_ref, og_ref, wout_ref,
                fg_ref, wr_ref, br_ref, bd_ref, x1_ref, hg_ref, route_ref, cnt_ref, run_ref):
    tm = x_ref.shape[0]

    @pl.when(pl.program_id(0) == 0)
    def _():
        run_ref[...] = jnp.zeros_like(run_ref)

    y = jnp.concatenate([y0_ref[...], y1_ref[...]], axis=-1)
    u = jnp.concatenate([u0_ref[...], u1_ref[...]], axis=-1)
    y = jax.nn.gelu(y + d_ref[...] * u)
    ob = y * jax.nn.sigmoid(_dot(y.astype(BF16), gluw_ref[...]) + glub_ref[...])
    o = jnp.concatenate([oa_ref[...], ob, oc_ref[...]], axis=-1)
    on = o * lax.rsqrt(_group_sumsq(o, bd_ref) * (1.0 / OUT_NORM_GROUP) + EPS) * og_ref[...]
    x1 = x_ref[...] + _dot(on.astype(BF16), wout_ref[...])
    x1_ref[...] = x1
    ms = jnp.mean(x1 * x1, axis=-1, keepdims=True)
    hn = x1 * lax.rsqrt(ms + EPS) * fg_ref[...]
    hg_ref[:, :D_MODEL] = hn
    lt = _dot(hn.astype(BF16), wr_ref[...]).T + br_ref[...]
    slabs, g_top = _router_gates(lt)
    pad = jnp.zeros((LANES - N_EXPERT_GROUPS * ROUTER_SLAB, tm), F32)
    hg_ref[:, D_MODEL:] = jnp.concatenate(slabs + [pad], axis=0).T

    row = lax.broadcasted_iota(jnp.int32, (ROUTER_SLAB, tm), 0)
    lane = lax.broadcasted_iota(jnp.int32, (ROUTER_SLAB, tm), 1)
    member = jnp.where(row == g_top, 1.0, 0.0)
    cum = member
    shift = 1
    while shift < tm:
        cum = cum + jnp.where(lane >= shift, pltpu.roll(cum, shift, 1), 0.0)
        shift *= 2
    run = run_ref[:, 0:1]
    rank = jnp.sum(member * (cum - 1.0 + run), axis=0, keepdims=True)
    total = run + cum[:, tm - 1:tm]
    run_ref[...] = jnp.broadcast_to(total, run_ref.shape)
    cnt_ref[...] = jnp.broadcast_to(total, cnt_ref.shape)
    route_ref[...] = jnp.where(row == 0, g_top, jnp.where(row == 1, rank.astype(jnp.int32), 0))


def _out_call(oa, y0, y1, u0, u1, oc, x, d, gluw, glub, og, wout, fg, wr, br, bd, tm):
    n = x.shape[0]
    row = lambda c: pl.BlockSpec((tm, c), lambda i: (i, 0))
    full = lambda a: pl.BlockSpec(a.shape, lambda i: (0,) * a.ndim)
    return pl.pallas_call(
        _out_kernel,
        grid=(n // tm,),
        in_specs=[row(SGU_WIDTH), row(LANES), row(LANES), row(LANES), row(LANES), row(ATTN_WIDTH), row(D_MODEL),
                  full(d), full(gluw),
                  full(glub), full(og), full(wout), full(fg), full(wr), full(br), full(bd)],
        out_specs=[row(D_MODEL), row(MOE_ROW), pl.BlockSpec((ROUTER_SLAB, tm), lambda i: (i, 0)),
                   pl.BlockSpec((ROUTER_SLAB, LANES), lambda i: (0, 0))],
        out_shape=[jax.ShapeDtypeStruct((n, D_MODEL), F32), jax.ShapeDtypeStruct((n, MOE_ROW), F32),
                   jax.ShapeDtypeStruct((n // tm * ROUTER_SLAB, tm), jnp.int32),
                   jax.ShapeDtypeStruct((ROUTER_SLAB, LANES), F32)],
        scratch_shapes=[pltpu.VMEM((ROUTER_SLAB, LANES), F32)],
        compiler_params=pltpu.CompilerParams(dimension_semantics=("arbitrary",), vmem_limit_bytes=VMEM_LIMIT_BYTES),
        name="out_proj",
    )(oa, y0, y1, u0, u1, oc, x, d, gluw, glub, og, wout, fg, wr, br, bd)


def _router_tables(wg, bg, we, be):
    w = jnp.zeros((D_MODEL, LANES), F32)
    b = jnp.full((LANES,), NEG_INF, F32)
    w = w.at[:, 0:N_EXPERT_GROUPS].set(wg.astype(F32))
    b = b.at[0:N_EXPERT_GROUPS].set(bg.astype(F32))
    for g in range(N_EXPERT_GROUPS):
        c0 = (g + 1) * ROUTER_SLAB
        w = w.at[:, c0:c0 + EXPERTS_PER_GROUP].set(we[g].astype(F32))
        b = b.at[c0:c0 + EXPERTS_PER_GROUP].set(be[g].astype(F32))
    return w.astype(BF16), b[:, None]


def _route_plan(route, counts, n, tm_out, tm_moe):
    r3 = route.reshape(n // tm_out, ROUTER_SLAB, tm_out)
    group = r3[:, 0, :].reshape(n)
    rank = r3[:, 1, :].reshape(n)
    cnt = counts[:N_EXPERT_GROUPS, 0].astype(jnp.int32)
    tiles_g = (cnt + tm_moe - 1) // tm_moe
    tile_end = jnp.cumsum(tiles_g)
    row_start = (tile_end - tiles_g) * tm_moe
    pos = rank
    for g in range(N_EXPERT_GROUPS):
        pos = pos + jnp.where(group == g, row_start[g], 0)
    n_tiles = n // tm_moe + N_EXPERT_GROUPS - 1
    tile_group = jnp.sum(jnp.arange(n_tiles)[:, None] >= tile_end[None, :], axis=1)
    meta = jnp.concatenate([jnp.minimum(tile_group, N_EXPERT_GROUPS - 1), tile_end[-1:]]).astype(jnp.int32)
    next_start = jnp.concatenate([row_start[1:], jnp.full((1,), n_tiles * tm_moe, jnp.int32)])
    pads = jnp.concatenate([row_start + cnt, next_start - row_start - cnt]).astype(jnp.int32)
    return pos.astype(jnp.int32), meta, pads, n_tiles


def _dispatch_kernel(pads_ref, pos_ref, hg_ref, hs_ref, zero_ref, sem):
    td = hg_ref.shape[0]

    def row_copy(r):
        return pltpu.make_async_copy(hg_ref.at[pl.ds(r, 1)], hs_ref.at[pl.ds(pos_ref[r], 1)], sem)

    def start(r, c):
        row_copy(r).start()
        return c

    def wait(r, c):
        row_copy(r).wait()
        return c

    lax.fori_loop(0, td, start, 0, unroll=8)
    lax.fori_loop(0, td, wait, 0, unroll=8)

    @pl.when(pl.program_id(0) == pl.num_programs(0) - 1)
    def _():
        zero_ref[...] = jnp.zeros_like(zero_ref)
        for g in range(N_EXPERT_GROUPS):
            first = pads_ref[g]
            count = pads_ref[N_EXPERT_GROUPS + g]

            def pad_copy(k):
                return pltpu.make_async_copy(zero_ref.at[pl.ds(0, 1)], hs_ref.at[pl.ds(first + k, 1)], sem)

            def pad_start(k, c):
                pad_copy(k).start()
                return c

            def pad_wait(k, c):
                pad_copy(k).wait()
                return c

            lax.fori_loop(0, count, pad_start, 0)
            lax.fori_loop(0, count, pad_wait, 0)


def _dispatch_call(pads, pos, hg, n_rows, td):
    n = hg.shape[0]
    return pl.pallas_call(
        _dispatch_kernel,
        grid_spec=pltpu.PrefetchScalarGridSpec(
            num_scalar_prefetch=1,
            grid=(n // td,),
            in_specs=[pl.BlockSpec((td,), lambda i, pads: (i,), memory_space=pltpu.SMEM),
                      pl.BlockSpec((td, MOE_ROW), lambda i, pads: (i, 0))],
            out_specs=pl.BlockSpec(memory_space=pl.ANY),
            scratch_shapes=[pltpu.VMEM((SUBLANES, MOE_ROW), F32), pltpu.SemaphoreType.DMA],
        ),
        out_shape=jax.ShapeDtypeStruct((n_rows, MOE_ROW), F32),
        compiler_params=pltpu.CompilerParams(dimension_semantics=("arbitrary",), vmem_limit_bytes=VMEM_LIMIT_BYTES),
        name="moe_dispatch",
    )(pads, pos, hg)


def _moe_group_kernel(meta_ref, hs_ref, wg_ref, wu_ref, wd_ref, o_ref, wgb_ref, wub_ref, wdb_ref):
    j = pl.program_id(0)
    n_used = meta_ref[pl.num_programs(0)]
    group = meta_ref[j]

    @pl.when((j == 0) | (group != meta_ref[jnp.maximum(j - 1, 0)]))
    def _():
        wgb_ref[...] = wg_ref[...].astype(BF16)
        wub_ref[...] = wu_ref[...].astype(BF16)
        wdb_ref[...] = wd_ref[...].astype(BF16)

    @pl.when(j < n_used)
    def _():
        h = hs_ref[:, :D_MODEL].astype(BF16)
        gates = hs_ref[:, D_MODEL:]
        lane = lax.broadcasted_iota(jnp.int32, (1, LANES), 1)
        out = None
        for e in range(EXPERTS_PER_GROUP):
            gt = _dot(h, wgb_ref[e])
            up = _dot(h, wub_ref[e])
            gate = jnp.sum(jnp.where(lane == group * ROUTER_SLAB + e, gates, 0.0), axis=-1, keepdims=True)
            a = (gt * jax.nn.sigmoid(gt)) * up * gate
            part = _dot(a.astype(BF16), wdb_ref[e])
            out = part if out is None else out + part
        o_ref[...] = out

    @pl.when(j >= n_used)
    def _():
        o_ref[...] = jnp.zeros_like(o_ref)


def _moe_group_call(meta, hs, wg, wu, wd, n_tiles, tm):
    by_group = lambda a: pl.BlockSpec((EXPERTS_PER_GROUP,) + a.shape[1:], lambda j, meta: (meta[j], 0, 0))
    return pl.pallas_call(
        _moe_group_kernel,
        grid_spec=pltpu.PrefetchScalarGridSpec(
            num_scalar_prefetch=1,
            grid=(n_tiles,),
            in_specs=[pl.BlockSpec((tm, MOE_ROW), lambda j, meta: (j, 0)),
                      by_group(wg), by_group(wu), by_group(wd)],
            out_specs=pl.BlockSpec((tm, D_MODEL), lambda j, meta: (j, 0)),
            scratch_shapes=[pltpu.VMEM((EXPERTS_PER_GROUP, D_MODEL, D_EXPERT), BF16),
                            pltpu.VMEM((EXPERTS_PER_GROUP, D_MODEL, D_EXPERT), BF16),
                            pltpu.VMEM((EXPERTS_PER_GROUP, D_EXPERT, D_MODEL), BF16)],
        ),
        out_shape=jax.ShapeDtypeStruct((n_tiles * tm, D_MODEL), F32),
        compiler_params=pltpu.CompilerParams(dimension_semantics=("arbitrary",), vmem_limit_bytes=VMEM_LIMIT_BYTES),
        name="moe_experts",
    )(meta, hs, wg, wu, wd)


def _combine_kernel(pos_ref, posn_ref, x1_ref, src_ref, o_ref, buf_ref, sem):
    i = pl.program_id(0)
    tc = x1_ref.shape[0]
    slot = i % 2

    def row_copy(p_ref, s, r):
        return pltpu.make_async_copy(src_ref.at[pl.ds(p_ref[r], 1)], buf_ref.at[s, pl.ds(r, 1)], sem.at[s])

    def gather(p_ref, s):
        def start(r, c):
            row_copy(p_ref, s, r).start()
            return c
        lax.fori_loop(0, tc, start, 0, unroll=8)

    @pl.when(i == 0)
    def _():
        gather(pos_ref, 0)

    @pl.when(i + 1 < pl.num_programs(0))
    def _():
        gather(posn_ref, 1 - slot)

    def wait(r, c):
        row_copy(pos_ref, slot, r).wait()
        return c
    lax.fori_loop(0, tc, wait, 0, unroll=8)
    o_ref[...] = x1_ref[...] + buf_ref[slot]


def _combine_call(pos, x1, src, tc):
    n = x1.shape[0]
    last = n // tc - 1
    return pl.pallas_call(
        _combine_kernel,
        grid=(n // tc,),
        in_specs=[pl.BlockSpec((tc,), lambda i: (i,), memory_space=pltpu.SMEM),
                  pl.BlockSpec((tc,), lambda i: (jnp.minimum(i + 1, last),), memory_space=pltpu.SMEM),
                  pl.BlockSpec((tc, D_MODEL), lambda i: (i, 0)),
                  pl.BlockSpec(memory_space=pl.ANY)],
        out_specs=pl.BlockSpec((tc, D_MODEL), lambda i: (i, 0)),
        out_shape=jax.ShapeDtypeStruct((n, D_MODEL), F32),
        scratch_shapes=[pltpu.VMEM((2, tc, D_MODEL), F32), pltpu.SemaphoreType.DMA((2,))],
        compiler_params=pltpu.CompilerParams(dimension_semantics=("arbitrary",), vmem_limit_bytes=VMEM_LIMIT_BYTES),
        name="moe_combine",
    )(pos, pos, x1, src)


def kernel(x, norm_mix, w_in, sgu_norm, sgu_w, sgu_b, s5_lambda_re, s5_lambda_im, s5_log_dt, s5_b_re, s5_b_im,
           s5_c_re, s5_c_im, s5_d, s5_glu_w, s5_glu_b, q_norm, k_norm, rel_bias, out_norm, w_out, norm_ffn,
           router_group_w, router_group_b, router_expert_w, router_expert_b, w_gate, w_up, w_down):
    batch, seq, _ = x.shape
    n = batch * seq
    depth = w_in.shape[0]
    t = _tiles(n, seq)
    assert seq % ATTN_TQ == 0 and seq % (S5_T * t["s5_rows"]) == 0
    row_vec = lambda a: a.astype(F32)[None, :]

    lane_group = jnp.arange(LANES) // HEAD_DIM
    bd = (lane_group[:, None] == lane_group[None, :]).astype(BF16)
    block_chunk = jnp.arange(SGU_BLOCK) // CHUNK
    sgu_mask = block_chunk[None, :] <= block_chunk[:, None]

    xf = x.reshape(n, D_MODEL)
    for l in range(depth):
        sguw = jnp.where(sgu_mask[None], sgu_w[l], 0).astype(BF16)
        sgub = jnp.repeat(sgu_b[l].astype(F32).T, HEAD_DIM, axis=1)
        qg = row_vec(jnp.tile(q_norm[l], ATTN_HEADS)) * (HEAD_DIM ** -0.5)
        kg = row_vec(jnp.tile(k_norm[l], ATTN_HEADS))
        oa, u0, u1, q, k, v = _in_call(xf, row_vec(norm_mix[l]), w_in[l].astype(BF16), row_vec(sgu_norm[l]), sguw, sgub,
                                      qg, kg, bd, t["tm_in"])

        wv, krev, wc, apow = _s5_tables(s5_lambda_re[l], s5_lambda_im[l], s5_log_dt[l], s5_b_re[l], s5_b_im[l],
                                        s5_c_re[l], s5_c_im[l], t["s5_rows"])
        y0, y1 = _s5_call(u0, u1, wv, krev, wc, apow, batch, t["s5_rows"])

        oc = _attn_call(q, k, v, _attn_bias(rel_bias[l]), batch, seq)

        wr, br = _router_tables(router_group_w[l], router_group_b[l], router_expert_w[l], router_expert_b[l])
        x1, hg, route, counts = _out_call(oa, y0, y1, u0, u1, oc, xf, row_vec(s5_d[l]), s5_glu_w[l].astype(BF16), row_vec(s5_glu_b[l]),
                                 row_vec(out_norm[l]), w_out[l].astype(BF16), row_vec(norm_ffn[l]), wr, br, bd,
                                 t["tm_out"])

        pos, meta, pads, n_tiles = _route_plan(route, counts, n, t["tm_out"], t["tm_moe"])
        hs = _dispatch_call(pads, pos, hg, n_tiles * t["tm_moe"], t["tm_perm"])
        ys = _moe_group_call(meta, hs, w_gate[l], w_up[l], w_down[l], n_tiles, t["tm_moe"])
        xf = _combine_call(pos, x1, ys, t["tm_perm"])
    return xf.reshape(batch, seq, D_MODEL)
```

```python
import functools

import jax
import jax.numpy as jnp
from jax import lax
from jax.experimental import pallas as pl
from jax.experimental.pallas import tpu as pltpu

F32 = jnp.float32
BF16 = jnp.bfloat16

D_MODEL = 1024
CHUNK = 64
HEAD_DIM = 64
SGU_WIDTH = 256
SGU_HEADS = 4
SGU_BLOCK = 128
S5_WIDTH = 256
S5_GROUP = 16
S5_N_GROUPS = 16
S5_STATE = 64
ATTN_WIDTH = 512
ATTN_HEADS = 8
BAND_CHUNKS = 9
MAX_REL = 256
IN_COLS = 2 * SGU_WIDTH + S5_WIDTH + 3 * ATTN_WIDTH
OUT_NORM_GROUP = 64
N_EXPERT_GROUPS = 4
EXPERTS_PER_GROUP = 4
N_EXPERTS = 16
D_EXPERT = 256
EPS = 1e-6
NEG_INF = -1e30

LANES = 128
SUBLANES = 8
VMEM_LIMIT_BYTES = 56 * 1024 * 1024

S5_T = 16
S5_ROW = S5_T * S5_WIDTH
S5_NSTATE = S5_N_GROUPS * S5_STATE
ATTN_TQ = 256
ATTN_PREV = (BAND_CHUNKS - 1) * CHUNK
ATTN_TK = ATTN_TQ + ATTN_PREV
ROUTER_SLAB = SUBLANES
MOE_ROW = D_MODEL + LANES


def _tiles(n_tokens, seq):
    def pick(pref, total):
        t = min(pref, total)
        assert total % t == 0
        return t
    return dict(
        tm_in=pick(512, seq),
        tm_out=pick(512, seq),
        tm_moe=pick(512, n_tokens),
        tm_perm=pick(512, n_tokens),
        s5_rows=pick(128, seq // S5_T),
    )


def _dot(a, b):
    return jnp.dot(a, b, preferred_element_type=F32)


def _dot_nt(a, b):
    return lax.dot_general(a, b, (((1,), (1,)), ((), ())), preferred_element_type=F32)


def _group_sumsq(x, bd_ref):
    x2 = (x * x).astype(BF16)
    parts = [_dot(x2[:, t * LANES:(t + 1) * LANES], bd_ref[...]) for t in range(x.shape[1] // LANES)]
    return jnp.concatenate(parts, axis=-1)


def _in_kernel(x_ref, g_ref, wf_ref, sgug_ref, sguw_ref, sgub_ref, qg_ref, kg_ref, bd_ref,
               oa_ref, u0_ref, u1_ref, q_ref, k_ref, v_ref, w_ref):
    tm = x_ref.shape[0]

    @pl.when(pl.program_id(0) == 0)
    def _():
        w_ref[...] = wf_ref[...].astype(BF16)

    x = x_ref[...]
    ms = jnp.mean(x * x, axis=-1, keepdims=True)
    hn = (x * lax.rsqrt(ms + EPS) * g_ref[...]).astype(BF16)

    z = jax.nn.gelu(_dot(hn, w_ref[:, 0:2 * SGU_WIDTH]))
    u = z[:, :SGU_WIDTH]
    v = z[:, SGU_WIDTH:]
    v = v * lax.rsqrt(jnp.mean(v * v, axis=-1, keepdims=True) + EPS) * sgug_ref[...]
    vb = v.astype(BF16)
    first_head = lax.broadcasted_iota(jnp.int32, (SGU_BLOCK, LANES), 1) < HEAD_DIM
    for r in range(tm // SGU_BLOCK):
        rows = slice(r * SGU_BLOCK, (r + 1) * SGU_BLOCK)
        for p in range(SGU_WIDTH // LANES):
            cols = slice(p * LANES, (p + 1) * LANES)
            vp = vb[rows, cols]
            mixed = jnp.where(first_head, _dot(sguw_ref[2 * p], vp), _dot(sguw_ref[2 * p + 1], vp))
            oa_ref[rows, cols] = u[rows, cols] * (mixed + sgub_ref[:, cols])

    c0 = 2 * SGU_WIDTH
    us = _dot(hn, w_ref[:, c0:c0 + S5_WIDTH])
    u0_ref[...] = us[:, :LANES]
    u1_ref[...] = us[:, LANES:]

    c0 += S5_WIDTH
    q = _dot(hn, w_ref[:, c0:c0 + ATTN_WIDTH])
    q_ref[...] = (q * lax.rsqrt(_group_sumsq(q, bd_ref) * (1.0 / HEAD_DIM) + EPS) * qg_ref[...]).astype(BF16)
    c0 += ATTN_WIDTH
    k = _dot(hn, w_ref[:, c0:c0 + ATTN_WIDTH])
    k_ref[...] = (k * lax.rsqrt(_group_sumsq(k, bd_ref) * (1.0 / HEAD_DIM) + EPS) * kg_ref[...]).astype(BF16)
    c0 += ATTN_WIDTH
    v_ref[...] = _dot(hn, w_ref[:, c0:c0 + ATTN_WIDTH]).astype(BF16)


def _in_call(x, g, w, sgug, sguw, sgub, qg, kg, bd, tm):
    n = x.shape[0]
    row = lambda c: pl.BlockSpec((tm, c), lambda i: (i, 0))
    full = lambda a: pl.BlockSpec(a.shape, lambda i: (0,) * a.ndim)
    once = lambda a: pl.BlockSpec(a.shape, lambda i: (0,) * a.ndim, pipeline_mode=pl.Buffered(1))
    return pl.pallas_call(
        _in_kernel,
        grid=(n // tm,),
        in_specs=[row(D_MODEL), full(g), once(w), full(sgug), full(sguw), full(sgub), full(qg), full(kg), full(bd)],
        out_specs=[row(SGU_WIDTH), row(LANES), row(LANES), row(ATTN_WIDTH), row(ATTN_WIDTH), row(ATTN_WIDTH)],
        out_shape=[jax.ShapeDtypeStruct((n, SGU_WIDTH), F32), jax.ShapeDtypeStruct((n, LANES), F32),
                   jax.ShapeDtypeStruct((n, LANES), F32), jax.ShapeDtypeStruct((n, ATTN_WIDTH), BF16),
                   jax.ShapeDtypeStruct((n, ATTN_WIDTH), BF16), jax.ShapeDtypeStruct((n, ATTN_WIDTH), BF16)],
        scratch_shapes=[pltpu.VMEM(w.shape, BF16)],
        compiler_params=pltpu.CompilerParams(dimension_semantics=("arbitrary",), vmem_limit_bytes=VMEM_LIMIT_BYTES),
        name="in_proj",
    )(x, g, w, sgug, sguw, sgub, qg, kg, bd)


def _s5_expand_tables(sv_ref, sc_ref, wv_ref, wc_ref):
    pair_of_row = lax.broadcasted_iota(jnp.int32, (S5_ROW, 1), 0) // S5_GROUP % S5_N_GROUPS // 2
    for part in range(2):
        src = sv_ref[part]
        for j in range(S5_NSTATE // LANES):
            cols = slice(part * S5_NSTATE + j * LANES, part * S5_NSTATE + (j + 1) * LANES)
            wv_ref[:, cols] = jnp.where(pair_of_row == j, src, 0.0).astype(BF16)

    wc_ref[...] = jnp.zeros_like(wc_ref)
    lane = lax.broadcasted_iota(jnp.int32, (1, LANES), 1)
    per_tile = LANES // S5_GROUP
    for part in range(2):
        for g in range(S5_N_GROUPS):
            rows = slice(part * S5_NSTATE + g * S5_STATE, part * S5_NSTATE + (g + 1) * S5_STATE)
            keep = (lane >= g % per_tile * S5_GROUP) & (lane < (g % per_tile + 1) * S5_GROUP)
            for i in range(S5_T):
                tile = sc_ref[rows, i // per_tile * LANES:(i // per_tile + 1) * LANES]
                shift = (g % per_tile - i % per_tile) * S5_GROUP % LANES
                if shift:
                    tile = pltpu.roll(tile, shift, 1)
                q = i * (S5_WIDTH // LANES) + g // per_tile
                wc_ref[rows, q * LANES:(q + 1) * LANES] = jnp.where(keep, tile, 0.0).astype(BF16)


def _s5_kernel(u0_ref, u1_ref, sv_ref, krev_ref, sc_ref, apow_ref, y0_ref, y1_ref,
               wv_ref, wc_ref, ur_ref, s_ref, sp_ref, carry_ref):
    rows = ur_ref.shape[0]
    n_steps = apow_ref.shape[0]

    @pl.when((pl.program_id(0) == 0) & (pl.program_id(1) == 0))
    def _():
        _s5_expand_tables(sv_ref, sc_ref, wv_ref, wc_ref)

    @pl.when(pl.program_id(1) == 0)
    def _():
        carry_ref[...] = jnp.zeros_like(carry_ref)

    for t in range(S5_T):
        ur_ref[:, t * S5_WIDTH:t * S5_WIDTH + LANES] = u0_ref[pl.ds(t, rows, stride=S5_T), :].astype(BF16)
        ur_ref[:, t * S5_WIDTH + LANES:(t + 1) * S5_WIDTH] = u1_ref[pl.ds(t, rows, stride=S5_T), :].astype(BF16)

    s_ref[...] = _dot(ur_ref[...], wv_ref[...])
    row_id = lax.broadcasted_iota(jnp.int32, (rows, LANES), 0)
    for cb in range(S5_NSTATE // LANES):
        cre = slice(cb * LANES, (cb + 1) * LANES)
        cim = slice(S5_NSTATE + cb * LANES, S5_NSTATE + (cb + 1) * LANES)
        re = s_ref[:, cre]
        im = s_ref[:, cim]
        c_re = carry_ref[0:1, cre]
        c_im = carry_ref[0:1, cim]
        a_re = apow_ref[0, 0:1, cre]
        a_im = apow_ref[0, 1:2, cre]
        re = re + jnp.where(row_id == 0, a_re * c_re - a_im * c_im, 0.0)
        im = im + jnp.where(row_id == 0, a_re * c_im + a_im * c_re, 0.0)
        for k in range(n_steps):
            shift = 1 << k
            a_re = apow_ref[k, 0:1, cre]
            a_im = apow_ref[k, 1:2, cre]
            re_s = jnp.where(row_id >= shift, pltpu.roll(re, shift, 0), 0.0)
            im_s = jnp.where(row_id >= shift, pltpu.roll(im, shift, 0), 0.0)
            re, im = re + a_re * re_s - a_im * im_s, im + a_re * im_s + a_im * re_s
        sp_ref[:, cre] = jnp.where(row_id >= 1, pltpu.roll(re, 1, 0), c_re).astype(BF16)
        sp_ref[:, cim] = jnp.where(row_id >= 1, pltpu.roll(im, 1, 0), c_im).astype(BF16)
        carry_ref[0:1, cre] = re[rows - 1:rows, :]
        carry_ref[0:1, cim] = im[rows - 1:rows, :]

    for i in range(S5_T):
        cols = slice(i * S5_WIDTH, (i + 1) * S5_WIDTH)
        intra = _dot(ur_ref[:, 0:(i + 1) * S5_WIDTH], krev_ref[(S5_T - 1 - i) * S5_WIDTH:, :])
        y = intra + _dot(sp_ref[...], wc_ref[:, cols])
        y0_ref[pl.ds(i, rows, stride=S5_T), :] = y[:, :LANES]
        y1_ref[pl.ds(i, rows, stride=S5_T), :] = y[:, LANES:]


def _s5_call(u0, u1, sv, krev, sc, apow, batch, rows):
    n = u0.shape[0]
    tok = rows * S5_T
    tiles_per_seq = n // batch // tok
    const = lambda a: pl.BlockSpec(a.shape, lambda b, t: (0,) * a.ndim, pipeline_mode=pl.Buffered(1))
    blk = pl.BlockSpec((tok, LANES), lambda b, t: (b * tiles_per_seq + t, 0))
    half = jax.ShapeDtypeStruct((n, LANES), F32)
    return pl.pallas_call(
        _s5_kernel,
        grid=(batch, tiles_per_seq),
        in_specs=[blk, blk, const(sv), const(krev), const(sc), const(apow)],
        out_specs=[blk, blk],
        out_shape=[half, half],
        scratch_shapes=[pltpu.VMEM((S5_ROW, 2 * S5_NSTATE), BF16), pltpu.VMEM((2 * S5_NSTATE, S5_ROW), BF16),
                        pltpu.VMEM((rows, S5_ROW), BF16), pltpu.VMEM((rows, 2 * S5_NSTATE), F32),
                        pltpu.VMEM((rows, 2 * S5_NSTATE), BF16), pltpu.VMEM((SUBLANES, 2 * S5_NSTATE), F32)],
        compiler_params=pltpu.CompilerParams(dimension_semantics=("arbitrary", "arbitrary"),
                                             vmem_limit_bytes=VMEM_LIMIT_BYTES),
        name="s5_scan",
    )(u0, u1, sv, krev, sc, apow)


def _s5_tables(lam_re, lam_im, log_dt, b_re, b_im, c_re, c_im, rows):
    g_, p_, h_ = S5_N_GROUPS, S5_STATE, S5_GROUP
    hi = lax.Precision.HIGHEST
    lam_re, lam_im = lam_re.astype(F32), lam_im.astype(F32)
    dt = jnp.exp(log_dt.astype(F32))[:, None]
    tau = jnp.arange(S5_T + 1, dtype=F32)[:, None, None]
    mag = jnp.exp(tau * (lam_re * dt)[None])
    ap_re = mag * jnp.cos(tau * (lam_im * dt)[None])
    ap_im = mag * jnp.sin(tau * (lam_im * dt)[None])
    n_re, n_im = ap_re[1] - 1.0, ap_im[1]
    den = lam_re * lam_re + lam_im * lam_im
    k_re = ((n_re * lam_re + n_im * lam_im) / den)[..., None]
    k_im = ((n_im * lam_re - n_re * lam_im) / den)[..., None]
    b_re, b_im = b_re.astype(F32), b_im.astype(F32)
    bb_re = k_re * b_re - k_im * b_im
    bb_im = k_re * b_im + k_im * b_re
    c_re, c_im = c_re.astype(F32), c_im.astype(F32)
    ab_re = ap_re[:S5_T, :, :, None] * bb_re[None] - ap_im[:S5_T, :, :, None] * bb_im[None]
    ab_im = ap_re[:S5_T, :, :, None] * bb_im[None] + ap_im[:S5_T, :, :, None] * bb_re[None]

    col_group = jnp.arange(S5_WIDTH) // h_
    chan_group = jnp.arange(S5_ROW) // h_ % g_

    k_tau = (jnp.einsum('gop,tgpi->tgio', c_re, ab_re, precision=hi)
             - jnp.einsum('gop,tgpi->tgio', c_im, ab_im, precision=hi))
    krev = jnp.tile(k_tau[::-1].reshape(S5_ROW, h_), (1, g_))
    krev = jnp.where(chan_group[:, None] == col_group[None, :], krev, 0.0)

    def sv_part(ab):
        w = ab[::-1].transpose(0, 1, 3, 2).reshape(S5_ROW, p_)
        odd = (chan_group % 2 == 1)[:, None]
        return jnp.concatenate([jnp.where(odd, 0.0, w), jnp.where(odd, w, 0.0)], axis=1)
    sv = jnp.stack([sv_part(ab_re), sv_part(ab_im)])

    cp = lambda c: c.transpose(0, 2, 1)[:, :, None, :]
    w_re = cp(c_re) * ap_re[1:].transpose(1, 2, 0)[..., None] - cp(c_im) * ap_im[1:].transpose(1, 2, 0)[..., None]
    w_im = cp(c_re) * ap_im[1:].transpose(1, 2, 0)[..., None] + cp(c_im) * ap_re[1:].transpose(1, 2, 0)[..., None]
    sc = jnp.concatenate([w_re.reshape(S5_NSTATE, S5_T * h_), -w_im.reshape(S5_NSTATE, S5_T * h_)], axis=0)

    n_steps = max(1, (rows - 1).bit_length())
    pows = [(ap_re[S5_T].reshape(S5_NSTATE), ap_im[S5_T].reshape(S5_NSTATE))]
    for _ in range(n_steps - 1):
        r, i = pows[-1]
        pows.append((r * r - i * i, 2.0 * r * i))
    apow = jnp.stack([jnp.stack(p) for p in pows])
    return sv, krev.astype(BF16), sc, apow


def _attn_kernel(q_ref, k0_ref, k1_ref, k2_ref, v0_ref, v1_ref, v2_ref, bias_ref, o_ref):
    lane = lax.broadcasted_iota(jnp.int32, (1, LANES), 1)
    for p in range(ATTN_WIDTH // LANES):
        cols = slice(p * LANES, (p + 1) * LANES)
        qp = q_ref[:, cols]
        kcat = jnp.concatenate([k0_ref[:, cols], k1_ref[:, cols], k2_ref[:, cols]], axis=0)
        vcat = jnp.concatenate([v0_ref[:, cols], v1_ref[:, cols], v2_ref[:, cols]], axis=0)
        acc = jnp.zeros((ATTN_TQ, LANES), F32)
        for hh in range(LANES // HEAD_DIM):
            in_head = (lane >= hh * HEAD_DIM) & (lane < (hh + 1) * HEAD_DIM)
            qm = jnp.where(in_head, qp, jnp.zeros_like(qp))
            s = _dot_nt(qm, kcat) + bias_ref[0, 2 * p + hh]
            m = jnp.max(s, axis=-1, keepdims=True)
            e = jnp.exp(s - m)
            denom = jnp.sum(e, axis=-1, keepdims=True)
            o = _dot(e.astype(BF16), vcat)
            acc = acc + jnp.where(in_head, o / denom, 0.0)
        o_ref[:, cols] = acc


def _attn_call(q, k, v, bias, batch, seq):
    n = q.shape[0]
    tiles = seq // ATTN_TQ
    n_prev = ATTN_PREV // ATTN_TQ
    assert n_prev == 2 and bias.shape[0] == n_prev + 1
    cur = pl.BlockSpec((ATTN_TQ, ATTN_WIDTH), lambda b, t: (b * tiles + t, 0))
    prev = lambda d: pl.BlockSpec((ATTN_TQ, ATTN_WIDTH), lambda b, t: (b * tiles + jnp.maximum(t - d, 0), 0))
    bias_spec = pl.BlockSpec((1, ATTN_HEADS, ATTN_TQ, ATTN_TK), lambda b, t: (jnp.minimum(t, n_prev), 0, 0, 0))
    return pl.pallas_call(
        _attn_kernel,
        grid=(batch, tiles),
        in_specs=[cur, prev(2), prev(1), cur, prev(2), prev(1), cur, bias_spec],
        out_specs=cur,
        out_shape=jax.ShapeDtypeStruct((n, ATTN_WIDTH), F32),
        compiler_params=pltpu.CompilerParams(dimension_semantics=("parallel", "arbitrary"),
                                             vmem_limit_bytes=VMEM_LIMIT_BYTES),
        name="band_attn",
    )(q, k, k, k, v, v, v, bias)


def _attn_bias(rel_bias):
    n_diag = ATTN_TQ + ATTN_TK - 1
    d = (ATTN_TQ - 1 + ATTN_PREV) - jnp.arange(n_diag)
    vec = jnp.take(rel_bias.astype(F32), jnp.clip(d, -MAX_REL, MAX_REL) + MAX_REL, axis=1)
    skew = jnp.tile(jnp.pad(vec, ((0, 0), (0, 1))), (1, ATTN_TQ))[:, :ATTN_TQ * n_diag]
    skew = skew.reshape(ATTN_HEADS, ATTN_TQ, n_diag)
    base = skew[:, :, ATTN_TQ - 1:ATTN_TQ - 1 + ATTN_TK]
    qi = jnp.arange(ATTN_TQ)
    kj = jnp.arange(ATTN_TK)
    q_chunk = qi[:, None] // CHUNK + (BAND_CHUNKS - 1)
    k_chunk = kj[None, :] // CHUNK
    in_band = (k_chunk <= q_chunk) & (k_chunk >= q_chunk - (BAND_CHUNKS - 1))
    tabs = []
    for t in range(ATTN_PREV // ATTN_TQ + 1):
        exists = kj[None, :] >= ATTN_PREV - t * ATTN_TQ
        tabs.append(jnp.where((in_band & exists)[None], base, NEG_INF))
    return jnp.stack(tabs)


def _router_gates(lt):
    tm = lt.shape[1]
    row = lax.broadcasted_iota(jnp.int32, (ROUTER_SLAB, tm), 0)
    gl = lt[0:ROUTER_SLAB]
    gmax = jnp.max(gl, axis=0, keepdims=True)
    p_g = 1.0 / jnp.sum(jnp.exp(gl - gmax), axis=0, keepdims=True)
    g_top = jnp.min(jnp.where(gl == gmax, row, ROUTER_SLAB), axis=0, keepdims=True)
    el = jnp.zeros((ROUTER_SLAB, tm), F32)
    for g in range(N_EXPERT_GROUPS):
        el = el + jnp.where(g_top == g, lt[(g + 1) * ROUTER_SLAB:(g + 2) * ROUTER_SLAB], 0.0)
    ee = jnp.exp(el - jnp.max(el, axis=0, keepdims=True))
    ep = ee / jnp.sum(ee, axis=0, keepdims=True)
    p1 = jnp.max(ep, axis=0, keepdims=True)
    i1 = jnp.min(jnp.where(ep == p1, row, ROUTER_SLAB), axis=0, keepdims=True)
    rest = jnp.where(row == i1, -1.0, ep)
    p2 = jnp.max(rest, axis=0, keepdims=True)
    i2 = jnp.min(jnp.where(rest == p2, row, ROUTER_SLAB), axis=0, keepdims=True)
    tot = p1 + p2
    w = (jnp.where(row == i1, p1 / tot, 0.0) + jnp.where(row == i2, p2 / tot, 0.0)) * p_g
    return [jnp.where(g_top == g, w, 0.0) for g in range(N_EXPERT_GROUPS)], g_top


def _out_kernel(oa_ref, y0_ref, y1_ref, u0_ref, u1_ref, oc_ref, x_ref, d_ref, gluw_ref, glub_ref, og_ref, woutf_ref,
                fg_ref, wr_ref, br_ref, bd_ref, x1_ref, hg_ref, route_ref, cnt_ref, run_ref, wout_ref):
    tm = x_ref.shape[0]

    @pl.when(pl.program_id(0) == 0)
    def _():
        run_ref[...] = jnp.zeros_like(run_ref)
        wout_ref[...] = woutf_ref[...].astype(BF16)

    y = jnp.concatenate([y0_ref[...], y1_ref[...]], axis=-1)
    u = jnp.concatenate([u0_ref[...], u1_ref[...]], axis=-1)
    y = jax.nn.gelu(y + d_ref[...] * u)
    ob = y * jax.nn.sigmoid(_dot(y.astype(BF16), gluw_ref[...]) + glub_ref[...])
    o = jnp.concatenate([oa_ref[...], ob, oc_ref[...]], axis=-1)
    on = o * lax.rsqrt(_group_sumsq(o, bd_ref) * (1.0 / OUT_NORM_GROUP) + EPS) * og_ref[...]
    x1 = x_ref[...] + _dot(on.astype(BF16), wout_ref[...])
    x1_ref[...] = x1
    ms = jnp.mean(x1 * x1, axis=-1, keepdims=True)
    hn = x1 * lax.rsqrt(ms + EPS) * fg_ref[...]
    hg_ref[:, :D_MODEL] = hn
    lt = _dot(hn.astype(BF16), wr_ref[...]).T + br_ref[...]
    slabs, g_top = _router_gates(lt)
    pad = jnp.zeros((LANES - N_EXPERT_GROUPS * ROUTER_SLAB, tm), F32)
    hg_ref[:, D_MODEL:] = jnp.concatenate(slabs + [pad], axis=0).T

    row = lax.broadcasted_iota(jnp.int32, (ROUTER_SLAB, tm), 0)
    lane = lax.broadcasted_iota(jnp.int32, (ROUTER_SLAB, tm), 1)
    member = jnp.where(row == g_top, 1.0, 0.0)
    cum = member
    shift = 1
    while shift < tm:
        cum = cum + jnp.where(lane >= shift, pltpu.roll(cum, shift, 1), 0.0)
        shift *= 2
    run = run_ref[:, 0:1]
    rank = jnp.sum(member * (cum - 1.0 + run), axis=0, keepdims=True)
    total = run + cum[:, tm - 1:tm]
    run_ref[...] = jnp.broadcast_to(total, run_ref.shape)
    cnt_ref[...] = jnp.broadcast_to(total, cnt_ref.shape)
    route_ref[...] = jnp.where(row == 0, g_top, jnp.where(row == 1, rank.astype(jnp.int32), 0))


def _out_call(oa, y0, y1, u0, u1, oc, x, d, gluw, glub, og, wout, fg, wr, br, bd, tm):
    n = x.shape[0]
    row = lambda c: pl.BlockSpec((tm, c), lambda i: (i, 0))
    full = lambda a: pl.BlockSpec(a.shape, lambda i: (0,) * a.ndim)
    return pl.pallas_call(
        _out_kernel,
        grid=(n // tm,),
        in_specs=[row(SGU_WIDTH), row(LANES), row(LANES), row(LANES), row(LANES), row(ATTN_WIDTH), row(D_MODEL),
                  full(d), full(gluw),
                  full(glub), full(og), pl.BlockSpec(wout.shape, lambda i: (0, 0), pipeline_mode=pl.Buffered(1)),
                  full(fg), full(wr), full(br), full(bd)],
        out_specs=[row(D_MODEL), row(MOE_ROW), pl.BlockSpec((ROUTER_SLAB, tm), lambda i: (i, 0)),
                   pl.BlockSpec((ROUTER_SLAB, LANES), lambda i: (0, 0))],
        out_shape=[jax.ShapeDtypeStruct((n, D_MODEL), F32), jax.ShapeDtypeStruct((n, MOE_ROW), F32),
                   jax.ShapeDtypeStruct((n // tm * ROUTER_SLAB, tm), jnp.int32),
                   jax.ShapeDtypeStruct((ROUTER_SLAB, LANES), F32)],
        scratch_shapes=[pltpu.VMEM((ROUTER_SLAB, LANES), F32), pltpu.VMEM(wout.shape, BF16)],
        compiler_params=pltpu.CompilerParams(dimension_semantics=("arbitrary",), vmem_limit_bytes=VMEM_LIMIT_BYTES),
        name="out_proj",
    )(oa, y0, y1, u0, u1, oc, x, d, gluw, glub, og, wout, fg, wr, br, bd)


def _router_tables(wg, bg, we, be):
    w = jnp.zeros((D_MODEL, LANES), F32)
    b = jnp.full((LANES,), NEG_INF, F32)
    w = w.at[:, 0:N_EXPERT_GROUPS].set(wg.astype(F32))
    b = b.at[0:N_EXPERT_GROUPS].set(bg.astype(F32))
    for g in range(N_EXPERT_GROUPS):
        c0 = (g + 1) * ROUTER_SLAB
        w = w.at[:, c0:c0 + EXPERTS_PER_GROUP].set(we[g].astype(F32))
        b = b.at[c0:c0 + EXPERTS_PER_GROUP].set(be[g].astype(F32))
    return w.astype(BF16), b[:, None]


def _route_plan(route, counts, n, tm_out, tm_moe):
    r3 = route.reshape(n // tm_out, ROUTER_SLAB, tm_out)
    group = r3[:, 0, :].reshape(n)
    rank = r3[:, 1, :].reshape(n)
    cnt = counts[:N_EXPERT_GROUPS, 0].astype(jnp.int32)
    tiles_g = (cnt + tm_moe - 1) // tm_moe
    tile_end = jnp.cumsum(tiles_g)
    row_start = (tile_end - tiles_g) * tm_moe
    pos = rank
    for g in range(N_EXPERT_GROUPS):
        pos = pos + jnp.where(group == g, row_start[g], 0)
    n_tiles = n // tm_moe + N_EXPERT_GROUPS - 1
    tile_group = jnp.sum(jnp.arange(n_tiles)[:, None] >= tile_end[None, :], axis=1)
    meta = jnp.concatenate([jnp.minimum(tile_group, N_EXPERT_GROUPS - 1), tile_end[-1:]]).astype(jnp.int32)
    next_start = jnp.concatenate([row_start[1:], jnp.full((1,), n_tiles * tm_moe, jnp.int32)])
    pads = jnp.concatenate([row_start + cnt, next_start - row_start - cnt]).astype(jnp.int32)
    return pos.astype(jnp.int32), meta, pads, n_tiles


def _dispatch_kernel(pads_ref, pos_ref, hg_ref, hs_ref, zero_ref, sem):
    td = hg_ref.shape[0]

    def row_copy(r):
        return pltpu.make_async_copy(hg_ref.at[pl.ds(r, 1)], hs_ref.at[pl.ds(pos_ref[r], 1)], sem)

    def start(r, c):
        row_copy(r).start()
        return c

    def wait(r, c):
        row_copy(r).wait()
        return c

    lax.fori_loop(0, td, start, 0, unroll=8)
    lax.fori_loop(0, td, wait, 0, unroll=8)

    @pl.when(pl.program_id(0) == pl.num_programs(0) - 1)
    def _():
        zero_ref[...] = jnp.zeros_like(zero_ref)
        for g in range(N_EXPERT_GROUPS):
            first = pads_ref[g]
            count = pads_ref[N_EXPERT_GROUPS + g]

            def pad_copy(k):
                return pltpu.make_async_copy(zero_ref.at[pl.ds(0, 1)], hs_ref.at[pl.ds(first + k, 1)], sem)

            def pad_start(k, c):
                pad_copy(k).start()
                return c

            def pad_wait(k, c):
                pad_copy(k).wait()
                return c

            lax.fori_loop(0, count, pad_start, 0)
            lax.fori_loop(0, count, pad_wait, 0)


def _dispatch_call(pads, pos, hg, n_rows, td):
    n = hg.shape[0]
    return pl.pallas_call(
        _dispatch_kernel,
        grid_spec=pltpu.PrefetchScalarGridSpec(
            num_scalar_prefetch=1,
            grid=(n // td,),
            in_specs=[pl.BlockSpec((td,), lambda i, pads: (i,), memory_space=pltpu.SMEM),
                      pl.BlockSpec((td, MOE_ROW), lambda i, pads: (i, 0))],
            out_specs=pl.BlockSpec(memory_space=pl.ANY),
            scratch_shapes=[pltpu.VMEM((SUBLANES, MOE_ROW), F32), pltpu.SemaphoreType.DMA],
        ),
        out_shape=jax.ShapeDtypeStruct((n_rows, MOE_ROW), F32),
        compiler_params=pltpu.CompilerParams(dimension_semantics=("arbitrary",), vmem_limit_bytes=VMEM_LIMIT_BYTES),
        name="moe_dispatch",
    )(pads, pos, hg)


def _moe_group_kernel(meta_ref, hs_ref, wg_ref, wu_ref, wd_ref, o_ref, wgb_ref, wub_ref, wdb_ref):
    j = pl.program_id(0)
    n_used = meta_ref[pl.num_programs(0)]
    group = meta_ref[j]

    @pl.when((j == 0) | (group != meta_ref[jnp.maximum(j - 1, 0)]))
    def _():
        wgb_ref[...] = wg_ref[...].astype(BF16)
        wub_ref[...] = wu_ref[...].astype(BF16)
        wdb_ref[...] = wd_ref[...].astype(BF16)

    @pl.when(j < n_used)
    def _():
        h = hs_ref[:, :D_MODEL].astype(BF16)
        gates = hs_ref[:, D_MODEL:]
        lane = lax.broadcasted_iota(jnp.int32, (1, LANES), 1)
        out = None
        for e in range(EXPERTS_PER_GROUP):
            gt = _dot(h, wgb_ref[e])
            up = _dot(h, wub_ref[e])
            gate = jnp.sum(jnp.where(lane == group * ROUTER_SLAB + e, gates, 0.0), axis=-1, keepdims=True)
            a = (gt * jax.nn.sigmoid(gt)) * up * gate
            part = _dot(a.astype(BF16), wdb_ref[e])
            out = part if out is None else out + part
        o_ref[...] = out

    @pl.when(j >= n_used)
    def _():
        o_ref[...] = jnp.zeros_like(o_ref)


def _moe_group_call(meta, hs, wg, wu, wd, n_tiles, tm):
    by_group = lambda a: pl.BlockSpec((EXPERTS_PER_GROUP,) + a.shape[1:], lambda j, meta: (meta[j], 0, 0))
    return pl.pallas_call(
        _moe_group_kernel,
        grid_spec=pltpu.PrefetchScalarGridSpec(
            num_scalar_prefetch=1,
            grid=(n_tiles,),
            in_specs=[pl.BlockSpec((tm, MOE_ROW), lambda j, meta: (j, 0)),
                      by_group(wg), by_group(wu), by_group(wd)],
            out_specs=pl.BlockSpec((tm, D_MODEL), lambda j, meta: (j, 0)),
            scratch_shapes=[pltpu.VMEM((EXPERTS_PER_GROUP, D_MODEL, D_EXPERT), BF16),
                            pltpu.VMEM((EXPERTS_PER_GROUP, D_MODEL, D_EXPERT), BF16),
                            pltpu.VMEM((EXPERTS_PER_GROUP, D_EXPERT, D_MODEL), BF16)],
        ),
        out_shape=jax.ShapeDtypeStruct((n_tiles * tm, D_MODEL), F32),
        compiler_params=pltpu.CompilerParams(dimension_semantics=("arbitrary",), vmem_limit_bytes=VMEM_LIMIT_BYTES),
        name="moe_experts",
    )(meta, hs, wg, wu, wd)


def _combine_kernel(pos_ref, posn_ref, x1_ref, src_ref, o_ref, buf_ref, sem):
    i = pl.program_id(0)
    tc = x1_ref.shape[0]
    slot = i % 2

    def row_copy(p_ref, s, r):
        return pltpu.make_async_copy(src_ref.at[pl.ds(p_ref[r], 1)], buf_ref.at[s, pl.ds(r, 1)], sem.at[s])

    def gather(p_ref, s):
        def start(r, c):
            row_copy(p_ref, s, r).start()
            return c
        lax.fori_loop(0, tc, start, 0, unroll=8)

    @pl.when(i == 0)
    def _():
        gather(pos_ref, 0)

    @pl.when(i + 1 < pl.num_programs(0))
    def _():
        gather(posn_ref, 1 - slot)

    def wait(r, c):
        row_copy(pos_ref, slot, r).wait()
        return c
    lax.fori_loop(0, tc, wait, 0, unroll=8)
    o_ref[...] = x1_ref[...] + buf_ref[slot]


def _combine_call(pos, x1, src, tc):
    n = x1.shape[0]
    last = n // tc - 1
    return pl.pallas_call(
        _combine_kernel,
        grid=(n // tc,),
        in_specs=[pl.BlockSpec((tc,), lambda i: (i,), memory_space=pltpu.SMEM),
                  pl.BlockSpec((tc,), lambda i: (jnp.minimum(i + 1, last),), memory_space=pltpu.SMEM),
                  pl.BlockSpec((tc, D_MODEL), lambda i: (i, 0)),
                  pl.BlockSpec(memory_space=pl.ANY)],
        out_specs=pl.BlockSpec((tc, D_MODEL), lambda i: (i, 0)),
        out_shape=jax.ShapeDtypeStruct((n, D_MODEL), F32),
        scratch_shapes=[pltpu.VMEM((2, tc, D_MODEL), F32), pltpu.SemaphoreType.DMA((2,))],
        compiler_params=pltpu.CompilerParams(dimension_semantics=("arbitrary",), vmem_limit_bytes=VMEM_LIMIT_BYTES),
        name="moe_combine",
    )(pos, pos, x1, src)


def kernel(x, norm_mix, w_in, sgu_norm, sgu_w, sgu_b, s5_lambda_re, s5_lambda_im, s5_log_dt, s5_b_re, s5_b_im,
           s5_c_re, s5_c_im, s5_d, s5_glu_w, s5_glu_b, q_norm, k_norm, rel_bias, out_norm, w_out, norm_ffn,
           router_group_w, router_group_b, router_expert_w, router_expert_b, w_gate, w_up, w_down):
    batch, seq, _ = x.shape
    n = batch * seq
    depth = w_in.shape[0]
    t = _tiles(n, seq)
    assert seq % ATTN_TQ == 0 and seq % (S5_T * t["s5_rows"]) == 0
    row_vec = lambda a: a.astype(F32)[None, :]

    lane_group = jnp.arange(LANES) // HEAD_DIM
    bd = (lane_group[:, None] == lane_group[None, :]).astype(BF16)
    block_chunk = jnp.arange(SGU_BLOCK) // CHUNK
    sgu_mask = block_chunk[None, :] <= block_chunk[:, None]

    xf = x.reshape(n, D_MODEL)
    for l in range(depth):
        sguw = jnp.where(sgu_mask[None], sgu_w[l], 0).astype(BF16)
        sgub = jnp.repeat(sgu_b[l].astype(F32).T, HEAD_DIM, axis=1)
        qg = row_vec(jnp.tile(q_norm[l], ATTN_HEADS)) * (HEAD_DIM ** -0.5)
        kg = row_vec(jnp.tile(k_norm[l], ATTN_HEADS))
        oa, u0, u1, q, k, v = _in_call(xf, row_vec(norm_mix[l]), w_in[l], row_vec(sgu_norm[l]), sguw, sgub,
                                      qg, kg, bd, t["tm_in"])

        sv, krev, sc, apow = _s5_tables(s5_lambda_re[l], s5_lambda_im[l], s5_log_dt[l], s5_b_re[l], s5_b_im[l],
                                        s5_c_re[l], s5_c_im[l], t["s5_rows"])
        y0, y1 = _s5_call(u0, u1, sv, krev, sc, apow, batch, t["s5_rows"])

        oc = _attn_call(q, k, v, _attn_bias(rel_bias[l]), batch, seq)

        wr, br = _router_tables(router_group_w[l], router_group_b[l], router_expert_w[l], router_expert_b[l])
        x1, hg, route, counts = _out_call(oa, y0, y1, u0, u1, oc, xf, row_vec(s5_d[l]), s5_glu_w[l].astype(BF16), row_vec(s5_glu_b[l]),
                                 row_vec(out_norm[l]), w_out[l], row_vec(norm_ffn[l]), wr, br, bd,
                                 t["tm_out"])

        pos, meta, pads, n_tiles = _route_plan(route, counts, n, t["tm_out"], t["tm_moe"])
        hs = _dispatch_call(pads, pos, hg, n_tiles * t["tm_moe"], t["tm_perm"])
        ys = _moe_group_call(meta, hs, w_gate[l], w_up[l], w_down[l], n_tiles, t["tm_moe"])
        xf = _combine_call(pos, x1, ys, t["tm_perm"])
    return xf.reshape(batch, seq, D_MODEL)
```

```python
import jax
import jax.numpy as jnp
from jax import lax
from jax.experimental import pallas as pl
from jax.experimental.pallas import tpu as pltpu

F32 = jnp.float32
BF16 = jnp.bfloat16

D_MODEL = 1024
CHUNK = 64
HEAD_DIM = 64
SGU_WIDTH = 256
SGU_HEADS = 4
SGU_BLOCK = 128
S5_WIDTH = 256
S5_GROUP = 16
S5_N_GROUPS = 16
S5_STATE = 64
ATTN_WIDTH = 512
ATTN_HEADS = 8
BAND_CHUNKS = 9
MAX_REL = 256
IN_COLS = 2 * SGU_WIDTH + S5_WIDTH + 3 * ATTN_WIDTH
OUT_NORM_GROUP = 64
N_EXPERT_GROUPS = 4
EXPERTS_PER_GROUP = 4
N_EXPERTS = 16
D_EXPERT = 256
EPS = 1e-6
NEG_INF = -1e30

LANES = 128
SUBLANES = 8
VMEM_LIMIT_BYTES = 56 * 1024 * 1024

S5_T = 16
S5_ROW = S5_T * S5_WIDTH
S5_NSTATE = S5_N_GROUPS * S5_STATE
ATTN_TQ = 256
ATTN_PREV = (BAND_CHUNKS - 1) * CHUNK
ATTN_TK = ATTN_TQ + ATTN_PREV
ROUTER_SLAB = SUBLANES
MOE_ROW = D_MODEL + LANES
MOE_UNIT = SUBLANES
MOE_PAD_ROWS = N_EXPERT_GROUPS * MOE_UNIT
SLOT_LANE = N_EXPERT_GROUPS * ROUTER_SLAB


def _tiles(n_tokens, seq):
    def pick(pref, total):
        t = min(pref, total)
        assert total % t == 0
        return t
    return dict(
        tm_in=pick(512, seq),
        tm_out=pick(512, seq),
        s5_rows=pick(128, seq // S5_T),
    )


def _dot(a, b):
    return jnp.dot(a, b, preferred_element_type=F32)


def _dot_nt(a, b):
    return lax.dot_general(a, b, (((1,), (1,)), ((), ())), preferred_element_type=F32)


def _group_sumsq(x, bd_ref):
    x2 = (x * x).astype(BF16)
    parts = [_dot(x2[:, t * LANES:(t + 1) * LANES], bd_ref[...]) for t in range(x.shape[1] // LANES)]
    return jnp.concatenate(parts, axis=-1)


def _in_kernel(x_ref, g_ref, wf_ref, sgug_ref, sguw_ref, sgub_ref, qg_ref, kg_ref, bd_ref,
               oa_ref, u0_ref, u1_ref, q_ref, k_ref, v_ref, w_ref):
    tm = x_ref.shape[0]

    @pl.when(pl.program_id(0) == 0)
    def _():
        w_ref[...] = wf_ref[...].astype(BF16)

    x = x_ref[...]
    ms = jnp.mean(x * x, axis=-1, keepdims=True)
    hn = (x * lax.rsqrt(ms + EPS) * g_ref[...]).astype(BF16)

    z = jax.nn.gelu(_dot(hn, w_ref[:, 0:2 * SGU_WIDTH]))
    u = z[:, :SGU_WIDTH]
    v = z[:, SGU_WIDTH:]
    v = v * lax.rsqrt(jnp.mean(v * v, axis=-1, keepdims=True) + EPS) * sgug_ref[...]
    vb = v.astype(BF16)
    first_head = lax.broadcasted_iota(jnp.int32, (SGU_BLOCK, LANES), 1) < HEAD_DIM
    for r in range(tm // SGU_BLOCK):
        rows = slice(r * SGU_BLOCK, (r + 1) * SGU_BLOCK)
        for p in range(SGU_WIDTH // LANES):
            cols = slice(p * LANES, (p + 1) * LANES)
            vp = vb[rows, cols]
            mixed = jnp.where(first_head, _dot(sguw_ref[2 * p], vp), _dot(sguw_ref[2 * p + 1], vp))
            oa_ref[rows, cols] = u[rows, cols] * (mixed + sgub_ref[:, cols])

    c0 = 2 * SGU_WIDTH
    us = _dot(hn, w_ref[:, c0:c0 + S5_WIDTH])
    u0_ref[...] = us[:, :LANES]
    u1_ref[...] = us[:, LANES:]

    c0 += S5_WIDTH
    q = _dot(hn, w_ref[:, c0:c0 + ATTN_WIDTH])
    q_ref[...] = (q * lax.rsqrt(_group_sumsq(q, bd_ref) * (1.0 / HEAD_DIM) + EPS) * qg_ref[...]).astype(BF16)
    c0 += ATTN_WIDTH
    k = _dot(hn, w_ref[:, c0:c0 + ATTN_WIDTH])
    k_ref[...] = (k * lax.rsqrt(_group_sumsq(k, bd_ref) * (1.0 / HEAD_DIM) + EPS) * kg_ref[...]).astype(BF16)
    c0 += ATTN_WIDTH
    v_ref[...] = _dot(hn, w_ref[:, c0:c0 + ATTN_WIDTH]).astype(BF16)


def _in_call(x, g, w, sgug, sguw, sgub, qg, kg, bd, tm):
    n = x.shape[0]
    row = lambda c: pl.BlockSpec((tm, c), lambda i: (i, 0))
    full = lambda a: pl.BlockSpec(a.shape, lambda i: (0,) * a.ndim)
    once = lambda a: pl.BlockSpec(a.shape, lambda i: (0,) * a.ndim, pipeline_mode=pl.Buffered(1))
    return pl.pallas_call(
        _in_kernel,
        grid=(n // tm,),
        in_specs=[row(D_MODEL), full(g), once(w), full(sgug), full(sguw), full(sgub), full(qg), full(kg), full(bd)],
        out_specs=[row(SGU_WIDTH), row(LANES), row(LANES), row(ATTN_WIDTH), row(ATTN_WIDTH), row(ATTN_WIDTH)],
        out_shape=[jax.ShapeDtypeStruct((n, SGU_WIDTH), F32), jax.ShapeDtypeStruct((n, LANES), F32),
                   jax.ShapeDtypeStruct((n, LANES), F32), jax.ShapeDtypeStruct((n, ATTN_WIDTH), BF16),
                   jax.ShapeDtypeStruct((n, ATTN_WIDTH), BF16), jax.ShapeDtypeStruct((n, ATTN_WIDTH), BF16)],
        scratch_shapes=[pltpu.VMEM(w.shape, BF16)],
        compiler_params=pltpu.CompilerParams(dimension_semantics=("arbitrary",), vmem_limit_bytes=VMEM_LIMIT_BYTES),
        name="in_proj",
    )(x, g, w, sgug, sguw, sgub, qg, kg, bd)


def _s5_expand_tables(sv_ref, sc_ref, wv_ref, wc_ref):
    pair_of_row = lax.broadcasted_iota(jnp.int32, (S5_ROW, 1), 0) // S5_GROUP % S5_N_GROUPS // 2
    for part in range(2):
        src = sv_ref[part]
        for j in range(S5_NSTATE // LANES):
            cols = slice(part * S5_NSTATE + j * LANES, part * S5_NSTATE + (j + 1) * LANES)
            wv_ref[:, cols] = jnp.where(pair_of_row == j, src, 0.0).astype(BF16)

    wc_ref[...] = jnp.zeros_like(wc_ref)
    lane = lax.broadcasted_iota(jnp.int32, (1, LANES), 1)
    per_tile = LANES // S5_GROUP
    for part in range(2):
        for g in range(S5_N_GROUPS):
            rows = slice(part * S5_NSTATE + g * S5_STATE, part * S5_NSTATE + (g + 1) * S5_STATE)
            keep = (lane >= g % per_tile * S5_GROUP) & (lane < (g % per_tile + 1) * S5_GROUP)
            for i in range(S5_T):
                tile = sc_ref[rows, i // per_tile * LANES:(i // per_tile + 1) * LANES]
                shift = (g % per_tile - i % per_tile) * S5_GROUP % LANES
                if shift:
                    tile = pltpu.roll(tile, shift, 1)
                q = i * (S5_WIDTH // LANES) + g // per_tile
                wc_ref[rows, q * LANES:(q + 1) * LANES] = jnp.where(keep, tile, 0.0).astype(BF16)


def _s5_kernel(u0_ref, u1_ref, sv_ref, krev_ref, sc_ref, apow_ref, y0_ref, y1_ref,
               wv_ref, wc_ref, ur_ref, s_ref, sp_ref, carry_ref):
    rows = ur_ref.shape[0]
    n_steps = apow_ref.shape[0]

    @pl.when((pl.program_id(0) == 0) & (pl.program_id(1) == 0))
    def _():
        _s5_expand_tables(sv_ref, sc_ref, wv_ref, wc_ref)

    @pl.when(pl.program_id(1) == 0)
    def _():
        carry_ref[...] = jnp.zeros_like(carry_ref)

    for t in range(S5_T):
        ur_ref[:, t * S5_WIDTH:t * S5_WIDTH + LANES] = u0_ref[pl.ds(t, rows, stride=S5_T), :].astype(BF16)
        ur_ref[:, t * S5_WIDTH + LANES:(t + 1) * S5_WIDTH] = u1_ref[pl.ds(t, rows, stride=S5_T), :].astype(BF16)

    s_ref[...] = _dot(ur_ref[...], wv_ref[...])
    row_id = lax.broadcasted_iota(jnp.int32, (rows, LANES), 0)
    for cb in range(S5_NSTATE // LANES):
        cre = slice(cb * LANES, (cb + 1) * LANES)
        cim = slice(S5_NSTATE + cb * LANES, S5_NSTATE + (cb + 1) * LANES)
        re = s_ref[:, cre]
        im = s_ref[:, cim]
        c_re = carry_ref[0:1, cre]
        c_im = carry_ref[0:1, cim]
        a_re = apow_ref[0, 0:1, cre]
        a_im = apow_ref[0, 1:2, cre]
        re = re + jnp.where(row_id == 0, a_re * c_re - a_im * c_im, 0.0)
        im = im + jnp.where(row_id == 0, a_re * c_im + a_im * c_re, 0.0)
        for k in range(n_steps):
            shift = 1 << k
            a_re = apow_ref[k, 0:1, cre]
            a_im = apow_ref[k, 1:2, cre]
            re_s = jnp.where(row_id >= shift, pltpu.roll(re, shift, 0), 0.0)
            im_s = jnp.where(row_id >= shift, pltpu.roll(im, shift, 0), 0.0)
            re, im = re + a_re * re_s - a_im * im_s, im + a_re * im_s + a_im * re_s
        sp_ref[:, cre] = jnp.where(row_id >= 1, pltpu.roll(re, 1, 0), c_re).astype(BF16)
        sp_ref[:, cim] = jnp.where(row_id >= 1, pltpu.roll(im, 1, 0), c_im).astype(BF16)
        carry_ref[0:1, cre] = re[rows - 1:rows, :]
        carry_ref[0:1, cim] = im[rows - 1:rows, :]

    for i in range(S5_T):
        cols = slice(i * S5_WIDTH, (i + 1) * S5_WIDTH)
        intra = _dot(ur_ref[:, 0:(i + 1) * S5_WIDTH], krev_ref[(S5_T - 1 - i) * S5_WIDTH:, :])
        y = intra + _dot(sp_ref[...], wc_ref[:, cols])
        y0_ref[pl.ds(i, rows, stride=S5_T), :] = y[:, :LANES]
        y1_ref[pl.ds(i, rows, stride=S5_T), :] = y[:, LANES:]


def _s5_call(u0, u1, sv, krev, sc, apow, batch, rows):
    n = u0.shape[0]
    tok = rows * S5_T
    tiles_per_seq = n // batch // tok
    const = lambda a: pl.BlockSpec(a.shape, lambda b, t: (0,) * a.ndim, pipeline_mode=pl.Buffered(1))
    blk = pl.BlockSpec((tok, LANES), lambda b, t: (b * tiles_per_seq + t, 0))
    half = jax.ShapeDtypeStruct((n, LANES), F32)
    return pl.pallas_call(
        _s5_kernel,
        grid=(batch, tiles_per_seq),
        in_specs=[blk, blk, const(sv), const(krev), const(sc), const(apow)],
        out_specs=[blk, blk],
        out_shape=[half, half],
        scratch_shapes=[pltpu.VMEM((S5_ROW, 2 * S5_NSTATE), BF16), pltpu.VMEM((2 * S5_NSTATE, S5_ROW), BF16),
                        pltpu.VMEM((rows, S5_ROW), BF16), pltpu.VMEM((rows, 2 * S5_NSTATE), F32),
                        pltpu.VMEM((rows, 2 * S5_NSTATE), BF16), pltpu.VMEM((SUBLANES, 2 * S5_NSTATE), F32)],
        compiler_params=pltpu.CompilerParams(dimension_semantics=("arbitrary", "arbitrary"),
                                             vmem_limit_bytes=VMEM_LIMIT_BYTES),
        name="s5_scan",
    )(u0, u1, sv, krev, sc, apow)


def _s5_tables(lam_re, lam_im, log_dt, b_re, b_im, c_re, c_im, rows):
    g_, p_, h_ = S5_N_GROUPS, S5_STATE, S5_GROUP
    hi = lax.Precision.HIGHEST
    lam_re, lam_im = lam_re.astype(F32), lam_im.astype(F32)
    dt = jnp.exp(log_dt.astype(F32))[:, None]
    tau = jnp.arange(S5_T + 1, dtype=F32)[:, None, None]
    mag = jnp.exp(tau * (lam_re * dt)[None])
    ap_re = mag * jnp.cos(tau * (lam_im * dt)[None])
    ap_im = mag * jnp.sin(tau * (lam_im * dt)[None])
    n_re, n_im = ap_re[1] - 1.0, ap_im[1]
    den = lam_re * lam_re + lam_im * lam_im
    k_re = ((n_re * lam_re + n_im * lam_im) / den)[..., None]
    k_im = ((n_im * lam_re - n_re * lam_im) / den)[..., None]
    b_re, b_im = b_re.astype(F32), b_im.astype(F32)
    bb_re = k_re * b_re - k_im * b_im
    bb_im = k_re * b_im + k_im * b_re
    c_re, c_im = c_re.astype(F32), c_im.astype(F32)
    ab_re = ap_re[:S5_T, :, :, None] * bb_re[None] - ap_im[:S5_T, :, :, None] * bb_im[None]
    ab_im = ap_re[:S5_T, :, :, None] * bb_im[None] + ap_im[:S5_T, :, :, None] * bb_re[None]

    col_group = jnp.arange(S5_WIDTH) // h_
    chan_group = jnp.arange(S5_ROW) // h_ % g_

    k_tau = (jnp.einsum('gop,tgpi->tgio', c_re, ab_re, precision=hi)
             - jnp.einsum('gop,tgpi->tgio', c_im, ab_im, precision=hi))
    krev = jnp.tile(k_tau[::-1].reshape(S5_ROW, h_), (1, g_))
    krev = jnp.where(chan_group[:, None] == col_group[None, :], krev, 0.0)

    def sv_part(ab):
        w = ab[::-1].transpose(0, 1, 3, 2).reshape(S5_ROW, p_)
        odd = (chan_group % 2 == 1)[:, None]
        return jnp.concatenate([jnp.where(odd, 0.0, w), jnp.where(odd, w, 0.0)], axis=1)
    sv = jnp.stack([sv_part(ab_re), sv_part(ab_im)])

    cp = lambda c: c.transpose(0, 2, 1)[:, :, None, :]
    w_re = cp(c_re) * ap_re[1:].transpose(1, 2, 0)[..., None] - cp(c_im) * ap_im[1:].transpose(1, 2, 0)[..., None]
    w_im = cp(c_re) * ap_im[1:].transpose(1, 2, 0)[..., None] + cp(c_im) * ap_re[1:].transpose(1, 2, 0)[..., None]
    sc = jnp.concatenate([w_re.reshape(S5_NSTATE, S5_T * h_), -w_im.reshape(S5_NSTATE, S5_T * h_)], axis=0)

    n_steps = max(1, (rows - 1).bit_length())
    pows = [(ap_re[S5_T].reshape(S5_NSTATE), ap_im[S5_T].reshape(S5_NSTATE))]
    for _ in range(n_steps - 1):
        r, i = pows[-1]
        pows.append((r * r - i * i, 2.0 * r * i))
    apow = jnp.stack([jnp.stack(p) for p in pows])
    return sv, krev.astype(BF16), sc, apow


def _attn_kernel(q_ref, k0_ref, k1_ref, k2_ref, v0_ref, v1_ref, v2_ref, bias_ref, o_ref):
    lane = lax.broadcasted_iota(jnp.int32, (1, LANES), 1)
    for p in range(ATTN_WIDTH // LANES):
        cols = slice(p * LANES, (p + 1) * LANES)
        qp = q_ref[:, cols]
        kcat = jnp.concatenate([k0_ref[:, cols], k1_ref[:, cols], k2_ref[:, cols]], axis=0)
        vcat = jnp.concatenate([v0_ref[:, cols], v1_ref[:, cols], v2_ref[:, cols]], axis=0)
        acc = jnp.zeros((ATTN_TQ, LANES), F32)
        for hh in range(LANES // HEAD_DIM):
            in_head = (lane >= hh * HEAD_DIM) & (lane < (hh + 1) * HEAD_DIM)
            qm = jnp.where(in_head, qp, jnp.zeros_like(qp))
            s = _dot_nt(qm, kcat) + bias_ref[0, 2 * p + hh]
            m = jnp.max(s, axis=-1, keepdims=True)
            e = jnp.exp(s - m)
            denom = jnp.sum(e, axis=-1, keepdims=True)
            o = _dot(e.astype(BF16), vcat)
            acc = acc + jnp.where(in_head, o / denom, 0.0)
        o_ref[:, cols] = acc


def _attn_call(q, k, v, bias, batch, seq):
    n = q.shape[0]
    tiles = seq // ATTN_TQ
    n_prev = ATTN_PREV // ATTN_TQ
    assert n_prev == 2 and bias.shape[0] == n_prev + 1
    cur = pl.BlockSpec((ATTN_TQ, ATTN_WIDTH), lambda b, t: (b * tiles + t, 0))
    prev = lambda d: pl.BlockSpec((ATTN_TQ, ATTN_WIDTH), lambda b, t: (b * tiles + jnp.maximum(t - d, 0), 0))
    bias_spec = pl.BlockSpec((1, ATTN_HEADS, ATTN_TQ, ATTN_TK), lambda b, t: (jnp.minimum(t, n_prev), 0, 0, 0))
    return pl.pallas_call(
        _attn_kernel,
        grid=(batch, tiles),
        in_specs=[cur, prev(2), prev(1), cur, prev(2), prev(1), cur, bias_spec],
        out_specs=cur,
        out_shape=jax.ShapeDtypeStruct((n, ATTN_WIDTH), F32),
        compiler_params=pltpu.CompilerParams(dimension_semantics=("parallel", "arbitrary"),
                                             vmem_limit_bytes=VMEM_LIMIT_BYTES),
        name="band_attn",
    )(q, k, k, k, v, v, v, bias)


def _attn_bias(rel_bias):
    n_diag = ATTN_TQ + ATTN_TK - 1
    d = (ATTN_TQ - 1 + ATTN_PREV) - jnp.arange(n_diag)
    vec = jnp.take(rel_bias.astype(F32), jnp.clip(d, -MAX_REL, MAX_REL) + MAX_REL, axis=1)
    skew = jnp.tile(jnp.pad(vec, ((0, 0), (0, 1))), (1, ATTN_TQ))[:, :ATTN_TQ * n_diag]
    skew = skew.reshape(ATTN_HEADS, ATTN_TQ, n_diag)
    base = skew[:, :, ATTN_TQ - 1:ATTN_TQ - 1 + ATTN_TK]
    qi = jnp.arange(ATTN_TQ)
    kj = jnp.arange(ATTN_TK)
    q_chunk = qi[:, None] // CHUNK + (BAND_CHUNKS - 1)
    k_chunk = kj[None, :] // CHUNK
    in_band = (k_chunk <= q_chunk) & (k_chunk >= q_chunk - (BAND_CHUNKS - 1))
    tabs = []
    for t in range(ATTN_PREV // ATTN_TQ + 1):
        exists = kj[None, :] >= ATTN_PREV - t * ATTN_TQ
        tabs.append(jnp.where((in_band & exists)[None], base, NEG_INF))
    return jnp.stack(tabs)


def _router_gates(lt):
    tm = lt.shape[1]
    row = lax.broadcasted_iota(jnp.int32, (ROUTER_SLAB, tm), 0)
    gl = lt[0:ROUTER_SLAB]
    gmax = jnp.max(gl, axis=0, keepdims=True)
    p_g = 1.0 / jnp.sum(jnp.exp(gl - gmax), axis=0, keepdims=True)
    g_top = jnp.min(jnp.where(gl == gmax, row, ROUTER_SLAB), axis=0, keepdims=True)
    el = jnp.zeros((ROUTER_SLAB, tm), F32)
    for g in range(N_EXPERT_GROUPS):
        el = el + jnp.where(g_top == g, lt[(g + 1) * ROUTER_SLAB:(g + 2) * ROUTER_SLAB], 0.0)
    ee = jnp.exp(el - jnp.max(el, axis=0, keepdims=True))
    ep = ee / jnp.sum(ee, axis=0, keepdims=True)
    p1 = jnp.max(ep, axis=0, keepdims=True)
    i1 = jnp.min(jnp.where(ep == p1, row, ROUTER_SLAB), axis=0, keepdims=True)
    rest = jnp.where(row == i1, -1.0, ep)
    p2 = jnp.max(rest, axis=0, keepdims=True)
    i2 = jnp.min(jnp.where(rest == p2, row, ROUTER_SLAB), axis=0, keepdims=True)
    tot = p1 + p2
    w = (jnp.where(row == i1, p1 / tot, 0.0) + jnp.where(row == i2, p2 / tot, 0.0)) * p_g
    return [jnp.where(g_top == g, w, 0.0) for g in range(N_EXPERT_GROUPS)], g_top


def _sort_matrix(slot, n_slots, slot_axis):
    shape = (n_slots, slot.shape[1]) if slot_axis == 0 else (slot.shape[0], n_slots)
    return jnp.where(lax.broadcasted_iota(jnp.int32, shape, slot_axis) == slot, 1.0, 0.0).astype(BF16)


def _out_kernel(oa_ref, y0_ref, y1_ref, u0_ref, u1_ref, oc_ref, x_ref, d_ref, gluw_ref, glub_ref, og_ref, woutf_ref,
                fg_ref, wr_ref, br_ref, bd_ref, x1_ref, hs_ref, tok_ref, cnt_ref, wout_ref):
    tm = x_ref.shape[0]
    n_slots = hs_ref.shape[0]

    @pl.when(pl.program_id(0) == 0)
    def _():
        wout_ref[...] = woutf_ref[...].astype(BF16)

    y = jnp.concatenate([y0_ref[...], y1_ref[...]], axis=-1)
    u = jnp.concatenate([u0_ref[...], u1_ref[...]], axis=-1)
    y = jax.nn.gelu(y + d_ref[...] * u)
    ob = y * jax.nn.sigmoid(_dot(y.astype(BF16), gluw_ref[...]) + glub_ref[...])
    o = jnp.concatenate([oa_ref[...], ob, oc_ref[...]], axis=-1)
    on = o * lax.rsqrt(_group_sumsq(o, bd_ref) * (1.0 / OUT_NORM_GROUP) + EPS) * og_ref[...]
    x1 = x_ref[...] + _dot(on.astype(BF16), wout_ref[...])
    x1_ref[...] = x1
    ms = jnp.mean(x1 * x1, axis=-1, keepdims=True)
    hb = (x1 * lax.rsqrt(ms + EPS) * fg_ref[...]).astype(BF16)
    lt = _dot(hb, wr_ref[...]).T + br_ref[...]
    slabs, g_top = _router_gates(lt)

    row = lax.broadcasted_iota(jnp.int32, (ROUTER_SLAB, tm), 0)
    lane = lax.broadcasted_iota(jnp.int32, (ROUTER_SLAB, tm), 1)
    member = jnp.where(row == g_top, 1.0, 0.0)
    cum = member
    shift = 1
    while shift < tm:
        cum = cum + jnp.where(lane >= shift, pltpu.roll(cum, shift, 1), 0.0)
        shift *= 2
    count = cum[:, tm - 1:tm]
    padded = jnp.floor((count + (MOE_UNIT - 1)) * (1.0 / MOE_UNIT)) * MOE_UNIT
    group_row = lax.broadcasted_iota(jnp.int32, (ROUTER_SLAB, 1), 0)
    start = jnp.zeros((ROUTER_SLAB, 1), F32)
    for g in range(1, N_EXPERT_GROUPS):
        start = start + jnp.where(group_row >= g, padded[g - 1:g, :], 0.0)
    slot = jnp.sum(member * (start + cum - 1.0), axis=0, keepdims=True)
    cnt_ref[...] = jnp.broadcast_to(count, cnt_ref.shape)

    pad = jnp.zeros((LANES - SLOT_LANE - ROUTER_SLAB, tm), F32)
    gates = jnp.concatenate(slabs + [jnp.broadcast_to(slot, (ROUTER_SLAB, tm)), pad], axis=0).T
    tok_ref[...] = gates

    perm = _sort_matrix(slot.astype(jnp.int32), n_slots, 0)
    hs_ref[:, :D_MODEL] = _dot(perm, hb)
    g_hi = gates.astype(BF16)
    g_mid = (gates - g_hi.astype(F32)).astype(BF16)
    g_lo = (gates - g_hi.astype(F32) - g_mid.astype(F32)).astype(BF16)
    gs = _dot(perm, jnp.concatenate([g_hi, g_mid, g_lo], axis=-1))
    hs_ref[:, D_MODEL:] = gs[:, :LANES] + gs[:, LANES:2 * LANES] + gs[:, 2 * LANES:]


def _out_call(oa, y0, y1, u0, u1, oc, x, d, gluw, glub, og, wout, fg, wr, br, bd, tm):
    n = x.shape[0]
    n_slots = tm + MOE_PAD_ROWS
    row = lambda c: pl.BlockSpec((tm, c), lambda i: (i, 0))
    full = lambda a: pl.BlockSpec(a.shape, lambda i: (0,) * a.ndim)
    return pl.pallas_call(
        _out_kernel,
        grid=(n // tm,),
        in_specs=[row(SGU_WIDTH), row(LANES), row(LANES), row(LANES), row(LANES), row(ATTN_WIDTH), row(D_MODEL),
                  full(d), full(gluw),
                  full(glub), full(og), pl.BlockSpec(wout.shape, lambda i: (0, 0), pipeline_mode=pl.Buffered(1)),
                  full(fg), full(wr), full(br), full(bd)],
        out_specs=[row(D_MODEL), pl.BlockSpec((n_slots, MOE_ROW), lambda i: (i, 0)), row(LANES),
                   pl.BlockSpec((ROUTER_SLAB, LANES), lambda i: (i, 0))],
        out_shape=[jax.ShapeDtypeStruct((n, D_MODEL), F32), jax.ShapeDtypeStruct((n // tm * n_slots, MOE_ROW), F32),
                   jax.ShapeDtypeStruct((n, LANES), F32),
                   jax.ShapeDtypeStruct((n // tm * ROUTER_SLAB, LANES), F32)],
        scratch_shapes=[pltpu.VMEM(wout.shape, BF16)],
        compiler_params=pltpu.CompilerParams(dimension_semantics=("arbitrary",), vmem_limit_bytes=VMEM_LIMIT_BYTES),
        name="out_proj",
    )(oa, y0, y1, u0, u1, oc, x, d, gluw, glub, og, wout, fg, wr, br, bd)


def _router_tables(wg, bg, we, be):
    w = jnp.zeros((D_MODEL, LANES), F32)
    b = jnp.full((LANES,), NEG_INF, F32)
    w = w.at[:, 0:N_EXPERT_GROUPS].set(wg.astype(F32))
    b = b.at[0:N_EXPERT_GROUPS].set(bg.astype(F32))
    for g in range(N_EXPERT_GROUPS):
        c0 = (g + 1) * ROUTER_SLAB
        w = w.at[:, c0:c0 + EXPERTS_PER_GROUP].set(we[g].astype(F32))
        b = b.at[c0:c0 + EXPERTS_PER_GROUP].set(be[g].astype(F32))
    return w.astype(BF16), b[:, None]


def _route_plan(counts, n, tm):
    n_tiles = n // tm
    units_per_tile = (tm + MOE_PAD_ROWS) // MOE_UNIT
    units_per_mtile = tm // MOE_UNIT
    cnt = counts.reshape(n_tiles, ROUTER_SLAB, LANES)[:, :N_EXPERT_GROUPS, 0].astype(jnp.int32)
    seg = (cnt + MOE_UNIT - 1) // MOE_UNIT
    seg_start = jnp.cumsum(seg, axis=1) - seg
    before = jnp.cumsum(seg, axis=0) - seg
    total = jnp.sum(seg, axis=0)
    mtiles = (total + units_per_mtile - 1) // units_per_mtile
    mtile_end = jnp.cumsum(mtiles)
    group_start = (mtile_end - mtiles) * units_per_mtile
    n_mtiles = n_tiles + N_EXPERT_GROUPS + -(-n_tiles * MOE_PAD_ROWS // tm)
    tile_group = jnp.minimum(jnp.sum(jnp.arange(n_mtiles)[:, None] >= mtile_end[None, :], axis=1),
                             N_EXPERT_GROUPS - 1)
    meta = jnp.concatenate([tile_group, mtile_end[-1:]]).astype(jnp.int32)
    pick = lambda table, g: jnp.sum(jnp.where(g[..., None] == jnp.arange(N_EXPERT_GROUPS), table, 0), axis=-1)

    q = jnp.arange(n_mtiles * units_per_mtile)
    g = tile_group[q // units_per_mtile]
    ql = q - group_start[g]
    ends_g = (before + seg).T[g]
    tile_of = jnp.minimum(jnp.sum(ql[:, None] >= ends_g, axis=1), n_tiles - 1)
    local = pick(seg_start[tile_of], g) + ql - pick(before[tile_of], g)
    valid = (ql < total[g]) & (q // units_per_mtile < mtile_end[-1])
    src = jnp.where(valid, tile_of * units_per_tile + local, units_per_tile - 1).astype(jnp.int32)

    ul = jnp.arange(units_per_tile)[None, :, None]
    seg_of = jnp.sum(ul >= (seg_start + seg)[:, None, :], axis=-1)
    gi = jnp.minimum(seg_of, N_EXPERT_GROUPS - 1)
    glob = group_start[gi] + pick(before[:, None, :], gi) + ul[..., 0] - pick(seg_start[:, None, :], gi)
    dst = jnp.where(seg_of < N_EXPERT_GROUPS, glob, 0).astype(jnp.int32).reshape(-1)
    return meta, src, dst, n_mtiles


def _unit_gather(table_ref, first, n_units, src_ref, dst_ref, sem, wait):
    def body(k, c):
        u = pl.multiple_of(table_ref[first + k] * MOE_UNIT, MOE_UNIT)
        cp = pltpu.make_async_copy(src_ref.at[pl.ds(u, MOE_UNIT)],
                                   dst_ref.at[pl.ds(pl.multiple_of(k * MOE_UNIT, MOE_UNIT), MOE_UNIT)], sem)
        if wait:
            cp.wait()
        else:
            cp.start()
        return c
    lax.fori_loop(0, n_units, body, 0)


def _moe_group_kernel(meta_ref, src_ref, hs_ref, wg_ref, wu_ref, wd_ref, o_ref, wgb_ref, wub_ref, wdb_ref,
                      buf_ref, sem):
    j = pl.program_id(0)
    n_steps = pl.num_programs(0)
    n_used = meta_ref[n_steps]
    group = meta_ref[j]
    n_units = o_ref.shape[0] // MOE_UNIT
    slot = j % 2

    @pl.when(j == 0)
    def _():
        _unit_gather(src_ref, 0, n_units, hs_ref, buf_ref.at[0], sem.at[0], wait=False)

    @pl.when(j + 1 < n_steps)
    def _():
        _unit_gather(src_ref, (j + 1) * n_units, n_units, hs_ref, buf_ref.at[1 - slot], sem.at[1 - slot], wait=False)

    @pl.when((j == 0) | (group != meta_ref[jnp.maximum(j - 1, 0)]))
    def _():
        wgb_ref[...] = wg_ref[...].astype(BF16)
        wub_ref[...] = wu_ref[...].astype(BF16)
        wdb_ref[...] = wd_ref[...].astype(BF16)

    _unit_gather(src_ref, j * n_units, n_units, hs_ref, buf_ref.at[slot], sem.at[slot], wait=True)

    @pl.when(j < n_used)
    def _():
        h = buf_ref[slot, :, :D_MODEL].astype(BF16)
        gates = buf_ref[slot, :, D_MODEL:]
        lane = lax.broadcasted_iota(jnp.int32, (1, LANES), 1)
        out = None
        for e in range(EXPERTS_PER_GROUP):
            gt = _dot(h, wgb_ref[e])
            up = _dot(h, wub_ref[e])
            gate = jnp.sum(jnp.where(lane == group * ROUTER_SLAB + e, gates, 0.0), axis=-1, keepdims=True)
            a = (gt * jax.nn.sigmoid(gt)) * up * gate
            part = _dot(a.astype(BF16), wdb_ref[e])
            out = part if out is None else out + part
        o_ref[...] = out

    @pl.when(j >= n_used)
    def _():
        o_ref[...] = jnp.zeros_like(o_ref)


def _moe_group_call(meta, src, hs, wg, wu, wd, n_mtiles, tm):
    by_group = lambda a: pl.BlockSpec((EXPERTS_PER_GROUP,) + a.shape[1:], lambda j, meta, src: (meta[j], 0, 0))
    return pl.pallas_call(
        _moe_group_kernel,
        grid_spec=pltpu.PrefetchScalarGridSpec(
            num_scalar_prefetch=2,
            grid=(n_mtiles,),
            in_specs=[pl.BlockSpec(memory_space=pl.ANY), by_group(wg), by_group(wu), by_group(wd)],
            out_specs=pl.BlockSpec((tm, D_MODEL), lambda j, meta, src: (j, 0)),
            scratch_shapes=[pltpu.VMEM((EXPERTS_PER_GROUP, D_MODEL, D_EXPERT), BF16),
                            pltpu.VMEM((EXPERTS_PER_GROUP, D_MODEL, D_EXPERT), BF16),
                            pltpu.VMEM((EXPERTS_PER_GROUP, D_EXPERT, D_MODEL), BF16),
                            pltpu.VMEM((2, tm, MOE_ROW), F32), pltpu.SemaphoreType.DMA((2,))],
        ),
        out_shape=jax.ShapeDtypeStruct((n_mtiles * tm, D_MODEL), F32),
        compiler_params=pltpu.CompilerParams(dimension_semantics=("arbitrary",), vmem_limit_bytes=VMEM_LIMIT_BYTES),
        name="moe_experts",
    )(meta, src, hs, wg, wu, wd)


def _combine_kernel(dst_ref, x1_ref, tok_ref, ys_ref, o_ref, buf_ref, sem):
    i = pl.program_id(0)
    n_slots = buf_ref.shape[1]
    n_units = n_slots // MOE_UNIT
    slot = i % 2

    @pl.when(i == 0)
    def _():
        _unit_gather(dst_ref, 0, n_units, ys_ref, buf_ref.at[0], sem.at[0], wait=False)

    @pl.when(i + 1 < pl.num_programs(0))
    def _():
        _unit_gather(dst_ref, (i + 1) * n_units, n_units, ys_ref, buf_ref.at[1 - slot], sem.at[1 - slot], wait=False)

    _unit_gather(dst_ref, i * n_units, n_units, ys_ref, buf_ref.at[slot], sem.at[slot], wait=True)

    y = buf_ref[slot]
    y_hi = y.astype(BF16)
    y_lo = (y - y_hi.astype(F32)).astype(BF16)
    unsort = _sort_matrix(tok_ref[:, SLOT_LANE:SLOT_LANE + 1].astype(jnp.int32), n_slots, 1)
    o_ref[...] = x1_ref[...] + _dot(unsort, y_hi) + _dot(unsort, y_lo)


def _combine_call(dst, x1, tok, ys, tm):
    n = x1.shape[0]
    n_slots = tm + MOE_PAD_ROWS
    return pl.pallas_call(
        _combine_kernel,
        grid_spec=pltpu.PrefetchScalarGridSpec(
            num_scalar_prefetch=1,
            grid=(n // tm,),
            in_specs=[pl.BlockSpec((tm, D_MODEL), lambda i, dst: (i, 0)),
                      pl.BlockSpec((tm, LANES), lambda i, dst: (i, 0)),
                      pl.BlockSpec(memory_space=pl.ANY)],
            out_specs=pl.BlockSpec((tm, D_MODEL), lambda i, dst: (i, 0)),
            scratch_shapes=[pltpu.VMEM((2, n_slots, D_MODEL), F32), pltpu.SemaphoreType.DMA((2,))],
        ),
        out_shape=jax.ShapeDtypeStruct((n, D_MODEL), F32),
        compiler_params=pltpu.CompilerParams(dimension_semantics=("arbitrary",), vmem_limit_bytes=VMEM_LIMIT_BYTES),
        name="moe_combine",
    )(dst, x1, tok, ys)


def kernel(x, norm_mix, w_in, sgu_norm, sgu_w, sgu_b, s5_lambda_re, s5_lambda_im, s5_log_dt, s5_b_re, s5_b_im,
           s5_c_re, s5_c_im, s5_d, s5_glu_w, s5_glu_b, q_norm, k_norm, rel_bias, out_norm, w_out, norm_ffn,
           router_group_w, router_group_b, router_expert_w, router_expert_b, w_gate, w_up, w_down):
    batch, seq, _ = x.shape
    n = batch * seq
    depth = w_in.shape[0]
    t = _tiles(n, seq)
    assert seq % ATTN_TQ == 0 and seq % (S5_T * t["s5_rows"]) == 0
    row_vec = lambda a: a.astype(F32)[None, :]

    lane_group = jnp.arange(LANES) // HEAD_DIM
    bd = (lane_group[:, None] == lane_group[None, :]).astype(BF16)
    block_chunk = jnp.arange(SGU_BLOCK) // CHUNK
    sgu_mask = block_chunk[None, :] <= block_chunk[:, None]

    xf = x.reshape(n, D_MODEL)
    for l in range(depth):
        sguw = jnp.where(sgu_mask[None], sgu_w[l], 0).astype(BF16)
        sgub = jnp.repeat(sgu_b[l].astype(F32).T, HEAD_DIM, axis=1)
        qg = row_vec(jnp.tile(q_norm[l], ATTN_HEADS)) * (HEAD_DIM ** -0.5)
        kg = row_vec(jnp.tile(k_norm[l], ATTN_HEADS))
        oa, u0, u1, q, k, v = _in_call(xf, row_vec(norm_mix[l]), w_in[l], row_vec(sgu_norm[l]), sguw, sgub,
                                      qg, kg, bd, t["tm_in"])

        sv, krev, sc, apow = _s5_tables(s5_lambda_re[l], s5_lambda_im[l], s5_log_dt[l], s5_b_re[l], s5_b_im[l],
                                        s5_c_re[l], s5_c_im[l], t["s5_rows"])
        y0, y1 = _s5_call(u0, u1, sv, krev, sc, apow, batch, t["s5_rows"])

        oc = _attn_call(q, k, v, _attn_bias(rel_bias[l]), batch, seq)

        wr, br = _router_tables(router_group_w[l], router_group_b[l], router_expert_w[l], router_expert_b[l])
        x1, hs, tok, counts = _out_call(oa, y0, y1, u0, u1, oc, xf, row_vec(s5_d[l]), s5_glu_w[l].astype(BF16), row_vec(s5_glu_b[l]),
                                 row_vec(out_norm[l]), w_out[l], row_vec(norm_ffn[l]), wr, br, bd,
                                 t["tm_out"])

        meta, src, dst, n_mtiles = _route_plan(counts, n, t["tm_out"])
        ys = _moe_group_call(meta, src, hs, w_gate[l], w_up[l], w_down[l], n_mtiles, t["tm_out"])
        xf = _combine_call(dst, x1, tok, ys, t["tm_out"])
    return xf.reshape(batch, seq, D_MODEL)
```

```python
import jax
import jax.numpy as jnp
from jax import lax
from jax.experimental import pallas as pl
from jax.experimental.pallas import tpu as pltpu

F32 = jnp.float32
BF16 = jnp.bfloat16

D_MODEL = 1024
CHUNK = 64
HEAD_DIM = 64
SGU_WIDTH = 256
SGU_HEADS = 4
SGU_BLOCK = 128
S5_WIDTH = 256
S5_GROUP = 16
S5_N_GROUPS = 16
S5_STATE = 64
ATTN_WIDTH = 512
ATTN_HEADS = 8
BAND_CHUNKS = 9
MAX_REL = 256
IN_COLS = 2 * SGU_WIDTH + S5_WIDTH + 3 * ATTN_WIDTH
OUT_NORM_GROUP = 64
N_EXPERT_GROUPS = 4
EXPERTS_PER_GROUP = 4
N_EXPERTS = 16
D_EXPERT = 256
EPS = 1e-6
NEG_INF = -1e30

LANES = 128
SUBLANES = 8
VMEM_LIMIT_BYTES = 56 * 1024 * 1024

S5_T = 16
S5_ROW = S5_T * S5_WIDTH
S5_NSTATE = S5_N_GROUPS * S5_STATE
ATTN_TQ = 256
ATTN_PREV = (BAND_CHUNKS - 1) * CHUNK
ATTN_TK = ATTN_TQ + ATTN_PREV
ROUTER_SLAB = SUBLANES
MOE_ROW = D_MODEL + LANES
MOE_UNIT = SUBLANES
MOE_PAD_ROWS = N_EXPERT_GROUPS * MOE_UNIT
SLOT_LANE = N_EXPERT_GROUPS * ROUTER_SLAB


def _tiles(n_tokens, seq):
    def pick(pref, total):
        t = min(pref, total)
        assert total % t == 0
        return t
    return dict(
        tm_in=pick(512, seq),
        tm_out=pick(512, seq),
        s5_rows=pick(128, seq // S5_T),
    )


def _dot(a, b):
    return jnp.dot(a, b, preferred_element_type=F32)


def _dot_nt(a, b):
    return lax.dot_general(a, b, (((1,), (1,)), ((), ())), preferred_element_type=F32)


def _group_sumsq(x, bd_ref):
    x2 = (x * x).astype(BF16)
    parts = [_dot(x2[:, t * LANES:(t + 1) * LANES], bd_ref[...]) for t in range(x.shape[1] // LANES)]
    return jnp.concatenate(parts, axis=-1)


def _in_kernel(x_ref, g_ref, wf_ref, sgug_ref, sguw_ref, sgub_ref, qg_ref, kg_ref, bd_ref,
               oa_ref, u0_ref, u1_ref, q_ref, k_ref, v_ref, w_ref):
    tm = x_ref.shape[0]

    @pl.when(pl.program_id(0) == 0)
    def _():
        w_ref[...] = wf_ref[...].astype(BF16)

    x = x_ref[...]
    ms = jnp.mean(x * x, axis=-1, keepdims=True)
    hn = (x * lax.rsqrt(ms + EPS) * g_ref[...]).astype(BF16)

    z = jax.nn.gelu(_dot(hn, w_ref[:, 0:2 * SGU_WIDTH]))
    u = z[:, :SGU_WIDTH]
    v = z[:, SGU_WIDTH:]
    v = v * lax.rsqrt(jnp.mean(v * v, axis=-1, keepdims=True) + EPS) * sgug_ref[...]
    vb = v.astype(BF16)
    first_head = lax.broadcasted_iota(jnp.int32, (SGU_BLOCK, LANES), 1) < HEAD_DIM
    for r in range(tm // SGU_BLOCK):
        rows = slice(r * SGU_BLOCK, (r + 1) * SGU_BLOCK)
        for p in range(SGU_WIDTH // LANES):
            cols = slice(p * LANES, (p + 1) * LANES)
            vp = vb[rows, cols]
            mixed = jnp.where(first_head, _dot(sguw_ref[2 * p], vp), _dot(sguw_ref[2 * p + 1], vp))
            oa_ref[rows, cols] = u[rows, cols] * (mixed + sgub_ref[:, cols])

    c0 = 2 * SGU_WIDTH
    us = _dot(hn, w_ref[:, c0:c0 + S5_WIDTH])
    u0_ref[...] = us[:, :LANES]
    u1_ref[...] = us[:, LANES:]

    c0 += S5_WIDTH
    q = _dot(hn, w_ref[:, c0:c0 + ATTN_WIDTH])
    q_ref[...] = (q * lax.rsqrt(_group_sumsq(q, bd_ref) * (1.0 / HEAD_DIM) + EPS) * qg_ref[...]).astype(BF16)
    c0 += ATTN_WIDTH
    k = _dot(hn, w_ref[:, c0:c0 + ATTN_WIDTH])
    k_ref[...] = (k * lax.rsqrt(_group_sumsq(k, bd_ref) * (1.0 / HEAD_DIM) + EPS) * kg_ref[...]).astype(BF16)
    c0 += ATTN_WIDTH
    v_ref[...] = _dot(hn, w_ref[:, c0:c0 + ATTN_WIDTH]).astype(BF16)


def _in_call(x, g, w, sgug, sguw, sgub, qg, kg, bd, tm):
    n = x.shape[0]
    row = lambda c: pl.BlockSpec((tm, c), lambda i: (i, 0))
    full = lambda a: pl.BlockSpec(a.shape, lambda i: (0,) * a.ndim)
    once = lambda a: pl.BlockSpec(a.shape, lambda i: (0,) * a.ndim, pipeline_mode=pl.Buffered(1))
    return pl.pallas_call(
        _in_kernel,
        grid=(n // tm,),
        in_specs=[row(D_MODEL), full(g), once(w), full(sgug), full(sguw), full(sgub), full(qg), full(kg), full(bd)],
        out_specs=[row(SGU_WIDTH), row(LANES), row(LANES), row(ATTN_WIDTH), row(ATTN_WIDTH), row(ATTN_WIDTH)],
        out_shape=[jax.ShapeDtypeStruct((n, SGU_WIDTH), F32), jax.ShapeDtypeStruct((n, LANES), F32),
                   jax.ShapeDtypeStruct((n, LANES), F32), jax.ShapeDtypeStruct((n, ATTN_WIDTH), BF16),
                   jax.ShapeDtypeStruct((n, ATTN_WIDTH), BF16), jax.ShapeDtypeStruct((n, ATTN_WIDTH), BF16)],
        scratch_shapes=[pltpu.VMEM(w.shape, BF16)],
        compiler_params=pltpu.CompilerParams(dimension_semantics=("arbitrary",), vmem_limit_bytes=VMEM_LIMIT_BYTES),
        name="in_proj",
    )(x, g, w, sgug, sguw, sgub, qg, kg, bd)


def _s5_expand_tables(sv_ref, sc_ref, wv_ref, wc_ref):
    pair_of_row = lax.broadcasted_iota(jnp.int32, (S5_ROW, 1), 0) // S5_GROUP % S5_N_GROUPS // 2
    for part in range(2):
        src = sv_ref[part]
        for j in range(S5_NSTATE // LANES):
            cols = slice(part * S5_NSTATE + j * LANES, part * S5_NSTATE + (j + 1) * LANES)
            wv_ref[:, cols] = jnp.where(pair_of_row == j, src, 0.0).astype(BF16)

    wc_ref[...] = jnp.zeros_like(wc_ref)
    lane = lax.broadcasted_iota(jnp.int32, (1, LANES), 1)
    per_tile = LANES // S5_GROUP
    for part in range(2):
        for g in range(S5_N_GROUPS):
            rows = slice(part * S5_NSTATE + g * S5_STATE, part * S5_NSTATE + (g + 1) * S5_STATE)
            keep = (lane >= g % per_tile * S5_GROUP) & (lane < (g % per_tile + 1) * S5_GROUP)
            for i in range(S5_T):
                tile = sc_ref[rows, i // per_tile * LANES:(i // per_tile + 1) * LANES]
                shift = (g % per_tile - i % per_tile) * S5_GROUP % LANES
                if shift:
                    tile = pltpu.roll(tile, shift, 1)
                q = i * (S5_WIDTH // LANES) + g // per_tile
                wc_ref[rows, q * LANES:(q + 1) * LANES] = jnp.where(keep, tile, 0.0).astype(BF16)


def _s5_kernel(u0_ref, u1_ref, sv_ref, krev_ref, sc_ref, apow_ref, y0_ref, y1_ref,
               wv_ref, wc_ref, ur_ref, s_ref, sp_ref, carry_ref):
    rows = ur_ref.shape[0]
    n_steps = apow_ref.shape[0]

    @pl.when((pl.program_id(0) == 0) & (pl.program_id(1) == 0))
    def _():
        _s5_expand_tables(sv_ref, sc_ref, wv_ref, wc_ref)

    @pl.when(pl.program_id(1) == 0)
    def _():
        carry_ref[...] = jnp.zeros_like(carry_ref)

    for t in range(S5_T):
        ur_ref[:, t * S5_WIDTH:t * S5_WIDTH + LANES] = u0_ref[pl.ds(t, rows, stride=S5_T), :].astype(BF16)
        ur_ref[:, t * S5_WIDTH + LANES:(t + 1) * S5_WIDTH] = u1_ref[pl.ds(t, rows, stride=S5_T), :].astype(BF16)

    s_ref[...] = _dot(ur_ref[...], wv_ref[...])
    row_id = lax.broadcasted_iota(jnp.int32, (rows, LANES), 0)
    for cb in range(S5_NSTATE // LANES):
        cre = slice(cb * LANES, (cb + 1) * LANES)
        cim = slice(S5_NSTATE + cb * LANES, S5_NSTATE + (cb + 1) * LANES)
        re = s_ref[:, cre]
        im = s_ref[:, cim]
        c_re = carry_ref[0:1, cre]
        c_im = carry_ref[0:1, cim]
        a_re = apow_ref[0, 0:1, cre]
        a_im = apow_ref[0, 1:2, cre]
        re = re + jnp.where(row_id == 0, a_re * c_re - a_im * c_im, 0.0)
        im = im + jnp.where(row_id == 0, a_re * c_im + a_im * c_re, 0.0)
        for k in range(n_steps):
            shift = 1 << k
            a_re = apow_ref[k, 0:1, cre]
            a_im = apow_ref[k, 1:2, cre]
            re_s = jnp.where(row_id >= shift, pltpu.roll(re, shift, 0), 0.0)
            im_s = jnp.where(row_id >= shift, pltpu.roll(im, shift, 0), 0.0)
            re, im = re + a_re * re_s - a_im * im_s, im + a_re * im_s + a_im * re_s
        sp_ref[:, cre] = jnp.where(row_id >= 1, pltpu.roll(re, 1, 0), c_re).astype(BF16)
        sp_ref[:, cim] = jnp.where(row_id >= 1, pltpu.roll(im, 1, 0), c_im).astype(BF16)
        carry_ref[0:1, cre] = re[rows - 1:rows, :]
        carry_ref[0:1, cim] = im[rows - 1:rows, :]

    for i in range(S5_T):
        cols = slice(i * S5_WIDTH, (i + 1) * S5_WIDTH)
        intra = _dot(ur_ref[:, 0:(i + 1) * S5_WIDTH], krev_ref[(S5_T - 1 - i) * S5_WIDTH:, :])
        y = intra + _dot(sp_ref[...], wc_ref[:, cols])
        y0_ref[pl.ds(i, rows, stride=S5_T), :] = y[:, :LANES]
        y1_ref[pl.ds(i, rows, stride=S5_T), :] = y[:, LANES:]


def _s5_call(u0, u1, sv, krev, sc, apow, batch, rows):
    n = u0.shape[0]
    tok = rows * S5_T
    tiles_per_seq = n // batch // tok
    const = lambda a: pl.BlockSpec(a.shape, lambda b, t: (0,) * a.ndim, pipeline_mode=pl.Buffered(1))
    blk = pl.BlockSpec((tok, LANES), lambda b, t: (b * tiles_per_seq + t, 0))
    half = jax.ShapeDtypeStruct((n, LANES), F32)
    return pl.pallas_call(
        _s5_kernel,
        grid=(batch, tiles_per_seq),
        in_specs=[blk, blk, const(sv), const(krev), const(sc), const(apow)],
        out_specs=[blk, blk],
        out_shape=[half, half],
        scratch_shapes=[pltpu.VMEM((S5_ROW, 2 * S5_NSTATE), BF16), pltpu.VMEM((2 * S5_NSTATE, S5_ROW), BF16),
                        pltpu.VMEM((rows, S5_ROW), BF16), pltpu.VMEM((rows, 2 * S5_NSTATE), F32),
                        pltpu.VMEM((rows, 2 * S5_NSTATE), BF16), pltpu.VMEM((SUBLANES, 2 * S5_NSTATE), F32)],
        compiler_params=pltpu.CompilerParams(dimension_semantics=("arbitrary", "arbitrary"),
                                             vmem_limit_bytes=VMEM_LIMIT_BYTES),
        name="s5_scan",
    )(u0, u1, sv, krev, sc, apow)


def _s5_tables(lam_re, lam_im, log_dt, b_re, b_im, c_re, c_im, rows):
    g_, p_, h_ = S5_N_GROUPS, S5_STATE, S5_GROUP
    hi = lax.Precision.HIGHEST
    lam_re, lam_im = lam_re.astype(F32), lam_im.astype(F32)
    dt = jnp.exp(log_dt.astype(F32))[:, None]
    tau = jnp.arange(S5_T + 1, dtype=F32)[:, None, None]
    mag = jnp.exp(tau * (lam_re * dt)[None])
    ap_re = mag * jnp.cos(tau * (lam_im * dt)[None])
    ap_im = mag * jnp.sin(tau * (lam_im * dt)[None])
    n_re, n_im = ap_re[1] - 1.0, ap_im[1]
    den = lam_re * lam_re + lam_im * lam_im
    k_re = ((n_re * lam_re + n_im * lam_im) / den)[..., None]
    k_im = ((n_im * lam_re - n_re * lam_im) / den)[..., None]
    b_re, b_im = b_re.astype(F32), b_im.astype(F32)
    bb_re = k_re * b_re - k_im * b_im
    bb_im = k_re * b_im + k_im * b_re
    c_re, c_im = c_re.astype(F32), c_im.astype(F32)
    ab_re = ap_re[:S5_T, :, :, None] * bb_re[None] - ap_im[:S5_T, :, :, None] * bb_im[None]
    ab_im = ap_re[:S5_T, :, :, None] * bb_im[None] + ap_im[:S5_T, :, :, None] * bb_re[None]

    col_group = jnp.arange(S5_WIDTH) // h_
    chan_group = jnp.arange(S5_ROW) // h_ % g_

    k_tau = (jnp.einsum('gop,tgpi->tgio', c_re, ab_re, precision=hi)
             - jnp.einsum('gop,tgpi->tgio', c_im, ab_im, precision=hi))
    krev = jnp.tile(k_tau[::-1].reshape(S5_ROW, h_), (1, g_))
    krev = jnp.where(chan_group[:, None] == col_group[None, :], krev, 0.0)

    def sv_part(ab):
        w = ab[::-1].transpose(0, 1, 3, 2).reshape(S5_ROW, p_)
        odd = (chan_group % 2 == 1)[:, None]
        return jnp.concatenate([jnp.where(odd, 0.0, w), jnp.where(odd, w, 0.0)], axis=1)
    sv = jnp.stack([sv_part(ab_re), sv_part(ab_im)])

    cp = lambda c: c.transpose(0, 2, 1)[:, :, None, :]
    w_re = cp(c_re) * ap_re[1:].transpose(1, 2, 0)[..., None] - cp(c_im) * ap_im[1:].transpose(1, 2, 0)[..., None]
    w_im = cp(c_re) * ap_im[1:].transpose(1, 2, 0)[..., None] + cp(c_im) * ap_re[1:].transpose(1, 2, 0)[..., None]
    sc = jnp.concatenate([w_re.reshape(S5_NSTATE, S5_T * h_), -w_im.reshape(S5_NSTATE, S5_T * h_)], axis=0)

    n_steps = max(1, (rows - 1).bit_length())
    pows = [(ap_re[S5_T].reshape(S5_NSTATE), ap_im[S5_T].reshape(S5_NSTATE))]
    for _ in range(n_steps - 1):
        r, i = pows[-1]
        pows.append((r * r - i * i, 2.0 * r * i))
    apow = jnp.stack([jnp.stack(p) for p in pows])
    return sv, krev.astype(BF16), sc, apow


def _attn_fill_bias(diag_ref, bias_ref):
    width = diag_ref.shape[1]
    qi = lax.broadcasted_iota(jnp.int32, (ATTN_TQ, ATTN_TK), 0)
    kj = lax.broadcasted_iota(jnp.int32, (ATTN_TQ, ATTN_TK), 1)
    q_chunk = qi // CHUNK + (BAND_CHUNKS - 1)
    k_chunk = kj // CHUNK
    in_band = (k_chunk <= q_chunk) & (k_chunk >= q_chunk - (BAND_CHUNKS - 1))
    for h in range(ATTN_HEADS):
        rows = jnp.broadcast_to(diag_ref[h:h + 1, :], (ATTN_TQ, width))
        base = pltpu.roll(rows, 0, 1, stride=1, stride_axis=0)[:, :ATTN_TK]
        for t in range(bias_ref.shape[0]):
            bias_ref[t, h] = jnp.where(in_band & (kj >= ATTN_PREV - t * ATTN_TQ), base, NEG_INF)


def _attn_kernel(q_ref, k0_ref, k1_ref, k2_ref, v0_ref, v1_ref, v2_ref, diag_ref, o_ref, bias_ref):
    @pl.when((pl.program_id(0) == 0) & (pl.program_id(1) == 0))
    def _():
        _attn_fill_bias(diag_ref, bias_ref)

    table = jnp.minimum(pl.program_id(1), bias_ref.shape[0] - 1)
    lane = lax.broadcasted_iota(jnp.int32, (1, LANES), 1)
    for p in range(ATTN_WIDTH // LANES):
        cols = slice(p * LANES, (p + 1) * LANES)
        qp = q_ref[:, cols]
        kcat = jnp.concatenate([k0_ref[:, cols], k1_ref[:, cols], k2_ref[:, cols]], axis=0)
        vcat = jnp.concatenate([v0_ref[:, cols], v1_ref[:, cols], v2_ref[:, cols]], axis=0)
        acc = jnp.zeros((ATTN_TQ, LANES), F32)
        for hh in range(LANES // HEAD_DIM):
            in_head = (lane >= hh * HEAD_DIM) & (lane < (hh + 1) * HEAD_DIM)
            qm = jnp.where(in_head, qp, jnp.zeros_like(qp))
            s = _dot_nt(qm, kcat) + bias_ref[table, 2 * p + hh]
            m = jnp.max(s, axis=-1, keepdims=True)
            e = jnp.exp(s - m)
            denom = jnp.sum(e, axis=-1, keepdims=True)
            o = _dot(e.astype(BF16), vcat)
            acc = acc + jnp.where(in_head, o / denom, 0.0)
        o_ref[:, cols] = acc


def _attn_call(q, k, v, diag, batch, seq):
    n = q.shape[0]
    tiles = seq // ATTN_TQ
    n_prev = ATTN_PREV // ATTN_TQ
    cur = pl.BlockSpec((ATTN_TQ, ATTN_WIDTH), lambda b, t: (b * tiles + t, 0))
    prev = lambda d: pl.BlockSpec((ATTN_TQ, ATTN_WIDTH), lambda b, t: (b * tiles + jnp.maximum(t - d, 0), 0))
    assert n_prev == 2
    return pl.pallas_call(
        _attn_kernel,
        grid=(batch, tiles),
        in_specs=[cur, prev(2), prev(1), cur, prev(2), prev(1), cur, pl.BlockSpec(diag.shape, lambda b, t: (0, 0))],
        out_specs=cur,
        out_shape=jax.ShapeDtypeStruct((n, ATTN_WIDTH), F32),
        scratch_shapes=[pltpu.VMEM((n_prev + 1, ATTN_HEADS, ATTN_TQ, ATTN_TK), F32)],
        compiler_params=pltpu.CompilerParams(dimension_semantics=("arbitrary", "arbitrary"),
                                             vmem_limit_bytes=VMEM_LIMIT_BYTES),
        name="band_attn",
    )(q, k, k, k, v, v, v, diag)


def _attn_diag(rel_bias):
    width = ATTN_TQ + ATTN_TK
    n = jnp.arange(width)
    offset = jnp.where(n < ATTN_TK, n, n - width)
    rel = jnp.clip(ATTN_PREV - offset, -MAX_REL, MAX_REL) + MAX_REL
    return jnp.take(rel_bias.astype(F32), rel, axis=1)


def _router_gates(lt):
    tm = lt.shape[1]
    row = lax.broadcasted_iota(jnp.int32, (ROUTER_SLAB, tm), 0)
    gl = lt[0:ROUTER_SLAB]
    gmax = jnp.max(gl, axis=0, keepdims=True)
    p_g = 1.0 / jnp.sum(jnp.exp(gl - gmax), axis=0, keepdims=True)
    g_top = jnp.min(jnp.where(gl == gmax, row, ROUTER_SLAB), axis=0, keepdims=True)
    el = jnp.zeros((ROUTER_SLAB, tm), F32)
    for g in range(N_EXPERT_GROUPS):
        el = el + jnp.where(g_top == g, lt[(g + 1) * ROUTER_SLAB:(g + 2) * ROUTER_SLAB], 0.0)
    ee = jnp.exp(el - jnp.max(el, axis=0, keepdims=True))
    ep = ee / jnp.sum(ee, axis=0, keepdims=True)
    p1 = jnp.max(ep, axis=0, keepdims=True)
    i1 = jnp.min(jnp.where(ep == p1, row, ROUTER_SLAB), axis=0, keepdims=True)
    rest = jnp.where(row == i1, -1.0, ep)
    p2 = jnp.max(rest, axis=0, keepdims=True)
    i2 = jnp.min(jnp.where(rest == p2, row, ROUTER_SLAB), axis=0, keepdims=True)
    tot = p1 + p2
    w = (jnp.where(row == i1, p1 / tot, 0.0) + jnp.where(row == i2, p2 / tot, 0.0)) * p_g
    return [jnp.where(g_top == g, w, 0.0) for g in range(N_EXPERT_GROUPS)], g_top


def _sort_matrix(slot, n_slots, slot_axis):
    shape = (n_slots, slot.shape[1]) if slot_axis == 0 else (slot.shape[0], n_slots)
    return jnp.where(lax.broadcasted_iota(jnp.int32, shape, slot_axis) == slot, 1.0, 0.0).astype(BF16)


def _out_kernel(oa_ref, y0_ref, y1_ref, u0_ref, u1_ref, oc_ref, x_ref, d_ref, gluw_ref, glub_ref, og_ref, woutf_ref,
                fg_ref, wr_ref, br_ref, bd_ref, x1_ref, hs_ref, tok_ref, cnt_ref, wout_ref):
    tm = x_ref.shape[0]
    n_slots = hs_ref.shape[0]

    @pl.when(pl.program_id(0) == 0)
    def _():
        wout_ref[...] = woutf_ref[...].astype(BF16)

    y = jnp.concatenate([y0_ref[...], y1_ref[...]], axis=-1)
    u = jnp.concatenate([u0_ref[...], u1_ref[...]], axis=-1)
    y = jax.nn.gelu(y + d_ref[...] * u)
    ob = y * jax.nn.sigmoid(_dot(y.astype(BF16), gluw_ref[...]) + glub_ref[...])
    o = jnp.concatenate([oa_ref[...], ob, oc_ref[...]], axis=-1)
    on = o * lax.rsqrt(_group_sumsq(o, bd_ref) * (1.0 / OUT_NORM_GROUP) + EPS) * og_ref[...]
    x1 = x_ref[...] + _dot(on.astype(BF16), wout_ref[...])
    x1_ref[...] = x1
    ms = jnp.mean(x1 * x1, axis=-1, keepdims=True)
    hb = (x1 * lax.rsqrt(ms + EPS) * fg_ref[...]).astype(BF16)
    lt = _dot(hb, wr_ref[...]).T + br_ref[...]
    slabs, g_top = _router_gates(lt)

    row = lax.broadcasted_iota(jnp.int32, (ROUTER_SLAB, tm), 0)
    lane = lax.broadcasted_iota(jnp.int32, (ROUTER_SLAB, tm), 1)
    member = jnp.where(row == g_top, 1.0, 0.0)
    cum = member
    shift = 1
    while shift < tm:
        cum = cum + jnp.where(lane >= shift, pltpu.roll(cum, shift, 1), 0.0)
        shift *= 2
    count = cum[:, tm - 1:tm]
    padded = jnp.floor((count + (MOE_UNIT - 1)) * (1.0 / MOE_UNIT)) * MOE_UNIT
    group_row = lax.broadcasted_iota(jnp.int32, (ROUTER_SLAB, 1), 0)
    start = jnp.zeros((ROUTER_SLAB, 1), F32)
    for g in range(1, N_EXPERT_GROUPS):
        start = start + jnp.where(group_row >= g, padded[g - 1:g, :], 0.0)
    slot = jnp.sum(member * (start + cum - 1.0), axis=0, keepdims=True)
    cnt_ref[...] = jnp.broadcast_to(count, cnt_ref.shape)

    pad = jnp.zeros((LANES - SLOT_LANE - ROUTER_SLAB, tm), F32)
    gates = jnp.concatenate(slabs + [jnp.broadcast_to(slot, (ROUTER_SLAB, tm)), pad], axis=0).T
    tok_ref[...] = gates

    perm = _sort_matrix(slot.astype(jnp.int32), n_slots, 0)
    hs_ref[:, :D_MODEL] = _dot(perm, hb)
    g_hi = gates.astype(BF16)
    g_mid = (gates - g_hi.astype(F32)).astype(BF16)
    g_lo = (gates - g_hi.astype(F32) - g_mid.astype(F32)).astype(BF16)
    gs = _dot(perm, jnp.concatenate([g_hi, g_mid, g_lo], axis=-1))
    hs_ref[:, D_MODEL:] = gs[:, :LANES] + gs[:, LANES:2 * LANES] + gs[:, 2 * LANES:]


def _out_call(oa, y0, y1, u0, u1, oc, x, d, gluw, glub, og, wout, fg, wr, br, bd, tm):
    n = x.shape[0]
    n_slots = tm + MOE_PAD_ROWS
    row = lambda c: pl.BlockSpec((tm, c), lambda i: (i, 0))
    full = lambda a: pl.BlockSpec(a.shape, lambda i: (0,) * a.ndim)
    return pl.pallas_call(
        _out_kernel,
        grid=(n // tm,),
        in_specs=[row(SGU_WIDTH), row(LANES), row(LANES), row(LANES), row(LANES), row(ATTN_WIDTH), row(D_MODEL),
                  full(d), full(gluw),
                  full(glub), full(og), pl.BlockSpec(wout.shape, lambda i: (0, 0), pipeline_mode=pl.Buffered(1)),
                  full(fg), full(wr), full(br), full(bd)],
        out_specs=[row(D_MODEL), pl.BlockSpec((n_slots, MOE_ROW), lambda i: (i, 0)), row(LANES),
                   pl.BlockSpec((ROUTER_SLAB, LANES), lambda i: (i, 0))],
        out_shape=[jax.ShapeDtypeStruct((n, D_MODEL), F32), jax.ShapeDtypeStruct((n // tm * n_slots, MOE_ROW), F32),
                   jax.ShapeDtypeStruct((n, LANES), F32),
                   jax.ShapeDtypeStruct((n // tm * ROUTER_SLAB, LANES), F32)],
        scratch_shapes=[pltpu.VMEM(wout.shape, BF16)],
        compiler_params=pltpu.CompilerParams(dimension_semantics=("arbitrary",), vmem_limit_bytes=VMEM_LIMIT_BYTES),
        name="out_proj",
    )(oa, y0, y1, u0, u1, oc, x, d, gluw, glub, og, wout, fg, wr, br, bd)


def _router_tables(wg, bg, we, be):
    w = jnp.zeros((D_MODEL, LANES), F32)
    b = jnp.full((LANES,), NEG_INF, F32)
    w = w.at[:, 0:N_EXPERT_GROUPS].set(wg.astype(F32))
    b = b.at[0:N_EXPERT_GROUPS].set(bg.astype(F32))
    for g in range(N_EXPERT_GROUPS):
        c0 = (g + 1) * ROUTER_SLAB
        w = w.at[:, c0:c0 + EXPERTS_PER_GROUP].set(we[g].astype(F32))
        b = b.at[c0:c0 + EXPERTS_PER_GROUP].set(be[g].astype(F32))
    return w.astype(BF16), b[:, None]


def _route_plan(counts, n, tm):
    n_tiles = n // tm
    units_per_tile = (tm + MOE_PAD_ROWS) // MOE_UNIT
    units_per_mtile = tm // MOE_UNIT
    cnt = counts.reshape(n_tiles, ROUTER_SLAB, LANES)[:, :N_EXPERT_GROUPS, 0].astype(jnp.int32)
    seg = (cnt + MOE_UNIT - 1) // MOE_UNIT
    seg_start = jnp.cumsum(seg, axis=1) - seg
    before = jnp.cumsum(seg, axis=0) - seg
    total = jnp.sum(seg, axis=0)
    mtiles = (total + units_per_mtile - 1) // units_per_mtile
    mtile_end = jnp.cumsum(mtiles)
    group_start = (mtile_end - mtiles) * units_per_mtile
    n_mtiles = n_tiles + N_EXPERT_GROUPS + -(-n_tiles * MOE_PAD_ROWS // tm)
    tile_group = jnp.minimum(jnp.sum(jnp.arange(n_mtiles)[:, None] >= mtile_end[None, :], axis=1),
                             N_EXPERT_GROUPS - 1)
    meta = jnp.concatenate([tile_group, mtile_end[-1:]]).astype(jnp.int32)
    groups = jnp.arange(N_EXPERT_GROUPS)
    pick = lambda table, g: jnp.sum(jnp.where(g[..., None] == groups, table, 0), axis=-1)

    q = jnp.arange(n_mtiles * units_per_mtile)
    g = jnp.repeat(tile_group, units_per_mtile)
    ql = q - pick(group_start, g)
    ends_g = pick((before + seg)[None], g[:, None])
    tile_of = jnp.minimum(jnp.sum(ql[:, None] >= ends_g, axis=1), n_tiles - 1)
    offset_g = pick((seg_start - before)[None], g[:, None])
    local = ql + jnp.sum(jnp.where(tile_of[:, None] == jnp.arange(n_tiles), offset_g, 0), axis=1)
    valid = (ql < pick(total, g)) & (q // units_per_mtile < mtile_end[-1])
    src = jnp.where(valid, tile_of * units_per_tile + local, units_per_tile - 1).astype(jnp.int32)

    ul = jnp.arange(units_per_tile)[None, :, None]
    seg_of = jnp.sum(ul >= (seg_start + seg)[:, None, :], axis=-1)
    gi = jnp.minimum(seg_of, N_EXPERT_GROUPS - 1)
    glob = pick((group_start + before - seg_start)[:, None, :], gi) + ul[..., 0]
    dst = jnp.where(seg_of < N_EXPERT_GROUPS, glob, 0).astype(jnp.int32).reshape(-1)
    return meta, src, dst, n_mtiles


def _unit_gather(table_ref, first, src_ref, dst_ref, sem, wait):
    n_rows = dst_ref.shape[0]
    if wait:
        pltpu.make_async_copy(src_ref.at[pl.ds(0, n_rows)], dst_ref, sem).wait()
        return

    def body(k, c):
        u = pl.multiple_of(table_ref[first + k] * MOE_UNIT, MOE_UNIT)
        pltpu.make_async_copy(src_ref.at[pl.ds(u, MOE_UNIT)],
                              dst_ref.at[pl.ds(pl.multiple_of(k * MOE_UNIT, MOE_UNIT), MOE_UNIT)], sem).start()
        return c
    lax.fori_loop(0, n_rows // MOE_UNIT, body, 0, unroll=4)


def _moe_group_kernel(meta_ref, src_ref, hs_ref, wg_ref, wu_ref, wd_ref, o_ref, wgb_ref, wub_ref, wdb_ref,
                      buf_ref, sem):
    j = pl.program_id(0)
    n_steps = pl.num_programs(0)
    n_used = meta_ref[n_steps]
    group = meta_ref[j]
    n_units = o_ref.shape[0] // MOE_UNIT
    slot = j % 2

    @pl.when(j == 0)
    def _():
        _unit_gather(src_ref, 0, hs_ref, buf_ref.at[0], sem.at[0], wait=False)

    @pl.when(j + 1 < n_steps)
    def _():
        _unit_gather(src_ref, (j + 1) * n_units, hs_ref, buf_ref.at[1 - slot], sem.at[1 - slot], wait=False)

    @pl.when((j == 0) | (group != meta_ref[jnp.maximum(j - 1, 0)]))
    def _():
        wgb_ref[...] = wg_ref[...].astype(BF16)
        wub_ref[...] = wu_ref[...].astype(BF16)
        wdb_ref[...] = wd_ref[...].astype(BF16)

    _unit_gather(src_ref, j * n_units, hs_ref, buf_ref.at[slot], sem.at[slot], wait=True)

    @pl.when(j < n_used)
    def _():
        h = buf_ref[slot, :, :D_MODEL].astype(BF16)
        gates = buf_ref[slot, :, D_MODEL:]
        lane = lax.broadcasted_iota(jnp.int32, (1, LANES), 1)
        out = None
        for e in range(EXPERTS_PER_GROUP):
            gt = _dot(h, wgb_ref[e])
            up = _dot(h, wub_ref[e])
            gate = jnp.sum(jnp.where(lane == group * ROUTER_SLAB + e, gates, 0.0), axis=-1, keepdims=True)
            a = (gt * jax.nn.sigmoid(gt)) * up * gate
            part = _dot(a.astype(BF16), wdb_ref[e])
            out = part if out is None else out + part
        o_ref[...] = out

    @pl.when(j >= n_used)
    def _():
        o_ref[...] = jnp.zeros_like(o_ref)


def _moe_group_call(meta, src, hs, wg, wu, wd, n_mtiles, tm):
    by_group = lambda a: pl.BlockSpec((EXPERTS_PER_GROUP,) + a.shape[1:], lambda j, meta, src: (meta[j], 0, 0))
    return pl.pallas_call(
        _moe_group_kernel,
        grid_spec=pltpu.PrefetchScalarGridSpec(
            num_scalar_prefetch=2,
            grid=(n_mtiles,),
            in_specs=[pl.BlockSpec(memory_space=pl.ANY), by_group(wg), by_group(wu), by_group(wd)],
            out_specs=pl.BlockSpec((tm, D_MODEL), lambda j, meta, src: (j, 0)),
            scratch_shapes=[pltpu.VMEM((EXPERTS_PER_GROUP, D_MODEL, D_EXPERT), BF16),
                            pltpu.VMEM((EXPERTS_PER_GROUP, D_MODEL, D_EXPERT), BF16),
                            pltpu.VMEM((EXPERTS_PER_GROUP, D_EXPERT, D_MODEL), BF16),
                            pltpu.VMEM((2, tm, MOE_ROW), F32), pltpu.SemaphoreType.DMA((2,))],
        ),
        out_shape=jax.ShapeDtypeStruct((n_mtiles * tm, D_MODEL), F32),
        compiler_params=pltpu.CompilerParams(dimension_semantics=("arbitrary",), vmem_limit_bytes=VMEM_LIMIT_BYTES),
        name="moe_experts",
    )(meta, src, hs, wg, wu, wd)


def _combine_kernel(dst_ref, x1_ref, tok_ref, ys_ref, o_ref, buf_ref, sem):
    i = pl.program_id(0)
    n_slots = buf_ref.shape[1]
    n_units = n_slots // MOE_UNIT
    slot = i % 2

    @pl.when(i == 0)
    def _():
        _unit_gather(dst_ref, 0, ys_ref, buf_ref.at[0], sem.at[0], wait=False)

    @pl.when(i + 1 < pl.num_programs(0))
    def _():
        _unit_gather(dst_ref, (i + 1) * n_units, ys_ref, buf_ref.at[1 - slot], sem.at[1 - slot], wait=False)

    _unit_gather(dst_ref, i * n_units, ys_ref, buf_ref.at[slot], sem.at[slot], wait=True)

    y = buf_ref[slot]
    y_hi = y.astype(BF16)
    y_lo = (y - y_hi.astype(F32)).astype(BF16)
    unsort = _sort_matrix(tok_ref[:, SLOT_LANE:SLOT_LANE + 1].astype(jnp.int32), n_slots, 1)
    o_ref[...] = x1_ref[...] + _dot(unsort, y_hi) + _dot(unsort, y_lo)


def _combine_call(dst, x1, tok, ys, tm):
    n = x1.shape[0]
    n_slots = tm + MOE_PAD_ROWS
    return pl.pallas_call(
        _combine_kernel,
        grid_spec=pltpu.PrefetchScalarGridSpec(
            num_scalar_prefetch=1,
            grid=(n // tm,),
            in_specs=[pl.BlockSpec((tm, D_MODEL), lambda i, dst: (i, 0)),
                      pl.BlockSpec((tm, LANES), lambda i, dst: (i, 0)),
                      pl.BlockSpec(memory_space=pl.ANY)],
            out_specs=pl.BlockSpec((tm, D_MODEL), lambda i, dst: (i, 0)),
            scratch_shapes=[pltpu.VMEM((2, n_slots, D_MODEL), F32), pltpu.SemaphoreType.DMA((2,))],
        ),
        out_shape=jax.ShapeDtypeStruct((n, D_MODEL), F32),
        compiler_params=pltpu.CompilerParams(dimension_semantics=("arbitrary",), vmem_limit_bytes=VMEM_LIMIT_BYTES),
        name="moe_combine",
    )(dst, x1, tok, ys)


def kernel(x, norm_mix, w_in, sgu_norm, sgu_w, sgu_b, s5_lambda_re, s5_lambda_im, s5_log_dt, s5_b_re, s5_b_im,
           s5_c_re, s5_c_im, s5_d, s5_glu_w, s5_glu_b, q_norm, k_norm, rel_bias, out_norm, w_out, norm_ffn,
           router_group_w, router_group_b, router_expert_w, router_expert_b, w_gate, w_up, w_down):
    batch, seq, _ = x.shape
    n = batch * seq
    depth = w_in.shape[0]
    t = _tiles(n, seq)
    assert seq % ATTN_TQ == 0 and seq % (S5_T * t["s5_rows"]) == 0
    row_vec = lambda a: a.astype(F32)[None, :]

    lane_group = jnp.arange(LANES) // HEAD_DIM
    bd = (lane_group[:, None] == lane_group[None, :]).astype(BF16)
    block_chunk = jnp.arange(SGU_BLOCK) // CHUNK
    sgu_mask = block_chunk[None, :] <= block_chunk[:, None]

    xf = x.reshape(n, D_MODEL)
    for l in range(depth):
        sguw = jnp.where(sgu_mask[None], sgu_w[l], 0).astype(BF16)
        sgub = jnp.repeat(sgu_b[l].astype(F32).T, HEAD_DIM, axis=1)
        qg = row_vec(jnp.tile(q_norm[l], ATTN_HEADS)) * (HEAD_DIM ** -0.5)
        kg = row_vec(jnp.tile(k_norm[l], ATTN_HEADS))
        oa, u0, u1, q, k, v = _in_call(xf, row_vec(norm_mix[l]), w_in[l], row_vec(sgu_norm[l]), sguw, sgub,
                                      qg, kg, bd, t["tm_in"])

        sv, krev, sc, apow = _s5_tables(s5_lambda_re[l], s5_lambda_im[l], s5_log_dt[l], s5_b_re[l], s5_b_im[l],
                                        s5_c_re[l], s5_c_im[l], t["s5_rows"])
        y0, y1 = _s5_call(u0, u1, sv, krev, sc, apow, batch, t["s5_rows"])

        oc = _attn_call(q, k, v, _attn_diag(rel_bias[l]), batch, seq)

        wr, br = _router_tables(router_group_w[l], router_group_b[l], router_expert_w[l], router_expert_b[l])
        x1, hs, tok, counts = _out_call(oa, y0, y1, u0, u1, oc, xf, row_vec(s5_d[l]), s5_glu_w[l].astype(BF16), row_vec(s5_glu_b[l]),
                                 row_vec(out_norm[l]), w_out[l], row_vec(norm_ffn[l]), wr, br, bd,
                                 t["tm_out"])

        meta, src, dst, n_mtiles = _route_plan(counts, n, t["tm_out"])
        ys = _moe_group_call(meta, src, hs, w_gate[l], w_up[l], w_down[l], n_mtiles, t["tm_out"])
        xf = _combine_call(dst, x1, tok, ys, t["tm_out"])
    return xf.reshape(batch, seq, D_MODEL)
```

```python
import jax
import jax.numpy as jnp
from jax import lax
from jax.experimental import pallas as pl
from jax.experimental.pallas import tpu as pltpu

F32 = jnp.float32
BF16 = jnp.bfloat16

D_MODEL = 1024
CHUNK = 64
HEAD_DIM = 64
SGU_WIDTH = 256
SGU_HEADS = 4
SGU_BLOCK = 128
S5_WIDTH = 256
S5_GROUP = 16
S5_N_GROUPS = 16
S5_STATE = 64
ATTN_WIDTH = 512
ATTN_HEADS = 8
BAND_CHUNKS = 9
MAX_REL = 256
IN_COLS = 2 * SGU_WIDTH + S5_WIDTH + 3 * ATTN_WIDTH
OUT_NORM_GROUP = 64
N_EXPERT_GROUPS = 4
EXPERTS_PER_GROUP = 4
N_EXPERTS = 16
D_EXPERT = 256
EPS = 1e-6
NEG_INF = -1e30

LANES = 128
SUBLANES = 8
VMEM_LIMIT_BYTES = 56 * 1024 * 1024

S5_T = 16
S5_ROW = S5_T * S5_WIDTH
S5_NSTATE = S5_N_GROUPS * S5_STATE
ATTN_TQ = 256
ATTN_PREV = (BAND_CHUNKS - 1) * CHUNK
ATTN_TK = ATTN_TQ + ATTN_PREV
ROUTER_SLAB = SUBLANES
MOE_ROW = D_MODEL + LANES
MOE_UNIT = SUBLANES
MOE_PAD_ROWS = N_EXPERT_GROUPS * MOE_UNIT
SLOT_LANE = N_EXPERT_GROUPS * ROUTER_SLAB


def _tiles(n_tokens, seq):
    def pick(pref, total):
        t = min(pref, total)
        assert total % t == 0
        return t
    return dict(
        tm_in=pick(512, seq),
        tm_out=pick(512, seq),
        s5_rows=pick(128, seq // S5_T),
    )


def _dot(a, b):
    return jnp.dot(a, b, preferred_element_type=F32)


def _dot_nt(a, b):
    return lax.dot_general(a, b, (((1,), (1,)), ((), ())), preferred_element_type=F32)


def _group_sumsq(x, bd_ref):
    x2 = (x * x).astype(BF16)
    parts = [_dot(x2[:, t * LANES:(t + 1) * LANES], bd_ref[...]) for t in range(x.shape[1] // LANES)]
    return jnp.concatenate(parts, axis=-1)


def _in_kernel(x_ref, g_ref, wf_ref, sgug_ref, sguw_ref, sgub_ref, qg_ref, kg_ref, bd_ref,
               oa_ref, u0_ref, u1_ref, q_ref, k_ref, v_ref, w_ref):
    tm = x_ref.shape[0]

    @pl.when(pl.program_id(0) == 0)
    def _():
        w_ref[...] = wf_ref[0].astype(BF16)

    x = x_ref[...]
    ms = jnp.mean(x * x, axis=-1, keepdims=True)
    hn = (x * lax.rsqrt(ms + EPS) * g_ref[...]).astype(BF16)

    z = jax.nn.gelu(_dot(hn, w_ref[:, 0:2 * SGU_WIDTH]))
    u = z[:, :SGU_WIDTH]
    v = z[:, SGU_WIDTH:]
    v = v * lax.rsqrt(jnp.mean(v * v, axis=-1, keepdims=True) + EPS) * sgug_ref[...]
    vb = v.astype(BF16)
    first_head = lax.broadcasted_iota(jnp.int32, (SGU_BLOCK, LANES), 1) < HEAD_DIM
    for r in range(tm // SGU_BLOCK):
        rows = slice(r * SGU_BLOCK, (r + 1) * SGU_BLOCK)
        for p in range(SGU_WIDTH // LANES):
            cols = slice(p * LANES, (p + 1) * LANES)
            vp = vb[rows, cols]
            mixed = jnp.where(first_head, _dot(sguw_ref[2 * p], vp), _dot(sguw_ref[2 * p + 1], vp))
            oa_ref[rows, cols] = u[rows, cols] * (mixed + sgub_ref[:, cols])

    c0 = 2 * SGU_WIDTH
    us = _dot(hn, w_ref[:, c0:c0 + S5_WIDTH])
    u0_ref[...] = us[:, :LANES]
    u1_ref[...] = us[:, LANES:]

    c0 += S5_WIDTH
    q = _dot(hn, w_ref[:, c0:c0 + ATTN_WIDTH])
    q_ref[...] = (q * lax.rsqrt(_group_sumsq(q, bd_ref) * (1.0 / HEAD_DIM) + EPS) * qg_ref[...]).astype(BF16)
    c0 += ATTN_WIDTH
    k = _dot(hn, w_ref[:, c0:c0 + ATTN_WIDTH])
    k_ref[...] = (k * lax.rsqrt(_group_sumsq(k, bd_ref) * (1.0 / HEAD_DIM) + EPS) * kg_ref[...]).astype(BF16)
    c0 += ATTN_WIDTH
    v_ref[...] = _dot(hn, w_ref[:, c0:c0 + ATTN_WIDTH]).astype(BF16)


def _in_call(x, g, w, layer, sgug, sguw, sgub, qg, kg, bd, tm):
    n = x.shape[0]
    row = lambda c: pl.BlockSpec((tm, c), lambda i: (i, 0))
    full = lambda a: pl.BlockSpec(a.shape, lambda i: (0,) * a.ndim)
    w_layer = pl.BlockSpec((1,) + w.shape[1:], lambda i: (layer, 0, 0), pipeline_mode=pl.Buffered(1))
    return pl.pallas_call(
        _in_kernel,
        grid=(n // tm,),
        in_specs=[row(D_MODEL), full(g), w_layer, full(sgug), full(sguw), full(sgub), full(qg), full(kg), full(bd)],
        out_specs=[row(SGU_WIDTH), row(LANES), row(LANES), row(ATTN_WIDTH), row(ATTN_WIDTH), row(ATTN_WIDTH)],
        out_shape=[jax.ShapeDtypeStruct((n, SGU_WIDTH), F32), jax.ShapeDtypeStruct((n, LANES), F32),
                   jax.ShapeDtypeStruct((n, LANES), F32), jax.ShapeDtypeStruct((n, ATTN_WIDTH), BF16),
                   jax.ShapeDtypeStruct((n, ATTN_WIDTH), BF16), jax.ShapeDtypeStruct((n, ATTN_WIDTH), BF16)],
        scratch_shapes=[pltpu.VMEM(w.shape[1:], BF16)],
        compiler_params=pltpu.CompilerParams(dimension_semantics=("arbitrary",), vmem_limit_bytes=VMEM_LIMIT_BYTES),
        name="in_proj",
    )(x, g, w, sgug, sguw, sgub, qg, kg, bd)


def _s5_expand_tables(sv_ref, sc_ref, wv_ref, wc_ref):
    pair_of_row = lax.broadcasted_iota(jnp.int32, (S5_ROW, 1), 0) // S5_GROUP % S5_N_GROUPS // 2
    for part in range(2):
        src = sv_ref[part]
        for j in range(S5_NSTATE // LANES):
            cols = slice(part * S5_NSTATE + j * LANES, part * S5_NSTATE + (j + 1) * LANES)
            wv_ref[:, cols] = jnp.where(pair_of_row == j, src, 0.0).astype(BF16)

    wc_ref[...] = jnp.zeros_like(wc_ref)
    lane = lax.broadcasted_iota(jnp.int32, (1, LANES), 1)
    per_tile = LANES // S5_GROUP
    for part in range(2):
        for g in range(S5_N_GROUPS):
            rows = slice(part * S5_NSTATE + g * S5_STATE, part * S5_NSTATE + (g + 1) * S5_STATE)
            keep = (lane >= g % per_tile * S5_GROUP) & (lane < (g % per_tile + 1) * S5_GROUP)
            for i in range(S5_T):
                tile = sc_ref[rows, i // per_tile * LANES:(i // per_tile + 1) * LANES]
                shift = (g % per_tile - i % per_tile) * S5_GROUP % LANES
                if shift:
                    tile = pltpu.roll(tile, shift, 1)
                q = i * (S5_WIDTH // LANES) + g // per_tile
                wc_ref[rows, q * LANES:(q + 1) * LANES] = jnp.where(keep, tile, 0.0).astype(BF16)


def _s5_kernel(u0_ref, u1_ref, sv_ref, krev_ref, sc_ref, apow_ref, y0_ref, y1_ref,
               wv_ref, wc_ref, ur_ref, s_ref, sp_ref, carry_ref):
    rows = ur_ref.shape[0]
    n_steps = apow_ref.shape[0]

    @pl.when((pl.program_id(0) == 0) & (pl.program_id(1) == 0))
    def _():
        _s5_expand_tables(sv_ref, sc_ref, wv_ref, wc_ref)

    @pl.when(pl.program_id(1) == 0)
    def _():
        carry_ref[...] = jnp.zeros_like(carry_ref)

    for t in range(S5_T):
        ur_ref[:, t * S5_WIDTH:t * S5_WIDTH + LANES] = u0_ref[pl.ds(t, rows, stride=S5_T), :].astype(BF16)
        ur_ref[:, t * S5_WIDTH + LANES:(t + 1) * S5_WIDTH] = u1_ref[pl.ds(t, rows, stride=S5_T), :].astype(BF16)

    s_ref[...] = _dot(ur_ref[...], wv_ref[...])
    row_id = lax.broadcasted_iota(jnp.int32, (rows, LANES), 0)
    for cb in range(S5_NSTATE // LANES):
        cre = slice(cb * LANES, (cb + 1) * LANES)
        cim = slice(S5_NSTATE + cb * LANES, S5_NSTATE + (cb + 1) * LANES)
        re = s_ref[:, cre]
        im = s_ref[:, cim]
        c_re = carry_ref[0:1, cre]
        c_im = carry_ref[0:1, cim]
        a_re = apow_ref[0, 0:1, cre]
        a_im = apow_ref[0, 1:2, cre]
        re = re + jnp.where(row_id == 0, a_re * c_re - a_im * c_im, 0.0)
        im = im + jnp.where(row_id == 0, a_re * c_im + a_im * c_re, 0.0)
        for k in range(n_steps):
            shift = 1 << k
            a_re = apow_ref[k, 0:1, cre]
            a_im = apow_ref[k, 1:2, cre]
            re_s = jnp.where(row_id >= shift, pltpu.roll(re, shift, 0), 0.0)
            im_s = jnp.where(row_id >= shift, pltpu.roll(im, shift, 0), 0.0)
            re, im = re + a_re * re_s - a_im * im_s, im + a_re * im_s + a_im * re_s
        sp_ref[:, cre] = jnp.where(row_id >= 1, pltpu.roll(re, 1, 0), c_re).astype(BF16)
        sp_ref[:, cim] = jnp.where(row_id >= 1, pltpu.roll(im, 1, 0), c_im).astype(BF16)
        carry_ref[0:1, cre] = re[rows - 1:rows, :]
        carry_ref[0:1, cim] = im[rows - 1:rows, :]

    for i in range(S5_T):
        cols = slice(i * S5_WIDTH, (i + 1) * S5_WIDTH)
        intra = _dot(ur_ref[:, 0:(i + 1) * S5_WIDTH], krev_ref[(S5_T - 1 - i) * S5_WIDTH:, :])
        y = intra + _dot(sp_ref[...], wc_ref[:, cols])
        y0_ref[pl.ds(i, rows, stride=S5_T), :] = y[:, :LANES]
        y1_ref[pl.ds(i, rows, stride=S5_T), :] = y[:, LANES:]


def _s5_call(u0, u1, sv, krev, sc, apow, batch, rows):
    n = u0.shape[0]
    tok = rows * S5_T
    tiles_per_seq = n // batch // tok
    const = lambda a: pl.BlockSpec(a.shape, lambda b, t: (0,) * a.ndim, pipeline_mode=pl.Buffered(1))
    blk = pl.BlockSpec((tok, LANES), lambda b, t: (b * tiles_per_seq + t, 0))
    half = jax.ShapeDtypeStruct((n, LANES), F32)
    return pl.pallas_call(
        _s5_kernel,
        grid=(batch, tiles_per_seq),
        in_specs=[blk, blk, const(sv), const(krev), const(sc), const(apow)],
        out_specs=[blk, blk],
        out_shape=[half, half],
        scratch_shapes=[pltpu.VMEM((S5_ROW, 2 * S5_NSTATE), BF16), pltpu.VMEM((2 * S5_NSTATE, S5_ROW), BF16),
                        pltpu.VMEM((rows, S5_ROW), BF16), pltpu.VMEM((rows, 2 * S5_NSTATE), F32),
                        pltpu.VMEM((rows, 2 * S5_NSTATE), BF16), pltpu.VMEM((SUBLANES, 2 * S5_NSTATE), F32)],
        compiler_params=pltpu.CompilerParams(dimension_semantics=("arbitrary", "arbitrary"),
                                             vmem_limit_bytes=VMEM_LIMIT_BYTES),
        name="s5_scan",
    )(u0, u1, sv, krev, sc, apow)


def _s5_tables(lam_re, lam_im, log_dt, b_re, b_im, c_re, c_im, rows):
    g_, p_, h_ = S5_N_GROUPS, S5_STATE, S5_GROUP
    hi = lax.Precision.HIGHEST
    lam_re, lam_im = lam_re.astype(F32), lam_im.astype(F32)
    dt = jnp.exp(log_dt.astype(F32))[:, None]
    tau = jnp.arange(S5_T + 1, dtype=F32)[:, None, None]
    mag = jnp.exp(tau * (lam_re * dt)[None])
    ap_re = mag * jnp.cos(tau * (lam_im * dt)[None])
    ap_im = mag * jnp.sin(tau * (lam_im * dt)[None])
    n_re, n_im = ap_re[1] - 1.0, ap_im[1]
    den = lam_re * lam_re + lam_im * lam_im
    k_re = ((n_re * lam_re + n_im * lam_im) / den)[..., None]
    k_im = ((n_im * lam_re - n_re * lam_im) / den)[..., None]
    b_re, b_im = b_re.astype(F32), b_im.astype(F32)
    bb_re = k_re * b_re - k_im * b_im
    bb_im = k_re * b_im + k_im * b_re
    c_re, c_im = c_re.astype(F32), c_im.astype(F32)
    ab_re = ap_re[:S5_T, :, :, None] * bb_re[None] - ap_im[:S5_T, :, :, None] * bb_im[None]
    ab_im = ap_re[:S5_T, :, :, None] * bb_im[None] + ap_im[:S5_T, :, :, None] * bb_re[None]

    col_group = jnp.arange(S5_WIDTH) // h_
    chan_group = jnp.arange(S5_ROW) // h_ % g_

    k_tau = (jnp.einsum('gop,tgpi->tgio', c_re, ab_re, precision=hi)
             - jnp.einsum('gop,tgpi->tgio', c_im, ab_im, precision=hi))
    krev = jnp.tile(k_tau[::-1].reshape(S5_ROW, h_), (1, g_))
    krev = jnp.where(chan_group[:, None] == col_group[None, :], krev, 0.0)

    def sv_part(ab):
        w = ab[::-1].transpose(0, 1, 3, 2).reshape(S5_ROW, p_)
        odd = (chan_group % 2 == 1)[:, None]
        return jnp.concatenate([jnp.where(odd, 0.0, w), jnp.where(odd, w, 0.0)], axis=1)
    sv = jnp.stack([sv_part(ab_re), sv_part(ab_im)])

    cp = lambda c: c.transpose(0, 2, 1)[:, :, None, :]
    w_re = cp(c_re) * ap_re[1:].transpose(1, 2, 0)[..., None] - cp(c_im) * ap_im[1:].transpose(1, 2, 0)[..., None]
    w_im = cp(c_re) * ap_im[1:].transpose(1, 2, 0)[..., None] + cp(c_im) * ap_re[1:].transpose(1, 2, 0)[..., None]
    sc = jnp.concatenate([w_re.reshape(S5_NSTATE, S5_T * h_), -w_im.reshape(S5_NSTATE, S5_T * h_)], axis=0)

    n_steps = max(1, (rows - 1).bit_length())
    pows = [(ap_re[S5_T].reshape(S5_NSTATE), ap_im[S5_T].reshape(S5_NSTATE))]
    for _ in range(n_steps - 1):
        r, i = pows[-1]
        pows.append((r * r - i * i, 2.0 * r * i))
    apow = jnp.stack([jnp.stack(p) for p in pows])
    return sv, krev.astype(BF16), sc, apow


def _attn_fill_bias(diag_ref, bias_ref):
    width = diag_ref.shape[1]
    qi = lax.broadcasted_iota(jnp.int32, (ATTN_TQ, ATTN_TK), 0)
    kj = lax.broadcasted_iota(jnp.int32, (ATTN_TQ, ATTN_TK), 1)
    q_chunk = qi // CHUNK + (BAND_CHUNKS - 1)
    k_chunk = kj // CHUNK
    in_band = (k_chunk <= q_chunk) & (k_chunk >= q_chunk - (BAND_CHUNKS - 1))
    for h in range(ATTN_HEADS):
        rows = jnp.broadcast_to(diag_ref[h:h + 1, :], (ATTN_TQ, width))
        base = pltpu.roll(rows, 0, 1, stride=1, stride_axis=0)[:, :ATTN_TK]
        for t in range(bias_ref.shape[0]):
            bias_ref[t, h] = jnp.where(in_band & (kj >= ATTN_PREV - t * ATTN_TQ), base, NEG_INF)


def _attn_kernel(q_ref, k0_ref, k1_ref, k2_ref, v0_ref, v1_ref, v2_ref, diag_ref, o_ref, bias_ref):
    @pl.when((pl.program_id(0) == 0) & (pl.program_id(1) == 0))
    def _():
        _attn_fill_bias(diag_ref, bias_ref)

    table = jnp.minimum(pl.program_id(1), bias_ref.shape[0] - 1)
    lane = lax.broadcasted_iota(jnp.int32, (1, LANES), 1)
    for p in range(ATTN_WIDTH // LANES):
        cols = slice(p * LANES, (p + 1) * LANES)
        qp = q_ref[:, cols]
        kcat = jnp.concatenate([k0_ref[:, cols], k1_ref[:, cols], k2_ref[:, cols]], axis=0)
        vcat = jnp.concatenate([v0_ref[:, cols], v1_ref[:, cols], v2_ref[:, cols]], axis=0)
        acc = jnp.zeros((ATTN_TQ, LANES), F32)
        for hh in range(LANES // HEAD_DIM):
            in_head = (lane >= hh * HEAD_DIM) & (lane < (hh + 1) * HEAD_DIM)
            qm = jnp.where(in_head, qp, jnp.zeros_like(qp))
            s = _dot_nt(qm, kcat) + bias_ref[table, 2 * p + hh]
            e = jnp.exp(s - jnp.max(s, axis=-1, keepdims=True)).astype(BF16)
            o = _dot(e, jnp.where(in_head, vcat, jnp.ones_like(vcat)))
            acc = acc + jnp.where(in_head, o / pltpu.roll(o, HEAD_DIM, 1), 0.0)
        o_ref[:, cols] = acc


def _attn_call(q, k, v, diag, batch, seq):
    n = q.shape[0]
    tiles = seq // ATTN_TQ
    n_prev = ATTN_PREV // ATTN_TQ
    cur = pl.BlockSpec((ATTN_TQ, ATTN_WIDTH), lambda b, t: (b * tiles + t, 0))
    prev = lambda d: pl.BlockSpec((ATTN_TQ, ATTN_WIDTH), lambda b, t: (b * tiles + jnp.maximum(t - d, 0), 0))
    assert n_prev == 2
    return pl.pallas_call(
        _attn_kernel,
        grid=(batch, tiles),
        in_specs=[cur, prev(2), prev(1), cur, prev(2), prev(1), cur, pl.BlockSpec(diag.shape, lambda b, t: (0, 0))],
        out_specs=cur,
        out_shape=jax.ShapeDtypeStruct((n, ATTN_WIDTH), F32),
        scratch_shapes=[pltpu.VMEM((n_prev + 1, ATTN_HEADS, ATTN_TQ, ATTN_TK), F32)],
        compiler_params=pltpu.CompilerParams(dimension_semantics=("arbitrary", "arbitrary"),
                                             vmem_limit_bytes=VMEM_LIMIT_BYTES),
        name="band_attn",
    )(q, k, k, k, v, v, v, diag)


def _attn_diag(rel_bias):
    width = ATTN_TQ + ATTN_TK
    n = jnp.arange(width)
    offset = jnp.where(n < ATTN_TK, n, n - width)
    rel = jnp.clip(ATTN_PREV - offset, -MAX_REL, MAX_REL) + MAX_REL
    return jnp.take(rel_bias.astype(F32), rel, axis=1)


def _router_gates(lt):
    tm = lt.shape[1]
    row = lax.broadcasted_iota(jnp.int32, (ROUTER_SLAB, tm), 0)
    gl = lt[0:ROUTER_SLAB]
    gmax = jnp.max(gl, axis=0, keepdims=True)
    p_g = 1.0 / jnp.sum(jnp.exp(gl - gmax), axis=0, keepdims=True)
    g_top = jnp.min(jnp.where(gl == gmax, row, ROUTER_SLAB), axis=0, keepdims=True)
    el = jnp.zeros((ROUTER_SLAB, tm), F32)
    for g in range(N_EXPERT_GROUPS):
        el = el + jnp.where(g_top == g, lt[(g + 1) * ROUTER_SLAB:(g + 2) * ROUTER_SLAB], 0.0)
    ee = jnp.exp(el - jnp.max(el, axis=0, keepdims=True))
    ep = ee / jnp.sum(ee, axis=0, keepdims=True)
    p1 = jnp.max(ep, axis=0, keepdims=True)
    i1 = jnp.min(jnp.where(ep == p1, row, ROUTER_SLAB), axis=0, keepdims=True)
    rest = jnp.where(row == i1, -1.0, ep)
    p2 = jnp.max(rest, axis=0, keepdims=True)
    i2 = jnp.min(jnp.where(rest == p2, row, ROUTER_SLAB), axis=0, keepdims=True)
    tot = p1 + p2
    w = (jnp.where(row == i1, p1 / tot, 0.0) + jnp.where(row == i2, p2 / tot, 0.0)) * p_g
    return [jnp.where(g_top == g, w, 0.0) for g in range(N_EXPERT_GROUPS)], g_top


def _sort_matrix(slot, n_slots, slot_axis):
    shape = (n_slots, slot.shape[1]) if slot_axis == 0 else (slot.shape[0], n_slots)
    return jnp.where(lax.broadcasted_iota(jnp.int32, shape, slot_axis) == slot, 1.0, 0.0).astype(BF16)


def _out_kernel(oa_ref, y0_ref, y1_ref, u0_ref, u1_ref, oc_ref, x_ref, d_ref, gluw_ref, glub_ref, og_ref, woutf_ref,
                fg_ref, wr_ref, br_ref, bd_ref, x1_ref, hs_ref, tok_ref, cnt_ref, wout_ref):
    tm = x_ref.shape[0]
    n_slots = hs_ref.shape[0]

    @pl.when(pl.program_id(0) == 0)
    def _():
        wout_ref[...] = woutf_ref[0].astype(BF16)

    y = jnp.concatenate([y0_ref[...], y1_ref[...]], axis=-1)
    u = jnp.concatenate([u0_ref[...], u1_ref[...]], axis=-1)
    y = jax.nn.gelu(y + d_ref[...] * u)
    ob = y * jax.nn.sigmoid(_dot(y.astype(BF16), gluw_ref[...]) + glub_ref[...])
    o = jnp.concatenate([oa_ref[...], ob, oc_ref[...]], axis=-1)
    on = o * lax.rsqrt(_group_sumsq(o, bd_ref) * (1.0 / OUT_NORM_GROUP) + EPS) * og_ref[...]
    x1 = x_ref[...] + _dot(on.astype(BF16), wout_ref[...])
    x1_ref[...] = x1
    ms = jnp.mean(x1 * x1, axis=-1, keepdims=True)
    hb = (x1 * lax.rsqrt(ms + EPS) * fg_ref[...]).astype(BF16)
    lt = _dot(hb, wr_ref[...]).T + br_ref[...]
    slabs, g_top = _router_gates(lt)

    row = lax.broadcasted_iota(jnp.int32, (ROUTER_SLAB, tm), 0)
    lane = lax.broadcasted_iota(jnp.int32, (ROUTER_SLAB, tm), 1)
    member = jnp.where(row == g_top, 1.0, 0.0)
    cum = member
    shift = 1
    while shift < tm:
        cum = cum + jnp.where(lane >= shift, pltpu.roll(cum, shift, 1), 0.0)
        shift *= 2
    count = cum[:, tm - 1:tm]
    padded = jnp.floor((count + (MOE_UNIT - 1)) * (1.0 / MOE_UNIT)) * MOE_UNIT
    group_row = lax.broadcasted_iota(jnp.int32, (ROUTER_SLAB, 1), 0)
    start = jnp.zeros((ROUTER_SLAB, 1), F32)
    for g in range(1, N_EXPERT_GROUPS):
        start = start + jnp.where(group_row >= g, padded[g - 1:g, :], 0.0)
    slot = jnp.sum(member * (start + cum - 1.0), axis=0, keepdims=True)
    cnt_ref[...] = jnp.broadcast_to(count, cnt_ref.shape)

    pad = jnp.zeros((LANES - SLOT_LANE - ROUTER_SLAB, tm), F32)
    gates = jnp.concatenate(slabs + [jnp.broadcast_to(slot, (ROUTER_SLAB, tm)), pad], axis=0).T
    tok_ref[...] = gates

    perm = _sort_matrix(slot.astype(jnp.int32), n_slots, 0)
    hs_ref[:, :D_MODEL] = _dot(perm, hb)
    g_hi = gates.astype(BF16)
    g_mid = (gates - g_hi.astype(F32)).astype(BF16)
    g_lo = (gates - g_hi.astype(F32) - g_mid.astype(F32)).astype(BF16)
    gs = _dot(perm, jnp.concatenate([g_hi, g_mid, g_lo], axis=-1))
    hs_ref[:, D_MODEL:] = gs[:, :LANES] + gs[:, LANES:2 * LANES] + gs[:, 2 * LANES:]


def _out_call(oa, y0, y1, u0, u1, oc, x, d, gluw, glub, og, wout, layer, fg, wr, br, bd, tm):
    n = x.shape[0]
    n_slots = tm + MOE_PAD_ROWS
    row = lambda c: pl.BlockSpec((tm, c), lambda i: (i, 0))
    full = lambda a: pl.BlockSpec(a.shape, lambda i: (0,) * a.ndim)
    return pl.pallas_call(
        _out_kernel,
        grid=(n // tm,),
        in_specs=[row(SGU_WIDTH), row(LANES), row(LANES), row(LANES), row(LANES), row(ATTN_WIDTH), row(D_MODEL),
                  full(d), full(gluw),
                  full(glub), full(og), pl.BlockSpec((1,) + wout.shape[1:], lambda i: (layer, 0, 0), pipeline_mode=pl.Buffered(1)),
                  full(fg), full(wr), full(br), full(bd)],
        out_specs=[row(D_MODEL), pl.BlockSpec((n_slots, MOE_ROW), lambda i: (i, 0)), row(LANES),
                   pl.BlockSpec((ROUTER_SLAB, LANES), lambda i: (i, 0))],
        out_shape=[jax.ShapeDtypeStruct((n, D_MODEL), F32), jax.ShapeDtypeStruct((n // tm * n_slots, MOE_ROW), F32),
                   jax.ShapeDtypeStruct((n, LANES), F32),
                   jax.ShapeDtypeStruct((n // tm * ROUTER_SLAB, LANES), F32)],
        scratch_shapes=[pltpu.VMEM(wout.shape[1:], BF16)],
        compiler_params=pltpu.CompilerParams(dimension_semantics=("arbitrary",), vmem_limit_bytes=VMEM_LIMIT_BYTES),
        name="out_proj",
    )(oa, y0, y1, u0, u1, oc, x, d, gluw, glub, og, wout, fg, wr, br, bd)


def _router_tables(wg, bg, we, be):
    w = jnp.zeros((D_MODEL, LANES), F32)
    b = jnp.full((LANES,), NEG_INF, F32)
    w = w.at[:, 0:N_EXPERT_GROUPS].set(wg.astype(F32))
    b = b.at[0:N_EXPERT_GROUPS].set(bg.astype(F32))
    for g in range(N_EXPERT_GROUPS):
        c0 = (g + 1) * ROUTER_SLAB
        w = w.at[:, c0:c0 + EXPERTS_PER_GROUP].set(we[g].astype(F32))
        b = b.at[c0:c0 + EXPERTS_PER_GROUP].set(be[g].astype(F32))
    return w.astype(BF16), b[:, None]


def _route_plan(counts, n, tm):
    n_tiles = n // tm
    units_per_tile = (tm + MOE_PAD_ROWS) // MOE_UNIT
    units_per_mtile = tm // MOE_UNIT
    cnt = counts.reshape(n_tiles, ROUTER_SLAB, LANES)[:, :N_EXPERT_GROUPS, 0].astype(jnp.int32)
    seg = (cnt + MOE_UNIT - 1) // MOE_UNIT
    seg_start = jnp.cumsum(seg, axis=1) - seg
    before = jnp.cumsum(seg, axis=0) - seg
    total = jnp.sum(seg, axis=0)
    mtiles = (total + units_per_mtile - 1) // units_per_mtile
    mtile_end = jnp.cumsum(mtiles)
    group_start = (mtile_end - mtiles) * units_per_mtile
    n_mtiles = n_tiles + N_EXPERT_GROUPS + -(-n_tiles * MOE_PAD_ROWS // tm)
    tile_group = jnp.minimum(jnp.sum(jnp.arange(n_mtiles)[:, None] >= mtile_end[None, :], axis=1),
                             N_EXPERT_GROUPS - 1)
    meta = jnp.concatenate([tile_group, mtile_end[-1:]]).astype(jnp.int32)
    groups = jnp.arange(N_EXPERT_GROUPS)
    pick = lambda table, g: jnp.sum(jnp.where(g[..., None] == groups, table, 0), axis=-1)

    q = jnp.arange(n_mtiles * units_per_mtile)
    g = jnp.repeat(tile_group, units_per_mtile)
    ql = q - pick(group_start, g)
    ends_g = pick((before + seg)[None], g[:, None])
    tile_of = jnp.minimum(jnp.sum(ql[:, None] >= ends_g, axis=1), n_tiles - 1)
    offset_g = pick((seg_start - before)[None], g[:, None])
    local = ql + jnp.sum(jnp.where(tile_of[:, None] == jnp.arange(n_tiles), offset_g, 0), axis=1)
    valid = (ql < pick(total, g)) & (q // units_per_mtile < mtile_end[-1])
    src = jnp.where(valid, tile_of * units_per_tile + local, units_per_tile - 1).astype(jnp.int32)

    ul = jnp.arange(units_per_tile)[None, :, None]
    seg_of = jnp.sum(ul >= (seg_start + seg)[:, None, :], axis=-1)
    gi = jnp.minimum(seg_of, N_EXPERT_GROUPS - 1)
    glob = pick((group_start + before - seg_start)[:, None, :], gi) + ul[..., 0]
    dst = jnp.where(seg_of < N_EXPERT_GROUPS, glob, 0).astype(jnp.int32).reshape(-1)
    return meta, src, dst, n_mtiles


def _unit_gather(table_ref, first, src_ref, dst_ref, sem, wait):
    n_rows = dst_ref.shape[0]
    if wait:
        pltpu.make_async_copy(src_ref.at[pl.ds(0, n_rows)], dst_ref, sem).wait()
        return

    def body(k, c):
        u = pl.multiple_of(table_ref[first + k] * MOE_UNIT, MOE_UNIT)
        pltpu.make_async_copy(src_ref.at[pl.ds(u, MOE_UNIT)],
                              dst_ref.at[pl.ds(pl.multiple_of(k * MOE_UNIT, MOE_UNIT), MOE_UNIT)], sem).start()
        return c
    lax.fori_loop(0, n_rows // MOE_UNIT, body, 0, unroll=4)


def _moe_group_kernel(meta_ref, src_ref, hs_ref, wg_ref, wu_ref, wd_ref, o_ref, wgb_ref, wub_ref, wdb_ref,
                      buf_ref, sem):
    j = pl.program_id(0)
    n_steps = pl.num_programs(0)
    n_used = meta_ref[n_steps]
    group = meta_ref[j]
    n_units = o_ref.shape[0] // MOE_UNIT
    slot = j % 2

    @pl.when(j == 0)
    def _():
        _unit_gather(src_ref, 0, hs_ref, buf_ref.at[0], sem.at[0], wait=False)

    @pl.when(j + 1 < n_steps)
    def _():
        _unit_gather(src_ref, (j + 1) * n_units, hs_ref, buf_ref.at[1 - slot], sem.at[1 - slot], wait=False)

    @pl.when((j == 0) | (group != meta_ref[jnp.maximum(j - 1, 0)]))
    def _():
        wgb_ref[...] = wg_ref[...].astype(BF16)
        wub_ref[...] = wu_ref[...].astype(BF16)
        wdb_ref[...] = wd_ref[...].astype(BF16)

    _unit_gather(src_ref, j * n_units, hs_ref, buf_ref.at[slot], sem.at[slot], wait=True)

    @pl.when(j < n_used)
    def _():
        h = buf_ref[slot, :, :D_MODEL].astype(BF16)
        gates = buf_ref[slot, :, D_MODEL:]
        lane = lax.broadcasted_iota(jnp.int32, (1, LANES), 1)
        out = None
        for e in range(EXPERTS_PER_GROUP):
            gt = _dot(h, wgb_ref[e])
            up = _dot(h, wub_ref[e])
            gate = jnp.sum(jnp.where(lane == group * ROUTER_SLAB + e, gates, 0.0), axis=-1, keepdims=True)
            a = (gt * jax.nn.sigmoid(gt)) * up * gate
            part = _dot(a.astype(BF16), wdb_ref[e])
            out = part if out is None else out + part
        o_ref[...] = out

    @pl.when(j >= n_used)
    def _():
        o_ref[...] = jnp.zeros_like(o_ref)


def _moe_group_call(meta, src, hs, wg, wu, wd, layer, n_mtiles, tm):
    wg, wu, wd = (a.reshape((-1,) + a.shape[2:]) for a in (wg, wu, wd))
    by_group = lambda a: pl.BlockSpec((EXPERTS_PER_GROUP,) + a.shape[1:],
                                      lambda j, meta, src: (layer * N_EXPERT_GROUPS + meta[j], 0, 0))
    return pl.pallas_call(
        _moe_group_kernel,
        grid_spec=pltpu.PrefetchScalarGridSpec(
            num_scalar_prefetch=2,
            grid=(n_mtiles,),
            in_specs=[pl.BlockSpec(memory_space=pl.ANY), by_group(wg), by_group(wu), by_group(wd)],
            out_specs=pl.BlockSpec((tm, D_MODEL), lambda j, meta, src: (j, 0)),
            scratch_shapes=[pltpu.VMEM((EXPERTS_PER_GROUP, D_MODEL, D_EXPERT), BF16),
                            pltpu.VMEM((EXPERTS_PER_GROUP, D_MODEL, D_EXPERT), BF16),
                            pltpu.VMEM((EXPERTS_PER_GROUP, D_EXPERT, D_MODEL), BF16),
                            pltpu.VMEM((2, tm, MOE_ROW), F32), pltpu.SemaphoreType.DMA((2,))],
        ),
        out_shape=jax.ShapeDtypeStruct((n_mtiles * tm, D_MODEL), F32),
        compiler_params=pltpu.CompilerParams(dimension_semantics=("arbitrary",), vmem_limit_bytes=VMEM_LIMIT_BYTES),
        name="moe_experts",
    )(meta, src, hs, wg, wu, wd)


def _combine_kernel(dst_ref, x1_ref, tok_ref, ys_ref, o_ref, buf_ref, sem):
    i = pl.program_id(0)
    n_slots = buf_ref.shape[1]
    n_units = n_slots // MOE_UNIT
    slot = i % 2

    @pl.when(i == 0)
    def _():
        _unit_gather(dst_ref, 0, ys_ref, buf_ref.at[0], sem.at[0], wait=False)

    @pl.when(i + 1 < pl.num_programs(0))
    def _():
        _unit_gather(dst_ref, (i + 1) * n_units, ys_ref, buf_ref.at[1 - slot], sem.at[1 - slot], wait=False)

    _unit_gather(dst_ref, i * n_units, ys_ref, buf_ref.at[slot], sem.at[slot], wait=True)

    y = buf_ref[slot]
    y_hi = y.astype(BF16)
    y_lo = (y - y_hi.astype(F32)).astype(BF16)
    unsort = _sort_matrix(tok_ref[:, SLOT_LANE:SLOT_LANE + 1].astype(jnp.int32), n_slots, 1)
    o_ref[...] = x1_ref[...] + _dot(unsort, y_hi) + _dot(unsort, y_lo)


def _combine_call(dst, x1, tok, ys, tm):
    n = x1.shape[0]
    n_slots = tm + MOE_PAD_ROWS
    return pl.pallas_call(
        _combine_kernel,
        grid_spec=pltpu.PrefetchScalarGridSpec(
            num_scalar_prefetch=1,
            grid=(n // tm,),
            in_specs=[pl.BlockSpec((tm, D_MODEL), lambda i, dst: (i, 0)),
                      pl.BlockSpec((tm, LANES), lambda i, dst: (i, 0)),
                      pl.BlockSpec(memory_space=pl.ANY)],
            out_specs=pl.BlockSpec((tm, D_MODEL), lambda i, dst: (i, 0)),
            scratch_shapes=[pltpu.VMEM((2, n_slots, D_MODEL), F32), pltpu.SemaphoreType.DMA((2,))],
        ),
        out_shape=jax.ShapeDtypeStruct((n, D_MODEL), F32),
        compiler_params=pltpu.CompilerParams(dimension_semantics=("arbitrary",), vmem_limit_bytes=VMEM_LIMIT_BYTES),
        name="moe_combine",
    )(dst, x1, tok, ys)


def kernel(x, norm_mix, w_in, sgu_norm, sgu_w, sgu_b, s5_lambda_re, s5_lambda_im, s5_log_dt, s5_b_re, s5_b_im,
           s5_c_re, s5_c_im, s5_d, s5_glu_w, s5_glu_b, q_norm, k_norm, rel_bias, out_norm, w_out, norm_ffn,
           router_group_w, router_group_b, router_expert_w, router_expert_b, w_gate, w_up, w_down):
    batch, seq, _ = x.shape
    n = batch * seq
    depth = w_in.shape[0]
    t = _tiles(n, seq)
    assert seq % ATTN_TQ == 0 and seq % (S5_T * t["s5_rows"]) == 0
    row_vec = lambda a: a.astype(F32)[None, :]

    lane_group = jnp.arange(LANES) // HEAD_DIM
    bd = (lane_group[:, None] == lane_group[None, :]).astype(BF16)
    block_chunk = jnp.arange(SGU_BLOCK) // CHUNK
    sgu_mask = block_chunk[None, :] <= block_chunk[:, None]

    xf = x.reshape(n, D_MODEL)
    for l in range(depth):
        sguw = jnp.where(sgu_mask[None], sgu_w[l], 0).astype(BF16)
        sgub = jnp.repeat(sgu_b[l].astype(F32).T, HEAD_DIM, axis=1)
        qg = row_vec(jnp.tile(q_norm[l], ATTN_HEADS)) * (HEAD_DIM ** -0.5)
        kg = row_vec(jnp.tile(k_norm[l], ATTN_HEADS))
        oa, u0, u1, q, k, v = _in_call(xf, row_vec(norm_mix[l]), w_in, l, row_vec(sgu_norm[l]), sguw, sgub,
                                      qg, kg, bd, t["tm_in"])

        sv, krev, sc, apow = _s5_tables(s5_lambda_re[l], s5_lambda_im[l], s5_log_dt[l], s5_b_re[l], s5_b_im[l],
                                        s5_c_re[l], s5_c_im[l], t["s5_rows"])
        y0, y1 = _s5_call(u0, u1, sv, krev, sc, apow, batch, t["s5_rows"])

        oc = _attn_call(q, k, v, _attn_diag(rel_bias[l]), batch, seq)

        wr, br = _router_tables(router_group_w[l], router_group_b[l], router_expert_w[l], router_expert_b[l])
        x1, hs, tok, counts = _out_call(oa, y0, y1, u0, u1, oc, xf, row_vec(s5_d[l]), s5_glu_w[l].astype(BF16), row_vec(s5_glu_b[l]),
                                 row_vec(out_norm[l]), w_out, l, row_vec(norm_ffn[l]), wr, br, bd,
                                 t["tm_out"])

        meta, src, dst, n_mtiles = _route_plan(counts, n, t["tm_out"])
        ys = _moe_group_call(meta, src, hs, w_gate, w_up, w_down, l, n_mtiles, t["tm_out"])
        xf = _combine_call(dst, x1, tok, ys, t["tm_out"])
    return xf.reshape(batch, seq, D_MODEL)
```

```python
import jax
import jax.numpy as jnp
from jax import lax
from jax.experimental import pallas as pl
from jax.experimental.pallas import tpu as pltpu

F32 = jnp.float32
BF16 = jnp.bfloat16

D_MODEL = 1024
CHUNK = 64
HEAD_DIM = 64
SGU_WIDTH = 256
SGU_HEADS = 4
SGU_BLOCK = 128
S5_WIDTH = 256
S5_GROUP = 16
S5_N_GROUPS = 16
S5_STATE = 64
ATTN_WIDTH = 512
ATTN_HEADS = 8
BAND_CHUNKS = 9
MAX_REL = 256
IN_COLS = 2 * SGU_WIDTH + S5_WIDTH + 3 * ATTN_WIDTH
OUT_NORM_GROUP = 64
N_EXPERT_GROUPS = 4
EXPERTS_PER_GROUP = 4
N_EXPERTS = 16
D_EXPERT = 256
EPS = 1e-6
NEG_INF = -1e30

LANES = 128
SUBLANES = 8
VMEM_LIMIT_BYTES = 56 * 1024 * 1024

S5_T = 16
S5_ROW = S5_T * S5_WIDTH
S5_NSTATE = S5_N_GROUPS * S5_STATE
ATTN_TQ = 256
ATTN_PREV = (BAND_CHUNKS - 1) * CHUNK
ATTN_TK = ATTN_TQ + ATTN_PREV
ROUTER_SLAB = SUBLANES
MOE_ROW = D_MODEL + LANES
MOE_UNIT = SUBLANES
PAIR_LO = (0, 0, 0, 1, 1, 2)
PAIR_HI = (1, 2, 3, 2, 3, 3)
N_PAIRS = len(PAIR_LO)
N_BUCKETS = N_EXPERT_GROUPS * N_PAIRS
MOE_PAD_ROWS = N_BUCKETS * MOE_UNIT
SLOT_LANE = N_EXPERT_GROUPS * ROUTER_SLAB


def _tiles(n_tokens, seq):
    def pick(pref, total):
        t = min(pref, total)
        assert total % t == 0
        return t
    return dict(
        tm_in=pick(512, seq),
        tm_out=pick(512, seq),
        s5_rows=pick(128, seq // S5_T),
    )


def _dot(a, b):
    return jnp.dot(a, b, preferred_element_type=F32)


def _dot_nt(a, b):
    return lax.dot_general(a, b, (((1,), (1,)), ((), ())), preferred_element_type=F32)


def _group_sumsq(x, bd_ref):
    x2 = (x * x).astype(BF16)
    parts = [_dot(x2[:, t * LANES:(t + 1) * LANES], bd_ref[...]) for t in range(x.shape[1] // LANES)]
    return jnp.concatenate(parts, axis=-1)


def _in_kernel(x_ref, g_ref, wf_ref, sgug_ref, sguw_ref, sgub_ref, qg_ref, kg_ref, bd_ref,
               oa_ref, u0_ref, u1_ref, q_ref, k_ref, v_ref, w_ref):
    tm = x_ref.shape[0]

    @pl.when(pl.program_id(0) == 0)
    def _():
        w_ref[...] = wf_ref[0].astype(BF16)

    x = x_ref[...]
    ms = jnp.mean(x * x, axis=-1, keepdims=True)
    hn = (x * lax.rsqrt(ms + EPS) * g_ref[...]).astype(BF16)

    z = jax.nn.gelu(_dot(hn, w_ref[:, 0:2 * SGU_WIDTH]))
    u = z[:, :SGU_WIDTH]
    v = z[:, SGU_WIDTH:]
    v = v * lax.rsqrt(jnp.mean(v * v, axis=-1, keepdims=True) + EPS) * sgug_ref[...]
    vb = v.astype(BF16)
    first_head = lax.broadcasted_iota(jnp.int32, (SGU_BLOCK, LANES), 1) < HEAD_DIM
    for r in range(tm // SGU_BLOCK):
        rows = slice(r * SGU_BLOCK, (r + 1) * SGU_BLOCK)
        for p in range(SGU_WIDTH // LANES):
            cols = slice(p * LANES, (p + 1) * LANES)
            vp = vb[rows, cols]
            mixed = jnp.where(first_head, _dot(sguw_ref[2 * p], vp), _dot(sguw_ref[2 * p + 1], vp))
            oa_ref[rows, cols] = u[rows, cols] * (mixed + sgub_ref[:, cols])

    c0 = 2 * SGU_WIDTH
    us = _dot(hn, w_ref[:, c0:c0 + S5_WIDTH])
    u0_ref[...] = us[:, :LANES]
    u1_ref[...] = us[:, LANES:]

    c0 += S5_WIDTH
    q = _dot(hn, w_ref[:, c0:c0 + ATTN_WIDTH])
    q_ref[...] = (q * lax.rsqrt(_group_sumsq(q, bd_ref) * (1.0 / HEAD_DIM) + EPS) * qg_ref[...]).astype(BF16)
    c0 += ATTN_WIDTH
    k = _dot(hn, w_ref[:, c0:c0 + ATTN_WIDTH])
    k_ref[...] = (k * lax.rsqrt(_group_sumsq(k, bd_ref) * (1.0 / HEAD_DIM) + EPS) * kg_ref[...]).astype(BF16)
    c0 += ATTN_WIDTH
    v_ref[...] = _dot(hn, w_ref[:, c0:c0 + ATTN_WIDTH]).astype(BF16)


def _in_call(x, g, w, layer, sgug, sguw, sgub, qg, kg, bd, tm):
    n = x.shape[0]
    row = lambda c: pl.BlockSpec((tm, c), lambda i: (i, 0))
    full = lambda a: pl.BlockSpec(a.shape, lambda i: (0,) * a.ndim)
    w_layer = pl.BlockSpec((1,) + w.shape[1:], lambda i: (layer, 0, 0), pipeline_mode=pl.Buffered(1))
    return pl.pallas_call(
        _in_kernel,
        grid=(n // tm,),
        in_specs=[row(D_MODEL), full(g), w_layer, full(sgug), full(sguw), full(sgub), full(qg), full(kg), full(bd)],
        out_specs=[row(SGU_WIDTH), row(LANES), row(LANES), row(ATTN_WIDTH), row(ATTN_WIDTH), row(ATTN_WIDTH)],
        out_shape=[jax.ShapeDtypeStruct((n, SGU_WIDTH), F32), jax.ShapeDtypeStruct((n, LANES), F32),
                   jax.ShapeDtypeStruct((n, LANES), F32), jax.ShapeDtypeStruct((n, ATTN_WIDTH), BF16),
                   jax.ShapeDtypeStruct((n, ATTN_WIDTH), BF16), jax.ShapeDtypeStruct((n, ATTN_WIDTH), BF16)],
        scratch_shapes=[pltpu.VMEM(w.shape[1:], BF16)],
        compiler_params=pltpu.CompilerParams(dimension_semantics=("arbitrary",), vmem_limit_bytes=VMEM_LIMIT_BYTES),
        name="in_proj",
    )(x, g, w, sgug, sguw, sgub, qg, kg, bd)


def _s5_expand_tables(sv_ref, sc_ref, wv_ref, wc_ref):
    pair_of_row = lax.broadcasted_iota(jnp.int32, (S5_ROW, 1), 0) // S5_GROUP % S5_N_GROUPS // 2
    for part in range(2):
        src = sv_ref[part]
        for j in range(S5_NSTATE // LANES):
            cols = slice(part * S5_NSTATE + j * LANES, part * S5_NSTATE + (j + 1) * LANES)
            wv_ref[:, cols] = jnp.where(pair_of_row == j, src, 0.0).astype(BF16)

    wc_ref[...] = jnp.zeros_like(wc_ref)
    lane = lax.broadcasted_iota(jnp.int32, (1, LANES), 1)
    per_tile = LANES // S5_GROUP
    for part in range(2):
        for g in range(S5_N_GROUPS):
            rows = slice(part * S5_NSTATE + g * S5_STATE, part * S5_NSTATE + (g + 1) * S5_STATE)
            keep = (lane >= g % per_tile * S5_GROUP) & (lane < (g % per_tile + 1) * S5_GROUP)
            for i in range(S5_T):
                tile = sc_ref[rows, i // per_tile * LANES:(i // per_tile + 1) * LANES]
                shift = (g % per_tile - i % per_tile) * S5_GROUP % LANES
                if shift:
                    tile = pltpu.roll(tile, shift, 1)
                q = i * (S5_WIDTH // LANES) + g // per_tile
                wc_ref[rows, q * LANES:(q + 1) * LANES] = jnp.where(keep, tile, 0.0).astype(BF16)


def _s5_kernel(u0_ref, u1_ref, sv_ref, krev_ref, sc_ref, apow_ref, y0_ref, y1_ref,
               wv_ref, wc_ref, ur_ref, s_ref, sp_ref, carry_ref):
    rows = ur_ref.shape[0]
    n_steps = apow_ref.shape[0]

    @pl.when((pl.program_id(0) == 0) & (pl.program_id(1) == 0))
    def _():
        _s5_expand_tables(sv_ref, sc_ref, wv_ref, wc_ref)

    @pl.when(pl.program_id(1) == 0)
    def _():
        carry_ref[...] = jnp.zeros_like(carry_ref)

    for t in range(S5_T):
        ur_ref[:, t * S5_WIDTH:t * S5_WIDTH + LANES] = u0_ref[pl.ds(t, rows, stride=S5_T), :].astype(BF16)
        ur_ref[:, t * S5_WIDTH + LANES:(t + 1) * S5_WIDTH] = u1_ref[pl.ds(t, rows, stride=S5_T), :].astype(BF16)

    s_ref[...] = _dot(ur_ref[...], wv_ref[...])
    row_id = lax.broadcasted_iota(jnp.int32, (rows, LANES), 0)
    for cb in range(S5_NSTATE // LANES):
        cre = slice(cb * LANES, (cb + 1) * LANES)
        cim = slice(S5_NSTATE + cb * LANES, S5_NSTATE + (cb + 1) * LANES)
        re = s_ref[:, cre]
        im = s_ref[:, cim]
        c_re = carry_ref[0:1, cre]
        c_im = carry_ref[0:1, cim]
        a_re = apow_ref[0, 0:1, cre]
        a_im = apow_ref[0, 1:2, cre]
        re = re + jnp.where(row_id == 0, a_re * c_re - a_im * c_im, 0.0)
        im = im + jnp.where(row_id == 0, a_re * c_im + a_im * c_re, 0.0)
        for k in range(n_steps):
            shift = 1 << k
            a_re = apow_ref[k, 0:1, cre]
            a_im = apow_ref[k, 1:2, cre]
            re_s = jnp.where(row_id >= shift, pltpu.roll(re, shift, 0), 0.0)
            im_s = jnp.where(row_id >= shift, pltpu.roll(im, shift, 0), 0.0)
            re, im = re + a_re * re_s - a_im * im_s, im + a_re * im_s + a_im * re_s
        sp_ref[:, cre] = jnp.where(row_id >= 1, pltpu.roll(re, 1, 0), c_re).astype(BF16)
        sp_ref[:, cim] = jnp.where(row_id >= 1, pltpu.roll(im, 1, 0), c_im).astype(BF16)
        carry_ref[0:1, cre] = re[rows - 1:rows, :]
        carry_ref[0:1, cim] = im[rows - 1:rows, :]

    for i in range(S5_T):
        cols = slice(i * S5_WIDTH, (i + 1) * S5_WIDTH)
        intra = _dot(ur_ref[:, 0:(i + 1) * S5_WIDTH], krev_ref[(S5_T - 1 - i) * S5_WIDTH:, :])
        y = intra + _dot(sp_ref[...], wc_ref[:, cols])
        y0_ref[pl.ds(i, rows, stride=S5_T), :] = y[:, :LANES]
        y1_ref[pl.ds(i, rows, stride=S5_T), :] = y[:, LANES:]


def _s5_call(u0, u1, sv, krev, sc, apow, batch, rows):
    n = u0.shape[0]
    tok = rows * S5_T
    tiles_per_seq = n // batch // tok
    const = lambda a: pl.BlockSpec(a.shape, lambda b, t: (0,) * a.ndim, pipeline_mode=pl.Buffered(1))
    blk = pl.BlockSpec((tok, LANES), lambda b, t: (b * tiles_per_seq + t, 0))
    half = jax.ShapeDtypeStruct((n, LANES), F32)
    return pl.pallas_call(
        _s5_kernel,
        grid=(batch, tiles_per_seq),
        in_specs=[blk, blk, const(sv), const(krev), const(sc), const(apow)],
        out_specs=[blk, blk],
        out_shape=[half, half],
        scratch_shapes=[pltpu.VMEM((S5_ROW, 2 * S5_NSTATE), BF16), pltpu.VMEM((2 * S5_NSTATE, S5_ROW), BF16),
                        pltpu.VMEM((rows, S5_ROW), BF16), pltpu.VMEM((rows, 2 * S5_NSTATE), F32),
                        pltpu.VMEM((rows, 2 * S5_NSTATE), BF16), pltpu.VMEM((SUBLANES, 2 * S5_NSTATE), F32)],
        compiler_params=pltpu.CompilerParams(dimension_semantics=("arbitrary", "arbitrary"),
                                             vmem_limit_bytes=VMEM_LIMIT_BYTES),
        name="s5_scan",
    )(u0, u1, sv, krev, sc, apow)


def _s5_tables(lam_re, lam_im, log_dt, b_re, b_im, c_re, c_im, rows):
    g_, p_, h_ = S5_N_GROUPS, S5_STATE, S5_GROUP
    hi = lax.Precision.HIGHEST
    lam_re, lam_im = lam_re.astype(F32), lam_im.astype(F32)
    dt = jnp.exp(log_dt.astype(F32))[:, None]
    tau = jnp.arange(S5_T + 1, dtype=F32)[:, None, None]
    mag = jnp.exp(tau * (lam_re * dt)[None])
    ap_re = mag * jnp.cos(tau * (lam_im * dt)[None])
    ap_im = mag * jnp.sin(tau * (lam_im * dt)[None])
    n_re, n_im = ap_re[1] - 1.0, ap_im[1]
    den = lam_re * lam_re + lam_im * lam_im
    k_re = ((n_re * lam_re + n_im * lam_im) / den)[..., None]
    k_im = ((n_im * lam_re - n_re * lam_im) / den)[..., None]
    b_re, b_im = b_re.astype(F32), b_im.astype(F32)
    bb_re = k_re * b_re - k_im * b_im
    bb_im = k_re * b_im + k_im * b_re
    c_re, c_im = c_re.astype(F32), c_im.astype(F32)
    ab_re = ap_re[:S5_T, :, :, None] * bb_re[None] - ap_im[:S5_T, :, :, None] * bb_im[None]
    ab_im = ap_re[:S5_T, :, :, None] * bb_im[None] + ap_im[:S5_T, :, :, None] * bb_re[None]

    col_group = jnp.arange(S5_WIDTH) // h_
    chan_group = jnp.arange(S5_ROW) // h_ % g_

    k_tau = (jnp.einsum('gop,tgpi->tgio', c_re, ab_re, precision=hi)
             - jnp.einsum('gop,tgpi->tgio', c_im, ab_im, precision=hi))
    krev = jnp.tile(k_tau[::-1].reshape(S5_ROW, h_), (1, g_))
    krev = jnp.where(chan_group[:, None] == col_group[None, :], krev, 0.0)

    def sv_part(ab):
        w = ab[::-1].transpose(0, 1, 3, 2).reshape(S5_ROW, p_)
        odd = (chan_group % 2 == 1)[:, None]
        return jnp.concatenate([jnp.where(odd, 0.0, w), jnp.where(odd, w, 0.0)], axis=1)
    sv = jnp.stack([sv_part(ab_re), sv_part(ab_im)])

    cp = lambda c: c.transpose(0, 2, 1)[:, :, None, :]
    w_re = cp(c_re) * ap_re[1:].transpose(1, 2, 0)[..., None] - cp(c_im) * ap_im[1:].transpose(1, 2, 0)[..., None]
    w_im = cp(c_re) * ap_im[1:].transpose(1, 2, 0)[..., None] + cp(c_im) * ap_re[1:].transpose(1, 2, 0)[..., None]
    sc = jnp.concatenate([w_re.reshape(S5_NSTATE, S5_T * h_), -w_im.reshape(S5_NSTATE, S5_T * h_)], axis=0)

    n_steps = max(1, (rows - 1).bit_length())
    pows = [(ap_re[S5_T].reshape(S5_NSTATE), ap_im[S5_T].reshape(S5_NSTATE))]
    for _ in range(n_steps - 1):
        r, i = pows[-1]
        pows.append((r * r - i * i, 2.0 * r * i))
    apow = jnp.stack([jnp.stack(p) for p in pows])
    return sv, krev.astype(BF16), sc, apow


def _attn_fill_bias(diag_ref, bias_ref):
    width = diag_ref.shape[1]
    qi = lax.broadcasted_iota(jnp.int32, (ATTN_TQ, ATTN_TK), 0)
    kj = lax.broadcasted_iota(jnp.int32, (ATTN_TQ, ATTN_TK), 1)
    q_chunk = qi // CHUNK + (BAND_CHUNKS - 1)
    k_chunk = kj // CHUNK
    in_band = (k_chunk <= q_chunk) & (k_chunk >= q_chunk - (BAND_CHUNKS - 1))
    for h in range(ATTN_HEADS):
        rows = jnp.broadcast_to(diag_ref[h:h + 1, :], (ATTN_TQ, width))
        base = pltpu.roll(rows, 0, 1, stride=1, stride_axis=0)[:, :ATTN_TK]
        for t in range(bias_ref.shape[0]):
            bias_ref[t, h] = jnp.where(in_band & (kj >= ATTN_PREV - t * ATTN_TQ), base, NEG_INF)


def _attn_kernel(q_ref, k0_ref, k1_ref, k2_ref, v0_ref, v1_ref, v2_ref, diag_ref, o_ref, bias_ref):
    @pl.when((pl.program_id(0) == 0) & (pl.program_id(1) == 0))
    def _():
        _attn_fill_bias(diag_ref, bias_ref)

    table = jnp.minimum(pl.program_id(1), bias_ref.shape[0] - 1)
    lane = lax.broadcasted_iota(jnp.int32, (1, LANES), 1)
    for p in range(ATTN_WIDTH // LANES):
        cols = slice(p * LANES, (p + 1) * LANES)
        qp = q_ref[:, cols]
        kcat = jnp.concatenate([k0_ref[:, cols], k1_ref[:, cols], k2_ref[:, cols]], axis=0)
        vcat = jnp.concatenate([v0_ref[:, cols], v1_ref[:, cols], v2_ref[:, cols]], axis=0)
        acc = jnp.zeros((ATTN_TQ, LANES), F32)
        for hh in range(LANES // HEAD_DIM):
            in_head = (lane >= hh * HEAD_DIM) & (lane < (hh + 1) * HEAD_DIM)
            qm = jnp.where(in_head, qp, jnp.zeros_like(qp))
            s = _dot_nt(qm, kcat) + bias_ref[table, 2 * p + hh]
            e = jnp.exp(s - jnp.max(s, axis=-1, keepdims=True)).astype(BF16)
            o = _dot(e, jnp.where(in_head, vcat, jnp.ones_like(vcat)))
            acc = acc + jnp.where(in_head, o / pltpu.roll(o, HEAD_DIM, 1), 0.0)
        o_ref[:, cols] = acc


def _attn_call(q, k, v, diag, batch, seq):
    n = q.shape[0]
    tiles = seq // ATTN_TQ
    n_prev = ATTN_PREV // ATTN_TQ
    cur = pl.BlockSpec((ATTN_TQ, ATTN_WIDTH), lambda b, t: (b * tiles + t, 0))
    prev = lambda d: pl.BlockSpec((ATTN_TQ, ATTN_WIDTH), lambda b, t: (b * tiles + jnp.maximum(t - d, 0), 0))
    assert n_prev == 2
    return pl.pallas_call(
        _attn_kernel,
        grid=(batch, tiles),
        in_specs=[cur, prev(2), prev(1), cur, prev(2), prev(1), cur, pl.BlockSpec(diag.shape, lambda b, t: (0, 0))],
        out_specs=cur,
        out_shape=jax.ShapeDtypeStruct((n, ATTN_WIDTH), F32),
        scratch_shapes=[pltpu.VMEM((n_prev + 1, ATTN_HEADS, ATTN_TQ, ATTN_TK), F32)],
        compiler_params=pltpu.CompilerParams(dimension_semantics=("arbitrary", "arbitrary"),
                                             vmem_limit_bytes=VMEM_LIMIT_BYTES),
        name="band_attn",
    )(q, k, k, k, v, v, v, diag)


def _attn_diag(rel_bias):
    width = ATTN_TQ + ATTN_TK
    n = jnp.arange(width)
    offset = jnp.where(n < ATTN_TK, n, n - width)
    rel = jnp.clip(ATTN_PREV - offset, -MAX_REL, MAX_REL) + MAX_REL
    return jnp.take(rel_bias.astype(F32), rel, axis=1)


def _router_gates(lt):
    tm = lt.shape[1]
    row = lax.broadcasted_iota(jnp.int32, (ROUTER_SLAB, tm), 0)
    gl = lt[0:ROUTER_SLAB]
    gmax = jnp.max(gl, axis=0, keepdims=True)
    p_g = 1.0 / jnp.sum(jnp.exp(gl - gmax), axis=0, keepdims=True)
    g_top = jnp.min(jnp.where(gl == gmax, row, ROUTER_SLAB), axis=0, keepdims=True)
    el = jnp.zeros((ROUTER_SLAB, tm), F32)
    for g in range(N_EXPERT_GROUPS):
        el = el + jnp.where(g_top == g, lt[(g + 1) * ROUTER_SLAB:(g + 2) * ROUTER_SLAB], 0.0)
    ee = jnp.exp(el - jnp.max(el, axis=0, keepdims=True))
    ep = ee / jnp.sum(ee, axis=0, keepdims=True)
    p1 = jnp.max(ep, axis=0, keepdims=True)
    i1 = jnp.min(jnp.where(ep == p1, row, ROUTER_SLAB), axis=0, keepdims=True)
    rest = jnp.where(row == i1, -1.0, ep)
    p2 = jnp.max(rest, axis=0, keepdims=True)
    i2 = jnp.min(jnp.where(rest == p2, row, ROUTER_SLAB), axis=0, keepdims=True)
    tot = p1 + p2
    w = (jnp.where(row == i1, p1 / tot, 0.0) + jnp.where(row == i2, p2 / tot, 0.0)) * p_g
    return [jnp.where(g_top == g, w, 0.0) for g in range(N_EXPERT_GROUPS)], g_top, i1, i2


def _sort_matrix(slot, n_slots, slot_axis):
    shape = (n_slots, slot.shape[1]) if slot_axis == 0 else (slot.shape[0], n_slots)
    return jnp.where(lax.broadcasted_iota(jnp.int32, shape, slot_axis) == slot, 1.0, 0.0).astype(BF16)


def _out_kernel(oa_ref, y0_ref, y1_ref, u0_ref, u1_ref, oc_ref, x_ref, d_ref, gluw_ref, glub_ref, og_ref, woutf_ref,
                fg_ref, wr_ref, br_ref, bd_ref, x1_ref, hs_ref, tok_ref, cnt_ref, wout_ref):
    tm = x_ref.shape[0]
    n_slots = hs_ref.shape[0]

    @pl.when(pl.program_id(0) == 0)
    def _():
        wout_ref[...] = woutf_ref[0].astype(BF16)

    y = jnp.concatenate([y0_ref[...], y1_ref[...]], axis=-1)
    u = jnp.concatenate([u0_ref[...], u1_ref[...]], axis=-1)
    y = jax.nn.gelu(y + d_ref[...] * u)
    ob = y * jax.nn.sigmoid(_dot(y.astype(BF16), gluw_ref[...]) + glub_ref[...])
    o = jnp.concatenate([oa_ref[...], ob, oc_ref[...]], axis=-1)
    on = o * lax.rsqrt(_group_sumsq(o, bd_ref) * (1.0 / OUT_NORM_GROUP) + EPS) * og_ref[...]
    x1 = x_ref[...] + _dot(on.astype(BF16), wout_ref[...])
    x1_ref[...] = x1
    ms = jnp.mean(x1 * x1, axis=-1, keepdims=True)
    hb = (x1 * lax.rsqrt(ms + EPS) * fg_ref[...]).astype(BF16)
    lt = _dot(hb, wr_ref[...]).T + br_ref[...]
    slabs, g_top, i1, i2 = _router_gates(lt)

    lo, hi = jnp.minimum(i1, i2), jnp.maximum(i1, i2)
    first_pair = jnp.where(lo == 0, 0, jnp.where(lo == 1, PAIR_LO.index(1), PAIR_LO.index(2)))
    bucket = g_top * N_PAIRS + first_pair + hi - lo - 1
    row = lax.broadcasted_iota(jnp.int32, (N_BUCKETS, tm), 0)
    lane = lax.broadcasted_iota(jnp.int32, (N_BUCKETS, tm), 1)
    member = jnp.where(row == bucket, 1.0, 0.0)
    cum = member
    shift = 1
    while shift < tm:
        cum = cum + jnp.where(lane >= shift, pltpu.roll(cum, shift, 1), 0.0)
        shift *= 2
    count = cum[:, tm - 1:tm]
    padded = jnp.floor((count + (MOE_UNIT - 1)) * (1.0 / MOE_UNIT)) * MOE_UNIT
    bucket_row = lax.broadcasted_iota(jnp.int32, (N_BUCKETS, 1), 0)
    start = jnp.zeros((N_BUCKETS, 1), F32)
    for b in range(1, N_BUCKETS):
        start = start + jnp.where(bucket_row >= b, padded[b - 1:b, :], 0.0)
    slot = jnp.sum(member * (start + cum - 1.0), axis=0, keepdims=True)
    cnt_ref[...] = jnp.broadcast_to(count, cnt_ref.shape)

    pad = jnp.zeros((LANES - SLOT_LANE - ROUTER_SLAB, tm), F32)
    gates = jnp.concatenate(slabs + [jnp.broadcast_to(slot, (ROUTER_SLAB, tm)), pad], axis=0).T
    tok_ref[...] = gates

    perm = _sort_matrix(slot.astype(jnp.int32), n_slots, 0)
    hs_ref[:, :D_MODEL] = _dot(perm, hb)
    g_hi = gates.astype(BF16)
    g_mid = (gates - g_hi.astype(F32)).astype(BF16)
    g_lo = (gates - g_hi.astype(F32) - g_mid.astype(F32)).astype(BF16)
    gs = _dot(perm, jnp.concatenate([g_hi, g_mid, g_lo], axis=-1))
    hs_ref[:, D_MODEL:] = gs[:, :LANES] + gs[:, LANES:2 * LANES] + gs[:, 2 * LANES:]


def _out_call(oa, y0, y1, u0, u1, oc, x, d, gluw, glub, og, wout, layer, fg, wr, br, bd, tm):
    n = x.shape[0]
    n_slots = tm + MOE_PAD_ROWS
    row = lambda c: pl.BlockSpec((tm, c), lambda i: (i, 0))
    full = lambda a: pl.BlockSpec(a.shape, lambda i: (0,) * a.ndim)
    return pl.pallas_call(
        _out_kernel,
        grid=(n // tm,),
        in_specs=[row(SGU_WIDTH), row(LANES), row(LANES), row(LANES), row(LANES), row(ATTN_WIDTH), row(D_MODEL),
                  full(d), full(gluw),
                  full(glub), full(og), pl.BlockSpec((1,) + wout.shape[1:], lambda i: (layer, 0, 0), pipeline_mode=pl.Buffered(1)),
                  full(fg), full(wr), full(br), full(bd)],
        out_specs=[row(D_MODEL), pl.BlockSpec((n_slots, MOE_ROW), lambda i: (i, 0)), row(LANES),
                   pl.BlockSpec((N_BUCKETS, LANES), lambda i: (i, 0))],
        out_shape=[jax.ShapeDtypeStruct((n, D_MODEL), F32), jax.ShapeDtypeStruct((n // tm * n_slots, MOE_ROW), F32),
                   jax.ShapeDtypeStruct((n, LANES), F32),
                   jax.ShapeDtypeStruct((n // tm * N_BUCKETS, LANES), F32)],
        scratch_shapes=[pltpu.VMEM(wout.shape[1:], BF16)],
        compiler_params=pltpu.CompilerParams(dimension_semantics=("arbitrary",), vmem_limit_bytes=VMEM_LIMIT_BYTES),
        name="out_proj",
    )(oa, y0, y1, u0, u1, oc, x, d, gluw, glub, og, wout, fg, wr, br, bd)


def _router_tables(wg, bg, we, be):
    w = jnp.zeros((D_MODEL, LANES), F32)
    b = jnp.full((LANES,), NEG_INF, F32)
    w = w.at[:, 0:N_EXPERT_GROUPS].set(wg.astype(F32))
    b = b.at[0:N_EXPERT_GROUPS].set(bg.astype(F32))
    for g in range(N_EXPERT_GROUPS):
        c0 = (g + 1) * ROUTER_SLAB
        w = w.at[:, c0:c0 + EXPERTS_PER_GROUP].set(we[g].astype(F32))
        b = b.at[c0:c0 + EXPERTS_PER_GROUP].set(be[g].astype(F32))
    return w.astype(BF16), b[:, None]


def _route_plan(counts, n, tm):
    n_tiles = n // tm
    units_per_tile = (tm + MOE_PAD_ROWS) // MOE_UNIT
    units_per_mtile = tm // MOE_UNIT
    cnt = counts.reshape(n_tiles, N_BUCKETS, LANES)[:, :, 0].astype(jnp.int32)
    seg = (cnt + MOE_UNIT - 1) // MOE_UNIT
    seg_start = jnp.cumsum(seg, axis=1) - seg
    before = jnp.cumsum(seg, axis=0) - seg
    total = jnp.sum(seg, axis=0)
    mtiles = (total + units_per_mtile - 1) // units_per_mtile
    mtile_end = jnp.cumsum(mtiles)
    group_start = (mtile_end - mtiles) * units_per_mtile
    n_mtiles = n_tiles + N_BUCKETS + -(-n_tiles * MOE_PAD_ROWS // tm)
    tile_group = jnp.minimum(jnp.sum(jnp.arange(n_mtiles)[:, None] >= mtile_end[None, :], axis=1), N_BUCKETS - 1)
    groups = jnp.arange(N_BUCKETS)
    pick = lambda table, g: jnp.sum(jnp.where(g[..., None] == groups, table, 0), axis=-1)

    q = jnp.arange(n_mtiles * units_per_mtile)
    g = jnp.repeat(tile_group, units_per_mtile)
    ql = q - pick(group_start, g)
    ends_g = pick((before + seg)[None], g[:, None])
    tile_of = jnp.minimum(jnp.sum(ql[:, None] >= ends_g, axis=1), n_tiles - 1)
    offset_g = pick((seg_start - before)[None], g[:, None])
    local = ql + jnp.sum(jnp.where(tile_of[:, None] == jnp.arange(n_tiles), offset_g, 0), axis=1)
    valid = (ql < pick(total, g)) & (q // units_per_mtile < mtile_end[-1])
    src = jnp.where(valid, tile_of * units_per_tile + local, units_per_tile - 1).astype(jnp.int32)

    ul = jnp.arange(units_per_tile)[None, :, None]
    seg_of = jnp.sum(ul >= (seg_start + seg)[:, None, :], axis=-1)
    gi = jnp.minimum(seg_of, N_BUCKETS - 1)
    glob = pick((group_start + before - seg_start)[:, None, :], gi) + ul[..., 0]
    dst = jnp.where(seg_of < N_BUCKETS, glob, 0).astype(jnp.int32).reshape(-1)

    first = tile_group // N_PAIRS * EXPERTS_PER_GROUP
    meta = jnp.concatenate([first + pick(jnp.array(PAIR_LO * N_EXPERT_GROUPS), tile_group),
                            first + pick(jnp.array(PAIR_HI * N_EXPERT_GROUPS), tile_group),
                            mtile_end[-1:]]).astype(jnp.int32)
    return meta, src, dst, n_mtiles


def _unit_gather(table_ref, first, src_ref, dst_ref, sem, wait):
    n_rows = dst_ref.shape[0]
    if wait:
        pltpu.make_async_copy(src_ref.at[pl.ds(0, n_rows)], dst_ref, sem).wait()
        return

    def body(k, c):
        u = pl.multiple_of(table_ref[first + k] * MOE_UNIT, MOE_UNIT)
        pltpu.make_async_copy(src_ref.at[pl.ds(u, MOE_UNIT)],
                              dst_ref.at[pl.ds(pl.multiple_of(k * MOE_UNIT, MOE_UNIT), MOE_UNIT)], sem).start()
        return c
    lax.fori_loop(0, n_rows // MOE_UNIT, body, 0, unroll=4)


def _moe_pair_kernel(meta_ref, src_ref, hs_ref, wga_ref, wua_ref, wda_ref, wgb_ref, wub_ref, wdb_ref, o_ref,
                     wa_ref, wad_ref, wb_ref, wbd_ref, buf_ref, sem):
    j = pl.program_id(0)
    n_steps = pl.num_programs(0)
    experts = (meta_ref[j], meta_ref[n_steps + j])
    n_used = meta_ref[2 * n_steps]
    n_units = o_ref.shape[0] // MOE_UNIT
    slot = j % 2
    prev = jnp.maximum(j - 1, 0)

    @pl.when(j == 0)
    def _():
        _unit_gather(src_ref, 0, hs_ref, buf_ref.at[0], sem.at[0], wait=False)

    @pl.when(j + 1 < n_steps)
    def _():
        _unit_gather(src_ref, (j + 1) * n_units, hs_ref, buf_ref.at[1 - slot], sem.at[1 - slot], wait=False)

    copies = ((wga_ref, wua_ref, wda_ref, wa_ref, wad_ref), (wgb_ref, wub_ref, wdb_ref, wb_ref, wbd_ref))
    for which, (wg_ref, wu_ref, wd_ref, w_ref, wdn_ref) in enumerate(copies):
        @pl.when((j == 0) | (experts[which] != meta_ref[which * n_steps + prev]))
        def _():
            w_ref[0] = wg_ref[0].astype(BF16)
            w_ref[1] = wu_ref[0].astype(BF16)
            wdn_ref[...] = wd_ref[0].astype(BF16)

    _unit_gather(src_ref, j * n_units, hs_ref, buf_ref.at[slot], sem.at[slot], wait=True)

    @pl.when(j < n_used)
    def _():
        h = buf_ref[slot, :, :D_MODEL].astype(BF16)
        gates = buf_ref[slot, :, D_MODEL:]
        lane = lax.broadcasted_iota(jnp.int32, (1, LANES), 1)
        out = None
        for e, w_ref, wdn_ref in zip(experts, (wa_ref, wb_ref), (wad_ref, wbd_ref)):
            gt = _dot(h, w_ref[0])
            up = _dot(h, w_ref[1])
            gate_lane = e // EXPERTS_PER_GROUP * ROUTER_SLAB + e % EXPERTS_PER_GROUP
            gate = jnp.sum(jnp.where(lane == gate_lane, gates, 0.0), axis=-1, keepdims=True)
            a = (gt * jax.nn.sigmoid(gt)) * up * gate
            part = _dot(a.astype(BF16), wdn_ref[...])
            out = part if out is None else out + part
        o_ref[...] = out

    @pl.when(j >= n_used)
    def _():
        o_ref[...] = jnp.zeros_like(o_ref)


def _moe_pair_call(meta, src, hs, wg, wu, wd, layer, n_mtiles, tm):
    wg, wu, wd = (a.reshape((-1,) + a.shape[2:]) for a in (wg, wu, wd))
    expert = lambda a, which: pl.BlockSpec(
        (1,) + a.shape[1:], lambda j, meta, src: (layer * N_EXPERTS + meta[which * n_mtiles + j], 0, 0))
    return pl.pallas_call(
        _moe_pair_kernel,
        grid_spec=pltpu.PrefetchScalarGridSpec(
            num_scalar_prefetch=2,
            grid=(n_mtiles,),
            in_specs=[pl.BlockSpec(memory_space=pl.ANY), expert(wg, 0), expert(wu, 0), expert(wd, 0),
                      expert(wg, 1), expert(wu, 1), expert(wd, 1)],
            out_specs=pl.BlockSpec((tm, D_MODEL), lambda j, meta, src: (j, 0)),
            scratch_shapes=[pltpu.VMEM((2, D_MODEL, D_EXPERT), BF16), pltpu.VMEM((D_EXPERT, D_MODEL), BF16),
                            pltpu.VMEM((2, D_MODEL, D_EXPERT), BF16), pltpu.VMEM((D_EXPERT, D_MODEL), BF16),
                            pltpu.VMEM((2, tm, MOE_ROW), F32), pltpu.SemaphoreType.DMA((2,))],
        ),
        out_shape=jax.ShapeDtypeStruct((n_mtiles * tm, D_MODEL), F32),
        compiler_params=pltpu.CompilerParams(dimension_semantics=("arbitrary",), vmem_limit_bytes=VMEM_LIMIT_BYTES),
        name="moe_experts",
    )(meta, src, hs, wg, wu, wd, wg, wu, wd)


def _combine_kernel(dst_ref, x1_ref, tok_ref, ys_ref, o_ref, buf_ref, sem):
    i = pl.program_id(0)
    n_slots = buf_ref.shape[1]
    n_units = n_slots // MOE_UNIT
    slot = i % 2

    @pl.when(i == 0)
    def _():
        _unit_gather(dst_ref, 0, ys_ref, buf_ref.at[0], sem.at[0], wait=False)

    @pl.when(i + 1 < pl.num_programs(0))
    def _():
        _unit_gather(dst_ref, (i + 1) * n_units, ys_ref, buf_ref.at[1 - slot], sem.at[1 - slot], wait=False)

    _unit_gather(dst_ref, i * n_units, ys_ref, buf_ref.at[slot], sem.at[slot], wait=True)

    y = buf_ref[slot]
    y_hi = y.astype(BF16)
    y_lo = (y - y_hi.astype(F32)).astype(BF16)
    unsort = _sort_matrix(tok_ref[:, SLOT_LANE:SLOT_LANE + 1].astype(jnp.int32), n_slots, 1)
    o_ref[...] = x1_ref[...] + _dot(unsort, y_hi) + _dot(unsort, y_lo)


def _combine_call(dst, x1, tok, ys, tm):
    n = x1.shape[0]
    n_slots = tm + MOE_PAD_ROWS
    return pl.pallas_call(
        _combine_kernel,
        grid_spec=pltpu.PrefetchScalarGridSpec(
            num_scalar_prefetch=1,
            grid=(n // tm,),
            in_specs=[pl.BlockSpec((tm, D_MODEL), lambda i, dst: (i, 0)),
                      pl.BlockSpec((tm, LANES), lambda i, dst: (i, 0)),
                      pl.BlockSpec(memory_space=pl.ANY)],
            out_specs=pl.BlockSpec((tm, D_MODEL), lambda i, dst: (i, 0)),
            scratch_shapes=[pltpu.VMEM((2, n_slots, D_MODEL), F32), pltpu.SemaphoreType.DMA((2,))],
        ),
        out_shape=jax.ShapeDtypeStruct((n, D_MODEL), F32),
        compiler_params=pltpu.CompilerParams(dimension_semantics=("arbitrary",), vmem_limit_bytes=VMEM_LIMIT_BYTES),
        name="moe_combine",
    )(dst, x1, tok, ys)


def kernel(x, norm_mix, w_in, sgu_norm, sgu_w, sgu_b, s5_lambda_re, s5_lambda_im, s5_log_dt, s5_b_re, s5_b_im,
           s5_c_re, s5_c_im, s5_d, s5_glu_w, s5_glu_b, q_norm, k_norm, rel_bias, out_norm, w_out, norm_ffn,
           router_group_w, router_group_b, router_expert_w, router_expert_b, w_gate, w_up, w_down):
    batch, seq, _ = x.shape
    n = batch * seq
    depth = w_in.shape[0]
    t = _tiles(n, seq)
    assert seq % ATTN_TQ == 0 and seq % (S5_T * t["s5_rows"]) == 0
    row_vec = lambda a: a.astype(F32)[None, :]

    lane_group = jnp.arange(LANES) // HEAD_DIM
    bd = (lane_group[:, None] == lane_group[None, :]).astype(BF16)
    block_chunk = jnp.arange(SGU_BLOCK) // CHUNK
    sgu_mask = block_chunk[None, :] <= block_chunk[:, None]

    xf = x.reshape(n, D_MODEL)
    for l in range(depth):
        sguw = jnp.where(sgu_mask[None], sgu_w[l], 0).astype(BF16)
        sgub = jnp.repeat(sgu_b[l].astype(F32).T, HEAD_DIM, axis=1)
        qg = row_vec(jnp.tile(q_norm[l], ATTN_HEADS)) * (HEAD_DIM ** -0.5)
        kg = row_vec(jnp.tile(k_norm[l], ATTN_HEADS))
        oa, u0, u1, q, k, v = _in_call(xf, row_vec(norm_mix[l]), w_in, l, row_vec(sgu_norm[l]), sguw, sgub,
                                      qg, kg, bd, t["tm_in"])

        sv, krev, sc, apow = _s5_tables(s5_lambda_re[l], s5_lambda_im[l], s5_log_dt[l], s5_b_re[l], s5_b_im[l],
                                        s5_c_re[l], s5_c_im[l], t["s5_rows"])
        y0, y1 = _s5_call(u0, u1, sv, krev, sc, apow, batch, t["s5_rows"])

        oc = _attn_call(q, k, v, _attn_diag(rel_bias[l]), batch, seq)

        wr, br = _router_tables(router_group_w[l], router_group_b[l], router_expert_w[l], router_expert_b[l])
        x1, hs, tok, counts = _out_call(oa, y0, y1, u0, u1, oc, xf, row_vec(s5_d[l]), s5_glu_w[l].astype(BF16), row_vec(s5_glu_b[l]),
                                 row_vec(out_norm[l]), w_out, l, row_vec(norm_ffn[l]), wr, br, bd,
                                 t["tm_out"])

        meta, src, dst, n_mtiles = _route_plan(counts, n, t["tm_out"])
        ys = _moe_pair_call(meta, src, hs, w_gate, w_up, w_down, l, n_mtiles, t["tm_out"])
        xf = _combine_call(dst, x1, tok, ys, t["tm_out"])
    return xf.reshape(batch, seq, D_MODEL)
```

```python
import jax
import jax.numpy as jnp
from jax import lax
from jax.experimental import pallas as pl
from jax.experimental.pallas import tpu as pltpu

F32 = jnp.float32
BF16 = jnp.bfloat16

D_MODEL = 1024
CHUNK = 64
HEAD_DIM = 64
SGU_WIDTH = 256
SGU_HEADS = 4
SGU_BLOCK = 128
S5_WIDTH = 256
S5_GROUP = 16
S5_N_GROUPS = 16
S5_STATE = 64
ATTN_WIDTH = 512
ATTN_HEADS = 8
BAND_CHUNKS = 9
MAX_REL = 256
IN_COLS = 2 * SGU_WIDTH + S5_WIDTH + 3 * ATTN_WIDTH
OUT_NORM_GROUP = 64
N_EXPERT_GROUPS = 4
EXPERTS_PER_GROUP = 4
N_EXPERTS = 16
D_EXPERT = 256
EPS = 1e-6
NEG_INF = -1e30

LANES = 128
SUBLANES = 8
VMEM_LIMIT_BYTES = 56 * 1024 * 1024

S5_T = 16
S5_ROW = S5_T * S5_WIDTH
S5_NSTATE = S5_N_GROUPS * S5_STATE
ATTN_TQ = 256
ATTN_PREV = (BAND_CHUNKS - 1) * CHUNK
ATTN_TK = ATTN_TQ + ATTN_PREV
ROUTER_SLAB = SUBLANES
MOE_ROW = D_MODEL + LANES
MOE_UNIT = SUBLANES
MOE_PAD_ROWS = N_EXPERT_GROUPS * MOE_UNIT
SLOT_LANE = N_EXPERT_GROUPS * ROUTER_SLAB


def _tiles(n_tokens, seq):
    def pick(pref, total):
        t = min(pref, total)
        assert total % t == 0
        return t
    return dict(
        tm_in=pick(512, seq),
        tm_out=pick(512, seq),
        s5_rows=pick(128, seq // S5_T),
    )


def _dot(a, b):
    return jnp.dot(a, b, preferred_element_type=F32)


def _dot_nt(a, b):
    return lax.dot_general(a, b, (((1,), (1,)), ((), ())), preferred_element_type=F32)


def _group_sumsq(x, bd_ref):
    x2 = (x * x).astype(BF16)
    parts = [_dot(x2[:, t * LANES:(t + 1) * LANES], bd_ref[...]) for t in range(x.shape[1] // LANES)]
    return jnp.concatenate(parts, axis=-1)


def _in_kernel(x_ref, g_ref, wf_ref, sgug_ref, sguw_ref, sgub_ref, qg_ref, kg_ref, bd_ref,
               oa_ref, u0_ref, u1_ref, q_ref, k_ref, v_ref, w_ref):
    tm = x_ref.shape[0]

    @pl.when(pl.program_id(0) == 0)
    def _():
        w_ref[...] = wf_ref[0].astype(BF16)

    x = x_ref[...]
    ms = jnp.mean(x * x, axis=-1, keepdims=True)
    hn = (x * lax.rsqrt(ms + EPS) * g_ref[...]).astype(BF16)

    z = jax.nn.gelu(_dot(hn, w_ref[:, 0:2 * SGU_WIDTH]))
    u = z[:, :SGU_WIDTH]
    v = z[:, SGU_WIDTH:]
    v = v * lax.rsqrt(jnp.mean(v * v, axis=-1, keepdims=True) + EPS) * sgug_ref[...]
    vb = v.astype(BF16)
    first_head = lax.broadcasted_iota(jnp.int32, (SGU_BLOCK, LANES), 1) < HEAD_DIM
    for r in range(tm // SGU_BLOCK):
        rows = slice(r * SGU_BLOCK, (r + 1) * SGU_BLOCK)
        for p in range(SGU_WIDTH // LANES):
            cols = slice(p * LANES, (p + 1) * LANES)
            vp = vb[rows, cols]
            mixed = jnp.where(first_head, _dot(sguw_ref[2 * p], vp), _dot(sguw_ref[2 * p + 1], vp))
            oa_ref[rows, cols] = u[rows, cols] * (mixed + sgub_ref[:, cols])

    c0 = 2 * SGU_WIDTH
    us = _dot(hn, w_ref[:, c0:c0 + S5_WIDTH])
    u0_ref[...] = us[:, :LANES]
    u1_ref[...] = us[:, LANES:]

    c0 += S5_WIDTH
    q = _dot(hn, w_ref[:, c0:c0 + ATTN_WIDTH])
    q_ref[...] = (q * lax.rsqrt(_group_sumsq(q, bd_ref) * (1.0 / HEAD_DIM) + EPS) * qg_ref[...]).astype(BF16)
    c0 += ATTN_WIDTH
    k = _dot(hn, w_ref[:, c0:c0 + ATTN_WIDTH])
    k_ref[...] = (k * lax.rsqrt(_group_sumsq(k, bd_ref) * (1.0 / HEAD_DIM) + EPS) * kg_ref[...]).astype(BF16)
    c0 += ATTN_WIDTH
    v_ref[...] = _dot(hn, w_ref[:, c0:c0 + ATTN_WIDTH]).astype(BF16)


def _in_call(x, g, w, layer, sgug, sguw, sgub, qg, kg, bd, tm):
    n = x.shape[0]
    row = lambda c: pl.BlockSpec((tm, c), lambda i: (i, 0))
    full = lambda a: pl.BlockSpec(a.shape, lambda i: (0,) * a.ndim)
    w_layer = pl.BlockSpec((1,) + w.shape[1:], lambda i: (layer, 0, 0), pipeline_mode=pl.Buffered(1))
    return pl.pallas_call(
        _in_kernel,
        grid=(n // tm,),
        in_specs=[row(D_MODEL), full(g), w_layer, full(sgug), full(sguw), full(sgub), full(qg), full(kg), full(bd)],
        out_specs=[row(SGU_WIDTH), row(LANES), row(LANES), row(ATTN_WIDTH), row(ATTN_WIDTH), row(ATTN_WIDTH)],
        out_shape=[jax.ShapeDtypeStruct((n, SGU_WIDTH), F32), jax.ShapeDtypeStruct((n, LANES), F32),
                   jax.ShapeDtypeStruct((n, LANES), F32), jax.ShapeDtypeStruct((n, ATTN_WIDTH), BF16),
                   jax.ShapeDtypeStruct((n, ATTN_WIDTH), BF16), jax.ShapeDtypeStruct((n, ATTN_WIDTH), BF16)],
        scratch_shapes=[pltpu.VMEM(w.shape[1:], BF16)],
        compiler_params=pltpu.CompilerParams(dimension_semantics=("arbitrary",), vmem_limit_bytes=VMEM_LIMIT_BYTES),
        name="in_proj",
    )(x, g, w, sgug, sguw, sgub, qg, kg, bd)


def _s5_expand_tables(sv_ref, sc_ref, wv_ref, wc_ref):
    pair_of_row = lax.broadcasted_iota(jnp.int32, (S5_ROW, 1), 0) // S5_GROUP % S5_N_GROUPS // 2
    for part in range(2):
        src = sv_ref[part]
        for j in range(S5_NSTATE // LANES):
            cols = slice(part * S5_NSTATE + j * LANES, part * S5_NSTATE + (j + 1) * LANES)
            wv_ref[:, cols] = jnp.where(pair_of_row == j, src, 0.0).astype(BF16)

    wc_ref[...] = jnp.zeros_like(wc_ref)
    lane = lax.broadcasted_iota(jnp.int32, (1, LANES), 1)
    per_tile = LANES // S5_GROUP
    for part in range(2):
        for g in range(S5_N_GROUPS):
            rows = slice(part * S5_NSTATE + g * S5_STATE, part * S5_NSTATE + (g + 1) * S5_STATE)
            keep = (lane >= g % per_tile * S5_GROUP) & (lane < (g % per_tile + 1) * S5_GROUP)
            for i in range(S5_T):
                tile = sc_ref[rows, i // per_tile * LANES:(i // per_tile + 1) * LANES]
                shift = (g % per_tile - i % per_tile) * S5_GROUP % LANES
                if shift:
                    tile = pltpu.roll(tile, shift, 1)
                q = i * (S5_WIDTH // LANES) + g // per_tile
                wc_ref[rows, q * LANES:(q + 1) * LANES] = jnp.where(keep, tile, 0.0).astype(BF16)


def _s5_kernel(u0_ref, u1_ref, sv_ref, krev_ref, sc_ref, apow_ref, y0_ref, y1_ref,
               wv_ref, wc_ref, ur_ref, s_ref, sp_ref, carry_ref):
    rows = ur_ref.shape[0]
    n_steps = apow_ref.shape[0]

    @pl.when((pl.program_id(0) == 0) & (pl.program_id(1) == 0))
    def _():
        _s5_expand_tables(sv_ref, sc_ref, wv_ref, wc_ref)

    @pl.when(pl.program_id(1) == 0)
    def _():
        carry_ref[...] = jnp.zeros_like(carry_ref)

    for t in range(S5_T):
        ur_ref[:, t * S5_WIDTH:t * S5_WIDTH + LANES] = u0_ref[pl.ds(t, rows, stride=S5_T), :].astype(BF16)
        ur_ref[:, t * S5_WIDTH + LANES:(t + 1) * S5_WIDTH] = u1_ref[pl.ds(t, rows, stride=S5_T), :].astype(BF16)

    s_ref[...] = _dot(ur_ref[...], wv_ref[...])
    row_id = lax.broadcasted_iota(jnp.int32, (rows, LANES), 0)
    for cb in range(S5_NSTATE // LANES):
        cre = slice(cb * LANES, (cb + 1) * LANES)
        cim = slice(S5_NSTATE + cb * LANES, S5_NSTATE + (cb + 1) * LANES)
        re = s_ref[:, cre]
        im = s_ref[:, cim]
        c_re = carry_ref[0:1, cre]
        c_im = carry_ref[0:1, cim]
        a_re = apow_ref[0, 0:1, cre]
        a_im = apow_ref[0, 1:2, cre]
        re = re + jnp.where(row_id == 0, a_re * c_re - a_im * c_im, 0.0)
        im = im + jnp.where(row_id == 0, a_re * c_im + a_im * c_re, 0.0)
        for k in range(n_steps):
            shift = 1 << k
            a_re = apow_ref[k, 0:1, cre]
            a_im = apow_ref[k, 1:2, cre]
            re_s = jnp.where(row_id >= shift, pltpu.roll(re, shift, 0), 0.0)
            im_s = jnp.where(row_id >= shift, pltpu.roll(im, shift, 0), 0.0)
            re, im = re + a_re * re_s - a_im * im_s, im + a_re * im_s + a_im * re_s
        sp_ref[:, cre] = jnp.where(row_id >= 1, pltpu.roll(re, 1, 0), c_re).astype(BF16)
        sp_ref[:, cim] = jnp.where(row_id >= 1, pltpu.roll(im, 1, 0), c_im).astype(BF16)
        carry_ref[0:1, cre] = re[rows - 1:rows, :]
        carry_ref[0:1, cim] = im[rows - 1:rows, :]

    for i in range(S5_T):
        cols = slice(i * S5_WIDTH, (i + 1) * S5_WIDTH)
        intra = _dot(ur_ref[:, 0:(i + 1) * S5_WIDTH], krev_ref[(S5_T - 1 - i) * S5_WIDTH:, :])
        y = intra + _dot(sp_ref[...], wc_ref[:, cols])
        y0_ref[pl.ds(i, rows, stride=S5_T), :] = y[:, :LANES]
        y1_ref[pl.ds(i, rows, stride=S5_T), :] = y[:, LANES:]


def _s5_call(u0, u1, sv, krev, sc, apow, batch, rows):
    n = u0.shape[0]
    tok = rows * S5_T
    tiles_per_seq = n // batch // tok
    const = lambda a: pl.BlockSpec(a.shape, lambda b, t: (0,) * a.ndim, pipeline_mode=pl.Buffered(1))
    blk = pl.BlockSpec((tok, LANES), lambda b, t: (b * tiles_per_seq + t, 0))
    half = jax.ShapeDtypeStruct((n, LANES), F32)
    return pl.pallas_call(
        _s5_kernel,
        grid=(batch, tiles_per_seq),
        in_specs=[blk, blk, const(sv), const(krev), const(sc), const(apow)],
        out_specs=[blk, blk],
        out_shape=[half, half],
        scratch_shapes=[pltpu.VMEM((S5_ROW, 2 * S5_NSTATE), BF16), pltpu.VMEM((2 * S5_NSTATE, S5_ROW), BF16),
                        pltpu.VMEM((rows, S5_ROW), BF16), pltpu.VMEM((rows, 2 * S5_NSTATE), F32),
                        pltpu.VMEM((rows, 2 * S5_NSTATE), BF16), pltpu.VMEM((SUBLANES, 2 * S5_NSTATE), F32)],
        compiler_params=pltpu.CompilerParams(dimension_semantics=("arbitrary", "arbitrary"),
                                             vmem_limit_bytes=VMEM_LIMIT_BYTES),
        name="s5_scan",
    )(u0, u1, sv, krev, sc, apow)


def _s5_tables(lam_re, lam_im, log_dt, b_re, b_im, c_re, c_im, rows):
    g_, p_, h_ = S5_N_GROUPS, S5_STATE, S5_GROUP
    hi = lax.Precision.HIGHEST
    lam_re, lam_im = lam_re.astype(F32), lam_im.astype(F32)
    dt = jnp.exp(log_dt.astype(F32))[:, None]
    tau = jnp.arange(S5_T + 1, dtype=F32)[:, None, None]
    mag = jnp.exp(tau * (lam_re * dt)[None])
    ap_re = mag * jnp.cos(tau * (lam_im * dt)[None])
    ap_im = mag * jnp.sin(tau * (lam_im * dt)[None])
    n_re, n_im = ap_re[1] - 1.0, ap_im[1]
    den = lam_re * lam_re + lam_im * lam_im
    k_re = ((n_re * lam_re + n_im * lam_im) / den)[..., None]
    k_im = ((n_im * lam_re - n_re * lam_im) / den)[..., None]
    b_re, b_im = b_re.astype(F32), b_im.astype(F32)
    bb_re = k_re * b_re - k_im * b_im
    bb_im = k_re * b_im + k_im * b_re
    c_re, c_im = c_re.astype(F32), c_im.astype(F32)
    ab_re = ap_re[:S5_T, :, :, None] * bb_re[None] - ap_im[:S5_T, :, :, None] * bb_im[None]
    ab_im = ap_re[:S5_T, :, :, None] * bb_im[None] + ap_im[:S5_T, :, :, None] * bb_re[None]

    col_group = jnp.arange(S5_WIDTH) // h_
    chan_group = jnp.arange(S5_ROW) // h_ % g_

    k_tau = (jnp.einsum('gop,tgpi->tgio', c_re, ab_re, precision=hi)
             - jnp.einsum('gop,tgpi->tgio', c_im, ab_im, precision=hi))
    krev = jnp.tile(k_tau[::-1].reshape(S5_ROW, h_), (1, g_))
    krev = jnp.where(chan_group[:, None] == col_group[None, :], krev, 0.0)

    def sv_part(ab):
        w = ab[::-1].transpose(0, 1, 3, 2).reshape(S5_ROW, p_)
        odd = (chan_group % 2 == 1)[:, None]
        return jnp.concatenate([jnp.where(odd, 0.0, w), jnp.where(odd, w, 0.0)], axis=1)
    sv = jnp.stack([sv_part(ab_re), sv_part(ab_im)])

    cp = lambda c: c.transpose(0, 2, 1)[:, :, None, :]
    w_re = cp(c_re) * ap_re[1:].transpose(1, 2, 0)[..., None] - cp(c_im) * ap_im[1:].transpose(1, 2, 0)[..., None]
    w_im = cp(c_re) * ap_im[1:].transpose(1, 2, 0)[..., None] + cp(c_im) * ap_re[1:].transpose(1, 2, 0)[..., None]
    sc = jnp.concatenate([w_re.reshape(S5_NSTATE, S5_T * h_), -w_im.reshape(S5_NSTATE, S5_T * h_)], axis=0)

    n_steps = max(1, (rows - 1).bit_length())
    pows = [(ap_re[S5_T].reshape(S5_NSTATE), ap_im[S5_T].reshape(S5_NSTATE))]
    for _ in range(n_steps - 1):
        r, i = pows[-1]
        pows.append((r * r - i * i, 2.0 * r * i))
    apow = jnp.stack([jnp.stack(p) for p in pows])
    return sv, krev.astype(BF16), sc, apow


def _attn_fill_bias(diag_ref, bias_ref):
    width = diag_ref.shape[1]
    qi = lax.broadcasted_iota(jnp.int32, (ATTN_TQ, ATTN_TK), 0)
    kj = lax.broadcasted_iota(jnp.int32, (ATTN_TQ, ATTN_TK), 1)
    q_chunk = qi // CHUNK + (BAND_CHUNKS - 1)
    k_chunk = kj // CHUNK
    in_band = (k_chunk <= q_chunk) & (k_chunk >= q_chunk - (BAND_CHUNKS - 1))
    for h in range(ATTN_HEADS):
        rows = jnp.broadcast_to(diag_ref[h:h + 1, :], (ATTN_TQ, width))
        base = pltpu.roll(rows, 0, 1, stride=1, stride_axis=0)[:, :ATTN_TK]
        for t in range(bias_ref.shape[0]):
            bias_ref[t, h] = jnp.where(in_band & (kj >= ATTN_PREV - t * ATTN_TQ), base, NEG_INF)


def _attn_kernel(q_ref, k0_ref, k1_ref, k2_ref, v0_ref, v1_ref, v2_ref, diag_ref, o_ref, bias_ref):
    @pl.when((pl.program_id(0) == 0) & (pl.program_id(1) == 0))
    def _():
        _attn_fill_bias(diag_ref, bias_ref)

    table = jnp.minimum(pl.program_id(1), bias_ref.shape[0] - 1)
    lane = lax.broadcasted_iota(jnp.int32, (1, LANES), 1)
    for p in range(ATTN_WIDTH // LANES):
        cols = slice(p * LANES, (p + 1) * LANES)
        qp = q_ref[:, cols]
        kcat = jnp.concatenate([k0_ref[:, cols], k1_ref[:, cols], k2_ref[:, cols]], axis=0)
        vcat = jnp.concatenate([v0_ref[:, cols], v1_ref[:, cols], v2_ref[:, cols]], axis=0)
        first = lane < HEAD_DIM
        zero = jnp.zeros_like(qp)
        qm = jnp.concatenate([jnp.where(first, qp, zero), jnp.where(first, zero, qp)], axis=0)
        bias = jnp.concatenate([bias_ref[table, 2 * p], bias_ref[table, 2 * p + 1]], axis=0)
        s = _dot_nt(qm, kcat) + bias
        e = jnp.exp(s - jnp.max(s, axis=-1, keepdims=True)).astype(BF16)
        o = _dot(e, jnp.concatenate([vcat, jnp.ones_like(vcat)], axis=1))
        r = o[:, :LANES] / o[:, LANES:]
        o_ref[:, cols] = jnp.where(first, r[:ATTN_TQ], r[ATTN_TQ:])


def _attn_call(q, k, v, diag, batch, seq):
    n = q.shape[0]
    tiles = seq // ATTN_TQ
    n_prev = ATTN_PREV // ATTN_TQ
    cur = pl.BlockSpec((ATTN_TQ, ATTN_WIDTH), lambda b, t: (b * tiles + t, 0))
    prev = lambda d: pl.BlockSpec((ATTN_TQ, ATTN_WIDTH), lambda b, t: (b * tiles + jnp.maximum(t - d, 0), 0))
    assert n_prev == 2
    return pl.pallas_call(
        _attn_kernel,
        grid=(batch, tiles),
        in_specs=[cur, prev(2), prev(1), cur, prev(2), prev(1), cur, pl.BlockSpec(diag.shape, lambda b, t: (0, 0))],
        out_specs=cur,
        out_shape=jax.ShapeDtypeStruct((n, ATTN_WIDTH), F32),
        scratch_shapes=[pltpu.VMEM((n_prev + 1, ATTN_HEADS, ATTN_TQ, ATTN_TK), F32)],
        compiler_params=pltpu.CompilerParams(dimension_semantics=("arbitrary", "arbitrary"),
                                             vmem_limit_bytes=VMEM_LIMIT_BYTES),
        name="band_attn",
    )(q, k, k, k, v, v, v, diag)


def _attn_diag(rel_bias):
    width = ATTN_TQ + ATTN_TK
    n = jnp.arange(width)
    offset = jnp.where(n < ATTN_TK, n, n - width)
    rel = jnp.clip(ATTN_PREV - offset, -MAX_REL, MAX_REL) + MAX_REL
    return jnp.take(rel_bias.astype(F32), rel, axis=1)


def _router_gates(lt):
    tm = lt.shape[1]
    row = lax.broadcasted_iota(jnp.int32, (ROUTER_SLAB, tm), 0)
    gl = lt[0:ROUTER_SLAB]
    gmax = jnp.max(gl, axis=0, keepdims=True)
    p_g = 1.0 / jnp.sum(jnp.exp(gl - gmax), axis=0, keepdims=True)
    g_top = jnp.min(jnp.where(gl == gmax, row, ROUTER_SLAB), axis=0, keepdims=True)
    el = jnp.zeros((ROUTER_SLAB, tm), F32)
    for g in range(N_EXPERT_GROUPS):
        el = el + jnp.where(g_top == g, lt[(g + 1) * ROUTER_SLAB:(g + 2) * ROUTER_SLAB], 0.0)
    ee = jnp.exp(el - jnp.max(el, axis=0, keepdims=True))
    ep = ee / jnp.sum(ee, axis=0, keepdims=True)
    p1 = jnp.max(ep, axis=0, keepdims=True)
    i1 = jnp.min(jnp.where(ep == p1, row, ROUTER_SLAB), axis=0, keepdims=True)
    rest = jnp.where(row == i1, -1.0, ep)
    p2 = jnp.max(rest, axis=0, keepdims=True)
    i2 = jnp.min(jnp.where(rest == p2, row, ROUTER_SLAB), axis=0, keepdims=True)
    tot = p1 + p2
    w = (jnp.where(row == i1, p1 / tot, 0.0) + jnp.where(row == i2, p2 / tot, 0.0)) * p_g
    return [jnp.where(g_top == g, w, 0.0) for g in range(N_EXPERT_GROUPS)], g_top


def _sort_matrix(slot, n_slots, slot_axis):
    shape = (n_slots, slot.shape[1]) if slot_axis == 0 else (slot.shape[0], n_slots)
    return jnp.where(lax.broadcasted_iota(jnp.int32, shape, slot_axis) == slot, 1.0, 0.0).astype(BF16)


def _out_kernel(oa_ref, y0_ref, y1_ref, u0_ref, u1_ref, oc_ref, x_ref, d_ref, gluw_ref, glub_ref, og_ref, woutf_ref,
                fg_ref, wr_ref, br_ref, bd_ref, x1_ref, hs_ref, tok_ref, cnt_ref, wout_ref):
    tm = x_ref.shape[0]
    n_slots = hs_ref.shape[0]

    @pl.when(pl.program_id(0) == 0)
    def _():
        wout_ref[...] = woutf_ref[0].astype(BF16)

    y = jnp.concatenate([y0_ref[...], y1_ref[...]], axis=-1)
    u = jnp.concatenate([u0_ref[...], u1_ref[...]], axis=-1)
    y = jax.nn.gelu(y + d_ref[...] * u)
    ob = y * jax.nn.sigmoid(_dot(y.astype(BF16), gluw_ref[...]) + glub_ref[...])
    o = jnp.concatenate([oa_ref[...], ob, oc_ref[...]], axis=-1)
    on = o * lax.rsqrt(_group_sumsq(o, bd_ref) * (1.0 / OUT_NORM_GROUP) + EPS) * og_ref[...]
    x1 = x_ref[...] + _dot(on.astype(BF16), wout_ref[...])
    x1_ref[...] = x1
    ms = jnp.mean(x1 * x1, axis=-1, keepdims=True)
    hb = (x1 * lax.rsqrt(ms + EPS) * fg_ref[...]).astype(BF16)
    lt = _dot(hb, wr_ref[...]).T + br_ref[...]
    slabs, g_top = _router_gates(lt)

    row = lax.broadcasted_iota(jnp.int32, (ROUTER_SLAB, tm), 0)
    lane = lax.broadcasted_iota(jnp.int32, (ROUTER_SLAB, tm), 1)
    member = jnp.where(row == g_top, 1.0, 0.0)
    cum = member
    shift = 1
    while shift < tm:
        cum = cum + jnp.where(lane >= shift, pltpu.roll(cum, shift, 1), 0.0)
        shift *= 2
    count = cum[:, tm - 1:tm]
    padded = jnp.floor((count + (MOE_UNIT - 1)) * (1.0 / MOE_UNIT)) * MOE_UNIT
    group_row = lax.broadcasted_iota(jnp.int32, (ROUTER_SLAB, 1), 0)
    start = jnp.zeros((ROUTER_SLAB, 1), F32)
    for g in range(1, N_EXPERT_GROUPS):
        start = start + jnp.where(group_row >= g, padded[g - 1:g, :], 0.0)
    slot = jnp.sum(member * (start + cum - 1.0), axis=0, keepdims=True)
    cnt_ref[...] = jnp.broadcast_to(count, cnt_ref.shape)

    pad = jnp.zeros((LANES - SLOT_LANE - ROUTER_SLAB, tm), F32)
    gates = jnp.concatenate(slabs + [jnp.broadcast_to(slot, (ROUTER_SLAB, tm)), pad], axis=0).T
    tok_ref[...] = gates

    perm = _sort_matrix(slot.astype(jnp.int32), n_slots, 0)
    hs_ref[:, :D_MODEL] = _dot(perm, hb)
    g_hi = gates.astype(BF16)
    g_mid = (gates - g_hi.astype(F32)).astype(BF16)
    g_lo = (gates - g_hi.astype(F32) - g_mid.astype(F32)).astype(BF16)
    gs = _dot(perm, jnp.concatenate([g_hi, g_mid, g_lo], axis=-1))
    hs_ref[:, D_MODEL:] = gs[:, :LANES] + gs[:, LANES:2 * LANES] + gs[:, 2 * LANES:]


def _out_call(oa, y0, y1, u0, u1, oc, x, d, gluw, glub, og, wout, layer, fg, wr, br, bd, tm):
    n = x.shape[0]
    n_slots = tm + MOE_PAD_ROWS
    row = lambda c: pl.BlockSpec((tm, c), lambda i: (i, 0))
    full = lambda a: pl.BlockSpec(a.shape, lambda i: (0,) * a.ndim)
    return pl.pallas_call(
        _out_kernel,
        grid=(n // tm,),
        in_specs=[row(SGU_WIDTH), row(LANES), row(LANES), row(LANES), row(LANES), row(ATTN_WIDTH), row(D_MODEL),
                  full(d), full(gluw),
                  full(glub), full(og), pl.BlockSpec((1,) + wout.shape[1:], lambda i: (layer, 0, 0), pipeline_mode=pl.Buffered(1)),
                  full(fg), full(wr), full(br), full(bd)],
        out_specs=[row(D_MODEL), pl.BlockSpec((n_slots, MOE_ROW), lambda i: (i, 0)), row(LANES),
                   pl.BlockSpec((ROUTER_SLAB, LANES), lambda i: (i, 0))],
        out_shape=[jax.ShapeDtypeStruct((n, D_MODEL), F32), jax.ShapeDtypeStruct((n // tm * n_slots, MOE_ROW), F32),
                   jax.ShapeDtypeStruct((n, LANES), F32),
                   jax.ShapeDtypeStruct((n // tm * ROUTER_SLAB, LANES), F32)],
        scratch_shapes=[pltpu.VMEM(wout.shape[1:], BF16)],
        compiler_params=pltpu.CompilerParams(dimension_semantics=("arbitrary",), vmem_limit_bytes=VMEM_LIMIT_BYTES),
        name="out_proj",
    )(oa, y0, y1, u0, u1, oc, x, d, gluw, glub, og, wout, fg, wr, br, bd)


def _router_tables(wg, bg, we, be):
    w = jnp.zeros((D_MODEL, LANES), F32)
    b = jnp.full((LANES,), NEG_INF, F32)
    w = w.at[:, 0:N_EXPERT_GROUPS].set(wg.astype(F32))
    b = b.at[0:N_EXPERT_GROUPS].set(bg.astype(F32))
    for g in range(N_EXPERT_GROUPS):
        c0 = (g + 1) * ROUTER_SLAB
        w = w.at[:, c0:c0 + EXPERTS_PER_GROUP].set(we[g].astype(F32))
        b = b.at[c0:c0 + EXPERTS_PER_GROUP].set(be[g].astype(F32))
    return w.astype(BF16), b[:, None]


def _route_plan(counts, n, tm):
    n_tiles = n // tm
    units_per_tile = (tm + MOE_PAD_ROWS) // MOE_UNIT
    units_per_mtile = tm // MOE_UNIT
    cnt = counts.reshape(n_tiles, ROUTER_SLAB, LANES)[:, :N_EXPERT_GROUPS, 0].astype(jnp.int32)
    seg = (cnt + MOE_UNIT - 1) // MOE_UNIT
    seg_start = jnp.cumsum(seg, axis=1) - seg
    before = jnp.cumsum(seg, axis=0) - seg
    total = jnp.sum(seg, axis=0)
    mtiles = (total + units_per_mtile - 1) // units_per_mtile
    mtile_end = jnp.cumsum(mtiles)
    group_start = (mtile_end - mtiles) * units_per_mtile
    n_mtiles = n_tiles + N_EXPERT_GROUPS + -(-n_tiles * MOE_PAD_ROWS // tm)
    tile_group = jnp.minimum(jnp.sum(jnp.arange(n_mtiles)[:, None] >= mtile_end[None, :], axis=1),
                             N_EXPERT_GROUPS - 1)
    meta = jnp.concatenate([tile_group, mtile_end[-1:]]).astype(jnp.int32)
    groups = jnp.arange(N_EXPERT_GROUPS)
    pick = lambda table, g: jnp.sum(jnp.where(g[..., None] == groups, table, 0), axis=-1)

    q = jnp.arange(n_mtiles * units_per_mtile)
    g = jnp.repeat(tile_group, units_per_mtile)
    ql = q - pick(group_start, g)
    ends_g = pick((before + seg)[None], g[:, None])
    tile_of = jnp.minimum(jnp.sum(ql[:, None] >= ends_g, axis=1), n_tiles - 1)
    offset_g = pick((seg_start - before)[None], g[:, None])
    local = ql + jnp.sum(jnp.where(tile_of[:, None] == jnp.arange(n_tiles), offset_g, 0), axis=1)
    valid = (ql < pick(total, g)) & (q // units_per_mtile < mtile_end[-1])
    src = jnp.where(valid, tile_of * units_per_tile + local, units_per_tile - 1).astype(jnp.int32)

    ul = jnp.arange(units_per_tile)[None, :, None]
    seg_of = jnp.sum(ul >= (seg_start + seg)[:, None, :], axis=-1)
    gi = jnp.minimum(seg_of, N_EXPERT_GROUPS - 1)
    glob = pick((group_start + before - seg_start)[:, None, :], gi) + ul[..., 0]
    dst = jnp.where(seg_of < N_EXPERT_GROUPS, glob, 0).astype(jnp.int32).reshape(-1)
    return meta, src, dst, n_mtiles


def _unit_gather(table_ref, first, src_ref, dst_ref, sem, wait):
    n_rows = dst_ref.shape[0]
    if wait:
        pltpu.make_async_copy(src_ref.at[pl.ds(0, n_rows)], dst_ref, sem).wait()
        return

    def body(k, c):
        u = pl.multiple_of(table_ref[first + k] * MOE_UNIT, MOE_UNIT)
        pltpu.make_async_copy(src_ref.at[pl.ds(u, MOE_UNIT)],
                              dst_ref.at[pl.ds(pl.multiple_of(k * MOE_UNIT, MOE_UNIT), MOE_UNIT)], sem).start()
        return c
    lax.fori_loop(0, n_rows // MOE_UNIT, body, 0, unroll=4)


def _moe_group_kernel(meta_ref, src_ref, hs_ref, wg_ref, wu_ref, wd_ref, o_ref, wgb_ref, wub_ref, wdb_ref,
                      buf_ref, sem):
    j = pl.program_id(0)
    n_steps = pl.num_programs(0)
    n_used = meta_ref[n_steps]
    group = meta_ref[j]
    n_units = o_ref.shape[0] // MOE_UNIT
    slot = j % 2

    @pl.when(j == 0)
    def _():
        _unit_gather(src_ref, 0, hs_ref, buf_ref.at[0], sem.at[0], wait=False)

    @pl.when(j + 1 < n_steps)
    def _():
        _unit_gather(src_ref, (j + 1) * n_units, hs_ref, buf_ref.at[1 - slot], sem.at[1 - slot], wait=False)

    @pl.when((j == 0) | (group != meta_ref[jnp.maximum(j - 1, 0)]))
    def _():
        wgb_ref[...] = wg_ref[...].astype(BF16)
        wub_ref[...] = wu_ref[...].astype(BF16)
        wdb_ref[...] = wd_ref[...].astype(BF16)

    _unit_gather(src_ref, j * n_units, hs_ref, buf_ref.at[slot], sem.at[slot], wait=True)

    @pl.when(j < n_used)
    def _():
        h = buf_ref[slot, :, :D_MODEL].astype(BF16)
        gates = buf_ref[slot, :, D_MODEL:]
        lane = lax.broadcasted_iota(jnp.int32, (1, LANES), 1)
        out = None
        for e in range(EXPERTS_PER_GROUP):
            gt = _dot(h, wgb_ref[e])
            up = _dot(h, wub_ref[e])
            gate = jnp.sum(jnp.where(lane == group * ROUTER_SLAB + e, gates, 0.0), axis=-1, keepdims=True)
            a = (gt * jax.nn.sigmoid(gt)) * up * gate
            part = _dot(a.astype(BF16), wdb_ref[e])
            out = part if out is None else out + part
        o_ref[...] = out

    @pl.when(j >= n_used)
    def _():
        o_ref[...] = jnp.zeros_like(o_ref)


def _moe_group_call(meta, src, hs, wg, wu, wd, layer, n_mtiles, tm):
    wg, wu, wd = (a.reshape((-1,) + a.shape[2:]) for a in (wg, wu, wd))
    by_group = lambda a: pl.BlockSpec((EXPERTS_PER_GROUP,) + a.shape[1:],
                                      lambda j, meta, src: (layer * N_EXPERT_GROUPS + meta[j], 0, 0))
    return pl.pallas_call(
        _moe_group_kernel,
        grid_spec=pltpu.PrefetchScalarGridSpec(
            num_scalar_prefetch=2,
            grid=(n_mtiles,),
            in_specs=[pl.BlockSpec(memory_space=pl.ANY), by_group(wg), by_group(wu), by_group(wd)],
            out_specs=pl.BlockSpec((tm, D_MODEL), lambda j, meta, src: (j, 0)),
            scratch_shapes=[pltpu.VMEM((EXPERTS_PER_GROUP, D_MODEL, D_EXPERT), BF16),
                            pltpu.VMEM((EXPERTS_PER_GROUP, D_MODEL, D_EXPERT), BF16),
                            pltpu.VMEM((EXPERTS_PER_GROUP, D_EXPERT, D_MODEL), BF16),
                            pltpu.VMEM((2, tm, MOE_ROW), F32), pltpu.SemaphoreType.DMA((2,))],
        ),
        out_shape=jax.ShapeDtypeStruct((n_mtiles * tm, D_MODEL), F32),
        compiler_params=pltpu.CompilerParams(dimension_semantics=("arbitrary",), vmem_limit_bytes=VMEM_LIMIT_BYTES),
        name="moe_experts",
    )(meta, src, hs, wg, wu, wd)


def _combine_kernel(dst_ref, x1_ref, tok_ref, ys_ref, o_ref, buf_ref, sem):
    i = pl.program_id(0)
    n_slots = buf_ref.shape[1]
    n_units = n_slots // MOE_UNIT
    slot = i % 2

    @pl.when(i == 0)
    def _():
        _unit_gather(dst_ref, 0, ys_ref, buf_ref.at[0], sem.at[0], wait=False)

    @pl.when(i + 1 < pl.num_programs(0))
    def _():
        _unit_gather(dst_ref, (i + 1) * n_units, ys_ref, buf_ref.at[1 - slot], sem.at[1 - slot], wait=False)

    _unit_gather(dst_ref, i * n_units, ys_ref, buf_ref.at[slot], sem.at[slot], wait=True)

    y = buf_ref[slot]
    y_hi = y.astype(BF16)
    y_lo = (y - y_hi.astype(F32)).astype(BF16)
    unsort = _sort_matrix(tok_ref[:, SLOT_LANE:SLOT_LANE + 1].astype(jnp.int32), n_slots, 1)
    o_ref[...] = x1_ref[...] + _dot(unsort, y_hi) + _dot(unsort, y_lo)


def _combine_call(dst, x1, tok, ys, tm):
    n = x1.shape[0]
    n_slots = tm + MOE_PAD_ROWS
    return pl.pallas_call(
        _combine_kernel,
        grid_spec=pltpu.PrefetchScalarGridSpec(
            num_scalar_prefetch=1,
            grid=(n // tm,),
            in_specs=[pl.BlockSpec((tm, D_MODEL), lambda i, dst: (i, 0)),
                      pl.BlockSpec((tm, LANES), lambda i, dst: (i, 0)),
                      pl.BlockSpec(memory_space=pl.ANY)],
            out_specs=pl.BlockSpec((tm, D_MODEL), lambda i, dst: (i, 0)),
            scratch_shapes=[pltpu.VMEM((2, n_slots, D_MODEL), F32), pltpu.SemaphoreType.DMA((2,))],
        ),
        out_shape=jax.ShapeDtypeStruct((n, D_MODEL), F32),
        compiler_params=pltpu.CompilerParams(dimension_semantics=("arbitrary",), vmem_limit_bytes=VMEM_LIMIT_BYTES),
        name="moe_combine",
    )(dst, x1, tok, ys)


def kernel(x, norm_mix, w_in, sgu_norm, sgu_w, sgu_b, s5_lambda_re, s5_lambda_im, s5_log_dt, s5_b_re, s5_b_im,
           s5_c_re, s5_c_im, s5_d, s5_glu_w, s5_glu_b, q_norm, k_norm, rel_bias, out_norm, w_out, norm_ffn,
           router_group_w, router_group_b, router_expert_w, router_expert_b, w_gate, w_up, w_down):
    batch, seq, _ = x.shape
    n = batch * seq
    depth = w_in.shape[0]
    t = _tiles(n, seq)
    assert seq % ATTN_TQ == 0 and seq % (S5_T * t["s5_rows"]) == 0
    row_vec = lambda a: a.astype(F32)[None, :]

    lane_group = jnp.arange(LANES) // HEAD_DIM
    bd = (lane_group[:, None] == lane_group[None, :]).astype(BF16)
    block_chunk = jnp.arange(SGU_BLOCK) // CHUNK
    sgu_mask = block_chunk[None, :] <= block_chunk[:, None]

    xf = x.reshape(n, D_MODEL)
    for l in range(depth):
        sguw = jnp.where(sgu_mask[None], sgu_w[l], 0).astype(BF16)
        sgub = jnp.repeat(sgu_b[l].astype(F32).T, HEAD_DIM, axis=1)
        qg = row_vec(jnp.tile(q_norm[l], ATTN_HEADS)) * (HEAD_DIM ** -0.5)
        kg = row_vec(jnp.tile(k_norm[l], ATTN_HEADS))
        oa, u0, u1, q, k, v = _in_call(xf, row_vec(norm_mix[l]), w_in, l, row_vec(sgu_norm[l]), sguw, sgub,
                                      qg, kg, bd, t["tm_in"])

        sv, krev, sc, apow = _s5_tables(s5_lambda_re[l], s5_lambda_im[l], s5_log_dt[l], s5_b_re[l], s5_b_im[l],
                                        s5_c_re[l], s5_c_im[l], t["s5_rows"])
        y0, y1 = _s5_call(u0, u1, sv, krev, sc, apow, batch, t["s5_rows"])

        oc = _attn_call(q, k, v, _attn_diag(rel_bias[l]), batch, seq)

        wr, br = _router_tables(router_group_w[l], router_group_b[l], router_expert_w[l], router_expert_b[l])
        x1, hs, tok, counts = _out_call(oa, y0, y1, u0, u1, oc, xf, row_vec(s5_d[l]), s5_glu_w[l].astype(BF16), row_vec(s5_glu_b[l]),
                                 row_vec(out_norm[l]), w_out, l, row_vec(norm_ffn[l]), wr, br, bd,
                                 t["tm_out"])

        meta, src, dst, n_mtiles = _route_plan(counts, n, t["tm_out"])
        ys = _moe_group_call(meta, src, hs, w_gate, w_up, w_down, l, n_mtiles, t["tm_out"])
        xf = _combine_call(dst, x1, tok, ys, t["tm_out"])
    return xf.reshape(batch, seq, D_MODEL)
```

```python
import jax
import jax.numpy as jnp
from jax import lax
from jax.experimental import pallas as pl
from jax.experimental.pallas import tpu as pltpu

F32 = jnp.float32
BF16 = jnp.bfloat16

D_MODEL = 1024
CHUNK = 64
HEAD_DIM = 64
SGU_WIDTH = 256
SGU_HEADS = 4
SGU_BLOCK = 128
S5_WIDTH = 256
S5_GROUP = 16
S5_N_GROUPS = 16
S5_STATE = 64
ATTN_WIDTH = 512
ATTN_HEADS = 8
BAND_CHUNKS = 9
MAX_REL = 256
IN_COLS = 2 * SGU_WIDTH + S5_WIDTH + 3 * ATTN_WIDTH
OUT_NORM_GROUP = 64
N_EXPERT_GROUPS = 4
EXPERTS_PER_GROUP = 4
N_EXPERTS = 16
D_EXPERT = 256
EPS = 1e-6
NEG_INF = -1e30

LANES = 128
SUBLANES = 8
VMEM_LIMIT_BYTES = 56 * 1024 * 1024

S5_T = 16
S5_ROW = S5_T * S5_WIDTH
S5_NSTATE = S5_N_GROUPS * S5_STATE
S5_QUAD = 4
S5_SETS = 4
S5_QUADS = S5_T // S5_QUAD
S5_BLOCK = S5_QUAD * (S5_N_GROUPS // S5_SETS) * S5_GROUP
ATTN_TQ = 256
ATTN_PREV = (BAND_CHUNKS - 1) * CHUNK
ATTN_TK = ATTN_TQ + ATTN_PREV
ROUTER_SLAB = SUBLANES
MOE_ROW = D_MODEL + LANES
MOE_UNIT = SUBLANES
MOE_PAD_ROWS = N_EXPERT_GROUPS * MOE_UNIT
SLOT_LANE = N_EXPERT_GROUPS * ROUTER_SLAB


def _tiles(n_tokens, seq):
    def pick(pref, total):
        t = min(pref, total)
        assert total % t == 0
        return t
    return dict(
        tm_in=pick(512, seq),
        tm_out=pick(512, seq),
        s5_rows=pick(256, seq // S5_T),
    )


def _dot(a, b):
    return jnp.dot(a, b, preferred_element_type=F32)


def _dot_nt(a, b):
    return lax.dot_general(a, b, (((1,), (1,)), ((), ())), preferred_element_type=F32)


def _group_sumsq(x, bd_ref):
    x2 = (x * x).astype(BF16)
    parts = [_dot(x2[:, t * LANES:(t + 1) * LANES], bd_ref[...]) for t in range(x.shape[1] // LANES)]
    return jnp.concatenate(parts, axis=-1)


def _in_kernel(x_ref, g_ref, wf_ref, sgug_ref, sguw_ref, sgub_ref, qg_ref, kg_ref, bd_ref,
               oa_ref, u0_ref, u1_ref, q_ref, k_ref, v_ref, w_ref):
    tm = x_ref.shape[0]

    @pl.when(pl.program_id(0) == 0)
    def _():
        w_ref[...] = wf_ref[0].astype(BF16)

    x = x_ref[...]
    ms = jnp.mean(x * x, axis=-1, keepdims=True)
    hn = (x * lax.rsqrt(ms + EPS) * g_ref[...]).astype(BF16)

    z = jax.nn.gelu(_dot(hn, w_ref[:, 0:2 * SGU_WIDTH]))
    u = z[:, :SGU_WIDTH]
    v = z[:, SGU_WIDTH:]
    v = v * lax.rsqrt(jnp.mean(v * v, axis=-1, keepdims=True) + EPS) * sgug_ref[...]
    vb = v.astype(BF16)
    first_head = lax.broadcasted_iota(jnp.int32, (SGU_BLOCK, LANES), 1) < HEAD_DIM
    for r in range(tm // SGU_BLOCK):
        rows = slice(r * SGU_BLOCK, (r + 1) * SGU_BLOCK)
        for p in range(SGU_WIDTH // LANES):
            cols = slice(p * LANES, (p + 1) * LANES)
            vp = vb[rows, cols]
            mixed = jnp.where(first_head, _dot(sguw_ref[2 * p], vp), _dot(sguw_ref[2 * p + 1], vp))
            oa_ref[rows, cols] = u[rows, cols] * (mixed + sgub_ref[:, cols])

    c0 = 2 * SGU_WIDTH
    us = _dot(hn, w_ref[:, c0:c0 + S5_WIDTH])
    u0_ref[...] = us[:, :LANES]
    u1_ref[...] = us[:, LANES:]

    c0 += S5_WIDTH
    q = _dot(hn, w_ref[:, c0:c0 + ATTN_WIDTH])
    q_ref[...] = (q * lax.rsqrt(_group_sumsq(q, bd_ref) * (1.0 / HEAD_DIM) + EPS) * qg_ref[...]).astype(BF16)
    c0 += ATTN_WIDTH
    k = _dot(hn, w_ref[:, c0:c0 + ATTN_WIDTH])
    k_ref[...] = (k * lax.rsqrt(_group_sumsq(k, bd_ref) * (1.0 / HEAD_DIM) + EPS) * kg_ref[...]).astype(BF16)
    c0 += ATTN_WIDTH
    v_ref[...] = _dot(hn, w_ref[:, c0:c0 + ATTN_WIDTH]).astype(BF16)


def _in_call(x, g, w, layer, sgug, sguw, sgub, qg, kg, bd, tm):
    n = x.shape[0]
    row = lambda c: pl.BlockSpec((tm, c), lambda i: (i, 0))
    full = lambda a: pl.BlockSpec(a.shape, lambda i: (0,) * a.ndim)
    w_layer = pl.BlockSpec((1,) + w.shape[1:], lambda i: (layer, 0, 0), pipeline_mode=pl.Buffered(1))
    return pl.pallas_call(
        _in_kernel,
        grid=(n // tm,),
        in_specs=[row(D_MODEL), full(g), w_layer, full(sgug), full(sguw), full(sgub), full(qg), full(kg), full(bd)],
        out_specs=[row(SGU_WIDTH), row(LANES), row(LANES), row(ATTN_WIDTH), row(ATTN_WIDTH), row(ATTN_WIDTH)],
        out_shape=[jax.ShapeDtypeStruct((n, SGU_WIDTH), F32), jax.ShapeDtypeStruct((n, LANES), F32),
                   jax.ShapeDtypeStruct((n, LANES), F32), jax.ShapeDtypeStruct((n, ATTN_WIDTH), BF16),
                   jax.ShapeDtypeStruct((n, ATTN_WIDTH), BF16), jax.ShapeDtypeStruct((n, ATTN_WIDTH), BF16)],
        scratch_shapes=[pltpu.VMEM(w.shape[1:], BF16)],
        compiler_params=pltpu.CompilerParams(dimension_semantics=("arbitrary",), vmem_limit_bytes=VMEM_LIMIT_BYTES),
        name="in_proj",
    )(x, g, w, sgug, sguw, sgub, qg, kg, bd)


def _s5_kernel(u0_ref, u1_ref, wv_ref, kb_ref, wc_ref, apow_ref, y0_ref, y1_ref, ur_ref, s_ref, sp_ref, yb_ref, carry_ref):
    rows = ur_ref.shape[0]
    n_steps = apow_ref.shape[0]
    half = LANES // 2
    low = lax.broadcasted_iota(jnp.int32, (1, LANES), 1) < half
    block = lambda tq, gs: slice((tq * S5_SETS + gs) * S5_BLOCK, (tq * S5_SETS + gs + 1) * S5_BLOCK)

    @pl.when(pl.program_id(1) == 0)
    def _():
        carry_ref[...] = jnp.zeros_like(carry_ref)

    for tq in range(S5_QUADS):
        for lt in range(S5_QUAD // 2):
            t0 = tq * S5_QUAD + 2 * lt
            for pair, u_ref in enumerate((u0_ref, u1_ref)):
                a = u_ref[pl.ds(t0, rows, stride=S5_T), :]
                b = u_ref[pl.ds(t0 + 1, rows, stride=S5_T), :]
                for par in range(2):
                    tile = jnp.where(low, a, pltpu.roll(b, half, 1)) if par == 0 else jnp.where(low, pltpu.roll(a, half, 1), b)
                    col = block(tq, 2 * pair + par).start + lt * LANES
                    ur_ref[:, col:col + LANES] = tile.astype(BF16)

    for gs in range(S5_SETS):
        v = None
        for tq in range(S5_QUADS):
            part = _dot(ur_ref[:, block(tq, gs)], wv_ref[tq * S5_SETS + gs])
            v = part if v is None else v + part
        s_ref[:, gs * S5_BLOCK:(gs + 1) * S5_BLOCK] = v[:, :S5_BLOCK]
        s_ref[:, S5_NSTATE + gs * S5_BLOCK:S5_NSTATE + (gs + 1) * S5_BLOCK] = v[:, S5_BLOCK:]

    row_id = lax.broadcasted_iota(jnp.int32, (rows, LANES), 0)
    for cb in range(S5_NSTATE // LANES):
        cre = slice(cb * LANES, (cb + 1) * LANES)
        cim = slice(S5_NSTATE + cb * LANES, S5_NSTATE + (cb + 1) * LANES)
        re = s_ref[:, cre]
        im = s_ref[:, cim]
        c_re = carry_ref[0:1, cre]
        c_im = carry_ref[0:1, cim]
        a_re = apow_ref[0, 0:1, cre]
        a_im = apow_ref[0, 1:2, cre]
        re = re + jnp.where(row_id == 0, a_re * c_re - a_im * c_im, 0.0)
        im = im + jnp.where(row_id == 0, a_re * c_im + a_im * c_re, 0.0)
        for k in range(n_steps):
            shift = 1 << k
            a_re = apow_ref[k, 0:1, cre]
            a_im = apow_ref[k, 1:2, cre]
            re_s = jnp.where(row_id >= shift, pltpu.roll(re, shift, 0), 0.0)
            im_s = jnp.where(row_id >= shift, pltpu.roll(im, shift, 0), 0.0)
            re, im = re + a_re * re_s - a_im * im_s, im + a_re * im_s + a_im * re_s
        sp_ref[:, cre] = jnp.where(row_id >= 1, pltpu.roll(re, 1, 0), c_re).astype(BF16)
        sp_ref[:, cim] = jnp.where(row_id >= 1, pltpu.roll(im, 1, 0), c_im).astype(BF16)
        carry_ref[0:1, cre] = re[rows - 1:rows, :]
        carry_ref[0:1, cim] = im[rows - 1:rows, :]

    for gs in range(S5_SETS):
        state = jnp.concatenate([sp_ref[:, gs * S5_BLOCK:(gs + 1) * S5_BLOCK],
                                 sp_ref[:, S5_NSTATE + gs * S5_BLOCK:S5_NSTATE + (gs + 1) * S5_BLOCK]], axis=1)
        for tq in range(S5_QUADS):
            y = _dot(state, wc_ref[tq * S5_SETS + gs])
            for tj in range(tq + 1):
                y = y + _dot(ur_ref[:, block(tj, gs)], kb_ref[(tq - tj) * S5_SETS + gs])
            yb_ref[:, block(tq, gs)] = y

    for tq in range(S5_QUADS):
        for lt in range(S5_QUAD // 2):
            t0 = tq * S5_QUAD + 2 * lt
            for pair, y_ref in enumerate((y0_ref, y1_ref)):
                a = yb_ref[:, block(tq, 2 * pair).start + lt * LANES:block(tq, 2 * pair).start + (lt + 1) * LANES]
                b = yb_ref[:, block(tq, 2 * pair + 1).start + lt * LANES:block(tq, 2 * pair + 1).start + (lt + 1) * LANES]
                y_ref[pl.ds(t0, rows, stride=S5_T), :] = jnp.where(low, a, pltpu.roll(b, half, 1))
                y_ref[pl.ds(t0 + 1, rows, stride=S5_T), :] = jnp.where(low, pltpu.roll(a, half, 1), b)


def _s5_call(u0, u1, wv, kb, wc, apow, batch, rows):
    n = u0.shape[0]
    tok = rows * S5_T
    tiles_per_seq = n // batch // tok
    const = lambda a: pl.BlockSpec(a.shape, lambda b, t: (0,) * a.ndim, pipeline_mode=pl.Buffered(1))
    blk = pl.BlockSpec((tok, LANES), lambda b, t: (b * tiles_per_seq + t, 0))
    half = jax.ShapeDtypeStruct((n, LANES), F32)
    return pl.pallas_call(
        _s5_kernel,
        grid=(batch, tiles_per_seq),
        in_specs=[blk, blk, const(wv), const(kb), const(wc), const(apow)],
        out_specs=[blk, blk],
        out_shape=[half, half],
        scratch_shapes=[pltpu.VMEM((rows, S5_ROW), BF16), pltpu.VMEM((rows, 2 * S5_NSTATE), F32),
                        pltpu.VMEM((rows, 2 * S5_NSTATE), BF16), pltpu.VMEM((rows, S5_ROW), F32),
                        pltpu.VMEM((SUBLANES, 2 * S5_NSTATE), F32)],
        compiler_params=pltpu.CompilerParams(dimension_semantics=("arbitrary", "arbitrary"),
                                             vmem_limit_bytes=VMEM_LIMIT_BYTES),
        name="s5_scan",
    )(u0, u1, wv, kb, wc, apow)


def _s5_tables(lam_re, lam_im, log_dt, b_re, b_im, c_re, c_im, rows):
    p_, h_ = S5_STATE, S5_GROUP
    q_, s_, m_ = S5_QUADS, S5_SETS, S5_N_GROUPS // S5_SETS
    hi = lax.Precision.HIGHEST
    lam_re, lam_im = lam_re.astype(F32), lam_im.astype(F32)
    dt = jnp.exp(log_dt.astype(F32))[:, None]
    tau = jnp.arange(S5_T + 1, dtype=F32)[:, None, None]
    mag = jnp.exp(tau * (lam_re * dt)[None])
    ap_re = mag * jnp.cos(tau * (lam_im * dt)[None])
    ap_im = mag * jnp.sin(tau * (lam_im * dt)[None])
    n_re, n_im = ap_re[1] - 1.0, ap_im[1]
    den = lam_re * lam_re + lam_im * lam_im
    k_re = ((n_re * lam_re + n_im * lam_im) / den)[..., None]
    k_im = ((n_im * lam_re - n_re * lam_im) / den)[..., None]
    b_re, b_im = b_re.astype(F32), b_im.astype(F32)
    bb_re = k_re * b_re - k_im * b_im
    bb_im = k_re * b_im + k_im * b_re
    c_re, c_im = c_re.astype(F32), c_im.astype(F32)
    ab_re = ap_re[:S5_T, :, :, None] * bb_re[None] - ap_im[:S5_T, :, :, None] * bb_im[None]
    ab_im = ap_re[:S5_T, :, :, None] * bb_im[None] + ap_im[:S5_T, :, :, None] * bb_re[None]

    same = jnp.eye(m_, dtype=F32)

    def wv_part(ab):
        w = ab[::-1].reshape(q_, S5_QUAD, s_, m_, p_, h_).transpose(0, 2, 1, 3, 5, 4)
        return w[..., None, :] * same[None, None, None, :, None, :, None]
    wv = jnp.stack([wv_part(ab_re), wv_part(ab_im)], axis=5)
    wv = wv.reshape(q_ * s_, S5_BLOCK, 2 * S5_BLOCK)

    k_tau = (jnp.einsum('gop,tgpi->tgio', c_re, ab_re, precision=hi)
             - jnp.einsum('gop,tgpi->tgio', c_im, ab_im, precision=hi))
    zero = jnp.zeros_like(k_tau[0])
    taps = jnp.stack([jnp.stack([jnp.stack([k_tau[S5_QUAD * dq + ti - tj] if S5_QUAD * dq + ti - tj >= 0 else zero
                                            for ti in range(S5_QUAD)]) for tj in range(S5_QUAD)])
                      for dq in range(q_)])
    kb = taps.reshape(q_, S5_QUAD, S5_QUAD, s_, m_, h_, h_).transpose(0, 3, 1, 4, 5, 2, 6)
    kb = kb[..., None, :] * same[None, None, None, :, None, None, :, None]
    kb = kb.reshape(q_ * s_, S5_BLOCK, S5_BLOCK)

    cp = lambda c: c.transpose(0, 2, 1)[:, :, None, :]
    w_re = cp(c_re) * ap_re[1:].transpose(1, 2, 0)[..., None] - cp(c_im) * ap_im[1:].transpose(1, 2, 0)[..., None]
    w_im = cp(c_re) * ap_im[1:].transpose(1, 2, 0)[..., None] + cp(c_im) * ap_re[1:].transpose(1, 2, 0)[..., None]
    def wc_part(w):
        w = w.reshape(s_, m_, p_, q_, S5_QUAD, h_).transpose(3, 0, 1, 2, 4, 5)
        return w[..., None, :] * same[None, None, :, None, None, :, None]
    wc = jnp.stack([wc_part(w_re), wc_part(-w_im)], axis=2)
    wc = wc.reshape(q_ * s_, 2 * S5_BLOCK, S5_BLOCK)

    n_steps = max(1, (rows - 1).bit_length())
    pows = [(ap_re[S5_T].reshape(S5_NSTATE), ap_im[S5_T].reshape(S5_NSTATE))]
    for _ in range(n_steps - 1):
        r, i = pows[-1]
        pows.append((r * r - i * i, 2.0 * r * i))
    apow = jnp.stack([jnp.stack(p) for p in pows])
    return wv.astype(BF16), kb.astype(BF16), wc.astype(BF16), apow


def _attn_fill_bias(diag_ref, bias_ref):
    width = diag_ref.shape[1]
    qi = lax.broadcasted_iota(jnp.int32, (ATTN_TQ, ATTN_TK), 0)
    kj = lax.broadcasted_iota(jnp.int32, (ATTN_TQ, ATTN_TK), 1)
    q_chunk = qi // CHUNK + (BAND_CHUNKS - 1)
    k_chunk = kj // CHUNK
    in_band = (k_chunk <= q_chunk) & (k_chunk >= q_chunk - (BAND_CHUNKS - 1))
    for h in range(ATTN_HEADS):
        rows = jnp.broadcast_to(diag_ref[h:h + 1, :], (ATTN_TQ, width))
        base = pltpu.roll(rows, 0, 1, stride=1, stride_axis=0)[:, :ATTN_TK]
        for t in range(bias_ref.shape[0]):
            bias_ref[t, h] = jnp.where(in_band & (kj >= ATTN_PREV - t * ATTN_TQ), base, NEG_INF)


def _attn_kernel(q_ref, k0_ref, k1_ref, k2_ref, v0_ref, v1_ref, v2_ref, diag_ref, o_ref, bias_ref):
    @pl.when((pl.program_id(0) == 0) & (pl.program_id(1) == 0))
    def _():
        _attn_fill_bias(diag_ref, bias_ref)

    table = jnp.minimum(pl.program_id(1), bias_ref.shape[0] - 1)
    lane = lax.broadcasted_iota(jnp.int32, (1, LANES), 1)
    for p in range(ATTN_WIDTH // LANES):
        cols = slice(p * LANES, (p + 1) * LANES)
        qp = q_ref[:, cols]
        kcat = jnp.concatenate([k0_ref[:, cols], k1_ref[:, cols], k2_ref[:, cols]], axis=0)
        vcat = jnp.concatenate([v0_ref[:, cols], v1_ref[:, cols], v2_ref[:, cols]], axis=0)
        first = lane < HEAD_DIM
        zero = jnp.zeros_like(qp)
        qm = jnp.concatenate([jnp.where(first, qp, zero), jnp.where(first, zero, qp)], axis=0)
        bias = jnp.concatenate([bias_ref[table, 2 * p], bias_ref[table, 2 * p + 1]], axis=0)
        s = _dot_nt(qm, kcat) + bias
        e = jnp.exp(s - jnp.max(s, axis=-1, keepdims=True)).astype(BF16)
        o = _dot(e, jnp.concatenate([vcat, jnp.ones_like(vcat)], axis=1))
        r = o[:, :LANES] / o[:, LANES:]
        o_ref[:, cols] = jnp.where(first, r[:ATTN_TQ], r[ATTN_TQ:])


def _attn_call(q, k, v, diag, batch, seq):
    n = q.shape[0]
    tiles = seq // ATTN_TQ
    n_prev = ATTN_PREV // ATTN_TQ
    cur = pl.BlockSpec((ATTN_TQ, ATTN_WIDTH), lambda b, t: (b * tiles + t, 0))
    prev = lambda d: pl.BlockSpec((ATTN_TQ, ATTN_WIDTH), lambda b, t: (b * tiles + jnp.maximum(t - d, 0), 0))
    assert n_prev == 2
    return pl.pallas_call(
        _attn_kernel,
        grid=(batch, tiles),
        in_specs=[cur, prev(2), prev(1), cur, prev(2), prev(1), cur, pl.BlockSpec(diag.shape, lambda b, t: (0, 0))],
        out_specs=cur,
        out_shape=jax.ShapeDtypeStruct((n, ATTN_WIDTH), F32),
        scratch_shapes=[pltpu.VMEM((n_prev + 1, ATTN_HEADS, ATTN_TQ, ATTN_TK), F32)],
        compiler_params=pltpu.CompilerParams(dimension_semantics=("arbitrary", "arbitrary"),
                                             vmem_limit_bytes=VMEM_LIMIT_BYTES),
        name="band_attn",
    )(q, k, k, k, v, v, v, diag)


def _attn_diag(rel_bias):
    width = ATTN_TQ + ATTN_TK
    n = jnp.arange(width)
    offset = jnp.where(n < ATTN_TK, n, n - width)
    rel = jnp.clip(ATTN_PREV - offset, -MAX_REL, MAX_REL) + MAX_REL
    return jnp.take(rel_bias.astype(F32), rel, axis=1)


def _router_gates(lt):
    tm = lt.shape[1]
    row = lax.broadcasted_iota(jnp.int32, (ROUTER_SLAB, tm), 0)
    gl = lt[0:ROUTER_SLAB]
    gmax = jnp.max(gl, axis=0, keepdims=True)
    p_g = 1.0 / jnp.sum(jnp.exp(gl - gmax), axis=0, keepdims=True)
    g_top = jnp.min(jnp.where(gl == gmax, row, ROUTER_SLAB), axis=0, keepdims=True)
    el = jnp.zeros((ROUTER_SLAB, tm), F32)
    for g in range(N_EXPERT_GROUPS):
        el = el + jnp.where(g_top == g, lt[(g + 1) * ROUTER_SLAB:(g + 2) * ROUTER_SLAB], 0.0)
    ee = jnp.exp(el - jnp.max(el, axis=0, keepdims=True))
    ep = ee / jnp.sum(ee, axis=0, keepdims=True)
    p1 = jnp.max(ep, axis=0, keepdims=True)
    i1 = jnp.min(jnp.where(ep == p1, row, ROUTER_SLAB), axis=0, keepdims=True)
    rest = jnp.where(row == i1, -1.0, ep)
    p2 = jnp.max(rest, axis=0, keepdims=True)
    i2 = jnp.min(jnp.where(rest == p2, row, ROUTER_SLAB), axis=0, keepdims=True)
    tot = p1 + p2
    w = (jnp.where(row == i1, p1 / tot, 0.0) + jnp.where(row == i2, p2 / tot, 0.0)) * p_g
    return [jnp.where(g_top == g, w, 0.0) for g in range(N_EXPERT_GROUPS)], g_top


def _sort_matrix(slot, n_slots, slot_axis):
    shape = (n_slots, slot.shape[1]) if slot_axis == 0 else (slot.shape[0], n_slots)
    return jnp.where(lax.broadcasted_iota(jnp.int32, shape, slot_axis) == slot, 1.0, 0.0).astype(BF16)


def _out_kernel(oa_ref, y0_ref, y1_ref, u0_ref, u1_ref, oc_ref, x_ref, d_ref, gluw_ref, glub_ref, og_ref, woutf_ref,
                fg_ref, wr_ref, br_ref, bd_ref, x1_ref, hs_ref, tok_ref, cnt_ref, wout_ref):
    tm = x_ref.shape[0]
    n_slots = hs_ref.shape[0]

    @pl.when(pl.program_id(0) == 0)
    def _():
        wout_ref[...] = woutf_ref[0].astype(BF16)

    y = jnp.concatenate([y0_ref[...], y1_ref[...]], axis=-1)
    u = jnp.concatenate([u0_ref[...], u1_ref[...]], axis=-1)
    y = jax.nn.gelu(y + d_ref[...] * u)
    ob = y * jax.nn.sigmoid(_dot(y.astype(BF16), gluw_ref[...]) + glub_ref[...])
    o = jnp.concatenate([oa_ref[...], ob, oc_ref[...]], axis=-1)
    on = o * lax.rsqrt(_group_sumsq(o, bd_ref) * (1.0 / OUT_NORM_GROUP) + EPS) * og_ref[...]
    x1 = x_ref[...] + _dot(on.astype(BF16), wout_ref[...])
    x1_ref[...] = x1
    ms = jnp.mean(x1 * x1, axis=-1, keepdims=True)
    hb = (x1 * lax.rsqrt(ms + EPS) * fg_ref[...]).astype(BF16)
    lt = _dot(hb, wr_ref[...]).T + br_ref[...]
    slabs, g_top = _router_gates(lt)

    row = lax.broadcasted_iota(jnp.int32, (ROUTER_SLAB, tm), 0)
    lane = lax.broadcasted_iota(jnp.int32, (ROUTER_SLAB, tm), 1)
    member = jnp.where(row == g_top, 1.0, 0.0)
    cum = member
    shift = 1
    while shift < tm:
        cum = cum + jnp.where(lane >= shift, pltpu.roll(cum, shift, 1), 0.0)
        shift *= 2
    count = cum[:, tm - 1:tm]
    padded = jnp.floor((count + (MOE_UNIT - 1)) * (1.0 / MOE_UNIT)) * MOE_UNIT
    group_row = lax.broadcasted_iota(jnp.int32, (ROUTER_SLAB, 1), 0)
    start = jnp.zeros((ROUTER_SLAB, 1), F32)
    for g in range(1, N_EXPERT_GROUPS):
        start = start + jnp.where(group_row >= g, padded[g - 1:g, :], 0.0)
    slot = jnp.sum(member * (start + cum - 1.0), axis=0, keepdims=True)
    cnt_ref[...] = jnp.broadcast_to(count, cnt_ref.shape)

    pad = jnp.zeros((LANES - SLOT_LANE - ROUTER_SLAB, tm), F32)
    gates = jnp.concatenate(slabs + [jnp.broadcast_to(slot, (ROUTER_SLAB, tm)), pad], axis=0).T
    tok_ref[...] = gates

    perm = _sort_matrix(slot.astype(jnp.int32), n_slots, 0)
    hs_ref[:, :D_MODEL] = _dot(perm, hb)
    g_hi = gates.astype(BF16)
    g_mid = (gates - g_hi.astype(F32)).astype(BF16)
    g_lo = (gates - g_hi.astype(F32) - g_mid.astype(F32)).astype(BF16)
    gs = _dot(perm, jnp.concatenate([g_hi, g_mid, g_lo], axis=-1))
    hs_ref[:, D_MODEL:] = gs[:, :LANES] + gs[:, LANES:2 * LANES] + gs[:, 2 * LANES:]


def _out_call(oa, y0, y1, u0, u1, oc, x, d, gluw, glub, og, wout, layer, fg, wr, br, bd, tm):
    n = x.shape[0]
    n_slots = tm + MOE_PAD_ROWS
    row = lambda c: pl.BlockSpec((tm, c), lambda i: (i, 0))
    full = lambda a: pl.BlockSpec(a.shape, lambda i: (0,) * a.ndim)
    return pl.pallas_call(
        _out_kernel,
        grid=(n // tm,),
        in_specs=[row(SGU_WIDTH), row(LANES), row(LANES), row(LANES), row(LANES), row(ATTN_WIDTH), row(D_MODEL),
                  full(d), full(gluw),
                  full(glub), full(og), pl.BlockSpec((1,) + wout.shape[1:], lambda i: (layer, 0, 0), pipeline_mode=pl.Buffered(1)),
                  full(fg), full(wr), full(br), full(bd)],
        out_specs=[row(D_MODEL), pl.BlockSpec((n_slots, MOE_ROW), lambda i: (i, 0)), row(LANES),
                   pl.BlockSpec((ROUTER_SLAB, LANES), lambda i: (i, 0))],
        out_shape=[jax.ShapeDtypeStruct((n, D_MODEL), F32), jax.ShapeDtypeStruct((n // tm * n_slots, MOE_ROW), F32),
                   jax.ShapeDtypeStruct((n, LANES), F32),
                   jax.ShapeDtypeStruct((n // tm * ROUTER_SLAB, LANES), F32)],
        scratch_shapes=[pltpu.VMEM(wout.shape[1:], BF16)],
        compiler_params=pltpu.CompilerParams(dimension_semantics=("arbitrary",), vmem_limit_bytes=VMEM_LIMIT_BYTES),
        name="out_proj",
    )(oa, y0, y1, u0, u1, oc, x, d, gluw, glub, og, wout, fg, wr, br, bd)


def _router_tables(wg, bg, we, be):
    w = jnp.zeros((D_MODEL, LANES), F32)
    b = jnp.full((LANES,), NEG_INF, F32)
    w = w.at[:, 0:N_EXPERT_GROUPS].set(wg.astype(F32))
    b = b.at[0:N_EXPERT_GROUPS].set(bg.astype(F32))
    for g in range(N_EXPERT_GROUPS):
        c0 = (g + 1) * ROUTER_SLAB
        w = w.at[:, c0:c0 + EXPERTS_PER_GROUP].set(we[g].astype(F32))
        b = b.at[c0:c0 + EXPERTS_PER_GROUP].set(be[g].astype(F32))
    return w.astype(BF16), b[:, None]


def _route_plan(counts, n, tm):
    n_tiles = n // tm
    units_per_tile = (tm + MOE_PAD_ROWS) // MOE_UNIT
    units_per_mtile = tm // MOE_UNIT
    cnt = counts.reshape(n_tiles, ROUTER_SLAB, LANES)[:, :N_EXPERT_GROUPS, 0].astype(jnp.int32)
    seg = (cnt + MOE_UNIT - 1) // MOE_UNIT
    seg_start = jnp.cumsum(seg, axis=1) - seg
    before = jnp.cumsum(seg, axis=0) - seg
    total = jnp.sum(seg, axis=0)
    mtiles = (total + units_per_mtile - 1) // units_per_mtile
    mtile_end = jnp.cumsum(mtiles)
    group_start = (mtile_end - mtiles) * units_per_mtile
    n_mtiles = n_tiles + N_EXPERT_GROUPS + -(-n_tiles * MOE_PAD_ROWS // tm)
    tile_group = jnp.minimum(jnp.sum(jnp.arange(n_mtiles)[:, None] >= mtile_end[None, :], axis=1),
                             N_EXPERT_GROUPS - 1)
    meta = jnp.concatenate([tile_group, mtile_end[-1:]]).astype(jnp.int32)
    groups = jnp.arange(N_EXPERT_GROUPS)
    pick = lambda table, g: jnp.sum(jnp.where(g[..., None] == groups, table, 0), axis=-1)

    q = jnp.arange(n_mtiles * units_per_mtile)
    g = jnp.repeat(tile_group, units_per_mtile)
    ql = q - pick(group_start, g)
    ends_g = pick((before + seg)[None], g[:, None])
    tile_of = jnp.minimum(jnp.sum(ql[:, None] >= ends_g, axis=1), n_tiles - 1)
    offset_g = pick((seg_start - before)[None], g[:, None])
    local = ql + jnp.sum(jnp.where(tile_of[:, None] == jnp.arange(n_tiles), offset_g, 0), axis=1)
    valid = (ql < pick(total, g)) & (q // units_per_mtile < mtile_end[-1])
    src = jnp.where(valid, tile_of * units_per_tile + local, units_per_tile - 1).astype(jnp.int32)

    ul = jnp.arange(units_per_tile)[None, :, None]
    seg_of = jnp.sum(ul >= (seg_start + seg)[:, None, :], axis=-1)
    gi = jnp.minimum(seg_of, N_EXPERT_GROUPS - 1)
    glob = pick((group_start + before - seg_start)[:, None, :], gi) + ul[..., 0]
    dst = jnp.where(seg_of < N_EXPERT_GROUPS, glob, 0).astype(jnp.int32).reshape(-1)
    return meta, src, dst, n_mtiles


def _unit_gather(table_ref, first, src_ref, dst_ref, sem, wait):
    n_rows = dst_ref.shape[0]
    if wait:
        pltpu.make_async_copy(src_ref.at[pl.ds(0, n_rows)], dst_ref, sem).wait()
        return

    def body(k, c):
        u = pl.multiple_of(table_ref[first + k] * MOE_UNIT, MOE_UNIT)
        pltpu.make_async_copy(src_ref.at[pl.ds(u, MOE_UNIT)],
                              dst_ref.at[pl.ds(pl.multiple_of(k * MOE_UNIT, MOE_UNIT), MOE_UNIT)], sem).start()
        return c
    lax.fori_loop(0, n_rows // MOE_UNIT, body, 0, unroll=4)


def _moe_group_kernel(meta_ref, src_ref, hs_ref, wg_ref, wu_ref, wd_ref, o_ref, wgb_ref, wub_ref, wdb_ref,
                      buf_ref, sem):
    j = pl.program_id(0)
    n_steps = pl.num_programs(0)
    n_used = meta_ref[n_steps]
    group = meta_ref[j]
    n_units = o_ref.shape[0] // MOE_UNIT
    slot = j % 2

    @pl.when(j == 0)
    def _():
        _unit_gather(src_ref, 0, hs_ref, buf_ref.at[0], sem.at[0], wait=False)

    @pl.when(j + 1 < n_steps)
    def _():
        _unit_gather(src_ref, (j + 1) * n_units, hs_ref, buf_ref.at[1 - slot], sem.at[1 - slot], wait=False)

    @pl.when((j == 0) | (group != meta_ref[jnp.maximum(j - 1, 0)]))
    def _():
        wgb_ref[...] = wg_ref[...].astype(BF16)
        wub_ref[...] = wu_ref[...].astype(BF16)
        wdb_ref[...] = wd_ref[...].astype(BF16)

    _unit_gather(src_ref, j * n_units, hs_ref, buf_ref.at[slot], sem.at[slot], wait=True)

    @pl.when(j < n_used)
    def _():
        h = buf_ref[slot, :, :D_MODEL].astype(BF16)
        gates = buf_ref[slot, :, D_MODEL:]
        lane = lax.broadcasted_iota(jnp.int32, (1, LANES), 1)
        out = None
        for e in range(EXPERTS_PER_GROUP):
            gt = _dot(h, wgb_ref[e])
            up = _dot(h, wub_ref[e])
            gate = jnp.sum(jnp.where(lane == group * ROUTER_SLAB + e, gates, 0.0), axis=-1, keepdims=True)
            a = (gt * jax.nn.sigmoid(gt)) * up * gate
            part = _dot(a.astype(BF16), wdb_ref[e])
            out = part if out is None else out + part
        o_ref[...] = out

    @pl.when(j >= n_used)
    def _():
        o_ref[...] = jnp.zeros_like(o_ref)


def _moe_group_call(meta, src, hs, wg, wu, wd, layer, n_mtiles, tm):
    wg, wu, wd = (a.reshape((-1,) + a.shape[2:]) for a in (wg, wu, wd))
    by_group = lambda a: pl.BlockSpec((EXPERTS_PER_GROUP,) + a.shape[1:],
                                      lambda j, meta, src: (layer * N_EXPERT_GROUPS + meta[j], 0, 0))
    return pl.pallas_call(
        _moe_group_kernel,
        grid_spec=pltpu.PrefetchScalarGridSpec(
            num_scalar_prefetch=2,
            grid=(n_mtiles,),
            in_specs=[pl.BlockSpec(memory_space=pl.ANY), by_group(wg), by_group(wu), by_group(wd)],
            out_specs=pl.BlockSpec((tm, D_MODEL), lambda j, meta, src: (j, 0)),
            scratch_shapes=[pltpu.VMEM((EXPERTS_PER_GROUP, D_MODEL, D_EXPERT), BF16),
                            pltpu.VMEM((EXPERTS_PER_GROUP, D_MODEL, D_EXPERT), BF16),
                            pltpu.VMEM((EXPERTS_PER_GROUP, D_EXPERT, D_MODEL), BF16),
                            pltpu.VMEM((2, tm, MOE_ROW), F32), pltpu.SemaphoreType.DMA((2,))],
        ),
        out_shape=jax.ShapeDtypeStruct((n_mtiles * tm, D_MODEL), F32),
        compiler_params=pltpu.CompilerParams(dimension_semantics=("arbitrary",), vmem_limit_bytes=VMEM_LIMIT_BYTES),
        name="moe_experts",
    )(meta, src, hs, wg, wu, wd)


def _combine_kernel(dst_ref, x1_ref, tok_ref, ys_ref, o_ref, buf_ref, sem):
    i = pl.program_id(0)
    n_slots = buf_ref.shape[1]
    n_units = n_slots // MOE_UNIT
    slot = i % 2

    @pl.when(i == 0)
    def _():
        _unit_gather(dst_ref, 0, ys_ref, buf_ref.at[0], sem.at[0], wait=False)

    @pl.when(i + 1 < pl.num_programs(0))
    def _():
        _unit_gather(dst_ref, (i + 1) * n_units, ys_ref, buf_ref.at[1 - slot], sem.at[1 - slot], wait=False)

    _unit_gather(dst_ref, i * n_units, ys_ref, buf_ref.at[slot], sem.at[slot], wait=True)

    y = buf_ref[slot]
    y_hi = y.astype(BF16)
    y_lo = (y - y_hi.astype(F32)).astype(BF16)
    unsort = _sort_matrix(tok_ref[:, SLOT_LANE:SLOT_LANE + 1].astype(jnp.int32), n_slots, 1)
    o_ref[...] = x1_ref[...] + _dot(unsort, y_hi) + _dot(unsort, y_lo)


def _combine_call(dst, x1, tok, ys, tm):
    n = x1.shape[0]
    n_slots = tm + MOE_PAD_ROWS
    return pl.pallas_call(
        _combine_kernel,
        grid_spec=pltpu.PrefetchScalarGridSpec(
            num_scalar_prefetch=1,
            grid=(n // tm,),
            in_specs=[pl.BlockSpec((tm, D_MODEL), lambda i, dst: (i, 0)),
                      pl.BlockSpec((tm, LANES), lambda i, dst: (i, 0)),
                      pl.BlockSpec(memory_space=pl.ANY)],
            out_specs=pl.BlockSpec((tm, D_MODEL), lambda i, dst: (i, 0)),
            scratch_shapes=[pltpu.VMEM((2, n_slots, D_MODEL), F32), pltpu.SemaphoreType.DMA((2,))],
        ),
        out_shape=jax.ShapeDtypeStruct((n, D_MODEL), F32),
        compiler_params=pltpu.CompilerParams(dimension_semantics=("arbitrary",), vmem_limit_bytes=VMEM_LIMIT_BYTES),
        name="moe_combine",
    )(dst, x1, tok, ys)


def kernel(x, norm_mix, w_in, sgu_norm, sgu_w, sgu_b, s5_lambda_re, s5_lambda_im, s5_log_dt, s5_b_re, s5_b_im,
           s5_c_re, s5_c_im, s5_d, s5_glu_w, s5_glu_b, q_norm, k_norm, rel_bias, out_norm, w_out, norm_ffn,
           router_group_w, router_group_b, router_expert_w, router_expert_b, w_gate, w_up, w_down):
    batch, seq, _ = x.shape
    n = batch * seq
    depth = w_in.shape[0]
    t = _tiles(n, seq)
    assert seq % ATTN_TQ == 0 and seq % (S5_T * t["s5_rows"]) == 0
    row_vec = lambda a: a.astype(F32)[None, :]

    lane_group = jnp.arange(LANES) // HEAD_DIM
    bd = (lane_group[:, None] == lane_group[None, :]).astype(BF16)
    block_chunk = jnp.arange(SGU_BLOCK) // CHUNK
    sgu_mask = block_chunk[None, :] <= block_chunk[:, None]

    xf = x.reshape(n, D_MODEL)
    for l in range(depth):
        sguw = jnp.where(sgu_mask[None], sgu_w[l], 0).astype(BF16)
        sgub = jnp.repeat(sgu_b[l].astype(F32).T, HEAD_DIM, axis=1)
        qg = row_vec(jnp.tile(q_norm[l], ATTN_HEADS)) * (HEAD_DIM ** -0.5)
        kg = row_vec(jnp.tile(k_norm[l], ATTN_HEADS))
        oa, u0, u1, q, k, v = _in_call(xf, row_vec(norm_mix[l]), w_in, l, row_vec(sgu_norm[l]), sguw, sgub,
                                      qg, kg, bd, t["tm_in"])

        wv, kb, wc, apow = _s5_tables(s5_lambda_re[l], s5_lambda_im[l], s5_log_dt[l], s5_b_re[l], s5_b_im[l],
                                        s5_c_re[l], s5_c_im[l], t["s5_rows"])
        y0, y1 = _s5_call(u0, u1, wv, kb, wc, apow, batch, t["s5_rows"])

        oc = _attn_call(q, k, v, _attn_diag(rel_bias[l]), batch, seq)

        wr, br = _router_tables(router_group_w[l], router_group_b[l], router_expert_w[l], router_expert_b[l])
        x1, hs, tok, counts = _out_call(oa, y0, y1, u0, u1, oc, xf, row_vec(s5_d[l]), s5_glu_w[l].astype(BF16), row_vec(s5_glu_b[l]),
                                 row_vec(out_norm[l]), w_out, l, row_vec(norm_ffn[l]), wr, br, bd,
                                 t["tm_out"])

        meta, src, dst, n_mtiles = _route_plan(counts, n, t["tm_out"])
        ys = _moe_group_call(meta, src, hs, w_gate, w_up, w_down, l, n_mtiles, t["tm_out"])
        xf = _combine_call(dst, x1, tok, ys, t["tm_out"])
    return xf.reshape(batch, seq, D_MODEL)
```

```python
import jax
import jax.numpy as jnp
from jax import lax
from jax.experimental import pallas as pl
from jax.experimental.pallas import tpu as pltpu

F32 = jnp.float32
BF16 = jnp.bfloat16

D_MODEL = 1024
CHUNK = 64
HEAD_DIM = 64
SGU_WIDTH = 256
SGU_HEADS = 4
SGU_BLOCK = 128
S5_WIDTH = 256
S5_GROUP = 16
S5_N_GROUPS = 16
S5_STATE = 64
ATTN_WIDTH = 512
ATTN_HEADS = 8
BAND_CHUNKS = 9
MAX_REL = 256
IN_COLS = 2 * SGU_WIDTH + S5_WIDTH + 3 * ATTN_WIDTH
OUT_NORM_GROUP = 64
N_EXPERT_GROUPS = 4
EXPERTS_PER_GROUP = 4
N_EXPERTS = 16
D_EXPERT = 256
EPS = 1e-6
NEG_INF = -1e30

LANES = 128
SUBLANES = 8
VMEM_LIMIT_BYTES = 56 * 1024 * 1024

S5_T = 16
S5_ROW = S5_T * S5_WIDTH
S5_NSTATE = S5_N_GROUPS * S5_STATE
S5_QUAD = 4
S5_SETS = 4
S5_QUADS = S5_T // S5_QUAD
S5_BLOCK = S5_QUAD * (S5_N_GROUPS // S5_SETS) * S5_GROUP
ATTN_TQ = 256
ATTN_PREV = (BAND_CHUNKS - 1) * CHUNK
ATTN_TK = ATTN_TQ + ATTN_PREV
ROUTER_SLAB = SUBLANES
MOE_ROW = D_MODEL + LANES
MOE_UNIT = SUBLANES
MOE_PAD_ROWS = N_EXPERT_GROUPS * MOE_UNIT
SLOT_LANE = N_EXPERT_GROUPS * ROUTER_SLAB


def _tiles(n_tokens, seq):
    def pick(pref, total):
        t = min(pref, total)
        assert total % t == 0
        return t
    return dict(
        tm_in=pick(512, seq),
        tm_out=pick(512, seq),
        s5_rows=pick(256, seq // S5_T),
    )


def _dot(a, b):
    return jnp.dot(a, b, preferred_element_type=F32)


def _dot_nt(a, b):
    return lax.dot_general(a, b, (((1,), (1,)), ((), ())), preferred_element_type=F32)


def _group_sumsq(x, bd_ref):
    x2 = (x * x).astype(BF16)
    parts = [_dot(x2[:, t * LANES:(t + 1) * LANES], bd_ref[...]) for t in range(x.shape[1] // LANES)]
    return jnp.concatenate(parts, axis=-1)


def _in_kernel(x_ref, g_ref, wf_ref, sgug_ref, sguw_ref, sgub_ref, qg_ref, kg_ref, bd_ref,
               oa_ref, u0_ref, u1_ref, q_ref, k_ref, v_ref, w_ref):
    tm = x_ref.shape[0]

    @pl.when(pl.program_id(0) == 0)
    def _():
        w_ref[...] = wf_ref[0].astype(BF16)

    x = x_ref[...]
    ms = jnp.mean(x * x, axis=-1, keepdims=True)
    hn = (x * lax.rsqrt(ms + EPS) * g_ref[...]).astype(BF16)

    z = jax.nn.gelu(_dot(hn, w_ref[:, 0:2 * SGU_WIDTH]))
    u = z[:, :SGU_WIDTH]
    v = z[:, SGU_WIDTH:]
    v = v * lax.rsqrt(jnp.mean(v * v, axis=-1, keepdims=True) + EPS) * sgug_ref[...]
    vb = v.astype(BF16)
    first_head = lax.broadcasted_iota(jnp.int32, (SGU_BLOCK, LANES), 1) < HEAD_DIM
    for r in range(tm // SGU_BLOCK):
        rows = slice(r * SGU_BLOCK, (r + 1) * SGU_BLOCK)
        for p in range(SGU_WIDTH // LANES):
            cols = slice(p * LANES, (p + 1) * LANES)
            vp = vb[rows, cols]
            mixed = jnp.where(first_head, _dot(sguw_ref[2 * p], vp), _dot(sguw_ref[2 * p + 1], vp))
            oa_ref[rows, cols] = u[rows, cols] * (mixed + sgub_ref[:, cols])

    c0 = 2 * SGU_WIDTH
    us = _dot(hn, w_ref[:, c0:c0 + S5_WIDTH])
    u0_ref[...] = us[:, :LANES]
    u1_ref[...] = us[:, LANES:]

    c0 += S5_WIDTH
    q = _dot(hn, w_ref[:, c0:c0 + ATTN_WIDTH])
    q_ref[...] = (q * lax.rsqrt(_group_sumsq(q, bd_ref) * (1.0 / HEAD_DIM) + EPS) * qg_ref[...]).astype(BF16)
    c0 += ATTN_WIDTH
    k = _dot(hn, w_ref[:, c0:c0 + ATTN_WIDTH])
    k_ref[...] = (k * lax.rsqrt(_group_sumsq(k, bd_ref) * (1.0 / HEAD_DIM) + EPS) * kg_ref[...]).astype(BF16)
    c0 += ATTN_WIDTH
    v_ref[...] = _dot(hn, w_ref[:, c0:c0 + ATTN_WIDTH]).astype(BF16)


def _in_call(x, g, w, layer, sgug, sguw, sgub, qg, kg, bd, tm):
    n = x.shape[0]
    row = lambda c: pl.BlockSpec((tm, c), lambda i: (i, 0))
    full = lambda a: pl.BlockSpec(a.shape, lambda i: (0,) * a.ndim)
    w_layer = pl.BlockSpec((1,) + w.shape[1:], lambda i: (layer, 0, 0), pipeline_mode=pl.Buffered(1))
    return pl.pallas_call(
        _in_kernel,
        grid=(n // tm,),
        in_specs=[row(D_MODEL), full(g), w_layer, full(sgug), full(sguw), full(sgub), full(qg), full(kg), full(bd)],
        out_specs=[row(SGU_WIDTH), row(LANES), row(LANES), row(ATTN_WIDTH), row(ATTN_WIDTH), row(ATTN_WIDTH)],
        out_shape=[jax.ShapeDtypeStruct((n, SGU_WIDTH), F32), jax.ShapeDtypeStruct((n, LANES), F32),
                   jax.ShapeDtypeStruct((n, LANES), F32), jax.ShapeDtypeStruct((n, ATTN_WIDTH), BF16),
                   jax.ShapeDtypeStruct((n, ATTN_WIDTH), BF16), jax.ShapeDtypeStruct((n, ATTN_WIDTH), BF16)],
        scratch_shapes=[pltpu.VMEM(w.shape[1:], BF16)],
        compiler_params=pltpu.CompilerParams(dimension_semantics=("arbitrary",), vmem_limit_bytes=VMEM_LIMIT_BYTES),
        name="in_proj",
    )(x, g, w, sgug, sguw, sgub, qg, kg, bd)


def _s5_kernel(u0_ref, u1_ref, wv_ref, kb_ref, wc_ref, apow_ref, y0_ref, y1_ref, ur_ref, s_ref, sp_ref, yb_ref, carry_ref):
    rows = ur_ref.shape[0]
    n_steps = apow_ref.shape[0]
    half = LANES // 2
    low = lax.broadcasted_iota(jnp.int32, (1, LANES), 1) < half
    block = lambda tq, gs: slice((tq * S5_SETS + gs) * S5_BLOCK, (tq * S5_SETS + gs + 1) * S5_BLOCK)

    @pl.when(pl.program_id(1) == 0)
    def _():
        carry_ref[...] = jnp.zeros_like(carry_ref)

    for tq in range(S5_QUADS):
        for lt in range(S5_QUAD // 2):
            t0 = tq * S5_QUAD + 2 * lt
            for pair, u_ref in enumerate((u0_ref, u1_ref)):
                a = u_ref[pl.ds(t0, rows, stride=S5_T), :]
                b = u_ref[pl.ds(t0 + 1, rows, stride=S5_T), :]
                for par in range(2):
                    tile = jnp.where(low, a, pltpu.roll(b, half, 1)) if par == 0 else jnp.where(low, pltpu.roll(a, half, 1), b)
                    col = block(tq, 2 * pair + par).start + lt * LANES
                    ur_ref[:, col:col + LANES] = tile.astype(BF16)

    for gs in range(S5_SETS):
        v = None
        for tq in range(S5_QUADS):
            part = _dot(ur_ref[:, block(tq, gs)], wv_ref[tq * S5_SETS + gs])
            v = part if v is None else v + part
        s_ref[:, gs * S5_BLOCK:(gs + 1) * S5_BLOCK] = v[:, :S5_BLOCK]
        s_ref[:, S5_NSTATE + gs * S5_BLOCK:S5_NSTATE + (gs + 1) * S5_BLOCK] = v[:, S5_BLOCK:]

    row_id = lax.broadcasted_iota(jnp.int32, (rows, LANES), 0)
    for cb in range(S5_NSTATE // LANES):
        cre = slice(cb * LANES, (cb + 1) * LANES)
        cim = slice(S5_NSTATE + cb * LANES, S5_NSTATE + (cb + 1) * LANES)
        re = s_ref[:, cre]
        im = s_ref[:, cim]
        c_re = carry_ref[0:1, cre]
        c_im = carry_ref[0:1, cim]
        a_re = apow_ref[0, 0:1, cre]
        a_im = apow_ref[0, 1:2, cre]
        re = re + jnp.where(row_id == 0, a_re * c_re - a_im * c_im, 0.0)
        im = im + jnp.where(row_id == 0, a_re * c_im + a_im * c_re, 0.0)
        for k in range(n_steps):
            shift = 1 << k
            a_re = apow_ref[k, 0:1, cre]
            a_im = apow_ref[k, 1:2, cre]
            re_s = jnp.where(row_id >= shift, pltpu.roll(re, shift, 0), 0.0)
            im_s = jnp.where(row_id >= shift, pltpu.roll(im, shift, 0), 0.0)
            re, im = re + a_re * re_s - a_im * im_s, im + a_re * im_s + a_im * re_s
        sp_ref[:, cre] = jnp.where(row_id >= 1, pltpu.roll(re, 1, 0), c_re).astype(BF16)
        sp_ref[:, cim] = jnp.where(row_id >= 1, pltpu.roll(im, 1, 0), c_im).astype(BF16)
        carry_ref[0:1, cre] = re[rows - 1:rows, :]
        carry_ref[0:1, cim] = im[rows - 1:rows, :]

    for gs in range(S5_SETS):
        state = jnp.concatenate([sp_ref[:, gs * S5_BLOCK:(gs + 1) * S5_BLOCK],
                                 sp_ref[:, S5_NSTATE + gs * S5_BLOCK:S5_NSTATE + (gs + 1) * S5_BLOCK]], axis=1)
        for tq in range(S5_QUADS):
            y = _dot(state, wc_ref[tq * S5_SETS + gs])
            for tj in range(tq + 1):
                y = y + _dot(ur_ref[:, block(tj, gs)], kb_ref[(tq - tj) * S5_SETS + gs])
            yb_ref[:, block(tq, gs)] = y

    for tq in range(S5_QUADS):
        for lt in range(S5_QUAD // 2):
            t0 = tq * S5_QUAD + 2 * lt
            for pair, y_ref in enumerate((y0_ref, y1_ref)):
                a = yb_ref[:, block(tq, 2 * pair).start + lt * LANES:block(tq, 2 * pair).start + (lt + 1) * LANES]
                b = yb_ref[:, block(tq, 2 * pair + 1).start + lt * LANES:block(tq, 2 * pair + 1).start + (lt + 1) * LANES]
                y_ref[pl.ds(t0, rows, stride=S5_T), :] = jnp.where(low, a, pltpu.roll(b, half, 1))
                y_ref[pl.ds(t0 + 1, rows, stride=S5_T), :] = jnp.where(low, pltpu.roll(a, half, 1), b)


def _s5_call(u0, u1, wv, kb, wc, apow, batch, rows):
    n = u0.shape[0]
    tok = rows * S5_T
    tiles_per_seq = n // batch // tok
    const = lambda a: pl.BlockSpec(a.shape, lambda b, t: (0,) * a.ndim, pipeline_mode=pl.Buffered(1))
    blk = pl.BlockSpec((tok, LANES), lambda b, t: (b * tiles_per_seq + t, 0))
    half = jax.ShapeDtypeStruct((n, LANES), F32)
    return pl.pallas_call(
        _s5_kernel,
        grid=(batch, tiles_per_seq),
        in_specs=[blk, blk, const(wv), const(kb), const(wc), const(apow)],
        out_specs=[blk, blk],
        out_shape=[half, half],
        scratch_shapes=[pltpu.VMEM((rows, S5_ROW), BF16), pltpu.VMEM((rows, 2 * S5_NSTATE), F32),
                        pltpu.VMEM((rows, 2 * S5_NSTATE), BF16), pltpu.VMEM((rows, S5_ROW), F32),
                        pltpu.VMEM((SUBLANES, 2 * S5_NSTATE), F32)],
        compiler_params=pltpu.CompilerParams(dimension_semantics=("arbitrary", "arbitrary"),
                                             vmem_limit_bytes=VMEM_LIMIT_BYTES),
        name="s5_scan",
    )(u0, u1, wv, kb, wc, apow)


def _s5_tables(lam_re, lam_im, log_dt, b_re, b_im, c_re, c_im, rows):
    p_, h_ = S5_STATE, S5_GROUP
    q_, s_, m_ = S5_QUADS, S5_SETS, S5_N_GROUPS // S5_SETS
    hi = lax.Precision.HIGHEST
    lam_re, lam_im = lam_re.astype(F32), lam_im.astype(F32)
    dt = jnp.exp(log_dt.astype(F32))[:, None]
    tau = jnp.arange(S5_T + 1, dtype=F32)[:, None, None]
    mag = jnp.exp(tau * (lam_re * dt)[None])
    ap_re = mag * jnp.cos(tau * (lam_im * dt)[None])
    ap_im = mag * jnp.sin(tau * (lam_im * dt)[None])
    n_re, n_im = ap_re[1] - 1.0, ap_im[1]
    den = lam_re * lam_re + lam_im * lam_im
    k_re = ((n_re * lam_re + n_im * lam_im) / den)[..., None]
    k_im = ((n_im * lam_re - n_re * lam_im) / den)[..., None]
    b_re, b_im = b_re.astype(F32), b_im.astype(F32)
    bb_re = k_re * b_re - k_im * b_im
    bb_im = k_re * b_im + k_im * b_re
    c_re, c_im = c_re.astype(F32), c_im.astype(F32)
    ab_re = ap_re[:S5_T, :, :, None] * bb_re[None] - ap_im[:S5_T, :, :, None] * bb_im[None]
    ab_im = ap_re[:S5_T, :, :, None] * bb_im[None] + ap_im[:S5_T, :, :, None] * bb_re[None]

    def spread(base, col_of_base):
        cols = jnp.arange(S5_BLOCK)
        expand = (col_of_base(cols)[None, :] == jnp.arange(base.shape[1])[:, None]).astype(F32)
        return jnp.dot(base, expand, precision=hi)
    state_col = lambda cols: cols % p_
    chan_col = lambda cols: cols // (m_ * h_) * h_ + cols % h_
    row_group = jnp.arange(S5_BLOCK) // h_ % m_
    keep_state = (row_group[:, None] == (jnp.arange(S5_BLOCK) // p_)[None, :])[None]
    keep_chan = (row_group[:, None] == (jnp.arange(S5_BLOCK) // h_ % m_)[None, :])[None]
    keep_out = ((jnp.arange(S5_BLOCK) // p_)[:, None] == (jnp.arange(S5_BLOCK) // h_ % m_)[None, :])[None]

    def wv_part(ab):
        base = ab[::-1].reshape(q_, S5_QUAD, s_, m_, p_, h_).transpose(0, 2, 1, 3, 5, 4).reshape(-1, p_)
        return jnp.where(keep_state, spread(base, state_col).reshape(q_ * s_, S5_BLOCK, S5_BLOCK), 0.0)
    wv = jnp.concatenate([wv_part(ab_re), wv_part(ab_im)], axis=-1)

    k_tau = (jnp.einsum('gop,tgpi->tgio', c_re, ab_re, precision=hi)
             - jnp.einsum('gop,tgpi->tgio', c_im, ab_im, precision=hi))
    zero = jnp.zeros_like(k_tau[0])
    taps = jnp.stack([jnp.stack([jnp.stack([k_tau[S5_QUAD * dq + ti - tj] if S5_QUAD * dq + ti - tj >= 0 else zero
                                            for ti in range(S5_QUAD)]) for tj in range(S5_QUAD)])
                      for dq in range(q_)])
    base = taps.reshape(q_, S5_QUAD, S5_QUAD, s_, m_, h_, h_).transpose(0, 3, 1, 4, 5, 2, 6).reshape(-1, S5_QUAD * h_)
    kb = jnp.where(keep_chan, spread(base, chan_col).reshape(q_ * s_, S5_BLOCK, S5_BLOCK), 0.0)

    cp = lambda c: c.transpose(0, 2, 1)[:, :, None, :]
    w_re = cp(c_re) * ap_re[1:].transpose(1, 2, 0)[..., None] - cp(c_im) * ap_im[1:].transpose(1, 2, 0)[..., None]
    w_im = cp(c_re) * ap_im[1:].transpose(1, 2, 0)[..., None] + cp(c_im) * ap_re[1:].transpose(1, 2, 0)[..., None]
    def wc_part(w):
        base = w.reshape(s_, m_, p_, q_, S5_QUAD, h_).transpose(3, 0, 1, 2, 4, 5).reshape(-1, S5_QUAD * h_)
        return jnp.where(keep_out, spread(base, chan_col).reshape(q_ * s_, S5_BLOCK, S5_BLOCK), 0.0)
    wc = jnp.concatenate([wc_part(w_re), wc_part(-w_im)], axis=1)

    n_steps = max(1, (rows - 1).bit_length())
    pows = [(ap_re[S5_T].reshape(S5_NSTATE), ap_im[S5_T].reshape(S5_NSTATE))]
    for _ in range(n_steps - 1):
        r, i = pows[-1]
        pows.append((r * r - i * i, 2.0 * r * i))
    apow = jnp.stack([jnp.stack(p) for p in pows])
    return wv.astype(BF16), kb.astype(BF16), wc.astype(BF16), apow


def _attn_fill_bias(diag_ref, bias_ref):
    width = diag_ref.shape[1]
    qi = lax.broadcasted_iota(jnp.int32, (ATTN_TQ, ATTN_TK), 0)
    kj = lax.broadcasted_iota(jnp.int32, (ATTN_TQ, ATTN_TK), 1)
    q_chunk = qi // CHUNK + (BAND_CHUNKS - 1)
    k_chunk = kj // CHUNK
    in_band = (k_chunk <= q_chunk) & (k_chunk >= q_chunk - (BAND_CHUNKS - 1))
    for h in range(ATTN_HEADS):
        rows = jnp.broadcast_to(diag_ref[h:h + 1, :], (ATTN_TQ, width))
        base = pltpu.roll(rows, 0, 1, stride=1, stride_axis=0)[:, :ATTN_TK]
        for t in range(bias_ref.shape[0]):
            bias_ref[t, h] = jnp.where(in_band & (kj >= ATTN_PREV - t * ATTN_TQ), base, NEG_INF)


def _attn_kernel(q_ref, k0_ref, k1_ref, k2_ref, v0_ref, v1_ref, v2_ref, diag_ref, o_ref, bias_ref):
    @pl.when((pl.program_id(0) == 0) & (pl.program_id(1) == 0))
    def _():
        _attn_fill_bias(diag_ref, bias_ref)

    table = jnp.minimum(pl.program_id(1), bias_ref.shape[0] - 1)
    lane = lax.broadcasted_iota(jnp.int32, (1, LANES), 1)
    for p in range(ATTN_WIDTH // LANES):
        cols = slice(p * LANES, (p + 1) * LANES)
        qp = q_ref[:, cols]
        kcat = jnp.concatenate([k0_ref[:, cols], k1_ref[:, cols], k2_ref[:, cols]], axis=0)
        vcat = jnp.concatenate([v0_ref[:, cols], v1_ref[:, cols], v2_ref[:, cols]], axis=0)
        first = lane < HEAD_DIM
        zero = jnp.zeros_like(qp)
        qm = jnp.concatenate([jnp.where(first, qp, zero), jnp.where(first, zero, qp)], axis=0)
        bias = jnp.concatenate([bias_ref[table, 2 * p], bias_ref[table, 2 * p + 1]], axis=0)
        s = _dot_nt(qm, kcat) + bias
        e = jnp.exp(s - jnp.max(s, axis=-1, keepdims=True)).astype(BF16)
        o = _dot(e, jnp.concatenate([vcat, jnp.ones_like(vcat)], axis=1))
        r = o[:, :LANES] / o[:, LANES:]
        o_ref[:, cols] = jnp.where(first, r[:ATTN_TQ], r[ATTN_TQ:])


def _attn_call(q, k, v, diag, batch, seq):
    n = q.shape[0]
    tiles = seq // ATTN_TQ
    n_prev = ATTN_PREV // ATTN_TQ
    cur = pl.BlockSpec((ATTN_TQ, ATTN_WIDTH), lambda b, t: (b * tiles + t, 0))
    prev = lambda d: pl.BlockSpec((ATTN_TQ, ATTN_WIDTH), lambda b, t: (b * tiles + jnp.maximum(t - d, 0), 0))
    assert n_prev == 2
    return pl.pallas_call(
        _attn_kernel,
        grid=(batch, tiles),
        in_specs=[cur, prev(2), prev(1), cur, prev(2), prev(1), cur, pl.BlockSpec(diag.shape, lambda b, t: (0, 0))],
        out_specs=cur,
        out_shape=jax.ShapeDtypeStruct((n, ATTN_WIDTH), F32),
        scratch_shapes=[pltpu.VMEM((n_prev + 1, ATTN_HEADS, ATTN_TQ, ATTN_TK), F32)],
        compiler_params=pltpu.CompilerParams(dimension_semantics=("arbitrary", "arbitrary"),
                                             vmem_limit_bytes=VMEM_LIMIT_BYTES),
        name="band_attn",
    )(q, k, k, k, v, v, v, diag)


def _attn_diag(rel_bias):
    width = ATTN_TQ + ATTN_TK
    n = jnp.arange(width)
    offset = jnp.where(n < ATTN_TK, n, n - width)
    rel = jnp.clip(ATTN_PREV - offset, -MAX_REL, MAX_REL) + MAX_REL
    return jnp.take(rel_bias.astype(F32), rel, axis=1)


def _router_gates(lt):
    tm = lt.shape[1]
    row = lax.broadcasted_iota(jnp.int32, (ROUTER_SLAB, tm), 0)
    gl = lt[0:ROUTER_SLAB]
    gmax = jnp.max(gl, axis=0, keepdims=True)
    p_g = 1.0 / jnp.sum(jnp.exp(gl - gmax), axis=0, keepdims=True)
    g_top = jnp.min(jnp.where(gl == gmax, row, ROUTER_SLAB), axis=0, keepdims=True)
    el = jnp.zeros((ROUTER_SLAB, tm), F32)
    for g in range(N_EXPERT_GROUPS):
        el = el + jnp.where(g_top == g, lt[(g + 1) * ROUTER_SLAB:(g + 2) * ROUTER_SLAB], 0.0)
    ee = jnp.exp(el - jnp.max(el, axis=0, keepdims=True))
    ep = ee / jnp.sum(ee, axis=0, keepdims=True)
    p1 = jnp.max(ep, axis=0, keepdims=True)
    i1 = jnp.min(jnp.where(ep == p1, row, ROUTER_SLAB), axis=0, keepdims=True)
    rest = jnp.where(row == i1, -1.0, ep)
    p2 = jnp.max(rest, axis=0, keepdims=True)
    i2 = jnp.min(jnp.where(rest == p2, row, ROUTER_SLAB), axis=0, keepdims=True)
    tot = p1 + p2
    w = (jnp.where(row == i1, p1 / tot, 0.0) + jnp.where(row == i2, p2 / tot, 0.0)) * p_g
    return [jnp.where(g_top == g, w, 0.0) for g in range(N_EXPERT_GROUPS)], g_top


def _sort_matrix(slot, n_slots, slot_axis):
    shape = (n_slots, slot.shape[1]) if slot_axis == 0 else (slot.shape[0], n_slots)
    return jnp.where(lax.broadcasted_iota(jnp.int32, shape, slot_axis) == slot, 1.0, 0.0).astype(BF16)


def _out_kernel(oa_ref, y0_ref, y1_ref, u0_ref, u1_ref, oc_ref, x_ref, d_ref, gluw_ref, glub_ref, og_ref, woutf_ref,
                fg_ref, wr_ref, br_ref, bd_ref, x1_ref, hs_ref, tok_ref, cnt_ref, wout_ref):
    tm = x_ref.shape[0]
    n_slots = hs_ref.shape[0]

    @pl.when(pl.program_id(0) == 0)
    def _():
        wout_ref[...] = woutf_ref[0].astype(BF16)

    y = jnp.concatenate([y0_ref[...], y1_ref[...]], axis=-1)
    u = jnp.concatenate([u0_ref[...], u1_ref[...]], axis=-1)
    y = jax.nn.gelu(y + d_ref[...] * u)
    ob = y * jax.nn.sigmoid(_dot(y.astype(BF16), gluw_ref[...]) + glub_ref[...])
    o = jnp.concatenate([oa_ref[...], ob, oc_ref[...]], axis=-1)
    on = o * lax.rsqrt(_group_sumsq(o, bd_ref) * (1.0 / OUT_NORM_GROUP) + EPS) * og_ref[...]
    x1 = x_ref[...] + _dot(on.astype(BF16), wout_ref[...])
    x1_ref[...] = x1
    ms = jnp.mean(x1 * x1, axis=-1, keepdims=True)
    hb = (x1 * lax.rsqrt(ms + EPS) * fg_ref[...]).astype(BF16)
    lt = _dot(hb, wr_ref[...]).T + br_ref[...]
    slabs, g_top = _router_gates(lt)

    row = lax.broadcasted_iota(jnp.int32, (ROUTER_SLAB, tm), 0)
    lane = lax.broadcasted_iota(jnp.int32, (ROUTER_SLAB, tm), 1)
    member = jnp.where(row == g_top, 1.0, 0.0)
    cum = member
    shift = 1
    while shift < tm:
        cum = cum + jnp.where(lane >= shift, pltpu.roll(cum, shift, 1), 0.0)
        shift *= 2
    count = cum[:, tm - 1:tm]
    padded = jnp.floor((count + (MOE_UNIT - 1)) * (1.0 / MOE_UNIT)) * MOE_UNIT
    group_row = lax.broadcasted_iota(jnp.int32, (ROUTER_SLAB, 1), 0)
    start = jnp.zeros((ROUTER_SLAB, 1), F32)
    for g in range(1, N_EXPERT_GROUPS):
        start = start + jnp.where(group_row >= g, padded[g - 1:g, :], 0.0)
    slot = jnp.sum(member * (start + cum - 1.0), axis=0, keepdims=True)
    cnt_ref[...] = jnp.broadcast_to(count, cnt_ref.shape)

    pad = jnp.zeros((LANES - SLOT_LANE - ROUTER_SLAB, tm), F32)
    gates = jnp.concatenate(slabs + [jnp.broadcast_to(slot, (ROUTER_SLAB, tm)), pad], axis=0).T
    tok_ref[...] = gates

    perm = _sort_matrix(slot.astype(jnp.int32), n_slots, 0)
    hs_ref[:, :D_MODEL] = _dot(perm, hb)
    g_hi = gates.astype(BF16)
    g_mid = (gates - g_hi.astype(F32)).astype(BF16)
    g_lo = (gates - g_hi.astype(F32) - g_mid.astype(F32)).astype(BF16)
    gs = _dot(perm, jnp.concatenate([g_hi, g_mid, g_lo], axis=-1))
    hs_ref[:, D_MODEL:] = gs[:, :LANES] + gs[:, LANES:2 * LANES] + gs[:, 2 * LANES:]


def _out_call(oa, y0, y1, u0, u1, oc, x, d, gluw, glub, og, wout, layer, fg, wr, br, bd, tm):
    n = x.shape[0]
    n_slots = tm + MOE_PAD_ROWS
    row = lambda c: pl.BlockSpec((tm, c), lambda i: (i, 0))
    full = lambda a: pl.BlockSpec(a.shape, lambda i: (0,) * a.ndim)
    return pl.pallas_call(
        _out_kernel,
        grid=(n // tm,),
        in_specs=[row(SGU_WIDTH), row(LANES), row(LANES), row(LANES), row(LANES), row(ATTN_WIDTH), row(D_MODEL),
                  full(d), full(gluw),
                  full(glub), full(og), pl.BlockSpec((1,) + wout.shape[1:], lambda i: (layer, 0, 0), pipeline_mode=pl.Buffered(1)),
                  full(fg), full(wr), full(br), full(bd)],
        out_specs=[row(D_MODEL), pl.BlockSpec((n_slots, MOE_ROW), lambda i: (i, 0)), row(LANES),
                   pl.BlockSpec((ROUTER_SLAB, LANES), lambda i: (i, 0))],
        out_shape=[jax.ShapeDtypeStruct((n, D_MODEL), F32), jax.ShapeDtypeStruct((n // tm * n_slots, MOE_ROW), F32),
                   jax.ShapeDtypeStruct((n, LANES), F32),
                   jax.ShapeDtypeStruct((n // tm * ROUTER_SLAB, LANES), F32)],
        scratch_shapes=[pltpu.VMEM(wout.shape[1:], BF16)],
        compiler_params=pltpu.CompilerParams(dimension_semantics=("arbitrary",), vmem_limit_bytes=VMEM_LIMIT_BYTES),
        name="out_proj",
    )(oa, y0, y1, u0, u1, oc, x, d, gluw, glub, og, wout, fg, wr, br, bd)


def _router_tables(wg, bg, we, be):
    w = jnp.zeros((D_MODEL, LANES), F32)
    b = jnp.full((LANES,), NEG_INF, F32)
    w = w.at[:, 0:N_EXPERT_GROUPS].set(wg.astype(F32))
    b = b.at[0:N_EXPERT_GROUPS].set(bg.astype(F32))
    for g in range(N_EXPERT_GROUPS):
        c0 = (g + 1) * ROUTER_SLAB
        w = w.at[:, c0:c0 + EXPERTS_PER_GROUP].set(we[g].astype(F32))
        b = b.at[c0:c0 + EXPERTS_PER_GROUP].set(be[g].astype(F32))
    return w.astype(BF16), b[:, None]


def _route_plan(counts, n, tm):
    n_tiles = n // tm
    units_per_tile = (tm + MOE_PAD_ROWS) // MOE_UNIT
    units_per_mtile = tm // MOE_UNIT
    cnt = counts.reshape(n_tiles, ROUTER_SLAB, LANES)[:, :N_EXPERT_GROUPS, 0].astype(jnp.int32)
    seg = (cnt + MOE_UNIT - 1) // MOE_UNIT
    seg_start = jnp.cumsum(seg, axis=1) - seg
    before = jnp.cumsum(seg, axis=0) - seg
    total = jnp.sum(seg, axis=0)
    mtiles = (total + units_per_mtile - 1) // units_per_mtile
    mtile_end = jnp.cumsum(mtiles)
    group_start = (mtile_end - mtiles) * units_per_mtile
    n_mtiles = n_tiles + N_EXPERT_GROUPS + -(-n_tiles * MOE_PAD_ROWS // tm)
    tile_group = jnp.minimum(jnp.sum(jnp.arange(n_mtiles)[:, None] >= mtile_end[None, :], axis=1),
                             N_EXPERT_GROUPS - 1)
    meta = jnp.concatenate([tile_group, mtile_end[-1:]]).astype(jnp.int32)
    groups = jnp.arange(N_EXPERT_GROUPS)
    pick = lambda table, g: jnp.sum(jnp.where(g[..., None] == groups, table, 0), axis=-1)

    q = jnp.arange(n_mtiles * units_per_mtile)
    g = jnp.repeat(tile_group, units_per_mtile)
    ql = q - pick(group_start, g)
    ends_g = pick((before + seg)[None], g[:, None])
    tile_of = jnp.minimum(jnp.sum(ql[:, None] >= ends_g, axis=1), n_tiles - 1)
    offset_g = pick((seg_start - before)[None], g[:, None])
    local = ql + jnp.sum(jnp.where(tile_of[:, None] == jnp.arange(n_tiles), offset_g, 0), axis=1)
    valid = (ql < pick(total, g)) & (q // units_per_mtile < mtile_end[-1])
    src = jnp.where(valid, tile_of * units_per_tile + local, units_per_tile - 1).astype(jnp.int32)

    ul = jnp.arange(units_per_tile)[None, :, None]
    seg_of = jnp.sum(ul >= (seg_start + seg)[:, None, :], axis=-1)
    gi = jnp.minimum(seg_of, N_EXPERT_GROUPS - 1)
    glob = pick((group_start + before - seg_start)[:, None, :], gi) + ul[..., 0]
    dst = jnp.where(seg_of < N_EXPERT_GROUPS, glob, 0).astype(jnp.int32).reshape(-1)
    return meta, src, dst, n_mtiles


def _unit_gather(table_ref, first, src_ref, dst_ref, sem, wait):
    n_rows = dst_ref.shape[0]
    if wait:
        pltpu.make_async_copy(src_ref.at[pl.ds(0, n_rows)], dst_ref, sem).wait()
        return

    def body(k, c):
        u = pl.multiple_of(table_ref[first + k] * MOE_UNIT, MOE_UNIT)
        pltpu.make_async_copy(src_ref.at[pl.ds(u, MOE_UNIT)],
                              dst_ref.at[pl.ds(pl.multiple_of(k * MOE_UNIT, MOE_UNIT), MOE_UNIT)], sem).start()
        return c
    lax.fori_loop(0, n_rows // MOE_UNIT, body, 0, unroll=4)


def _moe_group_kernel(meta_ref, src_ref, hs_ref, wg_ref, wu_ref, wd_ref, o_ref, wgb_ref, wub_ref, wdb_ref,
                      buf_ref, sem):
    j = pl.program_id(0)
    n_steps = pl.num_programs(0)
    n_used = meta_ref[n_steps]
    group = meta_ref[j]
    n_units = o_ref.shape[0] // MOE_UNIT
    slot = j % 2

    @pl.when(j == 0)
    def _():
        _unit_gather(src_ref, 0, hs_ref, buf_ref.at[0], sem.at[0], wait=False)

    @pl.when(j + 1 < n_steps)
    def _():
        _unit_gather(src_ref, (j + 1) * n_units, hs_ref, buf_ref.at[1 - slot], sem.at[1 - slot], wait=False)

    @pl.when((j == 0) | (group != meta_ref[jnp.maximum(j - 1, 0)]))
    def _():
        wgb_ref[...] = wg_ref[...].astype(BF16)
        wub_ref[...] = wu_ref[...].astype(BF16)
        wdb_ref[...] = wd_ref[...].astype(BF16)

    _unit_gather(src_ref, j * n_units, hs_ref, buf_ref.at[slot], sem.at[slot], wait=True)

    @pl.when(j < n_used)
    def _():
        h = buf_ref[slot, :, :D_MODEL].astype(BF16)
        gates = buf_ref[slot, :, D_MODEL:]
        lane = lax.broadcasted_iota(jnp.int32, (1, LANES), 1)
        out = None
        for e in range(EXPERTS_PER_GROUP):
            gt = _dot(h, wgb_ref[e])
            up = _dot(h, wub_ref[e])
            gate = jnp.sum(jnp.where(lane == group * ROUTER_SLAB + e, gates, 0.0), axis=-1, keepdims=True)
            a = (gt * jax.nn.sigmoid(gt)) * up * gate
            part = _dot(a.astype(BF16), wdb_ref[e])
            out = part if out is None else out + part
        o_ref[...] = out

    @pl.when(j >= n_used)
    def _():
        o_ref[...] = jnp.zeros_like(o_ref)


def _moe_group_call(meta, src, hs, wg, wu, wd, layer, n_mtiles, tm):
    wg, wu, wd = (a.reshape((-1,) + a.shape[2:]) for a in (wg, wu, wd))
    by_group = lambda a: pl.BlockSpec((EXPERTS_PER_GROUP,) + a.shape[1:],
                                      lambda j, meta, src: (layer * N_EXPERT_GROUPS + meta[j], 0, 0))
    return pl.pallas_call(
        _moe_group_kernel,
        grid_spec=pltpu.PrefetchScalarGridSpec(
            num_scalar_prefetch=2,
            grid=(n_mtiles,),
            in_specs=[pl.BlockSpec(memory_space=pl.ANY), by_group(wg), by_group(wu), by_group(wd)],
            out_specs=pl.BlockSpec((tm, D_MODEL), lambda j, meta, src: (j, 0)),
            scratch_shapes=[pltpu.VMEM((EXPERTS_PER_GROUP, D_MODEL, D_EXPERT), BF16),
                            pltpu.VMEM((EXPERTS_PER_GROUP, D_MODEL, D_EXPERT), BF16),
                            pltpu.VMEM((EXPERTS_PER_GROUP, D_EXPERT, D_MODEL), BF16),
                            pltpu.VMEM((2, tm, MOE_ROW), F32), pltpu.SemaphoreType.DMA((2,))],
        ),
        out_shape=jax.ShapeDtypeStruct((n_mtiles * tm, D_MODEL), F32),
        compiler_params=pltpu.CompilerParams(dimension_semantics=("arbitrary",), vmem_limit_bytes=VMEM_LIMIT_BYTES),
        name="moe_experts",
    )(meta, src, hs, wg, wu, wd)


def _combine_kernel(dst_ref, x1_ref, tok_ref, ys_ref, o_ref, buf_ref, sem):
    i = pl.program_id(0)
    n_slots = buf_ref.shape[1]
    n_units = n_slots // MOE_UNIT
    slot = i % 2

    @pl.when(i == 0)
    def _():
        _unit_gather(dst_ref, 0, ys_ref, buf_ref.at[0], sem.at[0], wait=False)

    @pl.when(i + 1 < pl.num_programs(0))
    def _():
        _unit_gather(dst_ref, (i + 1) * n_units, ys_ref, buf_ref.at[1 - slot], sem.at[1 - slot], wait=False)

    _unit_gather(dst_ref, i * n_units, ys_ref, buf_ref.at[slot], sem.at[slot], wait=True)

    y = buf_ref[slot]
    y_hi = y.astype(BF16)
    y_lo = (y - y_hi.astype(F32)).astype(BF16)
    unsort = _sort_matrix(tok_ref[:, SLOT_LANE:SLOT_LANE + 1].astype(jnp.int32), n_slots, 1)
    o_ref[...] = x1_ref[...] + _dot(unsort, y_hi) + _dot(unsort, y_lo)


def _combine_call(dst, x1, tok, ys, tm):
    n = x1.shape[0]
    n_slots = tm + MOE_PAD_ROWS
    return pl.pallas_call(
        _combine_kernel,
        grid_spec=pltpu.PrefetchScalarGridSpec(
            num_scalar_prefetch=1,
            grid=(n // tm,),
            in_specs=[pl.BlockSpec((tm, D_MODEL), lambda i, dst: (i, 0)),
                      pl.BlockSpec((tm, LANES), lambda i, dst: (i, 0)),
                      pl.BlockSpec(memory_space=pl.ANY)],
            out_specs=pl.BlockSpec((tm, D_MODEL), lambda i, dst: (i, 0)),
            scratch_shapes=[pltpu.VMEM((2, n_slots, D_MODEL), F32), pltpu.SemaphoreType.DMA((2,))],
        ),
        out_shape=jax.ShapeDtypeStruct((n, D_MODEL), F32),
        compiler_params=pltpu.CompilerParams(dimension_semantics=("arbitrary",), vmem_limit_bytes=VMEM_LIMIT_BYTES),
        name="moe_combine",
    )(dst, x1, tok, ys)


def kernel(x, norm_mix, w_in, sgu_norm, sgu_w, sgu_b, s5_lambda_re, s5_lambda_im, s5_log_dt, s5_b_re, s5_b_im,
           s5_c_re, s5_c_im, s5_d, s5_glu_w, s5_glu_b, q_norm, k_norm, rel_bias, out_norm, w_out, norm_ffn,
           router_group_w, router_group_b, router_expert_w, router_expert_b, w_gate, w_up, w_down):
    batch, seq, _ = x.shape
    n = batch * seq
    depth = w_in.shape[0]
    t = _tiles(n, seq)
    assert seq % ATTN_TQ == 0 and seq % (S5_T * t["s5_rows"]) == 0
    row_vec = lambda a: a.astype(F32)[None, :]

    lane_group = jnp.arange(LANES) // HEAD_DIM
    bd = (lane_group[:, None] == lane_group[None, :]).astype(BF16)
    block_chunk = jnp.arange(SGU_BLOCK) // CHUNK
    sgu_mask = block_chunk[None, :] <= block_chunk[:, None]

    xf = x.reshape(n, D_MODEL)
    for l in range(depth):
        sguw = jnp.where(sgu_mask[None], sgu_w[l], 0).astype(BF16)
        sgub = jnp.repeat(sgu_b[l].astype(F32).T, HEAD_DIM, axis=1)
        qg = row_vec(jnp.tile(q_norm[l], ATTN_HEADS)) * (HEAD_DIM ** -0.5)
        kg = row_vec(jnp.tile(k_norm[l], ATTN_HEADS))
        oa, u0, u1, q, k, v = _in_call(xf, row_vec(norm_mix[l]), w_in, l, row_vec(sgu_norm[l]), sguw, sgub,
                                      qg, kg, bd, t["tm_in"])

        wv, kb, wc, apow = _s5_tables(s5_lambda_re[l], s5_lambda_im[l], s5_log_dt[l], s5_b_re[l], s5_b_im[l],
                                        s5_c_re[l], s5_c_im[l], t["s5_rows"])
        y0, y1 = _s5_call(u0, u1, wv, kb, wc, apow, batch, t["s5_rows"])

        oc = _attn_call(q, k, v, _attn_diag(rel_bias[l]), batch, seq)

        wr, br = _router_tables(router_group_w[l], router_group_b[l], router_expert_w[l], router_expert_b[l])
        x1, hs, tok, counts = _out_call(oa, y0, y1, u0, u1, oc, xf, row_vec(s5_d[l]), s5_glu_w[l].astype(BF16), row_vec(s5_glu_b[l]),
                                 row_vec(out_norm[l]), w_out, l, row_vec(norm_ffn[l]), wr, br, bd,
                                 t["tm_out"])

        meta, src, dst, n_mtiles = _route_plan(counts, n, t["tm_out"])
        ys = _moe_group_call(meta, src, hs, w_gate, w_up, w_down, l, n_mtiles, t["tm_out"])
        xf = _combine_call(dst, x1, tok, ys, t["tm_out"])
    return xf.reshape(batch, seq, D_MODEL)
```

```python
import jax
import jax.numpy as jnp
from jax import lax
from jax.experimental import pallas as pl
from jax.experimental.pallas import tpu as pltpu

F32 = jnp.float32
BF16 = jnp.bfloat16

D_MODEL = 1024
CHUNK = 64
HEAD_DIM = 64
SGU_WIDTH = 256
SGU_HEADS = 4
SGU_BLOCK = 128
S5_WIDTH = 256
S5_GROUP = 16
S5_N_GROUPS = 16
S5_STATE = 64
ATTN_WIDTH = 512
ATTN_HEADS = 8
BAND_CHUNKS = 9
MAX_REL = 256
IN_COLS = 2 * SGU_WIDTH + S5_WIDTH + 3 * ATTN_WIDTH
OUT_NORM_GROUP = 64
N_EXPERT_GROUPS = 4
EXPERTS_PER_GROUP = 4
N_EXPERTS = 16
D_EXPERT = 256
EPS = 1e-6
NEG_INF = -1e30

LANES = 128
SUBLANES = 8
VMEM_LIMIT_BYTES = 56 * 1024 * 1024

S5_T = 16
S5_ROW = S5_T * S5_WIDTH
S5_NSTATE = S5_N_GROUPS * S5_STATE
S5_QUAD = 4
S5_SETS = 4
S5_QUADS = S5_T // S5_QUAD
S5_BLOCK = S5_QUAD * (S5_N_GROUPS // S5_SETS) * S5_GROUP
ATTN_TQ = 256
ATTN_PREV = (BAND_CHUNKS - 1) * CHUNK
ATTN_TK = ATTN_TQ + ATTN_PREV
ROUTER_SLAB = SUBLANES
MOE_ROW = D_MODEL + LANES
MOE_UNIT = SUBLANES
MOE_PAD_ROWS = N_EXPERT_GROUPS * MOE_UNIT
SLOT_LANE = N_EXPERT_GROUPS * ROUTER_SLAB


def _tiles(n_tokens, seq):
    def pick(pref, total):
        t = min(pref, total)
        assert total % t == 0
        return t
    return dict(
        tm_in=pick(512, seq),
        tm_out=pick(512, seq),
        s5_rows=pick(256, seq // S5_T),
    )


def _dot(a, b):
    return jnp.dot(a, b, preferred_element_type=F32)


def _dot_nt(a, b):
    return lax.dot_general(a, b, (((1,), (1,)), ((), ())), preferred_element_type=F32)


def _group_sumsq(x, bd_ref):
    x2 = (x * x).astype(BF16)
    parts = [_dot(x2[:, t * LANES:(t + 1) * LANES], bd_ref[...]) for t in range(x.shape[1] // LANES)]
    return jnp.concatenate(parts, axis=-1)


def _in_kernel(x_ref, g_ref, wf_ref, sgug_ref, sguw_ref, sgub_ref, qg_ref, kg_ref, bd_ref,
               oa_ref, u0_ref, u1_ref, q_ref, k_ref, v_ref, w_ref):
    tm = x_ref.shape[0]

    @pl.when(pl.program_id(0) == 0)
    def _():
        w_ref[...] = wf_ref[0].astype(BF16)

    x = x_ref[...]
    ms = jnp.mean(x * x, axis=-1, keepdims=True)
    hn = (x * lax.rsqrt(ms + EPS) * g_ref[...]).astype(BF16)

    z = jax.nn.gelu(_dot(hn, w_ref[:, 0:2 * SGU_WIDTH]))
    u = z[:, :SGU_WIDTH]
    v = z[:, SGU_WIDTH:]
    v = v * lax.rsqrt(jnp.mean(v * v, axis=-1, keepdims=True) + EPS) * sgug_ref[...]
    vb = v.astype(BF16)
    first_head = lax.broadcasted_iota(jnp.int32, (SGU_BLOCK, LANES), 1) < HEAD_DIM
    for r in range(tm // SGU_BLOCK):
        rows = slice(r * SGU_BLOCK, (r + 1) * SGU_BLOCK)
        for p in range(SGU_WIDTH // LANES):
            cols = slice(p * LANES, (p + 1) * LANES)
            vp = vb[rows, cols]
            mixed = jnp.where(first_head, _dot(sguw_ref[2 * p], vp), _dot(sguw_ref[2 * p + 1], vp))
            oa_ref[rows, cols] = u[rows, cols] * (mixed + sgub_ref[:, cols])

    c0 = 2 * SGU_WIDTH
    us = _dot(hn, w_ref[:, c0:c0 + S5_WIDTH])
    u0_ref[...] = us[:, :LANES]
    u1_ref[...] = us[:, LANES:]

    c0 += S5_WIDTH
    q = _dot(hn, w_ref[:, c0:c0 + ATTN_WIDTH])
    q_ref[...] = (q * lax.rsqrt(_group_sumsq(q, bd_ref) * (1.0 / HEAD_DIM) + EPS) * qg_ref[...]).astype(BF16)
    c0 += ATTN_WIDTH
    k = _dot(hn, w_ref[:, c0:c0 + ATTN_WIDTH])
    k_ref[...] = (k * lax.rsqrt(_group_sumsq(k, bd_ref) * (1.0 / HEAD_DIM) + EPS) * kg_ref[...]).astype(BF16)
    c0 += ATTN_WIDTH
    v_ref[...] = _dot(hn, w_ref[:, c0:c0 + ATTN_WIDTH]).astype(BF16)


def _in_call(x, g, w, layer, sgug, sguw, sgub, qg, kg, bd, tm):
    n = x.shape[0]
    row = lambda c: pl.BlockSpec((tm, c), lambda i: (i, 0))
    full = lambda a: pl.BlockSpec(a.shape, lambda i: (0,) * a.ndim)
    w_layer = pl.BlockSpec((1,) + w.shape[1:], lambda i: (layer, 0, 0), pipeline_mode=pl.Buffered(1))
    return pl.pallas_call(
        _in_kernel,
        grid=(n // tm,),
        in_specs=[row(D_MODEL), full(g), w_layer, full(sgug), full(sguw), full(sgub), full(qg), full(kg), full(bd)],
        out_specs=[row(SGU_WIDTH), row(LANES), row(LANES), row(ATTN_WIDTH), row(ATTN_WIDTH), row(ATTN_WIDTH)],
        out_shape=[jax.ShapeDtypeStruct((n, SGU_WIDTH), F32), jax.ShapeDtypeStruct((n, LANES), F32),
                   jax.ShapeDtypeStruct((n, LANES), F32), jax.ShapeDtypeStruct((n, ATTN_WIDTH), BF16),
                   jax.ShapeDtypeStruct((n, ATTN_WIDTH), BF16), jax.ShapeDtypeStruct((n, ATTN_WIDTH), BF16)],
        scratch_shapes=[pltpu.VMEM(w.shape[1:], BF16)],
        compiler_params=pltpu.CompilerParams(dimension_semantics=("arbitrary",), vmem_limit_bytes=VMEM_LIMIT_BYTES),
        name="in_proj",
    )(x, g, w, sgug, sguw, sgub, qg, kg, bd)


def _s5_kernel(u0_ref, u1_ref, wv_ref, kb_ref, wc_ref, apow_ref, y0_ref, y1_ref, ur_ref, s_ref, sp_ref, yb_ref, carry_ref):
    rows = ur_ref.shape[0]
    n_steps = apow_ref.shape[1]
    half = LANES // 2
    low = lax.broadcasted_iota(jnp.int32, (1, LANES), 1) < half
    block = lambda tq, gs: slice((tq * S5_SETS + gs) * S5_BLOCK, (tq * S5_SETS + gs + 1) * S5_BLOCK)

    @pl.when(pl.program_id(1) == 0)
    def _():
        carry_ref[...] = jnp.zeros_like(carry_ref)

    for tq in range(S5_QUADS):
        for lt in range(S5_QUAD // 2):
            t0 = tq * S5_QUAD + 2 * lt
            for pair, u_ref in enumerate((u0_ref, u1_ref)):
                a = u_ref[pl.ds(t0, rows, stride=S5_T), :]
                b = u_ref[pl.ds(t0 + 1, rows, stride=S5_T), :]
                for par in range(2):
                    tile = jnp.where(low, a, pltpu.roll(b, half, 1)) if par == 0 else jnp.where(low, pltpu.roll(a, half, 1), b)
                    col = block(tq, 2 * pair + par).start + lt * LANES
                    ur_ref[:, col:col + LANES] = tile.astype(BF16)

    for gs in range(S5_SETS):
        v = None
        for tq in range(S5_QUADS):
            part = _dot(ur_ref[:, block(tq, gs)], wv_ref[0, tq * S5_SETS + gs])
            v = part if v is None else v + part
        s_ref[:, gs * S5_BLOCK:(gs + 1) * S5_BLOCK] = v[:, :S5_BLOCK]
        s_ref[:, S5_NSTATE + gs * S5_BLOCK:S5_NSTATE + (gs + 1) * S5_BLOCK] = v[:, S5_BLOCK:]

    row_id = lax.broadcasted_iota(jnp.int32, (rows, LANES), 0)
    for cb in range(S5_NSTATE // LANES):
        cre = slice(cb * LANES, (cb + 1) * LANES)
        cim = slice(S5_NSTATE + cb * LANES, S5_NSTATE + (cb + 1) * LANES)
        re = s_ref[:, cre]
        im = s_ref[:, cim]
        c_re = carry_ref[0:1, cre]
        c_im = carry_ref[0:1, cim]
        a_re = apow_ref[0, 0, 0:1, cre]
        a_im = apow_ref[0, 0, 1:2, cre]
        re = re + jnp.where(row_id == 0, a_re * c_re - a_im * c_im, 0.0)
        im = im + jnp.where(row_id == 0, a_re * c_im + a_im * c_re, 0.0)
        for k in range(n_steps):
            shift = 1 << k
            a_re = apow_ref[0, k, 0:1, cre]
            a_im = apow_ref[0, k, 1:2, cre]
            re_s = jnp.where(row_id >= shift, pltpu.roll(re, shift, 0), 0.0)
            im_s = jnp.where(row_id >= shift, pltpu.roll(im, shift, 0), 0.0)
            re, im = re + a_re * re_s - a_im * im_s, im + a_re * im_s + a_im * re_s
        sp_ref[:, cre] = jnp.where(row_id >= 1, pltpu.roll(re, 1, 0), c_re).astype(BF16)
        sp_ref[:, cim] = jnp.where(row_id >= 1, pltpu.roll(im, 1, 0), c_im).astype(BF16)
        carry_ref[0:1, cre] = re[rows - 1:rows, :]
        carry_ref[0:1, cim] = im[rows - 1:rows, :]

    for gs in range(S5_SETS):
        state = jnp.concatenate([sp_ref[:, gs * S5_BLOCK:(gs + 1) * S5_BLOCK],
                                 sp_ref[:, S5_NSTATE + gs * S5_BLOCK:S5_NSTATE + (gs + 1) * S5_BLOCK]], axis=1)
        for tq in range(S5_QUADS):
            y = _dot(state, wc_ref[0, tq * S5_SETS + gs])
            for tj in range(tq + 1):
                y = y + _dot(ur_ref[:, block(tj, gs)], kb_ref[0, (tq - tj) * S5_SETS + gs])
            yb_ref[:, block(tq, gs)] = y

    for tq in range(S5_QUADS):
        for lt in range(S5_QUAD // 2):
            t0 = tq * S5_QUAD + 2 * lt
            for pair, y_ref in enumerate((y0_ref, y1_ref)):
                a = yb_ref[:, block(tq, 2 * pair).start + lt * LANES:block(tq, 2 * pair).start + (lt + 1) * LANES]
                b = yb_ref[:, block(tq, 2 * pair + 1).start + lt * LANES:block(tq, 2 * pair + 1).start + (lt + 1) * LANES]
                y_ref[pl.ds(t0, rows, stride=S5_T), :] = jnp.where(low, a, pltpu.roll(b, half, 1))
                y_ref[pl.ds(t0 + 1, rows, stride=S5_T), :] = jnp.where(low, pltpu.roll(a, half, 1), b)


def _s5_call(u0, u1, wv, kb, wc, apow, layer, batch, rows):
    n = u0.shape[0]
    tok = rows * S5_T
    tiles_per_seq = n // batch // tok
    const = lambda a: pl.BlockSpec((1,) + a.shape[1:], lambda b, t: (layer,) + (0,) * (a.ndim - 1),
                                   pipeline_mode=pl.Buffered(1))
    blk = pl.BlockSpec((tok, LANES), lambda b, t: (b * tiles_per_seq + t, 0))
    half = jax.ShapeDtypeStruct((n, LANES), F32)
    return pl.pallas_call(
        _s5_kernel,
        grid=(batch, tiles_per_seq),
        in_specs=[blk, blk, const(wv), const(kb), const(wc), const(apow)],
        out_specs=[blk, blk],
        out_shape=[half, half],
        scratch_shapes=[pltpu.VMEM((rows, S5_ROW), BF16), pltpu.VMEM((rows, 2 * S5_NSTATE), F32),
                        pltpu.VMEM((rows, 2 * S5_NSTATE), BF16), pltpu.VMEM((rows, S5_ROW), F32),
                        pltpu.VMEM((SUBLANES, 2 * S5_NSTATE), F32)],
        compiler_params=pltpu.CompilerParams(dimension_semantics=("arbitrary", "arbitrary"),
                                             vmem_limit_bytes=VMEM_LIMIT_BYTES),
        name="s5_scan",
    )(u0, u1, wv, kb, wc, apow)


def _s5_tables(lam_re, lam_im, log_dt, b_re, b_im, c_re, c_im, rows):
    p_, h_ = S5_STATE, S5_GROUP
    q_, s_, m_ = S5_QUADS, S5_SETS, S5_N_GROUPS // S5_SETS
    hi = lax.Precision.HIGHEST
    lam_re, lam_im = lam_re.astype(F32), lam_im.astype(F32)
    dt = jnp.exp(log_dt.astype(F32))[:, None]
    tau = jnp.arange(S5_T + 1, dtype=F32)[:, None, None]
    mag = jnp.exp(tau * (lam_re * dt)[None])
    ap_re = mag * jnp.cos(tau * (lam_im * dt)[None])
    ap_im = mag * jnp.sin(tau * (lam_im * dt)[None])
    n_re, n_im = ap_re[1] - 1.0, ap_im[1]
    den = lam_re * lam_re + lam_im * lam_im
    k_re = ((n_re * lam_re + n_im * lam_im) / den)[..., None]
    k_im = ((n_im * lam_re - n_re * lam_im) / den)[..., None]
    b_re, b_im = b_re.astype(F32), b_im.astype(F32)
    bb_re = k_re * b_re - k_im * b_im
    bb_im = k_re * b_im + k_im * b_re
    c_re, c_im = c_re.astype(F32), c_im.astype(F32)
    ab_re = ap_re[:S5_T, :, :, None] * bb_re[None] - ap_im[:S5_T, :, :, None] * bb_im[None]
    ab_im = ap_re[:S5_T, :, :, None] * bb_im[None] + ap_im[:S5_T, :, :, None] * bb_re[None]

    def spread(base, col_of_base):
        cols = jnp.arange(S5_BLOCK)
        expand = (col_of_base(cols)[None, :] == jnp.arange(base.shape[1])[:, None]).astype(F32)
        return jnp.dot(base, expand, precision=hi)
    state_col = lambda cols: cols % p_
    chan_col = lambda cols: cols // (m_ * h_) * h_ + cols % h_
    row_group = jnp.arange(S5_BLOCK) // h_ % m_
    keep_state = (row_group[:, None] == (jnp.arange(S5_BLOCK) // p_)[None, :])[None]
    keep_chan = (row_group[:, None] == (jnp.arange(S5_BLOCK) // h_ % m_)[None, :])[None]
    keep_out = ((jnp.arange(S5_BLOCK) // p_)[:, None] == (jnp.arange(S5_BLOCK) // h_ % m_)[None, :])[None]

    def wv_part(ab):
        base = ab[::-1].reshape(q_, S5_QUAD, s_, m_, p_, h_).transpose(0, 2, 1, 3, 5, 4).reshape(-1, p_)
        return jnp.where(keep_state, spread(base, state_col).reshape(q_ * s_, S5_BLOCK, S5_BLOCK), 0.0)
    wv = jnp.concatenate([wv_part(ab_re), wv_part(ab_im)], axis=-1)

    k_tau = (jnp.einsum('gop,tgpi->tgio', c_re, ab_re, precision=hi)
             - jnp.einsum('gop,tgpi->tgio', c_im, ab_im, precision=hi))
    zero = jnp.zeros_like(k_tau[0])
    taps = jnp.stack([jnp.stack([jnp.stack([k_tau[S5_QUAD * dq + ti - tj] if S5_QUAD * dq + ti - tj >= 0 else zero
                                            for ti in range(S5_QUAD)]) for tj in range(S5_QUAD)])
                      for dq in range(q_)])
    base = taps.reshape(q_, S5_QUAD, S5_QUAD, s_, m_, h_, h_).transpose(0, 3, 1, 4, 5, 2, 6).reshape(-1, S5_QUAD * h_)
    kb = jnp.where(keep_chan, spread(base, chan_col).reshape(q_ * s_, S5_BLOCK, S5_BLOCK), 0.0)

    cp = lambda c: c.transpose(0, 2, 1)[:, :, None, :]
    w_re = cp(c_re) * ap_re[1:].transpose(1, 2, 0)[..., None] - cp(c_im) * ap_im[1:].transpose(1, 2, 0)[..., None]
    w_im = cp(c_re) * ap_im[1:].transpose(1, 2, 0)[..., None] + cp(c_im) * ap_re[1:].transpose(1, 2, 0)[..., None]
    def wc_part(w):
        base = w.reshape(s_, m_, p_, q_, S5_QUAD, h_).transpose(3, 0, 1, 2, 4, 5).reshape(-1, S5_QUAD * h_)
        return jnp.where(keep_out, spread(base, chan_col).reshape(q_ * s_, S5_BLOCK, S5_BLOCK), 0.0)
    wc = jnp.concatenate([wc_part(w_re), wc_part(-w_im)], axis=1)

    n_steps = max(1, (rows - 1).bit_length())
    pows = [(ap_re[S5_T].reshape(S5_NSTATE), ap_im[S5_T].reshape(S5_NSTATE))]
    for _ in range(n_steps - 1):
        r, i = pows[-1]
        pows.append((r * r - i * i, 2.0 * r * i))
    apow = jnp.stack([jnp.stack(p) for p in pows])
    return wv.astype(BF16), kb.astype(BF16), wc.astype(BF16), apow


def _attn_fill_bias(diag_ref, bias_ref):
    width = diag_ref.shape[1]
    qi = lax.broadcasted_iota(jnp.int32, (ATTN_TQ, ATTN_TK), 0)
    kj = lax.broadcasted_iota(jnp.int32, (ATTN_TQ, ATTN_TK), 1)
    q_chunk = qi // CHUNK + (BAND_CHUNKS - 1)
    k_chunk = kj // CHUNK
    in_band = (k_chunk <= q_chunk) & (k_chunk >= q_chunk - (BAND_CHUNKS - 1))
    for h in range(ATTN_HEADS):
        rows = jnp.broadcast_to(diag_ref[h:h + 1, :], (ATTN_TQ, width))
        base = pltpu.roll(rows, 0, 1, stride=1, stride_axis=0)[:, :ATTN_TK]
        for t in range(bias_ref.shape[0]):
            bias_ref[t, h] = jnp.where(in_band & (kj >= ATTN_PREV - t * ATTN_TQ), base, NEG_INF)


def _attn_kernel(q_ref, k0_ref, k1_ref, k2_ref, v0_ref, v1_ref, v2_ref, diag_ref, o_ref, bias_ref):
    @pl.when((pl.program_id(0) == 0) & (pl.program_id(1) == 0))
    def _():
        _attn_fill_bias(diag_ref, bias_ref)

    table = jnp.minimum(pl.program_id(1), bias_ref.shape[0] - 1)
    lane = lax.broadcasted_iota(jnp.int32, (1, LANES), 1)
    for p in range(ATTN_WIDTH // LANES):
        cols = slice(p * LANES, (p + 1) * LANES)
        qp = q_ref[:, cols]
        kcat = jnp.concatenate([k0_ref[:, cols], k1_ref[:, cols], k2_ref[:, cols]], axis=0)
        vcat = jnp.concatenate([v0_ref[:, cols], v1_ref[:, cols], v2_ref[:, cols]], axis=0)
        first = lane < HEAD_DIM
        zero = jnp.zeros_like(qp)
        qm = jnp.concatenate([jnp.where(first, qp, zero), jnp.where(first, zero, qp)], axis=0)
        bias = jnp.concatenate([bias_ref[table, 2 * p], bias_ref[table, 2 * p + 1]], axis=0)
        s = _dot_nt(qm, kcat) + bias
        e = jnp.exp(s - jnp.max(s, axis=-1, keepdims=True)).astype(BF16)
        o = _dot(e, jnp.concatenate([vcat, jnp.ones_like(vcat)], axis=1))
        r = o[:, :LANES] / o[:, LANES:]
        o_ref[:, cols] = jnp.where(first, r[:ATTN_TQ], r[ATTN_TQ:])


def _attn_call(q, k, v, diag, batch, seq):
    n = q.shape[0]
    tiles = seq // ATTN_TQ
    n_prev = ATTN_PREV // ATTN_TQ
    cur = pl.BlockSpec((ATTN_TQ, ATTN_WIDTH), lambda b, t: (b * tiles + t, 0))
    prev = lambda d: pl.BlockSpec((ATTN_TQ, ATTN_WIDTH), lambda b, t: (b * tiles + jnp.maximum(t - d, 0), 0))
    assert n_prev == 2
    return pl.pallas_call(
        _attn_kernel,
        grid=(batch, tiles),
        in_specs=[cur, prev(2), prev(1), cur, prev(2), prev(1), cur, pl.BlockSpec(diag.shape, lambda b, t: (0, 0))],
        out_specs=cur,
        out_shape=jax.ShapeDtypeStruct((n, ATTN_WIDTH), F32),
        scratch_shapes=[pltpu.VMEM((n_prev + 1, ATTN_HEADS, ATTN_TQ, ATTN_TK), F32)],
        compiler_params=pltpu.CompilerParams(dimension_semantics=("arbitrary", "arbitrary"),
                                             vmem_limit_bytes=VMEM_LIMIT_BYTES),
        name="band_attn",
    )(q, k, k, k, v, v, v, diag)


def _attn_diag(rel_bias):
    width = ATTN_TQ + ATTN_TK
    n = jnp.arange(width)
    offset = jnp.where(n < ATTN_TK, n, n - width)
    rel = jnp.clip(ATTN_PREV - offset, -MAX_REL, MAX_REL) + MAX_REL
    return jnp.take(rel_bias.astype(F32), rel, axis=1)


def _router_gates(lt):
    tm = lt.shape[1]
    row = lax.broadcasted_iota(jnp.int32, (ROUTER_SLAB, tm), 0)
    gl = lt[0:ROUTER_SLAB]
    gmax = jnp.max(gl, axis=0, keepdims=True)
    p_g = 1.0 / jnp.sum(jnp.exp(gl - gmax), axis=0, keepdims=True)
    g_top = jnp.min(jnp.where(gl == gmax, row, ROUTER_SLAB), axis=0, keepdims=True)
    el = jnp.zeros((ROUTER_SLAB, tm), F32)
    for g in range(N_EXPERT_GROUPS):
        el = el + jnp.where(g_top == g, lt[(g + 1) * ROUTER_SLAB:(g + 2) * ROUTER_SLAB], 0.0)
    ee = jnp.exp(el - jnp.max(el, axis=0, keepdims=True))
    ep = ee / jnp.sum(ee, axis=0, keepdims=True)
    p1 = jnp.max(ep, axis=0, keepdims=True)
    i1 = jnp.min(jnp.where(ep == p1, row, ROUTER_SLAB), axis=0, keepdims=True)
    rest = jnp.where(row == i1, -1.0, ep)
    p2 = jnp.max(rest, axis=0, keepdims=True)
    i2 = jnp.min(jnp.where(rest == p2, row, ROUTER_SLAB), axis=0, keepdims=True)
    tot = p1 + p2
    w = (jnp.where(row == i1, p1 / tot, 0.0) + jnp.where(row == i2, p2 / tot, 0.0)) * p_g
    return [jnp.where(g_top == g, w, 0.0) for g in range(N_EXPERT_GROUPS)], g_top


def _sort_matrix(slot, n_slots, slot_axis):
    shape = (n_slots, slot.shape[1]) if slot_axis == 0 else (slot.shape[0], n_slots)
    return jnp.where(lax.broadcasted_iota(jnp.int32, shape, slot_axis) == slot, 1.0, 0.0).astype(BF16)


def _out_kernel(oa_ref, y0_ref, y1_ref, u0_ref, u1_ref, oc_ref, x_ref, d_ref, gluw_ref, glub_ref, og_ref, woutf_ref,
                fg_ref, wr_ref, br_ref, bd_ref, x1_ref, hs_ref, tok_ref, cnt_ref, wout_ref):
    tm = x_ref.shape[0]
    n_slots = hs_ref.shape[0]

    @pl.when(pl.program_id(0) == 0)
    def _():
        wout_ref[...] = woutf_ref[0].astype(BF16)

    y = jnp.concatenate([y0_ref[...], y1_ref[...]], axis=-1)
    u = jnp.concatenate([u0_ref[...], u1_ref[...]], axis=-1)
    y = jax.nn.gelu(y + d_ref[...] * u)
    ob = y * jax.nn.sigmoid(_dot(y.astype(BF16), gluw_ref[...]) + glub_ref[...])
    o = jnp.concatenate([oa_ref[...], ob, oc_ref[...]], axis=-1)
    on = o * lax.rsqrt(_group_sumsq(o, bd_ref) * (1.0 / OUT_NORM_GROUP) + EPS) * og_ref[...]
    x1 = x_ref[...] + _dot(on.astype(BF16), wout_ref[...])
    x1_ref[...] = x1
    ms = jnp.mean(x1 * x1, axis=-1, keepdims=True)
    hb = (x1 * lax.rsqrt(ms + EPS) * fg_ref[...]).astype(BF16)
    lt = _dot(hb, wr_ref[...]).T + br_ref[...]
    slabs, g_top = _router_gates(lt)

    row = lax.broadcasted_iota(jnp.int32, (ROUTER_SLAB, tm), 0)
    lane = lax.broadcasted_iota(jnp.int32, (ROUTER_SLAB, tm), 1)
    member = jnp.where(row == g_top, 1.0, 0.0)
    cum = member
    shift = 1
    while shift < tm:
        cum = cum + jnp.where(lane >= shift, pltpu.roll(cum, shift, 1), 0.0)
        shift *= 2
    count = cum[:, tm - 1:tm]
    padded = jnp.floor((count + (MOE_UNIT - 1)) * (1.0 / MOE_UNIT)) * MOE_UNIT
    group_row = lax.broadcasted_iota(jnp.int32, (ROUTER_SLAB, 1), 0)
    start = jnp.zeros((ROUTER_SLAB, 1), F32)
    for g in range(1, N_EXPERT_GROUPS):
        start = start + jnp.where(group_row >= g, padded[g - 1:g, :], 0.0)
    slot = jnp.sum(member * (start + cum - 1.0), axis=0, keepdims=True)
    cnt_ref[...] = jnp.broadcast_to(count, cnt_ref.shape)

    pad = jnp.zeros((LANES - SLOT_LANE - ROUTER_SLAB, tm), F32)
    gates = jnp.concatenate(slabs + [jnp.broadcast_to(slot, (ROUTER_SLAB, tm)), pad], axis=0).T
    tok_ref[...] = gates

    perm = _sort_matrix(slot.astype(jnp.int32), n_slots, 0)
    hs_ref[:, :D_MODEL] = _dot(perm, hb)
    g_hi = gates.astype(BF16)
    g_mid = (gates - g_hi.astype(F32)).astype(BF16)
    g_lo = (gates - g_hi.astype(F32) - g_mid.astype(F32)).astype(BF16)
    gs = _dot(perm, jnp.concatenate([g_hi, g_mid, g_lo], axis=-1))
    hs_ref[:, D_MODEL:] = gs[:, :LANES] + gs[:, LANES:2 * LANES] + gs[:, 2 * LANES:]


def _out_call(oa, y0, y1, u0, u1, oc, x, d, gluw, glub, og, wout, layer, fg, wr, br, bd, tm):
    n = x.shape[0]
    n_slots = tm + MOE_PAD_ROWS
    row = lambda c: pl.BlockSpec((tm, c), lambda i: (i, 0))
    full = lambda a: pl.BlockSpec(a.shape, lambda i: (0,) * a.ndim)
    return pl.pallas_call(
        _out_kernel,
        grid=(n // tm,),
        in_specs=[row(SGU_WIDTH), row(LANES), row(LANES), row(LANES), row(LANES), row(ATTN_WIDTH), row(D_MODEL),
                  full(d), full(gluw),
                  full(glub), full(og), pl.BlockSpec((1,) + wout.shape[1:], lambda i: (layer, 0, 0), pipeline_mode=pl.Buffered(1)),
                  full(fg), full(wr), full(br), full(bd)],
        out_specs=[row(D_MODEL), pl.BlockSpec((n_slots, MOE_ROW), lambda i: (i, 0)), row(LANES),
                   pl.BlockSpec((ROUTER_SLAB, LANES), lambda i: (i, 0))],
        out_shape=[jax.ShapeDtypeStruct((n, D_MODEL), F32), jax.ShapeDtypeStruct((n // tm * n_slots, MOE_ROW), F32),
                   jax.ShapeDtypeStruct((n, LANES), F32),
                   jax.ShapeDtypeStruct((n // tm * ROUTER_SLAB, LANES), F32)],
        scratch_shapes=[pltpu.VMEM(wout.shape[1:], BF16)],
        compiler_params=pltpu.CompilerParams(dimension_semantics=("arbitrary",), vmem_limit_bytes=VMEM_LIMIT_BYTES),
        name="out_proj",
    )(oa, y0, y1, u0, u1, oc, x, d, gluw, glub, og, wout, fg, wr, br, bd)


def _router_tables(wg, bg, we, be):
    w = jnp.zeros((D_MODEL, LANES), F32)
    b = jnp.full((LANES,), NEG_INF, F32)
    w = w.at[:, 0:N_EXPERT_GROUPS].set(wg.astype(F32))
    b = b.at[0:N_EXPERT_GROUPS].set(bg.astype(F32))
    for g in range(N_EXPERT_GROUPS):
        c0 = (g + 1) * ROUTER_SLAB
        w = w.at[:, c0:c0 + EXPERTS_PER_GROUP].set(we[g].astype(F32))
        b = b.at[c0:c0 + EXPERTS_PER_GROUP].set(be[g].astype(F32))
    return w.astype(BF16), b[:, None]


def _route_plan(counts, n, tm):
    n_tiles = n // tm
    units_per_tile = (tm + MOE_PAD_ROWS) // MOE_UNIT
    units_per_mtile = tm // MOE_UNIT
    cnt = counts.reshape(n_tiles, ROUTER_SLAB, LANES)[:, :N_EXPERT_GROUPS, 0].astype(jnp.int32)
    seg = (cnt + MOE_UNIT - 1) // MOE_UNIT
    seg_start = jnp.cumsum(seg, axis=1) - seg
    before = jnp.cumsum(seg, axis=0) - seg
    total = jnp.sum(seg, axis=0)
    mtiles = (total + units_per_mtile - 1) // units_per_mtile
    mtile_end = jnp.cumsum(mtiles)
    group_start = (mtile_end - mtiles) * units_per_mtile
    n_mtiles = n_tiles + N_EXPERT_GROUPS + -(-n_tiles * MOE_PAD_ROWS // tm)
    tile_group = jnp.minimum(jnp.sum(jnp.arange(n_mtiles)[:, None] >= mtile_end[None, :], axis=1),
                             N_EXPERT_GROUPS - 1)
    meta = jnp.concatenate([tile_group, mtile_end[-1:]]).astype(jnp.int32)
    groups = jnp.arange(N_EXPERT_GROUPS)
    pick = lambda table, g: jnp.sum(jnp.where(g[..., None] == groups, table, 0), axis=-1)

    q = jnp.arange(n_mtiles * units_per_mtile)
    g = jnp.repeat(tile_group, units_per_mtile)
    ql = q - pick(group_start, g)
    ends_g = pick((before + seg)[None], g[:, None])
    tile_of = jnp.minimum(jnp.sum(ql[:, None] >= ends_g, axis=1), n_tiles - 1)
    offset_g = pick((seg_start - before)[None], g[:, None])
    local = ql + jnp.sum(jnp.where(tile_of[:, None] == jnp.arange(n_tiles), offset_g, 0), axis=1)
    valid = (ql < pick(total, g)) & (q // units_per_mtile < mtile_end[-1])
    src = jnp.where(valid, tile_of * units_per_tile + local, units_per_tile - 1).astype(jnp.int32)

    ul = jnp.arange(units_per_tile)[None, :, None]
    seg_of = jnp.sum(ul >= (seg_start + seg)[:, None, :], axis=-1)
    gi = jnp.minimum(seg_of, N_EXPERT_GROUPS - 1)
    glob = pick((group_start + before - seg_start)[:, None, :], gi) + ul[..., 0]
    dst = jnp.where(seg_of < N_EXPERT_GROUPS, glob, 0).astype(jnp.int32).reshape(-1)
    return meta, src, dst, n_mtiles


def _unit_gather(table_ref, first, src_ref, dst_ref, sem, wait):
    n_rows = dst_ref.shape[0]
    if wait:
        pltpu.make_async_copy(src_ref.at[pl.ds(0, n_rows)], dst_ref, sem).wait()
        return

    def body(k, c):
        u = pl.multiple_of(table_ref[first + k] * MOE_UNIT, MOE_UNIT)
        pltpu.make_async_copy(src_ref.at[pl.ds(u, MOE_UNIT)],
                              dst_ref.at[pl.ds(pl.multiple_of(k * MOE_UNIT, MOE_UNIT), MOE_UNIT)], sem).start()
        return c
    lax.fori_loop(0, n_rows // MOE_UNIT, body, 0, unroll=4)


def _moe_group_kernel(meta_ref, src_ref, hs_ref, wg_ref, wu_ref, wd_ref, o_ref, wgb_ref, wub_ref, wdb_ref,
                      buf_ref, sem):
    j = pl.program_id(0)
    n_steps = pl.num_programs(0)
    n_used = meta_ref[n_steps]
    group = meta_ref[j]
    n_units = o_ref.shape[0] // MOE_UNIT
    slot = j % 2

    @pl.when(j == 0)
    def _():
        _unit_gather(src_ref, 0, hs_ref, buf_ref.at[0], sem.at[0], wait=False)

    @pl.when(j + 1 < n_steps)
    def _():
        _unit_gather(src_ref, (j + 1) * n_units, hs_ref, buf_ref.at[1 - slot], sem.at[1 - slot], wait=False)

    @pl.when((j == 0) | (group != meta_ref[jnp.maximum(j - 1, 0)]))
    def _():
        wgb_ref[...] = wg_ref[...].astype(BF16)
        wub_ref[...] = wu_ref[...].astype(BF16)
        wdb_ref[...] = wd_ref[...].astype(BF16)

    _unit_gather(src_ref, j * n_units, hs_ref, buf_ref.at[slot], sem.at[slot], wait=True)

    @pl.when(j < n_used)
    def _():
        h = buf_ref[slot, :, :D_MODEL].astype(BF16)
        gates = buf_ref[slot, :, D_MODEL:]
        lane = lax.broadcasted_iota(jnp.int32, (1, LANES), 1)
        out = None
        for e in range(EXPERTS_PER_GROUP):
            gt = _dot(h, wgb_ref[e])
            up = _dot(h, wub_ref[e])
            gate = jnp.sum(jnp.where(lane == group * ROUTER_SLAB + e, gates, 0.0), axis=-1, keepdims=True)
            a = (gt * jax.nn.sigmoid(gt)) * up * gate
            part = _dot(a.astype(BF16), wdb_ref[e])
            out = part if out is None else out + part
        o_ref[...] = out

    @pl.when(j >= n_used)
    def _():
        o_ref[...] = jnp.zeros_like(o_ref)


def _moe_group_call(meta, src, hs, wg, wu, wd, layer, n_mtiles, tm):
    wg, wu, wd = (a.reshape((-1,) + a.shape[2:]) for a in (wg, wu, wd))
    by_group = lambda a: pl.BlockSpec((EXPERTS_PER_GROUP,) + a.shape[1:],
                                      lambda j, meta, src: (layer * N_EXPERT_GROUPS + meta[j], 0, 0))
    return pl.pallas_call(
        _moe_group_kernel,
        grid_spec=pltpu.PrefetchScalarGridSpec(
            num_scalar_prefetch=2,
            grid=(n_mtiles,),
            in_specs=[pl.BlockSpec(memory_space=pl.ANY), by_group(wg), by_group(wu), by_group(wd)],
            out_specs=pl.BlockSpec((tm, D_MODEL), lambda j, meta, src: (j, 0)),
            scratch_shapes=[pltpu.VMEM((EXPERTS_PER_GROUP, D_MODEL, D_EXPERT), BF16),
                            pltpu.VMEM((EXPERTS_PER_GROUP, D_MODEL, D_EXPERT), BF16),
                            pltpu.VMEM((EXPERTS_PER_GROUP, D_EXPERT, D_MODEL), BF16),
                            pltpu.VMEM((2, tm, MOE_ROW), F32), pltpu.SemaphoreType.DMA((2,))],
        ),
        out_shape=jax.ShapeDtypeStruct((n_mtiles * tm, D_MODEL), F32),
        compiler_params=pltpu.CompilerParams(dimension_semantics=("arbitrary",), vmem_limit_bytes=VMEM_LIMIT_BYTES),
        name="moe_experts",
    )(meta, src, hs, wg, wu, wd)


def _combine_kernel(dst_ref, x1_ref, tok_ref, ys_ref, o_ref, buf_ref, sem):
    i = pl.program_id(0)
    n_slots = buf_ref.shape[1]
    n_units = n_slots // MOE_UNIT
    slot = i % 2

    @pl.when(i == 0)
    def _():
        _unit_gather(dst_ref, 0, ys_ref, buf_ref.at[0], sem.at[0], wait=False)

    @pl.when(i + 1 < pl.num_programs(0))
    def _():
        _unit_gather(dst_ref, (i + 1) * n_units, ys_ref, buf_ref.at[1 - slot], sem.at[1 - slot], wait=False)

    _unit_gather(dst_ref, i * n_units, ys_ref, buf_ref.at[slot], sem.at[slot], wait=True)

    y = buf_ref[slot]
    y_hi = y.astype(BF16)
    y_lo = (y - y_hi.astype(F32)).astype(BF16)
    unsort = _sort_matrix(tok_ref[:, SLOT_LANE:SLOT_LANE + 1].astype(jnp.int32), n_slots, 1)
    o_ref[...] = x1_ref[...] + _dot(unsort, y_hi) + _dot(unsort, y_lo)


def _combine_call(dst, x1, tok, ys, tm):
    n = x1.shape[0]
    n_slots = tm + MOE_PAD_ROWS
    return pl.pallas_call(
        _combine_kernel,
        grid_spec=pltpu.PrefetchScalarGridSpec(
            num_scalar_prefetch=1,
            grid=(n // tm,),
            in_specs=[pl.BlockSpec((tm, D_MODEL), lambda i, dst: (i, 0)),
                      pl.BlockSpec((tm, LANES), lambda i, dst: (i, 0)),
                      pl.BlockSpec(memory_space=pl.ANY)],
            out_specs=pl.BlockSpec((tm, D_MODEL), lambda i, dst: (i, 0)),
            scratch_shapes=[pltpu.VMEM((2, n_slots, D_MODEL), F32), pltpu.SemaphoreType.DMA((2,))],
        ),
        out_shape=jax.ShapeDtypeStruct((n, D_MODEL), F32),
        compiler_params=pltpu.CompilerParams(dimension_semantics=("arbitrary",), vmem_limit_bytes=VMEM_LIMIT_BYTES),
        name="moe_combine",
    )(dst, x1, tok, ys)


def kernel(x, norm_mix, w_in, sgu_norm, sgu_w, sgu_b, s5_lambda_re, s5_lambda_im, s5_log_dt, s5_b_re, s5_b_im,
           s5_c_re, s5_c_im, s5_d, s5_glu_w, s5_glu_b, q_norm, k_norm, rel_bias, out_norm, w_out, norm_ffn,
           router_group_w, router_group_b, router_expert_w, router_expert_b, w_gate, w_up, w_down):
    batch, seq, _ = x.shape
    n = batch * seq
    depth = w_in.shape[0]
    t = _tiles(n, seq)
    assert seq % ATTN_TQ == 0 and seq % (S5_T * t["s5_rows"]) == 0
    row_vec = lambda a: a.astype(F32)[None, :]

    lane_group = jnp.arange(LANES) // HEAD_DIM
    bd = (lane_group[:, None] == lane_group[None, :]).astype(BF16)
    block_chunk = jnp.arange(SGU_BLOCK) // CHUNK
    sgu_mask = block_chunk[None, :] <= block_chunk[:, None]

    s5_tabs = jax.vmap(lambda *p: _s5_tables(*p, t["s5_rows"]))(
        s5_lambda_re, s5_lambda_im, s5_log_dt, s5_b_re, s5_b_im, s5_c_re, s5_c_im)

    xf = x.reshape(n, D_MODEL)
    for l in range(depth):
        sguw = jnp.where(sgu_mask[None], sgu_w[l], 0).astype(BF16)
        sgub = jnp.repeat(sgu_b[l].astype(F32).T, HEAD_DIM, axis=1)
        qg = row_vec(jnp.tile(q_norm[l], ATTN_HEADS)) * (HEAD_DIM ** -0.5)
        kg = row_vec(jnp.tile(k_norm[l], ATTN_HEADS))
        oa, u0, u1, q, k, v = _in_call(xf, row_vec(norm_mix[l]), w_in, l, row_vec(sgu_norm[l]), sguw, sgub,
                                      qg, kg, bd, t["tm_in"])

        y0, y1 = _s5_call(u0, u1, *s5_tabs, l, batch, t["s5_rows"])

        oc = _attn_call(q, k, v, _attn_diag(rel_bias[l]), batch, seq)

        wr, br = _router_tables(router_group_w[l], router_group_b[l], router_expert_w[l], router_expert_b[l])
        x1, hs, tok, counts = _out_call(oa, y0, y1, u0, u1, oc, xf, row_vec(s5_d[l]), s5_glu_w[l].astype(BF16), row_vec(s5_glu_b[l]),
                                 row_vec(out_norm[l]), w_out, l, row_vec(norm_ffn[l]), wr, br, bd,
                                 t["tm_out"])

        meta, src, dst, n_mtiles = _route_plan(counts, n, t["tm_out"])
        ys = _moe_group_call(meta, src, hs, w_gate, w_up, w_down, l, n_mtiles, t["tm_out"])
        xf = _combine_call(dst, x1, tok, ys, t["tm_out"])
    return xf.reshape(batch, seq, D_MODEL)
```

```python
import jax
import jax.numpy as jnp
from jax import lax
from jax.experimental import pallas as pl
from jax.experimental.pallas import tpu as pltpu

F32 = jnp.float32
BF16 = jnp.bfloat16

D_MODEL = 1024
CHUNK = 64
HEAD_DIM = 64
SGU_WIDTH = 256
SGU_HEADS = 4
SGU_BLOCK = 128
S5_WIDTH = 256
S5_GROUP = 16
S5_N_GROUPS = 16
S5_STATE = 64
ATTN_WIDTH = 512
ATTN_HEADS = 8
BAND_CHUNKS = 9
MAX_REL = 256
IN_COLS = 2 * SGU_WIDTH + S5_WIDTH + 3 * ATTN_WIDTH
OUT_NORM_GROUP = 64
N_EXPERT_GROUPS = 4
EXPERTS_PER_GROUP = 4
N_EXPERTS = 16
D_EXPERT = 256
EPS = 1e-6
NEG_INF = -1e30

LANES = 128
SUBLANES = 8
VMEM_LIMIT_BYTES = 56 * 1024 * 1024

S5_T = 16
S5_ROW = S5_T * S5_WIDTH
S5_NSTATE = S5_N_GROUPS * S5_STATE
S5_QUAD = 4
S5_SETS = 4
S5_QUADS = S5_T // S5_QUAD
S5_BLOCK = S5_QUAD * (S5_N_GROUPS // S5_SETS) * S5_GROUP
ATTN_TQ = 256
ATTN_PREV = (BAND_CHUNKS - 1) * CHUNK
ATTN_TK = ATTN_TQ + ATTN_PREV
ROUTER_SLAB = SUBLANES
MOE_ROW = D_MODEL + LANES
MOE_UNIT = SUBLANES
MOE_PAD_ROWS = N_EXPERT_GROUPS * MOE_UNIT
SLOT_LANE = N_EXPERT_GROUPS * ROUTER_SLAB


def _tiles(n_tokens, seq):
    def pick(pref, total):
        t = min(pref, total)
        assert total % t == 0
        return t
    return dict(
        tm_in=pick(512, seq),
        tm_out=pick(512, seq),
        s5_rows=pick(256, seq // S5_T),
    )


def _dot(a, b):
    return jnp.dot(a, b, preferred_element_type=F32)


def _dot_nt(a, b):
    return lax.dot_general(a, b, (((1,), (1,)), ((), ())), preferred_element_type=F32)


def _group_sumsq(x, bd_ref):
    x2 = (x * x).astype(BF16)
    parts = [_dot(x2[:, t * LANES:(t + 1) * LANES], bd_ref[...]) for t in range(x.shape[1] // LANES)]
    return jnp.concatenate(parts, axis=-1)


def _in_kernel(x_ref, g_ref, wf_ref, sgug_ref, sguw_ref, sgub_ref, qg_ref, kg_ref, bd_ref,
               oa_ref, u0_ref, u1_ref, q_ref, k_ref, v_ref, w_ref):
    tm = x_ref.shape[0]

    @pl.when(pl.program_id(0) == 0)
    def _():
        w_ref[...] = wf_ref[0].astype(BF16)

    x = x_ref[...]
    ms = jnp.mean(x * x, axis=-1, keepdims=True)
    hn = (x * lax.rsqrt(ms + EPS) * g_ref[...]).astype(BF16)

    z = jax.nn.gelu(_dot(hn, w_ref[:, 0:2 * SGU_WIDTH]))
    u = z[:, :SGU_WIDTH]
    v = z[:, SGU_WIDTH:]
    v = v * lax.rsqrt(jnp.mean(v * v, axis=-1, keepdims=True) + EPS) * sgug_ref[...]
    vb = v.astype(BF16)
    first_head = lax.broadcasted_iota(jnp.int32, (SGU_BLOCK, LANES), 1) < HEAD_DIM
    for r in range(tm // SGU_BLOCK):
        rows = slice(r * SGU_BLOCK, (r + 1) * SGU_BLOCK)
        for p in range(SGU_WIDTH // LANES):
            cols = slice(p * LANES, (p + 1) * LANES)
            vp = vb[rows, cols]
            mixed = jnp.where(first_head, _dot(sguw_ref[2 * p], vp), _dot(sguw_ref[2 * p + 1], vp))
            oa_ref[rows, cols] = u[rows, cols] * (mixed + sgub_ref[:, cols])

    c0 = 2 * SGU_WIDTH
    us = _dot(hn, w_ref[:, c0:c0 + S5_WIDTH])
    u0_ref[...] = us[:, :LANES]
    u1_ref[...] = us[:, LANES:]

    c0 += S5_WIDTH
    q = _dot(hn, w_ref[:, c0:c0 + ATTN_WIDTH])
    q_ref[...] = (q * lax.rsqrt(_group_sumsq(q, bd_ref) * (1.0 / HEAD_DIM) + EPS) * qg_ref[...]).astype(BF16)
    c0 += ATTN_WIDTH
    k = _dot(hn, w_ref[:, c0:c0 + ATTN_WIDTH])
    k_ref[...] = (k * lax.rsqrt(_group_sumsq(k, bd_ref) * (1.0 / HEAD_DIM) + EPS) * kg_ref[...]).astype(BF16)
    c0 += ATTN_WIDTH
    v_ref[...] = _dot(hn, w_ref[:, c0:c0 + ATTN_WIDTH]).astype(BF16)


def _in_call(x, g, w, layer, sgug, sguw, sgub, qg, kg, bd, tm):
    n = x.shape[0]
    row = lambda c: pl.BlockSpec((tm, c), lambda i: (i, 0))
    full = lambda a: pl.BlockSpec(a.shape, lambda i: (0,) * a.ndim)
    w_layer = pl.BlockSpec((1,) + w.shape[1:], lambda i: (layer, 0, 0), pipeline_mode=pl.Buffered(1))
    return pl.pallas_call(
        _in_kernel,
        grid=(n // tm,),
        in_specs=[row(D_MODEL), full(g), w_layer, full(sgug), full(sguw), full(sgub), full(qg), full(kg), full(bd)],
        out_specs=[row(SGU_WIDTH), row(LANES), row(LANES), row(ATTN_WIDTH), row(ATTN_WIDTH), row(ATTN_WIDTH)],
        out_shape=[jax.ShapeDtypeStruct((n, SGU_WIDTH), F32), jax.ShapeDtypeStruct((n, LANES), F32),
                   jax.ShapeDtypeStruct((n, LANES), F32), jax.ShapeDtypeStruct((n, ATTN_WIDTH), BF16),
                   jax.ShapeDtypeStruct((n, ATTN_WIDTH), BF16), jax.ShapeDtypeStruct((n, ATTN_WIDTH), BF16)],
        scratch_shapes=[pltpu.VMEM(w.shape[1:], BF16)],
        compiler_params=pltpu.CompilerParams(dimension_semantics=("arbitrary",), vmem_limit_bytes=VMEM_LIMIT_BYTES),
        name="in_proj",
    )(x, g, w, sgug, sguw, sgub, qg, kg, bd)


def _s5_kernel(u0_ref, u1_ref, wv_ref, kb_ref, wc_ref, apow_ref, y0_ref, y1_ref, ur_ref, s_ref, sp_ref, yb_ref, carry_ref):
    rows = ur_ref.shape[0]
    n_steps = apow_ref.shape[0]
    half = LANES // 2
    low = lax.broadcasted_iota(jnp.int32, (1, LANES), 1) < half
    block = lambda tq, gs: slice((tq * S5_SETS + gs) * S5_BLOCK, (tq * S5_SETS + gs + 1) * S5_BLOCK)

    @pl.when(pl.program_id(1) == 0)
    def _():
        carry_ref[...] = jnp.zeros_like(carry_ref)

    for tq in range(S5_QUADS):
        for lt in range(S5_QUAD // 2):
            t0 = tq * S5_QUAD + 2 * lt
            for pair, u_ref in enumerate((u0_ref, u1_ref)):
                a = u_ref[pl.ds(t0, rows, stride=S5_T), :]
                b = u_ref[pl.ds(t0 + 1, rows, stride=S5_T), :]
                for par in range(2):
                    tile = jnp.where(low, a, pltpu.roll(b, half, 1)) if par == 0 else jnp.where(low, pltpu.roll(a, half, 1), b)
                    col = block(tq, 2 * pair + par).start + lt * LANES
                    ur_ref[:, col:col + LANES] = tile.astype(BF16)

    for gs in range(S5_SETS):
        v = None
        for tq in range(S5_QUADS):
            part = _dot(ur_ref[:, block(tq, gs)], wv_ref[tq * S5_SETS + gs])
            v = part if v is None else v + part
        s_ref[:, gs * S5_BLOCK:(gs + 1) * S5_BLOCK] = v[:, :S5_BLOCK]
        s_ref[:, S5_NSTATE + gs * S5_BLOCK:S5_NSTATE + (gs + 1) * S5_BLOCK] = v[:, S5_BLOCK:]

    row_id = lax.broadcasted_iota(jnp.int32, (rows, LANES), 0)
    for cb in range(S5_NSTATE // LANES):
        cre = slice(cb * LANES, (cb + 1) * LANES)
        cim = slice(S5_NSTATE + cb * LANES, S5_NSTATE + (cb + 1) * LANES)
        re = s_ref[:, cre]
        im = s_ref[:, cim]
        c_re = carry_ref[0:1, cre]
        c_im = carry_ref[0:1, cim]
        a_re = apow_ref[0, 0:1, cre]
        a_im = apow_ref[0, 1:2, cre]
        re = re + jnp.where(row_id == 0, a_re * c_re - a_im * c_im, 0.0)
        im = im + jnp.where(row_id == 0, a_re * c_im + a_im * c_re, 0.0)
        for k in range(n_steps):
            shift = 1 << k
            a_re = apow_ref[k, 0:1, cre]
            a_im = apow_ref[k, 1:2, cre]
            re_s = jnp.where(row_id >= shift, pltpu.roll(re, shift, 0), 0.0)
            im_s = jnp.where(row_id >= shift, pltpu.roll(im, shift, 0), 0.0)
            re, im = re + a_re * re_s - a_im * im_s, im + a_re * im_s + a_im * re_s
        sp_ref[:, cre] = jnp.where(row_id >= 1, pltpu.roll(re, 1, 0), c_re).astype(BF16)
        sp_ref[:, cim] = jnp.where(row_id >= 1, pltpu.roll(im, 1, 0), c_im).astype(BF16)
        carry_ref[0:1, cre] = re[rows - 1:rows, :]
        carry_ref[0:1, cim] = im[rows - 1:rows, :]

    for gs in range(S5_SETS):
        state = jnp.concatenate([sp_ref[:, gs * S5_BLOCK:(gs + 1) * S5_BLOCK],
                                 sp_ref[:, S5_NSTATE + gs * S5_BLOCK:S5_NSTATE + (gs + 1) * S5_BLOCK]], axis=1)
        for tq in range(S5_QUADS):
            y = _dot(state, wc_ref[tq * S5_SETS + gs])
            for tj in range(tq + 1):
                y = y + _dot(ur_ref[:, block(tj, gs)], kb_ref[(tq - tj) * S5_SETS + gs])
            yb_ref[:, block(tq, gs)] = y

    for tq in range(S5_QUADS):
        for lt in range(S5_QUAD // 2):
            t0 = tq * S5_QUAD + 2 * lt
            for pair, y_ref in enumerate((y0_ref, y1_ref)):
                a = yb_ref[:, block(tq, 2 * pair).start + lt * LANES:block(tq, 2 * pair).start + (lt + 1) * LANES]
                b = yb_ref[:, block(tq, 2 * pair + 1).start + lt * LANES:block(tq, 2 * pair + 1).start + (lt + 1) * LANES]
                y_ref[pl.ds(t0, rows, stride=S5_T), :] = jnp.where(low, a, pltpu.roll(b, half, 1))
                y_ref[pl.ds(t0 + 1, rows, stride=S5_T), :] = jnp.where(low, pltpu.roll(a, half, 1), b)


def _s5_call(u0, u1, wv, kb, wc, apow, batch, rows):
    n = u0.shape[0]
    tok = rows * S5_T
    tiles_per_seq = n // batch // tok
    const = lambda a: pl.BlockSpec(a.shape, lambda b, t: (0,) * a.ndim, pipeline_mode=pl.Buffered(1))
    blk = pl.BlockSpec((tok, LANES), lambda b, t: (b * tiles_per_seq + t, 0))
    half = jax.ShapeDtypeStruct((n, LANES), F32)
    return pl.pallas_call(
        _s5_kernel,
        grid=(batch, tiles_per_seq),
        in_specs=[blk, blk, const(wv), const(kb), const(wc), const(apow)],
        out_specs=[blk, blk],
        out_shape=[half, half],
        scratch_shapes=[pltpu.VMEM((rows, S5_ROW), BF16), pltpu.VMEM((rows, 2 * S5_NSTATE), F32),
                        pltpu.VMEM((rows, 2 * S5_NSTATE), BF16), pltpu.VMEM((rows, S5_ROW), F32),
                        pltpu.VMEM((SUBLANES, 2 * S5_NSTATE), F32)],
        compiler_params=pltpu.CompilerParams(dimension_semantics=("arbitrary", "arbitrary"),
                                             vmem_limit_bytes=VMEM_LIMIT_BYTES),
        name="s5_scan",
    )(u0, u1, wv, kb, wc, apow)


def _s5_tables(lam_re, lam_im, log_dt, b_re, b_im, c_re, c_im, rows):
    p_, h_ = S5_STATE, S5_GROUP
    q_, s_, m_ = S5_QUADS, S5_SETS, S5_N_GROUPS // S5_SETS
    hi = lax.Precision.HIGHEST
    lam_re, lam_im = lam_re.astype(F32), lam_im.astype(F32)
    dt = jnp.exp(log_dt.astype(F32))[:, None]
    tau = jnp.arange(S5_T + 1, dtype=F32)[:, None, None]
    mag = jnp.exp(tau * (lam_re * dt)[None])
    ap_re = mag * jnp.cos(tau * (lam_im * dt)[None])
    ap_im = mag * jnp.sin(tau * (lam_im * dt)[None])
    n_re, n_im = ap_re[1] - 1.0, ap_im[1]
    den = lam_re * lam_re + lam_im * lam_im
    k_re = ((n_re * lam_re + n_im * lam_im) / den)[..., None]
    k_im = ((n_im * lam_re - n_re * lam_im) / den)[..., None]
    b_re, b_im = b_re.astype(F32), b_im.astype(F32)
    bb_re = k_re * b_re - k_im * b_im
    bb_im = k_re * b_im + k_im * b_re
    c_re, c_im = c_re.astype(F32), c_im.astype(F32)
    ab_re = ap_re[:S5_T, :, :, None] * bb_re[None] - ap_im[:S5_T, :, :, None] * bb_im[None]
    ab_im = ap_re[:S5_T, :, :, None] * bb_im[None] + ap_im[:S5_T, :, :, None] * bb_re[None]

    def spread(base, col_of_base):
        cols = jnp.arange(S5_BLOCK)
        expand = (col_of_base(cols)[None, :] == jnp.arange(base.shape[1])[:, None]).astype(F32)
        return jnp.dot(base, expand, precision=hi)
    state_col = lambda cols: cols % p_
    chan_col = lambda cols: cols // (m_ * h_) * h_ + cols % h_
    row_group = jnp.arange(S5_BLOCK) // h_ % m_
    keep_state = (row_group[:, None] == (jnp.arange(S5_BLOCK) // p_)[None, :])[None]
    keep_chan = (row_group[:, None] == (jnp.arange(S5_BLOCK) // h_ % m_)[None, :])[None]
    keep_out = ((jnp.arange(S5_BLOCK) // p_)[:, None] == (jnp.arange(S5_BLOCK) // h_ % m_)[None, :])[None]

    def wv_part(ab):
        base = ab[::-1].reshape(q_, S5_QUAD, s_, m_, p_, h_).transpose(0, 2, 1, 3, 5, 4).reshape(-1, p_)
        return jnp.where(keep_state, spread(base, state_col).reshape(q_ * s_, S5_BLOCK, S5_BLOCK), 0.0)
    wv = jnp.concatenate([wv_part(ab_re), wv_part(ab_im)], axis=-1)

    k_tau = (jnp.einsum('gop,tgpi->tgio', c_re, ab_re, precision=hi)
             - jnp.einsum('gop,tgpi->tgio', c_im, ab_im, precision=hi))
    zero = jnp.zeros_like(k_tau[0])
    taps = jnp.stack([jnp.stack([jnp.stack([k_tau[S5_QUAD * dq + ti - tj] if S5_QUAD * dq + ti - tj >= 0 else zero
                                            for ti in range(S5_QUAD)]) for tj in range(S5_QUAD)])
                      for dq in range(q_)])
    base = taps.reshape(q_, S5_QUAD, S5_QUAD, s_, m_, h_, h_).transpose(0, 3, 1, 4, 5, 2, 6).reshape(-1, S5_QUAD * h_)
    kb = jnp.where(keep_chan, spread(base, chan_col).reshape(q_ * s_, S5_BLOCK, S5_BLOCK), 0.0)

    cp = lambda c: c.transpose(0, 2, 1)[:, :, None, :]
    w_re = cp(c_re) * ap_re[1:].transpose(1, 2, 0)[..., None] - cp(c_im) * ap_im[1:].transpose(1, 2, 0)[..., None]
    w_im = cp(c_re) * ap_im[1:].transpose(1, 2, 0)[..., None] + cp(c_im) * ap_re[1:].transpose(1, 2, 0)[..., None]
    def wc_part(w):
        base = w.reshape(s_, m_, p_, q_, S5_QUAD, h_).transpose(3, 0, 1, 2, 4, 5).reshape(-1, S5_QUAD * h_)
        return jnp.where(keep_out, spread(base, chan_col).reshape(q_ * s_, S5_BLOCK, S5_BLOCK), 0.0)
    wc = jnp.concatenate([wc_part(w_re), wc_part(-w_im)], axis=1)

    n_steps = max(1, (rows - 1).bit_length())
    pows = [(ap_re[S5_T].reshape(S5_NSTATE), ap_im[S5_T].reshape(S5_NSTATE))]
    for _ in range(n_steps - 1):
        r, i = pows[-1]
        pows.append((r * r - i * i, 2.0 * r * i))
    apow = jnp.stack([jnp.stack(p) for p in pows])
    return wv.astype(BF16), kb.astype(BF16), wc.astype(BF16), apow


def _attn_fill_bias(diag_ref, bias_ref):
    width = diag_ref.shape[1]
    qi = lax.broadcasted_iota(jnp.int32, (ATTN_TQ, ATTN_TK), 0)
    kj = lax.broadcasted_iota(jnp.int32, (ATTN_TQ, ATTN_TK), 1)
    q_chunk = qi // CHUNK + (BAND_CHUNKS - 1)
    k_chunk = kj // CHUNK
    in_band = (k_chunk <= q_chunk) & (k_chunk >= q_chunk - (BAND_CHUNKS - 1))
    for h in range(ATTN_HEADS):
        rows = jnp.broadcast_to(diag_ref[h:h + 1, :], (ATTN_TQ, width))
        base = pltpu.roll(rows, 0, 1, stride=1, stride_axis=0)[:, :ATTN_TK]
        for t in range(bias_ref.shape[0]):
            bias_ref[t, h] = jnp.where(in_band & (kj >= ATTN_PREV - t * ATTN_TQ), base, NEG_INF)


def _attn_kernel(q_ref, k0_ref, k1_ref, k2_ref, v0_ref, v1_ref, v2_ref, diag_ref, o_ref, bias_ref):
    @pl.when((pl.program_id(0) == 0) & (pl.program_id(1) == 0))
    def _():
        _attn_fill_bias(diag_ref, bias_ref)

    table = jnp.minimum(pl.program_id(1), bias_ref.shape[0] - 1)
    lane = lax.broadcasted_iota(jnp.int32, (1, LANES), 1)
    for p in range(ATTN_WIDTH // LANES):
        cols = slice(p * LANES, (p + 1) * LANES)
        qp = q_ref[:, cols]
        kcat = jnp.concatenate([k0_ref[:, cols], k1_ref[:, cols], k2_ref[:, cols]], axis=0)
        vcat = jnp.concatenate([v0_ref[:, cols], v1_ref[:, cols], v2_ref[:, cols]], axis=0)
        first = lane < HEAD_DIM
        zero = jnp.zeros_like(qp)
        qm = jnp.concatenate([jnp.where(first, qp, zero), jnp.where(first, zero, qp)], axis=0)
        bias = jnp.concatenate([bias_ref[table, 2 * p], bias_ref[table, 2 * p + 1]], axis=0)
        s = _dot_nt(qm, kcat) + bias
        e = jnp.exp(s - jnp.max(s, axis=-1, keepdims=True)).astype(BF16)
        o = _dot(e, jnp.concatenate([vcat, jnp.ones_like(vcat)], axis=1))
        r = o[:, :LANES] / o[:, LANES:]
        o_ref[:, cols] = jnp.where(first, r[:ATTN_TQ], r[ATTN_TQ:])


def _attn_call(q, k, v, diag, batch, seq):
    n = q.shape[0]
    tiles = seq // ATTN_TQ
    n_prev = ATTN_PREV // ATTN_TQ
    cur = pl.BlockSpec((ATTN_TQ, ATTN_WIDTH), lambda b, t: (b * tiles + t, 0))
    prev = lambda d: pl.BlockSpec((ATTN_TQ, ATTN_WIDTH), lambda b, t: (b * tiles + jnp.maximum(t - d, 0), 0))
    assert n_prev == 2
    return pl.pallas_call(
        _attn_kernel,
        grid=(batch, tiles),
        in_specs=[cur, prev(2), prev(1), cur, prev(2), prev(1), cur, pl.BlockSpec(diag.shape, lambda b, t: (0, 0))],
        out_specs=cur,
        out_shape=jax.ShapeDtypeStruct((n, ATTN_WIDTH), F32),
        scratch_shapes=[pltpu.VMEM((n_prev + 1, ATTN_HEADS, ATTN_TQ, ATTN_TK), F32)],
        compiler_params=pltpu.CompilerParams(dimension_semantics=("arbitrary", "arbitrary"),
                                             vmem_limit_bytes=VMEM_LIMIT_BYTES),
        name="band_attn",
    )(q, k, k, k, v, v, v, diag)


def _attn_diag(rel_bias):
    width = ATTN_TQ + ATTN_TK
    n = jnp.arange(width)
    offset = jnp.where(n < ATTN_TK, n, n - width)
    rel = jnp.clip(ATTN_PREV - offset, -MAX_REL, MAX_REL) + MAX_REL
    return jnp.take(rel_bias.astype(F32), rel, axis=1)


def _router_gates(lt):
    tm = lt.shape[1]
    row = lax.broadcasted_iota(jnp.int32, (ROUTER_SLAB, tm), 0)
    gl = lt[0:ROUTER_SLAB]
    gmax = jnp.max(gl, axis=0, keepdims=True)
    p_g = 1.0 / jnp.sum(jnp.exp(gl - gmax), axis=0, keepdims=True)
    g_top = jnp.min(jnp.where(gl == gmax, row, ROUTER_SLAB), axis=0, keepdims=True)
    el = jnp.zeros((ROUTER_SLAB, tm), F32)
    for g in range(N_EXPERT_GROUPS):
        el = el + jnp.where(g_top == g, lt[(g + 1) * ROUTER_SLAB:(g + 2) * ROUTER_SLAB], 0.0)
    ee = jnp.exp(el - jnp.max(el, axis=0, keepdims=True))
    ep = ee / jnp.sum(ee, axis=0, keepdims=True)
    p1 = jnp.max(ep, axis=0, keepdims=True)
    i1 = jnp.min(jnp.where(ep == p1, row, ROUTER_SLAB), axis=0, keepdims=True)
    rest = jnp.where(row == i1, -1.0, ep)
    p2 = jnp.max(rest, axis=0, keepdims=True)
    i2 = jnp.min(jnp.where(rest == p2, row, ROUTER_SLAB), axis=0, keepdims=True)
    tot = p1 + p2
    w = (jnp.where(row == i1, p1 / tot, 0.0) + jnp.where(row == i2, p2 / tot, 0.0)) * p_g
    return [jnp.where(g_top == g, w, 0.0) for g in range(N_EXPERT_GROUPS)], g_top


def _sort_matrix(slot, n_slots, slot_axis):
    shape = (n_slots, slot.shape[1]) if slot_axis == 0 else (slot.shape[0], n_slots)
    return jnp.where(lax.broadcasted_iota(jnp.int32, shape, slot_axis) == slot, 1.0, 0.0).astype(BF16)


def _out_kernel(oa_ref, y0_ref, y1_ref, u0_ref, u1_ref, oc_ref, x_ref, d_ref, gluw_ref, glub_ref, og_ref, woutf_ref,
                fg_ref, wr_ref, br_ref, bd_ref, x1_ref, hs_ref, tok_ref, cnt_ref, wout_ref, hb_ref, gt_ref, slot_ref):
    tm = x_ref.shape[0]
    n_slots = hs_ref.shape[0]

    @pl.when(pl.program_id(0) == 0)
    def _():
        wout_ref[...] = woutf_ref[0].astype(BF16)
        hb_ref[...] = jnp.zeros_like(hb_ref)
        gt_ref[...] = jnp.zeros_like(gt_ref)
        slot_ref[...] = jnp.zeros_like(slot_ref)


    y = jnp.concatenate([y0_ref[...], y1_ref[...]], axis=-1)
    u = jnp.concatenate([u0_ref[...], u1_ref[...]], axis=-1)
    y = jax.nn.gelu(y + d_ref[...] * u)
    ob = y * jax.nn.sigmoid(_dot(y.astype(BF16), gluw_ref[...]) + glub_ref[...])
    o = jnp.concatenate([oa_ref[...], ob, oc_ref[...]], axis=-1)
    on = o * lax.rsqrt(_group_sumsq(o, bd_ref) * (1.0 / OUT_NORM_GROUP) + EPS) * og_ref[...]
    x1 = x_ref[...] + _dot(on.astype(BF16), wout_ref[...])
    x1_ref[...] = x1
    ms = jnp.mean(x1 * x1, axis=-1, keepdims=True)
    hb = (x1 * lax.rsqrt(ms + EPS) * fg_ref[...]).astype(BF16)
    lt = _dot(hb, wr_ref[...]).T + br_ref[...]
    slabs, g_top = _router_gates(lt)

    row = lax.broadcasted_iota(jnp.int32, (ROUTER_SLAB, tm), 0)
    lane = lax.broadcasted_iota(jnp.int32, (ROUTER_SLAB, tm), 1)
    member = jnp.where(row == g_top, 1.0, 0.0)
    cum = member
    shift = 1
    while shift < tm:
        cum = cum + jnp.where(lane >= shift, pltpu.roll(cum, shift, 1), 0.0)
        shift *= 2
    count = cum[:, tm - 1:tm]
    padded = jnp.floor((count + (MOE_UNIT - 1)) * (1.0 / MOE_UNIT)) * MOE_UNIT
    group_row = lax.broadcasted_iota(jnp.int32, (ROUTER_SLAB, 1), 0)
    start = jnp.zeros((ROUTER_SLAB, 1), F32)
    for g in range(1, N_EXPERT_GROUPS):
        start = start + jnp.where(group_row >= g, padded[g - 1:g, :], 0.0)
    slot = jnp.sum(member * (start + cum - 1.0), axis=0, keepdims=True)
    cnt_ref[...] = jnp.broadcast_to(count, cnt_ref.shape)

    pad = jnp.zeros((LANES - SLOT_LANE - ROUTER_SLAB, tm), F32)
    gates = jnp.concatenate(slabs + [jnp.broadcast_to(slot, (ROUTER_SLAB, tm)), pad], axis=0).T
    tok_ref[...] = gates

    gates_p = gt_ref[...]
    perm = _sort_matrix(slot_ref[0:1, :].astype(jnp.int32), n_slots, 0)
    hs_ref[:, :D_MODEL] = _dot(perm, hb_ref[...])
    g_hi = gates_p.astype(BF16)
    g_mid = (gates_p - g_hi.astype(F32)).astype(BF16)
    g_lo = (gates_p - g_hi.astype(F32) - g_mid.astype(F32)).astype(BF16)
    gs = _dot(perm, jnp.concatenate([g_hi, g_mid, g_lo], axis=-1))
    hs_ref[:, D_MODEL:] = gs[:, :LANES] + gs[:, LANES:2 * LANES] + gs[:, 2 * LANES:]

    hb_ref[...] = hb
    gt_ref[...] = gates
    slot_ref[...] = jnp.broadcast_to(slot, slot_ref.shape)


def _out_call(oa, y0, y1, u0, u1, oc, x, d, gluw, glub, og, wout, layer, fg, wr, br, bd, tm):
    n = x.shape[0]
    n_tiles = n // tm
    n_slots = tm + MOE_PAD_ROWS
    cur = lambda i: (jnp.minimum(i, n_tiles - 1), 0)
    row = lambda c: pl.BlockSpec((tm, c), cur)
    full = lambda a: pl.BlockSpec(a.shape, lambda i: (0,) * a.ndim)
    return pl.pallas_call(
        _out_kernel,
        grid=(n_tiles + 1,),
        in_specs=[row(SGU_WIDTH), row(LANES), row(LANES), row(LANES), row(LANES), row(ATTN_WIDTH), row(D_MODEL),
                  full(d), full(gluw),
                  full(glub), full(og), pl.BlockSpec((1,) + wout.shape[1:], lambda i: (layer, 0, 0), pipeline_mode=pl.Buffered(1)),
                  full(fg), full(wr), full(br), full(bd)],
        out_specs=[row(D_MODEL), pl.BlockSpec((n_slots, MOE_ROW), lambda i: (jnp.maximum(i - 1, 0), 0)), row(LANES),
                   pl.BlockSpec((ROUTER_SLAB, LANES), cur)],
        out_shape=[jax.ShapeDtypeStruct((n, D_MODEL), F32), jax.ShapeDtypeStruct((n_tiles * n_slots, MOE_ROW), F32),
                   jax.ShapeDtypeStruct((n, LANES), F32),
                   jax.ShapeDtypeStruct((n_tiles * ROUTER_SLAB, LANES), F32)],
        scratch_shapes=[pltpu.VMEM(wout.shape[1:], BF16), pltpu.VMEM((tm, D_MODEL), BF16), pltpu.VMEM((tm, LANES), F32),
                        pltpu.VMEM((ROUTER_SLAB, tm), F32)],
        compiler_params=pltpu.CompilerParams(dimension_semantics=("arbitrary",), vmem_limit_bytes=VMEM_LIMIT_BYTES),
        name="out_proj",
    )(oa, y0, y1, u0, u1, oc, x, d, gluw, glub, og, wout, fg, wr, br, bd)


def _router_tables(wg, bg, we, be):
    w = jnp.zeros((D_MODEL, LANES), F32)
    b = jnp.full((LANES,), NEG_INF, F32)
    w = w.at[:, 0:N_EXPERT_GROUPS].set(wg.astype(F32))
    b = b.at[0:N_EXPERT_GROUPS].set(bg.astype(F32))
    for g in range(N_EXPERT_GROUPS):
        c0 = (g + 1) * ROUTER_SLAB
        w = w.at[:, c0:c0 + EXPERTS_PER_GROUP].set(we[g].astype(F32))
        b = b.at[c0:c0 + EXPERTS_PER_GROUP].set(be[g].astype(F32))
    return w.astype(BF16), b[:, None]


def _route_plan(counts, n, tm):
    n_tiles = n // tm
    units_per_tile = (tm + MOE_PAD_ROWS) // MOE_UNIT
    units_per_mtile = tm // MOE_UNIT
    cnt = counts.reshape(n_tiles, ROUTER_SLAB, LANES)[:, :N_EXPERT_GROUPS, 0].astype(jnp.int32)
    seg = (cnt + MOE_UNIT - 1) // MOE_UNIT
    seg_start = jnp.cumsum(seg, axis=1) - seg
    before = jnp.cumsum(seg, axis=0) - seg
    total = jnp.sum(seg, axis=0)
    mtiles = (total + units_per_mtile - 1) // units_per_mtile
    mtile_end = jnp.cumsum(mtiles)
    group_start = (mtile_end - mtiles) * units_per_mtile
    n_mtiles = n_tiles + N_EXPERT_GROUPS + -(-n_tiles * MOE_PAD_ROWS // tm)
    tile_group = jnp.minimum(jnp.sum(jnp.arange(n_mtiles)[:, None] >= mtile_end[None, :], axis=1),
                             N_EXPERT_GROUPS - 1)
    meta = jnp.concatenate([tile_group, mtile_end[-1:]]).astype(jnp.int32)
    groups = jnp.arange(N_EXPERT_GROUPS)
    pick = lambda table, g: jnp.sum(jnp.where(g[..., None] == groups, table, 0), axis=-1)

    q = jnp.arange(n_mtiles * units_per_mtile)
    g = jnp.repeat(tile_group, units_per_mtile)
    ql = q - pick(group_start, g)
    ends_g = pick((before + seg)[None], g[:, None])
    tile_of = jnp.minimum(jnp.sum(ql[:, None] >= ends_g, axis=1), n_tiles - 1)
    offset_g = pick((seg_start - before)[None], g[:, None])
    local = ql + jnp.sum(jnp.where(tile_of[:, None] == jnp.arange(n_tiles), offset_g, 0), axis=1)
    valid = (ql < pick(total, g)) & (q // units_per_mtile < mtile_end[-1])
    src = jnp.where(valid, tile_of * units_per_tile + local, units_per_tile - 1).astype(jnp.int32)

    ul = jnp.arange(units_per_tile)[None, :, None]
    seg_of = jnp.sum(ul >= (seg_start + seg)[:, None, :], axis=-1)
    gi = jnp.minimum(seg_of, N_EXPERT_GROUPS - 1)
    glob = pick((group_start + before - seg_start)[:, None, :], gi) + ul[..., 0]
    dst = jnp.where(seg_of < N_EXPERT_GROUPS, glob, 0).astype(jnp.int32).reshape(-1)
    return meta, src, dst, n_mtiles


def _unit_gather(table_ref, first, src_ref, dst_ref, sem, wait):
    n_rows = dst_ref.shape[0]
    if wait:
        pltpu.make_async_copy(src_ref.at[pl.ds(0, n_rows)], dst_ref, sem).wait()
        return

    def body(k, c):
        u = pl.multiple_of(table_ref[first + k] * MOE_UNIT, MOE_UNIT)
        pltpu.make_async_copy(src_ref.at[pl.ds(u, MOE_UNIT)],
                              dst_ref.at[pl.ds(pl.multiple_of(k * MOE_UNIT, MOE_UNIT), MOE_UNIT)], sem).start()
        return c
    lax.fori_loop(0, n_rows // MOE_UNIT, body, 0, unroll=4)


def _moe_group_kernel(meta_ref, src_ref, hs_ref, wg_ref, wu_ref, wd_ref, o_ref, wgb_ref, wub_ref, wdb_ref,
                      buf_ref, sem):
    j = pl.program_id(0)
    n_steps = pl.num_programs(0)
    n_used = meta_ref[n_steps]
    group = meta_ref[j]
    n_units = o_ref.shape[0] // MOE_UNIT
    slot = j % 2

    @pl.when(j == 0)
    def _():
        _unit_gather(src_ref, 0, hs_ref, buf_ref.at[0], sem.at[0], wait=False)

    @pl.when(j + 1 < n_steps)
    def _():
        _unit_gather(src_ref, (j + 1) * n_units, hs_ref, buf_ref.at[1 - slot], sem.at[1 - slot], wait=False)

    @pl.when((j == 0) | (group != meta_ref[jnp.maximum(j - 1, 0)]))
    def _():
        wgb_ref[...] = wg_ref[...].astype(BF16)
        wub_ref[...] = wu_ref[...].astype(BF16)
        wdb_ref[...] = wd_ref[...].astype(BF16)

    _unit_gather(src_ref, j * n_units, hs_ref, buf_ref.at[slot], sem.at[slot], wait=True)

    @pl.when(j < n_used)
    def _():
        h = buf_ref[slot, :, :D_MODEL].astype(BF16)
        gates = buf_ref[slot, :, D_MODEL:]
        lane = lax.broadcasted_iota(jnp.int32, (1, LANES), 1)
        out = None
        for e in range(EXPERTS_PER_GROUP):
            gt = _dot(h, wgb_ref[e])
            up = _dot(h, wub_ref[e])
            gate = jnp.sum(jnp.where(lane == group * ROUTER_SLAB + e, gates, 0.0), axis=-1, keepdims=True)
            a = (gt * jax.nn.sigmoid(gt)) * up * gate
            part = _dot(a.astype(BF16), wdb_ref[e])
            out = part if out is None else out + part
        o_ref[...] = out

    @pl.when(j >= n_used)
    def _():
        o_ref[...] = jnp.zeros_like(o_ref)


def _moe_group_call(meta, src, hs, wg, wu, wd, layer, n_mtiles, tm):
    wg, wu, wd = (a.reshape((-1,) + a.shape[2:]) for a in (wg, wu, wd))
    by_group = lambda a: pl.BlockSpec((EXPERTS_PER_GROUP,) + a.shape[1:],
                                      lambda j, meta, src: (layer * N_EXPERT_GROUPS + meta[j], 0, 0))
    return pl.pallas_call(
        _moe_group_kernel,
        grid_spec=pltpu.PrefetchScalarGridSpec(
            num_scalar_prefetch=2,
            grid=(n_mtiles,),
            in_specs=[pl.BlockSpec(memory_space=pl.ANY), by_group(wg), by_group(wu), by_group(wd)],
            out_specs=pl.BlockSpec((tm, D_MODEL), lambda j, meta, src: (j, 0)),
            scratch_shapes=[pltpu.VMEM((EXPERTS_PER_GROUP, D_MODEL, D_EXPERT), BF16),
                            pltpu.VMEM((EXPERTS_PER_GROUP, D_MODEL, D_EXPERT), BF16),
                            pltpu.VMEM((EXPERTS_PER_GROUP, D_EXPERT, D_MODEL), BF16),
                            pltpu.VMEM((2, tm, MOE_ROW), F32), pltpu.SemaphoreType.DMA((2,))],
        ),
        out_shape=jax.ShapeDtypeStruct((n_mtiles * tm, D_MODEL), F32),
        compiler_params=pltpu.CompilerParams(dimension_semantics=("arbitrary",), vmem_limit_bytes=VMEM_LIMIT_BYTES),
        name="moe_experts",
    )(meta, src, hs, wg, wu, wd)


def _combine_kernel(dst_ref, x1_ref, tok_ref, ys_ref, o_ref, buf_ref, sem):
    i = pl.program_id(0)
    n_slots = buf_ref.shape[1]
    n_units = n_slots // MOE_UNIT
    slot = i % 2

    @pl.when(i == 0)
    def _():
        _unit_gather(dst_ref, 0, ys_ref, buf_ref.at[0], sem.at[0], wait=False)

    @pl.when(i + 1 < pl.num_programs(0))
    def _():
        _unit_gather(dst_ref, (i + 1) * n_units, ys_ref, buf_ref.at[1 - slot], sem.at[1 - slot], wait=False)

    _unit_gather(dst_ref, i * n_units, ys_ref, buf_ref.at[slot], sem.at[slot], wait=True)

    y = buf_ref[slot]
    y_hi = y.astype(BF16)
    y_lo = (y - y_hi.astype(F32)).astype(BF16)
    unsort = _sort_matrix(tok_ref[:, SLOT_LANE:SLOT_LANE + 1].astype(jnp.int32), n_slots, 1)
    o_ref[...] = x1_ref[...] + _dot(unsort, y_hi) + _dot(unsort, y_lo)


def _combine_call(dst, x1, tok, ys, tm):
    n = x1.shape[0]
    n_slots = tm + MOE_PAD_ROWS
    return pl.pallas_call(
        _combine_kernel,
        grid_spec=pltpu.PrefetchScalarGridSpec(
            num_scalar_prefetch=1,
            grid=(n // tm,),
            in_specs=[pl.BlockSpec((tm, D_MODEL), lambda i, dst: (i, 0)),
                      pl.BlockSpec((tm, LANES), lambda i, dst: (i, 0)),
                      pl.BlockSpec(memory_space=pl.ANY)],
            out_specs=pl.BlockSpec((tm, D_MODEL), lambda i, dst: (i, 0)),
            scratch_shapes=[pltpu.VMEM((2, n_slots, D_MODEL), F32), pltpu.SemaphoreType.DMA((2,))],
        ),
        out_shape=jax.ShapeDtypeStruct((n, D_MODEL), F32),
        compiler_params=pltpu.CompilerParams(dimension_semantics=("arbitrary",), vmem_limit_bytes=VMEM_LIMIT_BYTES),
        name="moe_combine",
    )(dst, x1, tok, ys)


def kernel(x, norm_mix, w_in, sgu_norm, sgu_w, sgu_b, s5_lambda_re, s5_lambda_im, s5_log_dt, s5_b_re, s5_b_im,
           s5_c_re, s5_c_im, s5_d, s5_glu_w, s5_glu_b, q_norm, k_norm, rel_bias, out_norm, w_out, norm_ffn,
           router_group_w, router_group_b, router_expert_w, router_expert_b, w_gate, w_up, w_down):
    batch, seq, _ = x.shape
    n = batch * seq
    depth = w_in.shape[0]
    t = _tiles(n, seq)
    assert seq % ATTN_TQ == 0 and seq % (S5_T * t["s5_rows"]) == 0
    row_vec = lambda a: a.astype(F32)[None, :]

    lane_group = jnp.arange(LANES) // HEAD_DIM
    bd = (lane_group[:, None] == lane_group[None, :]).astype(BF16)
    block_chunk = jnp.arange(SGU_BLOCK) // CHUNK
    sgu_mask = block_chunk[None, :] <= block_chunk[:, None]

    xf = x.reshape(n, D_MODEL)
    for l in range(depth):
        sguw = jnp.where(sgu_mask[None], sgu_w[l], 0).astype(BF16)
        sgub = jnp.repeat(sgu_b[l].astype(F32).T, HEAD_DIM, axis=1)
        qg = row_vec(jnp.tile(q_norm[l], ATTN_HEADS)) * (HEAD_DIM ** -0.5)
        kg = row_vec(jnp.tile(k_norm[l], ATTN_HEADS))
        oa, u0, u1, q, k, v = _in_call(xf, row_vec(norm_mix[l]), w_in, l, row_vec(sgu_norm[l]), sguw, sgub,
                                      qg, kg, bd, t["tm_in"])

        wv, kb, wc, apow = _s5_tables(s5_lambda_re[l], s5_lambda_im[l], s5_log_dt[l], s5_b_re[l], s5_b_im[l],
                                        s5_c_re[l], s5_c_im[l], t["s5_rows"])
        y0, y1 = _s5_call(u0, u1, wv, kb, wc, apow, batch, t["s5_rows"])

        oc = _attn_call(q, k, v, _attn_diag(rel_bias[l]), batch, seq)

        wr, br = _router_tables(router_group_w[l], router_group_b[l], router_expert_w[l], router_expert_b[l])
        x1, hs, tok, counts = _out_call(oa, y0, y1, u0, u1, oc, xf, row_vec(s5_d[l]), s5_glu_w[l].astype(BF16), row_vec(s5_glu_b[l]),
                                 row_vec(out_norm[l]), w_out, l, row_vec(norm_ffn[l]), wr, br, bd,
                                 t["tm_out"])

        meta, src, dst, n_mtiles = _route_plan(counts, n, t["tm_out"])
        ys = _moe_group_call(meta, src, hs, w_gate, w_up, w_down, l, n_mtiles, t["tm_out"])
        xf = _combine_call(dst, x1, tok, ys, t["tm_out"])
    return xf.reshape(batch, seq, D_MODEL)
```

```python
import jax
import jax.numpy as jnp
from jax import lax
from jax.experimental import pallas as pl
from jax.experimental.pallas import tpu as pltpu

F32 = jnp.float32
BF16 = jnp.bfloat16

D_MODEL = 1024
CHUNK = 64
HEAD_DIM = 64
SGU_WIDTH = 256
SGU_HEADS = 4
SGU_BLOCK = 128
S5_WIDTH = 256
S5_GROUP = 16
S5_N_GROUPS = 16
S5_STATE = 64
ATTN_WIDTH = 512
ATTN_HEADS = 8
BAND_CHUNKS = 9
MAX_REL = 256
IN_COLS = 2 * SGU_WIDTH + S5_WIDTH + 3 * ATTN_WIDTH
OUT_NORM_GROUP = 64
N_EXPERT_GROUPS = 4
EXPERTS_PER_GROUP = 4
N_EXPERTS = 16
D_EXPERT = 256
EPS = 1e-6
NEG_INF = -1e30

LANES = 128
SUBLANES = 8
VMEM_LIMIT_BYTES = 56 * 1024 * 1024

S5_T = 16
S5_ROW = S5_T * S5_WIDTH
S5_NSTATE = S5_N_GROUPS * S5_STATE
S5_QUAD = 4
S5_SETS = 4
S5_QUADS = S5_T // S5_QUAD
S5_BLOCK = S5_QUAD * (S5_N_GROUPS // S5_SETS) * S5_GROUP
ATTN_TQ = 256
ATTN_PREV = (BAND_CHUNKS - 1) * CHUNK
ATTN_TK = ATTN_TQ + ATTN_PREV
ROUTER_SLAB = SUBLANES
MOE_ROW = D_MODEL + LANES
MOE_UNIT = SUBLANES
MOE_PAD_ROWS = N_EXPERT_GROUPS * MOE_UNIT
SLOT_LANE = N_EXPERT_GROUPS * ROUTER_SLAB


def _tiles(n_tokens, seq):
    def pick(pref, total):
        t = min(pref, total)
        assert total % t == 0
        return t
    return dict(
        tm_in=pick(512, seq),
        tm_out=pick(512, seq),
        s5_rows=pick(256, seq // S5_T),
    )


def _dot(a, b):
    return jnp.dot(a, b, preferred_element_type=F32)


def _dot_nt(a, b):
    return lax.dot_general(a, b, (((1,), (1,)), ((), ())), preferred_element_type=F32)


def _group_sumsq(x, bd_ref):
    x2 = (x * x).astype(BF16)
    parts = [_dot(x2[:, t * LANES:(t + 1) * LANES], bd_ref[...]) for t in range(x.shape[1] // LANES)]
    return jnp.concatenate(parts, axis=-1)


def _in_kernel(x_ref, g_ref, wf_ref, sgug_ref, sguw_ref, sgub_ref, qg_ref, kg_ref, bd_ref,
               oa_ref, u0_ref, u1_ref, q_ref, k_ref, v_ref, w_ref, ha_ref, hb_ref):
    tm = x_ref.shape[0]

    @pl.when(pl.program_id(0) == 0)
    def _():
        w_ref[...] = wf_ref[0].astype(BF16)
        hb_ref[...] = jnp.zeros_like(hb_ref)

    def project(h_ref):
        x = x_ref[...]
        ms = jnp.mean(x * x, axis=-1, keepdims=True)
        hn = (x * lax.rsqrt(ms + EPS) * g_ref[...]).astype(BF16)
        c0 = 0
        for width in (2 * SGU_WIDTH, S5_WIDTH, ATTN_WIDTH, ATTN_WIDTH, ATTN_WIDTH):
            h_ref[:, c0:c0 + width] = _dot(hn, w_ref[:, c0:c0 + width])
            c0 += width

    def mix(h_ref):
        z = jax.nn.gelu(h_ref[:, 0:2 * SGU_WIDTH])
        u = z[:, :SGU_WIDTH]
        v = z[:, SGU_WIDTH:]
        v = v * lax.rsqrt(jnp.mean(v * v, axis=-1, keepdims=True) + EPS) * sgug_ref[...]
        vb = v.astype(BF16)
        first_head = lax.broadcasted_iota(jnp.int32, (SGU_BLOCK, LANES), 1) < HEAD_DIM
        for r in range(tm // SGU_BLOCK):
            rows = slice(r * SGU_BLOCK, (r + 1) * SGU_BLOCK)
            for p in range(SGU_WIDTH // LANES):
                cols = slice(p * LANES, (p + 1) * LANES)
                vp = vb[rows, cols]
                mixed = jnp.where(first_head, _dot(sguw_ref[2 * p], vp), _dot(sguw_ref[2 * p + 1], vp))
                oa_ref[rows, cols] = u[rows, cols] * (mixed + sgub_ref[:, cols])

        c0 = 2 * SGU_WIDTH
        u0_ref[...] = h_ref[:, c0:c0 + LANES]
        u1_ref[...] = h_ref[:, c0 + LANES:c0 + 2 * LANES]

        c0 += S5_WIDTH
        q = h_ref[:, c0:c0 + ATTN_WIDTH]
        q_ref[...] = (q * lax.rsqrt(_group_sumsq(q, bd_ref) * (1.0 / HEAD_DIM) + EPS) * qg_ref[...]).astype(BF16)
        c0 += ATTN_WIDTH
        k = h_ref[:, c0:c0 + ATTN_WIDTH]
        k_ref[...] = (k * lax.rsqrt(_group_sumsq(k, bd_ref) * (1.0 / HEAD_DIM) + EPS) * kg_ref[...]).astype(BF16)
        c0 += ATTN_WIDTH
        v_ref[...] = h_ref[:, c0:c0 + ATTN_WIDTH].astype(BF16)

    parity = pl.program_id(0) % 2

    @pl.when(parity == 0)
    def _():
        project(ha_ref)
        mix(hb_ref)

    @pl.when(parity == 1)
    def _():
        project(hb_ref)
        mix(ha_ref)


def _in_call(x, g, w, layer, sgug, sguw, sgub, qg, kg, bd, tm):
    n = x.shape[0]
    n_tiles = n // tm
    row = lambda c: pl.BlockSpec((tm, c), lambda i: (jnp.maximum(i - 1, 0), 0))
    full = lambda a: pl.BlockSpec(a.shape, lambda i: (0,) * a.ndim)
    w_layer = pl.BlockSpec((1,) + w.shape[1:], lambda i: (layer, 0, 0), pipeline_mode=pl.Buffered(1))
    return pl.pallas_call(
        _in_kernel,
        grid=(n_tiles + 1,),
        in_specs=[pl.BlockSpec((tm, D_MODEL), lambda i: (jnp.minimum(i, n_tiles - 1), 0)), full(g), w_layer, full(sgug),
                  full(sguw), full(sgub), full(qg), full(kg), full(bd)],
        out_specs=[row(SGU_WIDTH), row(LANES), row(LANES), row(ATTN_WIDTH), row(ATTN_WIDTH), row(ATTN_WIDTH)],
        out_shape=[jax.ShapeDtypeStruct((n, SGU_WIDTH), F32), jax.ShapeDtypeStruct((n, LANES), F32),
                   jax.ShapeDtypeStruct((n, LANES), F32), jax.ShapeDtypeStruct((n, ATTN_WIDTH), BF16),
                   jax.ShapeDtypeStruct((n, ATTN_WIDTH), BF16), jax.ShapeDtypeStruct((n, ATTN_WIDTH), BF16)],
        scratch_shapes=[pltpu.VMEM(w.shape[1:], BF16), pltpu.VMEM((tm, w.shape[2]), F32), pltpu.VMEM((tm, w.shape[2]), F32)],
        compiler_params=pltpu.CompilerParams(dimension_semantics=("arbitrary",), vmem_limit_bytes=VMEM_LIMIT_BYTES),
        name="in_proj",
    )(x, g, w, sgug, sguw, sgub, qg, kg, bd)


def _s5_kernel(u0_ref, u1_ref, wv_ref, kb_ref, wc_ref, apow_ref, y0_ref, y1_ref, ur_ref, s_ref, sp_ref, yb_ref, carry_ref):
    rows = ur_ref.shape[0]
    n_steps = apow_ref.shape[0]
    half = LANES // 2
    low = lax.broadcasted_iota(jnp.int32, (1, LANES), 1) < half
    block = lambda tq, gs: slice((tq * S5_SETS + gs) * S5_BLOCK, (tq * S5_SETS + gs + 1) * S5_BLOCK)

    @pl.when(pl.program_id(1) == 0)
    def _():
        carry_ref[...] = jnp.zeros_like(carry_ref)

    for tq in range(S5_QUADS):
        for lt in range(S5_QUAD // 2):
            t0 = tq * S5_QUAD + 2 * lt
            for pair, u_ref in enumerate((u0_ref, u1_ref)):
                a = u_ref[pl.ds(t0, rows, stride=S5_T), :]
                b = u_ref[pl.ds(t0 + 1, rows, stride=S5_T), :]
                for par in range(2):
                    tile = jnp.where(low, a, pltpu.roll(b, half, 1)) if par == 0 else jnp.where(low, pltpu.roll(a, half, 1), b)
                    col = block(tq, 2 * pair + par).start + lt * LANES
                    ur_ref[:, col:col + LANES] = tile.astype(BF16)

    for gs in range(S5_SETS):
        v = None
        for tq in range(S5_QUADS):
            part = _dot(ur_ref[:, block(tq, gs)], wv_ref[tq * S5_SETS + gs])
            v = part if v is None else v + part
        s_ref[:, gs * S5_BLOCK:(gs + 1) * S5_BLOCK] = v[:, :S5_BLOCK]
        s_ref[:, S5_NSTATE + gs * S5_BLOCK:S5_NSTATE + (gs + 1) * S5_BLOCK] = v[:, S5_BLOCK:]

    row_id = lax.broadcasted_iota(jnp.int32, (rows, LANES), 0)
    for cb in range(S5_NSTATE // LANES):
        cre = slice(cb * LANES, (cb + 1) * LANES)
        cim = slice(S5_NSTATE + cb * LANES, S5_NSTATE + (cb + 1) * LANES)
        re = s_ref[:, cre]
        im = s_ref[:, cim]
        c_re = carry_ref[0:1, cre]
        c_im = carry_ref[0:1, cim]
        a_re = apow_ref[0, 0:1, cre]
        a_im = apow_ref[0, 1:2, cre]
        re = re + jnp.where(row_id == 0, a_re * c_re - a_im * c_im, 0.0)
        im = im + jnp.where(row_id == 0, a_re * c_im + a_im * c_re, 0.0)
        for k in range(n_steps):
            shift = 1 << k
            a_re = apow_ref[k, 0:1, cre]
            a_im = apow_ref[k, 1:2, cre]
            re_s = jnp.where(row_id >= shift, pltpu.roll(re, shift, 0), 0.0)
            im_s = jnp.where(row_id >= shift, pltpu.roll(im, shift, 0), 0.0)
            re, im = re + a_re * re_s - a_im * im_s, im + a_re * im_s + a_im * re_s
        sp_ref[:, cre] = jnp.where(row_id >= 1, pltpu.roll(re, 1, 0), c_re).astype(BF16)
        sp_ref[:, cim] = jnp.where(row_id >= 1, pltpu.roll(im, 1, 0), c_im).astype(BF16)
        carry_ref[0:1, cre] = re[rows - 1:rows, :]
        carry_ref[0:1, cim] = im[rows - 1:rows, :]

    for gs in range(S5_SETS):
        state = jnp.concatenate([sp_ref[:, gs * S5_BLOCK:(gs + 1) * S5_BLOCK],
                                 sp_ref[:, S5_NSTATE + gs * S5_BLOCK:S5_NSTATE + (gs + 1) * S5_BLOCK]], axis=1)
        for tq in range(S5_QUADS):
            y = _dot(state, wc_ref[tq * S5_SETS + gs])
            for tj in range(tq + 1):
                y = y + _dot(ur_ref[:, block(tj, gs)], kb_ref[(tq - tj) * S5_SETS + gs])
            yb_ref[:, block(tq, gs)] = y

    for tq in range(S5_QUADS):
        for lt in range(S5_QUAD // 2):
            t0 = tq * S5_QUAD + 2 * lt
            for pair, y_ref in enumerate((y0_ref, y1_ref)):
                a = yb_ref[:, block(tq, 2 * pair).start + lt * LANES:block(tq, 2 * pair).start + (lt + 1) * LANES]
                b = yb_ref[:, block(tq, 2 * pair + 1).start + lt * LANES:block(tq, 2 * pair + 1).start + (lt + 1) * LANES]
                y_ref[pl.ds(t0, rows, stride=S5_T), :] = jnp.where(low, a, pltpu.roll(b, half, 1))
                y_ref[pl.ds(t0 + 1, rows, stride=S5_T), :] = jnp.where(low, pltpu.roll(a, half, 1), b)


def _s5_call(u0, u1, wv, kb, wc, apow, batch, rows):
    n = u0.shape[0]
    tok = rows * S5_T
    tiles_per_seq = n // batch // tok
    const = lambda a: pl.BlockSpec(a.shape, lambda b, t: (0,) * a.ndim, pipeline_mode=pl.Buffered(1))
    blk = pl.BlockSpec((tok, LANES), lambda b, t: (b * tiles_per_seq + t, 0))
    half = jax.ShapeDtypeStruct((n, LANES), F32)
    return pl.pallas_call(
        _s5_kernel,
        grid=(batch, tiles_per_seq),
        in_specs=[blk, blk, const(wv), const(kb), const(wc), const(apow)],
        out_specs=[blk, blk],
        out_shape=[half, half],
        scratch_shapes=[pltpu.VMEM((rows, S5_ROW), BF16), pltpu.VMEM((rows, 2 * S5_NSTATE), F32),
                        pltpu.VMEM((rows, 2 * S5_NSTATE), BF16), pltpu.VMEM((rows, S5_ROW), F32),
                        pltpu.VMEM((SUBLANES, 2 * S5_NSTATE), F32)],
        compiler_params=pltpu.CompilerParams(dimension_semantics=("arbitrary", "arbitrary"),
                                             vmem_limit_bytes=VMEM_LIMIT_BYTES),
        name="s5_scan",
    )(u0, u1, wv, kb, wc, apow)


def _s5_tables(lam_re, lam_im, log_dt, b_re, b_im, c_re, c_im, rows):
    p_, h_ = S5_STATE, S5_GROUP
    q_, s_, m_ = S5_QUADS, S5_SETS, S5_N_GROUPS // S5_SETS
    hi = lax.Precision.HIGHEST
    lam_re, lam_im = lam_re.astype(F32), lam_im.astype(F32)
    dt = jnp.exp(log_dt.astype(F32))[:, None]
    tau = jnp.arange(S5_T + 1, dtype=F32)[:, None, None]
    mag = jnp.exp(tau * (lam_re * dt)[None])
    ap_re = mag * jnp.cos(tau * (lam_im * dt)[None])
    ap_im = mag * jnp.sin(tau * (lam_im * dt)[None])
    n_re, n_im = ap_re[1] - 1.0, ap_im[1]
    den = lam_re * lam_re + lam_im * lam_im
    k_re = ((n_re * lam_re + n_im * lam_im) / den)[..., None]
    k_im = ((n_im * lam_re - n_re * lam_im) / den)[..., None]
    b_re, b_im = b_re.astype(F32), b_im.astype(F32)
    bb_re = k_re * b_re - k_im * b_im
    bb_im = k_re * b_im + k_im * b_re
    c_re, c_im = c_re.astype(F32), c_im.astype(F32)
    ab_re = ap_re[:S5_T, :, :, None] * bb_re[None] - ap_im[:S5_T, :, :, None] * bb_im[None]
    ab_im = ap_re[:S5_T, :, :, None] * bb_im[None] + ap_im[:S5_T, :, :, None] * bb_re[None]

    def spread(base, col_of_base):
        cols = jnp.arange(S5_BLOCK)
        expand = (col_of_base(cols)[None, :] == jnp.arange(base.shape[1])[:, None]).astype(F32)
        return jnp.dot(base, expand, precision=hi)
    state_col = lambda cols: cols % p_
    chan_col = lambda cols: cols // (m_ * h_) * h_ + cols % h_
    row_group = jnp.arange(S5_BLOCK) // h_ % m_
    keep_state = (row_group[:, None] == (jnp.arange(S5_BLOCK) // p_)[None, :])[None]
    keep_chan = (row_group[:, None] == (jnp.arange(S5_BLOCK) // h_ % m_)[None, :])[None]
    keep_out = ((jnp.arange(S5_BLOCK) // p_)[:, None] == (jnp.arange(S5_BLOCK) // h_ % m_)[None, :])[None]

    def wv_part(ab):
        base = ab[::-1].reshape(q_, S5_QUAD, s_, m_, p_, h_).transpose(0, 2, 1, 3, 5, 4).reshape(-1, p_)
        return jnp.where(keep_state, spread(base, state_col).reshape(q_ * s_, S5_BLOCK, S5_BLOCK), 0.0)
    wv = jnp.concatenate([wv_part(ab_re), wv_part(ab_im)], axis=-1)

    k_tau = (jnp.einsum('gop,tgpi->tgio', c_re, ab_re, precision=hi)
             - jnp.einsum('gop,tgpi->tgio', c_im, ab_im, precision=hi))
    zero = jnp.zeros_like(k_tau[0])
    taps = jnp.stack([jnp.stack([jnp.stack([k_tau[S5_QUAD * dq + ti - tj] if S5_QUAD * dq + ti - tj >= 0 else zero
                                            for ti in range(S5_QUAD)]) for tj in range(S5_QUAD)])
                      for dq in range(q_)])
    base = taps.reshape(q_, S5_QUAD, S5_QUAD, s_, m_, h_, h_).transpose(0, 3, 1, 4, 5, 2, 6).reshape(-1, S5_QUAD * h_)
    kb = jnp.where(keep_chan, spread(base, chan_col).reshape(q_ * s_, S5_BLOCK, S5_BLOCK), 0.0)

    cp = lambda c: c.transpose(0, 2, 1)[:, :, None, :]
    w_re = cp(c_re) * ap_re[1:].transpose(1, 2, 0)[..., None] - cp(c_im) * ap_im[1:].transpose(1, 2, 0)[..., None]
    w_im = cp(c_re) * ap_im[1:].transpose(1, 2, 0)[..., None] + cp(c_im) * ap_re[1:].transpose(1, 2, 0)[..., None]
    def wc_part(w):
        base = w.reshape(s_, m_, p_, q_, S5_QUAD, h_).transpose(3, 0, 1, 2, 4, 5).reshape(-1, S5_QUAD * h_)
        return jnp.where(keep_out, spread(base, chan_col).reshape(q_ * s_, S5_BLOCK, S5_BLOCK), 0.0)
    wc = jnp.concatenate([wc_part(w_re), wc_part(-w_im)], axis=1)

    n_steps = max(1, (rows - 1).bit_length())
    pows = [(ap_re[S5_T].reshape(S5_NSTATE), ap_im[S5_T].reshape(S5_NSTATE))]
    for _ in range(n_steps - 1):
        r, i = pows[-1]
        pows.append((r * r - i * i, 2.0 * r * i))
    apow = jnp.stack([jnp.stack(p) for p in pows])
    return wv.astype(BF16), kb.astype(BF16), wc.astype(BF16), apow


def _attn_fill_bias(diag_ref, bias_ref):
    width = diag_ref.shape[1]
    qi = lax.broadcasted_iota(jnp.int32, (ATTN_TQ, ATTN_TK), 0)
    kj = lax.broadcasted_iota(jnp.int32, (ATTN_TQ, ATTN_TK), 1)
    q_chunk = qi // CHUNK + (BAND_CHUNKS - 1)
    k_chunk = kj // CHUNK
    in_band = (k_chunk <= q_chunk) & (k_chunk >= q_chunk - (BAND_CHUNKS - 1))
    for h in range(ATTN_HEADS):
        rows = jnp.broadcast_to(diag_ref[h:h + 1, :], (ATTN_TQ, width))
        base = pltpu.roll(rows, 0, 1, stride=1, stride_axis=0)[:, :ATTN_TK]
        for t in range(bias_ref.shape[0]):
            bias_ref[t, h] = jnp.where(in_band & (kj >= ATTN_PREV - t * ATTN_TQ), base, NEG_INF)


def _attn_kernel(q_ref, k0_ref, k1_ref, k2_ref, v0_ref, v1_ref, v2_ref, diag_ref, o_ref, bias_ref):
    @pl.when((pl.program_id(0) == 0) & (pl.program_id(1) == 0))
    def _():
        _attn_fill_bias(diag_ref, bias_ref)

    table = jnp.minimum(pl.program_id(1), bias_ref.shape[0] - 1)
    lane = lax.broadcasted_iota(jnp.int32, (1, LANES), 1)
    for p in range(ATTN_WIDTH // LANES):
        cols = slice(p * LANES, (p + 1) * LANES)
        qp = q_ref[:, cols]
        kcat = jnp.concatenate([k0_ref[:, cols], k1_ref[:, cols], k2_ref[:, cols]], axis=0)
        vcat = jnp.concatenate([v0_ref[:, cols], v1_ref[:, cols], v2_ref[:, cols]], axis=0)
        first = lane < HEAD_DIM
        zero = jnp.zeros_like(qp)
        qm = jnp.concatenate([jnp.where(first, qp, zero), jnp.where(first, zero, qp)], axis=0)
        bias = jnp.concatenate([bias_ref[table, 2 * p], bias_ref[table, 2 * p + 1]], axis=0)
        s = _dot_nt(qm, kcat) + bias
        e = jnp.exp(s - jnp.max(s, axis=-1, keepdims=True)).astype(BF16)
        o = _dot(e, jnp.concatenate([vcat, jnp.ones_like(vcat)], axis=1))
        r = o[:, :LANES] / o[:, LANES:]
        o_ref[:, cols] = jnp.where(first, r[:ATTN_TQ], r[ATTN_TQ:])


def _attn_call(q, k, v, diag, batch, seq):
    n = q.shape[0]
    tiles = seq // ATTN_TQ
    n_prev = ATTN_PREV // ATTN_TQ
    cur = pl.BlockSpec((ATTN_TQ, ATTN_WIDTH), lambda b, t: (b * tiles + t, 0))
    prev = lambda d: pl.BlockSpec((ATTN_TQ, ATTN_WIDTH), lambda b, t: (b * tiles + jnp.maximum(t - d, 0), 0))
    assert n_prev == 2
    return pl.pallas_call(
        _attn_kernel,
        grid=(batch, tiles),
        in_specs=[cur, prev(2), prev(1), cur, prev(2), prev(1), cur, pl.BlockSpec(diag.shape, lambda b, t: (0, 0))],
        out_specs=cur,
        out_shape=jax.ShapeDtypeStruct((n, ATTN_WIDTH), F32),
        scratch_shapes=[pltpu.VMEM((n_prev + 1, ATTN_HEADS, ATTN_TQ, ATTN_TK), F32)],
        compiler_params=pltpu.CompilerParams(dimension_semantics=("arbitrary", "arbitrary"),
                                             vmem_limit_bytes=VMEM_LIMIT_BYTES),
        name="band_attn",
    )(q, k, k, k, v, v, v, diag)


def _attn_diag(rel_bias):
    width = ATTN_TQ + ATTN_TK
    n = jnp.arange(width)
    offset = jnp.where(n < ATTN_TK, n, n - width)
    rel = jnp.clip(ATTN_PREV - offset, -MAX_REL, MAX_REL) + MAX_REL
    return jnp.take(rel_bias.astype(F32), rel, axis=1)


def _router_gates(lt):
    tm = lt.shape[1]
    row = lax.broadcasted_iota(jnp.int32, (ROUTER_SLAB, tm), 0)
    gl = lt[0:ROUTER_SLAB]
    gmax = jnp.max(gl, axis=0, keepdims=True)
    p_g = 1.0 / jnp.sum(jnp.exp(gl - gmax), axis=0, keepdims=True)
    g_top = jnp.min(jnp.where(gl == gmax, row, ROUTER_SLAB), axis=0, keepdims=True)
    el = jnp.zeros((ROUTER_SLAB, tm), F32)
    for g in range(N_EXPERT_GROUPS):
        el = el + jnp.where(g_top == g, lt[(g + 1) * ROUTER_SLAB:(g + 2) * ROUTER_SLAB], 0.0)
    ee = jnp.exp(el - jnp.max(el, axis=0, keepdims=True))
    ep = ee / jnp.sum(ee, axis=0, keepdims=True)
    p1 = jnp.max(ep, axis=0, keepdims=True)
    i1 = jnp.min(jnp.where(ep == p1, row, ROUTER_SLAB), axis=0, keepdims=True)
    rest = jnp.where(row == i1, -1.0, ep)
    p2 = jnp.max(rest, axis=0, keepdims=True)
    i2 = jnp.min(jnp.where(rest == p2, row, ROUTER_SLAB), axis=0, keepdims=True)
    tot = p1 + p2
    w = (jnp.where(row == i1, p1 / tot, 0.0) + jnp.where(row == i2, p2 / tot, 0.0)) * p_g
    return [jnp.where(g_top == g, w, 0.0) for g in range(N_EXPERT_GROUPS)], g_top


def _sort_matrix(slot, n_slots, slot_axis):
    shape = (n_slots, slot.shape[1]) if slot_axis == 0 else (slot.shape[0], n_slots)
    return jnp.where(lax.broadcasted_iota(jnp.int32, shape, slot_axis) == slot, 1.0, 0.0).astype(BF16)


def _out_kernel(oa_ref, y0_ref, y1_ref, u0_ref, u1_ref, oc_ref, x_ref, d_ref, gluw_ref, glub_ref, og_ref, woutf_ref,
                fg_ref, wr_ref, br_ref, bd_ref, x1_ref, hs_ref, tok_ref, cnt_ref, wout_ref, hb_ref, gt_ref, slot_ref):
    tm = x_ref.shape[0]
    n_slots = hs_ref.shape[0]

    @pl.when(pl.program_id(0) == 0)
    def _():
        wout_ref[...] = woutf_ref[0].astype(BF16)
        hb_ref[...] = jnp.zeros_like(hb_ref)
        gt_ref[...] = jnp.zeros_like(gt_ref)
        slot_ref[...] = jnp.zeros_like(slot_ref)


    y = jnp.concatenate([y0_ref[...], y1_ref[...]], axis=-1)
    u = jnp.concatenate([u0_ref[...], u1_ref[...]], axis=-1)
    y = jax.nn.gelu(y + d_ref[...] * u)
    ob = y * jax.nn.sigmoid(_dot(y.astype(BF16), gluw_ref[...]) + glub_ref[...])
    o = jnp.concatenate([oa_ref[...], ob, oc_ref[...]], axis=-1)
    on = o * lax.rsqrt(_group_sumsq(o, bd_ref) * (1.0 / OUT_NORM_GROUP) + EPS) * og_ref[...]
    x1 = x_ref[...] + _dot(on.astype(BF16), wout_ref[...])
    x1_ref[...] = x1
    ms = jnp.mean(x1 * x1, axis=-1, keepdims=True)
    hb = (x1 * lax.rsqrt(ms + EPS) * fg_ref[...]).astype(BF16)
    lt = _dot(hb, wr_ref[...]).T + br_ref[...]
    slabs, g_top = _router_gates(lt)

    row = lax.broadcasted_iota(jnp.int32, (ROUTER_SLAB, tm), 0)
    lane = lax.broadcasted_iota(jnp.int32, (ROUTER_SLAB, tm), 1)
    member = jnp.where(row == g_top, 1.0, 0.0)
    cum = member
    shift = 1
    while shift < tm:
        cum = cum + jnp.where(lane >= shift, pltpu.roll(cum, shift, 1), 0.0)
        shift *= 2
    count = cum[:, tm - 1:tm]
    padded = jnp.floor((count + (MOE_UNIT - 1)) * (1.0 / MOE_UNIT)) * MOE_UNIT
    group_row = lax.broadcasted_iota(jnp.int32, (ROUTER_SLAB, 1), 0)
    start = jnp.zeros((ROUTER_SLAB, 1), F32)
    for g in range(1, N_EXPERT_GROUPS):
        start = start + jnp.where(group_row >= g, padded[g - 1:g, :], 0.0)
    slot = jnp.sum(member * (start + cum - 1.0), axis=0, keepdims=True)
    cnt_ref[...] = jnp.broadcast_to(count, cnt_ref.shape)

    pad = jnp.zeros((LANES - SLOT_LANE - ROUTER_SLAB, tm), F32)
    gates = jnp.concatenate(slabs + [jnp.broadcast_to(slot, (ROUTER_SLAB, tm)), pad], axis=0).T
    tok_ref[...] = gates

    gates_p = gt_ref[...]
    perm = _sort_matrix(slot_ref[0:1, :].astype(jnp.int32), n_slots, 0)
    hs_ref[:, :D_MODEL] = _dot(perm, hb_ref[...])
    g_hi = gates_p.astype(BF16)
    g_mid = (gates_p - g_hi.astype(F32)).astype(BF16)
    g_lo = (gates_p - g_hi.astype(F32) - g_mid.astype(F32)).astype(BF16)
    gs = _dot(perm, jnp.concatenate([g_hi, g_mid, g_lo], axis=-1))
    hs_ref[:, D_MODEL:] = gs[:, :LANES] + gs[:, LANES:2 * LANES] + gs[:, 2 * LANES:]

    hb_ref[...] = hb
    gt_ref[...] = gates
    slot_ref[...] = jnp.broadcast_to(slot, slot_ref.shape)


def _out_call(oa, y0, y1, u0, u1, oc, x, d, gluw, glub, og, wout, layer, fg, wr, br, bd, tm):
    n = x.shape[0]
    n_tiles = n // tm
    n_slots = tm + MOE_PAD_ROWS
    cur = lambda i: (jnp.minimum(i, n_tiles - 1), 0)
    row = lambda c: pl.BlockSpec((tm, c), cur)
    full = lambda a: pl.BlockSpec(a.shape, lambda i: (0,) * a.ndim)
    return pl.pallas_call(
        _out_kernel,
        grid=(n_tiles + 1,),
        in_specs=[row(SGU_WIDTH), row(LANES), row(LANES), row(LANES), row(LANES), row(ATTN_WIDTH), row(D_MODEL),
                  full(d), full(gluw),
                  full(glub), full(og), pl.BlockSpec((1,) + wout.shape[1:], lambda i: (layer, 0, 0), pipeline_mode=pl.Buffered(1)),
                  full(fg), full(wr), full(br), full(bd)],
        out_specs=[row(D_MODEL), pl.BlockSpec((n_slots, MOE_ROW), lambda i: (jnp.maximum(i - 1, 0), 0)), row(LANES),
                   pl.BlockSpec((ROUTER_SLAB, LANES), cur)],
        out_shape=[jax.ShapeDtypeStruct((n, D_MODEL), F32), jax.ShapeDtypeStruct((n_tiles * n_slots, MOE_ROW), F32),
                   jax.ShapeDtypeStruct((n, LANES), F32),
                   jax.ShapeDtypeStruct((n_tiles * ROUTER_SLAB, LANES), F32)],
        scratch_shapes=[pltpu.VMEM(wout.shape[1:], BF16), pltpu.VMEM((tm, D_MODEL), BF16), pltpu.VMEM((tm, LANES), F32),
                        pltpu.VMEM((ROUTER_SLAB, tm), F32)],
        compiler_params=pltpu.CompilerParams(dimension_semantics=("arbitrary",), vmem_limit_bytes=VMEM_LIMIT_BYTES),
        name="out_proj",
    )(oa, y0, y1, u0, u1, oc, x, d, gluw, glub, og, wout, fg, wr, br, bd)


def _router_tables(wg, bg, we, be):
    w = jnp.zeros((D_MODEL, LANES), F32)
    b = jnp.full((LANES,), NEG_INF, F32)
    w = w.at[:, 0:N_EXPERT_GROUPS].set(wg.astype(F32))
    b = b.at[0:N_EXPERT_GROUPS].set(bg.astype(F32))
    for g in range(N_EXPERT_GROUPS):
        c0 = (g + 1) * ROUTER_SLAB
        w = w.at[:, c0:c0 + EXPERTS_PER_GROUP].set(we[g].astype(F32))
        b = b.at[c0:c0 + EXPERTS_PER_GROUP].set(be[g].astype(F32))
    return w.astype(BF16), b[:, None]


def _route_plan(counts, n, tm):
    n_tiles = n // tm
    units_per_tile = (tm + MOE_PAD_ROWS) // MOE_UNIT
    units_per_mtile = tm // MOE_UNIT
    cnt = counts.reshape(n_tiles, ROUTER_SLAB, LANES)[:, :N_EXPERT_GROUPS, 0].astype(jnp.int32)
    seg = (cnt + MOE_UNIT - 1) // MOE_UNIT
    seg_start = jnp.cumsum(seg, axis=1) - seg
    before = jnp.cumsum(seg, axis=0) - seg
    total = jnp.sum(seg, axis=0)
    mtiles = (total + units_per_mtile - 1) // units_per_mtile
    mtile_end = jnp.cumsum(mtiles)
    group_start = (mtile_end - mtiles) * units_per_mtile
    n_mtiles = n_tiles + N_EXPERT_GROUPS + -(-n_tiles * MOE_PAD_ROWS // tm)
    tile_group = jnp.minimum(jnp.sum(jnp.arange(n_mtiles)[:, None] >= mtile_end[None, :], axis=1),
                             N_EXPERT_GROUPS - 1)
    meta = jnp.concatenate([tile_group, mtile_end[-1:]]).astype(jnp.int32)
    groups = jnp.arange(N_EXPERT_GROUPS)
    pick = lambda table, g: jnp.sum(jnp.where(g[..., None] == groups, table, 0), axis=-1)

    q = jnp.arange(n_mtiles * units_per_mtile)
    g = jnp.repeat(tile_group, units_per_mtile)
    ql = q - pick(group_start, g)
    ends_g = pick((before + seg)[None], g[:, None])
    tile_of = jnp.minimum(jnp.sum(ql[:, None] >= ends_g, axis=1), n_tiles - 1)
    offset_g = pick((seg_start - before)[None], g[:, None])
    local = ql + jnp.sum(jnp.where(tile_of[:, None] == jnp.arange(n_tiles), offset_g, 0), axis=1)
    valid = (ql < pick(total, g)) & (q // units_per_mtile < mtile_end[-1])
    src = jnp.where(valid, tile_of * units_per_tile + local, units_per_tile - 1).astype(jnp.int32)

    ul = jnp.arange(units_per_tile)[None, :, None]
    seg_of = jnp.sum(ul >= (seg_start + seg)[:, None, :], axis=-1)
    gi = jnp.minimum(seg_of, N_EXPERT_GROUPS - 1)
    glob = pick((group_start + before - seg_start)[:, None, :], gi) + ul[..., 0]
    dst = jnp.where(seg_of < N_EXPERT_GROUPS, glob, 0).astype(jnp.int32).reshape(-1)
    return meta, src, dst, n_mtiles


def _unit_gather(table_ref, first, src_ref, dst_ref, sem, wait):
    n_rows = dst_ref.shape[0]
    if wait:
        pltpu.make_async_copy(src_ref.at[pl.ds(0, n_rows)], dst_ref, sem).wait()
        return

    def body(k, c):
        u = pl.multiple_of(table_ref[first + k] * MOE_UNIT, MOE_UNIT)
        pltpu.make_async_copy(src_ref.at[pl.ds(u, MOE_UNIT)],
                              dst_ref.at[pl.ds(pl.multiple_of(k * MOE_UNIT, MOE_UNIT), MOE_UNIT)], sem).start()
        return c
    lax.fori_loop(0, n_rows // MOE_UNIT, body, 0, unroll=4)


def _moe_group_kernel(meta_ref, src_ref, hs_ref, wg_ref, wu_ref, wd_ref, o_ref, wgb_ref, wub_ref, wdb_ref,
                      buf_ref, sem):
    j = pl.program_id(0)
    n_steps = pl.num_programs(0)
    n_used = meta_ref[n_steps]
    group = meta_ref[j]
    n_units = o_ref.shape[0] // MOE_UNIT
    slot = j % 2

    @pl.when(j == 0)
    def _():
        _unit_gather(src_ref, 0, hs_ref, buf_ref.at[0], sem.at[0], wait=False)

    @pl.when(j + 1 < n_steps)
    def _():
        _unit_gather(src_ref, (j + 1) * n_units, hs_ref, buf_ref.at[1 - slot], sem.at[1 - slot], wait=False)

    @pl.when((j == 0) | (group != meta_ref[jnp.maximum(j - 1, 0)]))
    def _():
        wgb_ref[...] = wg_ref[...].astype(BF16)
        wub_ref[...] = wu_ref[...].astype(BF16)
        wdb_ref[...] = wd_ref[...].astype(BF16)

    _unit_gather(src_ref, j * n_units, hs_ref, buf_ref.at[slot], sem.at[slot], wait=True)

    @pl.when(j < n_used)
    def _():
        h = buf_ref[slot, :, :D_MODEL].astype(BF16)
        gates = buf_ref[slot, :, D_MODEL:]
        lane = lax.broadcasted_iota(jnp.int32, (1, LANES), 1)
        out = None
        for e in range(EXPERTS_PER_GROUP):
            gt = _dot(h, wgb_ref[e])
            up = _dot(h, wub_ref[e])
            gate = jnp.sum(jnp.where(lane == group * ROUTER_SLAB + e, gates, 0.0), axis=-1, keepdims=True)
            a = (gt * jax.nn.sigmoid(gt)) * up * gate
            part = _dot(a.astype(BF16), wdb_ref[e])
            out = part if out is None else out + part
        o_ref[...] = out

    @pl.when(j >= n_used)
    def _():
        o_ref[...] = jnp.zeros_like(o_ref)


def _moe_group_call(meta, src, hs, wg, wu, wd, layer, n_mtiles, tm):
    wg, wu, wd = (a.reshape((-1,) + a.shape[2:]) for a in (wg, wu, wd))
    by_group = lambda a: pl.BlockSpec((EXPERTS_PER_GROUP,) + a.shape[1:],
                                      lambda j, meta, src: (layer * N_EXPERT_GROUPS + meta[j], 0, 0))
    return pl.pallas_call(
        _moe_group_kernel,
        grid_spec=pltpu.PrefetchScalarGridSpec(
            num_scalar_prefetch=2,
            grid=(n_mtiles,),
            in_specs=[pl.BlockSpec(memory_space=pl.ANY), by_group(wg), by_group(wu), by_group(wd)],
            out_specs=pl.BlockSpec((tm, D_MODEL), lambda j, meta, src: (j, 0)),
            scratch_shapes=[pltpu.VMEM((EXPERTS_PER_GROUP, D_MODEL, D_EXPERT), BF16),
                            pltpu.VMEM((EXPERTS_PER_GROUP, D_MODEL, D_EXPERT), BF16),
                            pltpu.VMEM((EXPERTS_PER_GROUP, D_EXPERT, D_MODEL), BF16),
                            pltpu.VMEM((2, tm, MOE_ROW), F32), pltpu.SemaphoreType.DMA((2,))],
        ),
        out_shape=jax.ShapeDtypeStruct((n_mtiles * tm, D_MODEL), F32),
        compiler_params=pltpu.CompilerParams(dimension_semantics=("arbitrary",), vmem_limit_bytes=VMEM_LIMIT_BYTES),
        name="moe_experts",
    )(meta, src, hs, wg, wu, wd)


def _combine_kernel(dst_ref, x1_ref, tok_ref, ys_ref, o_ref, buf_ref, sem):
    i = pl.program_id(0)
    n_slots = buf_ref.shape[1]
    n_units = n_slots // MOE_UNIT
    slot = i % 2

    @pl.when(i == 0)
    def _():
        _unit_gather(dst_ref, 0, ys_ref, buf_ref.at[0], sem.at[0], wait=False)

    @pl.when(i + 1 < pl.num_programs(0))
    def _():
        _unit_gather(dst_ref, (i + 1) * n_units, ys_ref, buf_ref.at[1 - slot], sem.at[1 - slot], wait=False)

    _unit_gather(dst_ref, i * n_units, ys_ref, buf_ref.at[slot], sem.at[slot], wait=True)

    y = buf_ref[slot]
    y_hi = y.astype(BF16)
    y_lo = (y - y_hi.astype(F32)).astype(BF16)
    unsort = _sort_matrix(tok_ref[:, SLOT_LANE:SLOT_LANE + 1].astype(jnp.int32), n_slots, 1)
    o_ref[...] = x1_ref[...] + _dot(unsort, y_hi) + _dot(unsort, y_lo)


def _combine_call(dst, x1, tok, ys, tm):
    n = x1.shape[0]
    n_slots = tm + MOE_PAD_ROWS
    return pl.pallas_call(
        _combine_kernel,
        grid_spec=pltpu.PrefetchScalarGridSpec(
            num_scalar_prefetch=1,
            grid=(n // tm,),
            in_specs=[pl.BlockSpec((tm, D_MODEL), lambda i, dst: (i, 0)),
                      pl.BlockSpec((tm, LANES), lambda i, dst: (i, 0)),
                      pl.BlockSpec(memory_space=pl.ANY)],
            out_specs=pl.BlockSpec((tm, D_MODEL), lambda i, dst: (i, 0)),
            scratch_shapes=[pltpu.VMEM((2, n_slots, D_MODEL), F32), pltpu.SemaphoreType.DMA((2,))],
        ),
        out_shape=jax.ShapeDtypeStruct((n, D_MODEL), F32),
        compiler_params=pltpu.CompilerParams(dimension_semantics=("arbitrary",), vmem_limit_bytes=VMEM_LIMIT_BYTES),
        name="moe_combine",
    )(dst, x1, tok, ys)


def kernel(x, norm_mix, w_in, sgu_norm, sgu_w, sgu_b, s5_lambda_re, s5_lambda_im, s5_log_dt, s5_b_re, s5_b_im,
           s5_c_re, s5_c_im, s5_d, s5_glu_w, s5_glu_b, q_norm, k_norm, rel_bias, out_norm, w_out, norm_ffn,
           router_group_w, router_group_b, router_expert_w, router_expert_b, w_gate, w_up, w_down):
    batch, seq, _ = x.shape
    n = batch * seq
    depth = w_in.shape[0]
    t = _tiles(n, seq)
    assert seq % ATTN_TQ == 0 and seq % (S5_T * t["s5_rows"]) == 0
    row_vec = lambda a: a.astype(F32)[None, :]

    lane_group = jnp.arange(LANES) // HEAD_DIM
    bd = (lane_group[:, None] == lane_group[None, :]).astype(BF16)
    block_chunk = jnp.arange(SGU_BLOCK) // CHUNK
    sgu_mask = block_chunk[None, :] <= block_chunk[:, None]

    xf = x.reshape(n, D_MODEL)
    for l in range(depth):
        sguw = jnp.where(sgu_mask[None], sgu_w[l], 0).astype(BF16)
        sgub = jnp.repeat(sgu_b[l].astype(F32).T, HEAD_DIM, axis=1)
        qg = row_vec(jnp.tile(q_norm[l], ATTN_HEADS)) * (HEAD_DIM ** -0.5)
        kg = row_vec(jnp.tile(k_norm[l], ATTN_HEADS))
        oa, u0, u1, q, k, v = _in_call(xf, row_vec(norm_mix[l]), w_in, l, row_vec(sgu_norm[l]), sguw, sgub,
                                      qg, kg, bd, t["tm_in"])

        wv, kb, wc, apow = _s5_tables(s5_lambda_re[l], s5_lambda_im[l], s5_log_dt[l], s5_b_re[l], s5_b_im[l],
                                        s5_c_re[l], s5_c_im[l], t["s5_rows"])
        y0, y1 = _s5_call(u0, u1, wv, kb, wc, apow, batch, t["s5_rows"])

        oc = _attn_call(q, k, v, _attn_diag(rel_bias[l]), batch, seq)

        wr, br = _router_tables(router_group_w[l], router_group_b[l], router_expert_w[l], router_expert_b[l])
        x1, hs, tok, counts = _out_call(oa, y0, y1, u0, u1, oc, xf, row_vec(s5_d[l]), s5_glu_w[l].astype(BF16), row_vec(s5_glu_b[l]),
                                 row_vec(out_norm[l]), w_out, l, row_vec(norm_ffn[l]), wr, br, bd,
                                 t["tm_out"])

        meta, src, dst, n_mtiles = _route_plan(counts, n, t["tm_out"])
        ys = _moe_group_call(meta, src, hs, w_gate, w_up, w_down, l, n_mtiles, t["tm_out"])
        xf = _combine_call(dst, x1, tok, ys, t["tm_out"])
    return xf.reshape(batch, seq, D_MODEL)
```

```python
import jax
import jax.numpy as jnp
from jax import lax
from jax.experimental import pallas as pl
from jax.experimental.pallas import tpu as pltpu

F32 = jnp.float32
BF16 = jnp.bfloat16

D_MODEL = 1024
CHUNK = 64
HEAD_DIM = 64
SGU_WIDTH = 256
SGU_HEADS = 4
SGU_BLOCK = 128
S5_WIDTH = 256
S5_GROUP = 16
S5_N_GROUPS = 16
S5_STATE = 64
ATTN_WIDTH = 512
ATTN_HEADS = 8
BAND_CHUNKS = 9
MAX_REL = 256
IN_COLS = 2 * SGU_WIDTH + S5_WIDTH + 3 * ATTN_WIDTH
OUT_NORM_GROUP = 64
N_EXPERT_GROUPS = 4
EXPERTS_PER_GROUP = 4
N_EXPERTS = 16
D_EXPERT = 256
EPS = 1e-6
NEG_INF = -1e30

LANES = 128
SUBLANES = 8
VMEM_LIMIT_BYTES = 56 * 1024 * 1024

S5_T = 16
S5_ROW = S5_T * S5_WIDTH
S5_NSTATE = S5_N_GROUPS * S5_STATE
S5_QUAD = 4
S5_SETS = 4
S5_QUADS = S5_T // S5_QUAD
S5_BLOCK = S5_QUAD * (S5_N_GROUPS // S5_SETS) * S5_GROUP
ATTN_TQ = 256
ATTN_PREV = (BAND_CHUNKS - 1) * CHUNK
ATTN_TK = ATTN_TQ + ATTN_PREV
ROUTER_SLAB = SUBLANES
MOE_ROW = D_MODEL + LANES
MOE_UNIT = SUBLANES
MOE_PAD_ROWS = N_EXPERT_GROUPS * MOE_UNIT
SLOT_LANE = N_EXPERT_GROUPS * ROUTER_SLAB


def _tiles(n_tokens, seq):
    def pick(pref, total):
        t = min(pref, total)
        assert total % t == 0
        return t
    return dict(
        tm_in=pick(512, seq),
        tm_out=pick(512, seq),
        s5_rows=pick(256, seq // S5_T),
    )


def _dot(a, b):
    return jnp.dot(a, b, preferred_element_type=F32)


def _dot_nt(a, b):
    return lax.dot_general(a, b, (((1,), (1,)), ((), ())), preferred_element_type=F32)


def _group_sumsq(x, bd_ref):
    x2 = (x * x).astype(BF16)
    parts = [_dot(x2[:, t * LANES:(t + 1) * LANES], bd_ref[...]) for t in range(x.shape[1] // LANES)]
    return jnp.concatenate(parts, axis=-1)


def _in_kernel(x_ref, g_ref, wf_ref, sgug_ref, sguw_ref, sgub_ref, qg_ref, kg_ref, bd_ref,
               oa_ref, u0_ref, u1_ref, q_ref, k_ref, v_ref, w_ref, ha_ref, hb_ref):
    tm = x_ref.shape[0]

    @pl.when(pl.program_id(0) == 0)
    def _():
        w_ref[...] = wf_ref[0].astype(BF16)
        hb_ref[...] = jnp.zeros_like(hb_ref)

    def project(h_ref):
        x = x_ref[...]
        ms = jnp.mean(x * x, axis=-1, keepdims=True)
        hn = (x * lax.rsqrt(ms + EPS) * g_ref[...]).astype(BF16)
        c0 = 0
        for width in (2 * SGU_WIDTH, S5_WIDTH, ATTN_WIDTH, ATTN_WIDTH, ATTN_WIDTH):
            h_ref[:, c0:c0 + width] = _dot(hn, w_ref[:, c0:c0 + width])
            c0 += width

    def mix(h_ref):
        z = jax.nn.gelu(h_ref[:, 0:2 * SGU_WIDTH])
        u = z[:, :SGU_WIDTH]
        v = z[:, SGU_WIDTH:]
        v = v * lax.rsqrt(jnp.mean(v * v, axis=-1, keepdims=True) + EPS) * sgug_ref[...]
        vb = v.astype(BF16)
        first_head = lax.broadcasted_iota(jnp.int32, (SGU_BLOCK, LANES), 1) < HEAD_DIM
        for r in range(tm // SGU_BLOCK):
            rows = slice(r * SGU_BLOCK, (r + 1) * SGU_BLOCK)
            for p in range(SGU_WIDTH // LANES):
                cols = slice(p * LANES, (p + 1) * LANES)
                vp = vb[rows, cols]
                mixed = jnp.where(first_head, _dot(sguw_ref[2 * p], vp), _dot(sguw_ref[2 * p + 1], vp))
                oa_ref[rows, cols] = u[rows, cols] * (mixed + sgub_ref[:, cols])

        c0 = 2 * SGU_WIDTH
        u0_ref[...] = h_ref[:, c0:c0 + LANES]
        u1_ref[...] = h_ref[:, c0 + LANES:c0 + 2 * LANES]

        c0 += S5_WIDTH
        q = h_ref[:, c0:c0 + ATTN_WIDTH]
        q_ref[...] = (q * lax.rsqrt(_group_sumsq(q, bd_ref) * (1.0 / HEAD_DIM) + EPS) * qg_ref[...]).astype(BF16)
        c0 += ATTN_WIDTH
        k = h_ref[:, c0:c0 + ATTN_WIDTH]
        k_ref[...] = (k * lax.rsqrt(_group_sumsq(k, bd_ref) * (1.0 / HEAD_DIM) + EPS) * kg_ref[...]).astype(BF16)
        c0 += ATTN_WIDTH
        v_ref[...] = h_ref[:, c0:c0 + ATTN_WIDTH].astype(BF16)

    parity = pl.program_id(0) % 2

    @pl.when(parity == 0)
    def _():
        project(ha_ref)
        mix(hb_ref)

    @pl.when(parity == 1)
    def _():
        project(hb_ref)
        mix(ha_ref)


def _in_call(x, g, w, layer, sgug, sguw, sgub, qg, kg, bd, tm):
    n = x.shape[0]
    n_tiles = n // tm
    row = lambda c: pl.BlockSpec((tm, c), lambda i: (jnp.maximum(i - 1, 0), 0))
    full = lambda a: pl.BlockSpec(a.shape, lambda i: (0,) * a.ndim)
    w_layer = pl.BlockSpec((1,) + w.shape[1:], lambda i: (layer, 0, 0), pipeline_mode=pl.Buffered(1))
    return pl.pallas_call(
        _in_kernel,
        grid=(n_tiles + 1,),
        in_specs=[pl.BlockSpec((tm, D_MODEL), lambda i: (jnp.minimum(i, n_tiles - 1), 0)), full(g), w_layer, full(sgug),
                  full(sguw), full(sgub), full(qg), full(kg), full(bd)],
        out_specs=[row(SGU_WIDTH), row(LANES), row(LANES), row(ATTN_WIDTH), row(ATTN_WIDTH), row(ATTN_WIDTH)],
        out_shape=[jax.ShapeDtypeStruct((n, SGU_WIDTH), F32), jax.ShapeDtypeStruct((n, LANES), F32),
                   jax.ShapeDtypeStruct((n, LANES), F32), jax.ShapeDtypeStruct((n, ATTN_WIDTH), BF16),
                   jax.ShapeDtypeStruct((n, ATTN_WIDTH), BF16), jax.ShapeDtypeStruct((n, ATTN_WIDTH), BF16)],
        scratch_shapes=[pltpu.VMEM(w.shape[1:], BF16), pltpu.VMEM((tm, w.shape[2]), F32), pltpu.VMEM((tm, w.shape[2]), F32)],
        compiler_params=pltpu.CompilerParams(dimension_semantics=("arbitrary",), vmem_limit_bytes=VMEM_LIMIT_BYTES),
        name="in_proj",
    )(x, g, w, sgug, sguw, sgub, qg, kg, bd)


def _s5_kernel(u0_ref, u1_ref, wv_ref, kb_ref, wc_ref, apow_ref, y0_ref, y1_ref, ur_ref, s_ref, sp_ref, yb_ref, carry_ref):
    rows = ur_ref.shape[0]
    n_steps = apow_ref.shape[0]
    half = LANES // 2
    low = lax.broadcasted_iota(jnp.int32, (1, LANES), 1) < half
    block = lambda tq, gs: slice((tq * S5_SETS + gs) * S5_BLOCK, (tq * S5_SETS + gs + 1) * S5_BLOCK)

    @pl.when(pl.program_id(1) == 0)
    def _():
        carry_ref[...] = jnp.zeros_like(carry_ref)

    for tq in range(S5_QUADS):
        for lt in range(S5_QUAD // 2):
            t0 = tq * S5_QUAD + 2 * lt
            for pair, u_ref in enumerate((u0_ref, u1_ref)):
                a = u_ref[pl.ds(t0, rows, stride=S5_T), :]
                b = u_ref[pl.ds(t0 + 1, rows, stride=S5_T), :]
                for par in range(2):
                    tile = jnp.where(low, a, pltpu.roll(b, half, 1)) if par == 0 else jnp.where(low, pltpu.roll(a, half, 1), b)
                    col = block(tq, 2 * pair + par).start + lt * LANES
                    ur_ref[:, col:col + LANES] = tile.astype(BF16)

    for gs in range(S5_SETS):
        v = None
        for tq in range(S5_QUADS):
            part = _dot(ur_ref[:, block(tq, gs)], wv_ref[tq * S5_SETS + gs])
            v = part if v is None else v + part
        s_ref[:, gs * S5_BLOCK:(gs + 1) * S5_BLOCK] = v[:, :S5_BLOCK]
        s_ref[:, S5_NSTATE + gs * S5_BLOCK:S5_NSTATE + (gs + 1) * S5_BLOCK] = v[:, S5_BLOCK:]

    row_id = lax.broadcasted_iota(jnp.int32, (rows, LANES), 0)
    for cb in range(S5_NSTATE // LANES):
        cre = slice(cb * LANES, (cb + 1) * LANES)
        cim = slice(S5_NSTATE + cb * LANES, S5_NSTATE + (cb + 1) * LANES)
        re = s_ref[:, cre]
        im = s_ref[:, cim]
        c_re = carry_ref[0:1, cre]
        c_im = carry_ref[0:1, cim]
        a_re = apow_ref[0, 0:1, cre]
        a_im = apow_ref[0, 1:2, cre]
        re = re + jnp.where(row_id == 0, a_re * c_re - a_im * c_im, 0.0)
        im = im + jnp.where(row_id == 0, a_re * c_im + a_im * c_re, 0.0)
        for k in range(n_steps):
            shift = 1 << k
            a_re = apow_ref[k, 0:1, cre]
            a_im = apow_ref[k, 1:2, cre]
            re_s = jnp.where(row_id >= shift, pltpu.roll(re, shift, 0), 0.0)
            im_s = jnp.where(row_id >= shift, pltpu.roll(im, shift, 0), 0.0)
            re, im = re + a_re * re_s - a_im * im_s, im + a_re * im_s + a_im * re_s
        sp_ref[:, cre] = jnp.where(row_id >= 1, pltpu.roll(re, 1, 0), c_re).astype(BF16)
        sp_ref[:, cim] = jnp.where(row_id >= 1, pltpu.roll(im, 1, 0), c_im).astype(BF16)
        carry_ref[0:1, cre] = re[rows - 1:rows, :]
        carry_ref[0:1, cim] = im[rows - 1:rows, :]

    for gs in range(S5_SETS):
        state = jnp.concatenate([sp_ref[:, gs * S5_BLOCK:(gs + 1) * S5_BLOCK],
                                 sp_ref[:, S5_NSTATE + gs * S5_BLOCK:S5_NSTATE + (gs + 1) * S5_BLOCK]], axis=1)
        for tq in range(S5_QUADS):
            y = _dot(state, wc_ref[tq * S5_SETS + gs])
            for tj in range(tq + 1):
                y = y + _dot(ur_ref[:, block(tj, gs)], kb_ref[(tq - tj) * S5_SETS + gs])
            yb_ref[:, block(tq, gs)] = y

    for tq in range(S5_QUADS):
        for lt in range(S5_QUAD // 2):
            t0 = tq * S5_QUAD + 2 * lt
            for pair, y_ref in enumerate((y0_ref, y1_ref)):
                a = yb_ref[:, block(tq, 2 * pair).start + lt * LANES:block(tq, 2 * pair).start + (lt + 1) * LANES]
                b = yb_ref[:, block(tq, 2 * pair + 1).start + lt * LANES:block(tq, 2 * pair + 1).start + (lt + 1) * LANES]
                y_ref[pl.ds(t0, rows, stride=S5_T), :] = jnp.where(low, a, pltpu.roll(b, half, 1))
                y_ref[pl.ds(t0 + 1, rows, stride=S5_T), :] = jnp.where(low, pltpu.roll(a, half, 1), b)


def _s5_call(u0, u1, wv, kb, wc, apow, batch, rows):
    n = u0.shape[0]
    tok = rows * S5_T
    tiles_per_seq = n // batch // tok
    const = lambda a: pl.BlockSpec(a.shape, lambda b, t: (0,) * a.ndim, pipeline_mode=pl.Buffered(1))
    blk = pl.BlockSpec((tok, LANES), lambda b, t: (b * tiles_per_seq + t, 0))
    half = jax.ShapeDtypeStruct((n, LANES), F32)
    return pl.pallas_call(
        _s5_kernel,
        grid=(batch, tiles_per_seq),
        in_specs=[blk, blk, const(wv), const(kb), const(wc), const(apow)],
        out_specs=[blk, blk],
        out_shape=[half, half],
        scratch_shapes=[pltpu.VMEM((rows, S5_ROW), BF16), pltpu.VMEM((rows, 2 * S5_NSTATE), F32),
                        pltpu.VMEM((rows, 2 * S5_NSTATE), BF16), pltpu.VMEM((rows, S5_ROW), F32),
                        pltpu.VMEM((SUBLANES, 2 * S5_NSTATE), F32)],
        compiler_params=pltpu.CompilerParams(dimension_semantics=("arbitrary", "arbitrary"),
                                             vmem_limit_bytes=VMEM_LIMIT_BYTES),
        name="s5_scan",
    )(u0, u1, wv, kb, wc, apow)


def _s5_tables(lam_re, lam_im, log_dt, b_re, b_im, c_re, c_im, rows):
    p_, h_ = S5_STATE, S5_GROUP
    q_, s_, m_ = S5_QUADS, S5_SETS, S5_N_GROUPS // S5_SETS
    hi = lax.Precision.HIGHEST
    lam_re, lam_im = lam_re.astype(F32), lam_im.astype(F32)
    dt = jnp.exp(log_dt.astype(F32))[:, None]
    tau = jnp.arange(S5_T + 1, dtype=F32)[:, None, None]
    mag = jnp.exp(tau * (lam_re * dt)[None])
    ap_re = mag * jnp.cos(tau * (lam_im * dt)[None])
    ap_im = mag * jnp.sin(tau * (lam_im * dt)[None])
    n_re, n_im = ap_re[1] - 1.0, ap_im[1]
    den = lam_re * lam_re + lam_im * lam_im
    k_re = ((n_re * lam_re + n_im * lam_im) / den)[..., None]
    k_im = ((n_im * lam_re - n_re * lam_im) / den)[..., None]
    b_re, b_im = b_re.astype(F32), b_im.astype(F32)
    bb_re = k_re * b_re - k_im * b_im
    bb_im = k_re * b_im + k_im * b_re
    c_re, c_im = c_re.astype(F32), c_im.astype(F32)
    ab_re = ap_re[:S5_T, :, :, None] * bb_re[None] - ap_im[:S5_T, :, :, None] * bb_im[None]
    ab_im = ap_re[:S5_T, :, :, None] * bb_im[None] + ap_im[:S5_T, :, :, None] * bb_re[None]

    def spread(base, col_of_base):
        cols = jnp.arange(S5_BLOCK)
        expand = (col_of_base(cols)[None, :] == jnp.arange(base.shape[1])[:, None]).astype(F32)
        return jnp.dot(base, expand, precision=hi)
    state_col = lambda cols: cols % p_
    chan_col = lambda cols: cols // (m_ * h_) * h_ + cols % h_
    row_group = jnp.arange(S5_BLOCK) // h_ % m_
    keep_state = (row_group[:, None] == (jnp.arange(S5_BLOCK) // p_)[None, :])[None]
    keep_chan = (row_group[:, None] == (jnp.arange(S5_BLOCK) // h_ % m_)[None, :])[None]
    keep_out = ((jnp.arange(S5_BLOCK) // p_)[:, None] == (jnp.arange(S5_BLOCK) // h_ % m_)[None, :])[None]

    def wv_part(ab):
        base = ab[::-1].reshape(q_, S5_QUAD, s_, m_, p_, h_).transpose(0, 2, 1, 3, 5, 4).reshape(-1, p_)
        return jnp.where(keep_state, spread(base, state_col).reshape(q_ * s_, S5_BLOCK, S5_BLOCK), 0.0)
    wv = jnp.concatenate([wv_part(ab_re), wv_part(ab_im)], axis=-1)

    k_tau = (jnp.einsum('gop,tgpi->tgio', c_re, ab_re, precision=hi)
             - jnp.einsum('gop,tgpi->tgio', c_im, ab_im, precision=hi))
    zero = jnp.zeros_like(k_tau[0])
    taps = jnp.stack([jnp.stack([jnp.stack([k_tau[S5_QUAD * dq + ti - tj] if S5_QUAD * dq + ti - tj >= 0 else zero
                                            for ti in range(S5_QUAD)]) for tj in range(S5_QUAD)])
                      for dq in range(q_)])
    base = taps.reshape(q_, S5_QUAD, S5_QUAD, s_, m_, h_, h_).transpose(0, 3, 1, 4, 5, 2, 6).reshape(-1, S5_QUAD * h_)
    kb = jnp.where(keep_chan, spread(base, chan_col).reshape(q_ * s_, S5_BLOCK, S5_BLOCK), 0.0)

    cp = lambda c: c.transpose(0, 2, 1)[:, :, None, :]
    w_re = cp(c_re) * ap_re[1:].transpose(1, 2, 0)[..., None] - cp(c_im) * ap_im[1:].transpose(1, 2, 0)[..., None]
    w_im = cp(c_re) * ap_im[1:].transpose(1, 2, 0)[..., None] + cp(c_im) * ap_re[1:].transpose(1, 2, 0)[..., None]
    def wc_part(w):
        base = w.reshape(s_, m_, p_, q_, S5_QUAD, h_).transpose(3, 0, 1, 2, 4, 5).reshape(-1, S5_QUAD * h_)
        return jnp.where(keep_out, spread(base, chan_col).reshape(q_ * s_, S5_BLOCK, S5_BLOCK), 0.0)
    wc = jnp.concatenate([wc_part(w_re), wc_part(-w_im)], axis=1)

    n_steps = max(1, (rows - 1).bit_length())
    pows = [(ap_re[S5_T].reshape(S5_NSTATE), ap_im[S5_T].reshape(S5_NSTATE))]
    for _ in range(n_steps - 1):
        r, i = pows[-1]
        pows.append((r * r - i * i, 2.0 * r * i))
    apow = jnp.stack([jnp.stack(p) for p in pows])
    return wv.astype(BF16), kb.astype(BF16), wc.astype(BF16), apow


def _attn_fill_bias(diag_ref, bias_ref):
    width = diag_ref.shape[1]
    qi = lax.broadcasted_iota(jnp.int32, (ATTN_TQ, ATTN_TK), 0)
    kj = lax.broadcasted_iota(jnp.int32, (ATTN_TQ, ATTN_TK), 1)
    q_chunk = qi // CHUNK + (BAND_CHUNKS - 1)
    k_chunk = kj // CHUNK
    in_band = (k_chunk <= q_chunk) & (k_chunk >= q_chunk - (BAND_CHUNKS - 1))
    for h in range(ATTN_HEADS):
        rows = jnp.broadcast_to(diag_ref[h:h + 1, :], (ATTN_TQ, width))
        base = pltpu.roll(rows, 0, 1, stride=1, stride_axis=0)[:, :ATTN_TK]
        for t in range(bias_ref.shape[0]):
            bias_ref[t, h] = jnp.where(in_band & (kj >= ATTN_PREV - t * ATTN_TQ), base, NEG_INF)


def _attn_kernel(q_ref, k0_ref, k1_ref, k2_ref, v0_ref, v1_ref, v2_ref, diag_ref, o_ref, bias_ref, ea_ref, eb_ref):
    @pl.when((pl.program_id(0) == 0) & (pl.program_id(1) == 0))
    def _():
        _attn_fill_bias(diag_ref, bias_ref)
        eb_ref[...] = jnp.ones_like(eb_ref)

    table = jnp.minimum(pl.program_id(1), bias_ref.shape[0] - 1)
    lane = lax.broadcasted_iota(jnp.int32, (1, LANES), 1)
    first = lane < HEAD_DIM
    n_pairs = ATTN_WIDTH // LANES

    def step(e_ref, ep_ref):
        for p in range(n_pairs):
            cols = slice(p * LANES, (p + 1) * LANES)
            qp = q_ref[:, cols]
            kcat = jnp.concatenate([k0_ref[:, cols], k1_ref[:, cols], k2_ref[:, cols]], axis=0)
            zero = jnp.zeros_like(qp)
            qm = jnp.concatenate([jnp.where(first, qp, zero), jnp.where(first, zero, qp)], axis=0)
            bias = jnp.concatenate([bias_ref[table, 2 * p], bias_ref[table, 2 * p + 1]], axis=0)
            s = _dot_nt(qm, kcat) + bias
            e_ref[p] = jnp.exp(s - jnp.max(s, axis=-1, keepdims=True)).astype(BF16)
        for p in range(n_pairs):
            cols = slice(p * LANES, (p + 1) * LANES)
            vcat = jnp.concatenate([v0_ref[:, cols], v1_ref[:, cols], v2_ref[:, cols]], axis=0)
            o = _dot(ep_ref[p], jnp.concatenate([vcat, jnp.ones_like(vcat)], axis=1))
            r = o[:, :LANES] / o[:, LANES:]
            o_ref[:, cols] = jnp.where(first, r[:ATTN_TQ], r[ATTN_TQ:])

    parity = (pl.program_id(0) * pl.num_programs(1) + pl.program_id(1)) % 2

    @pl.when(parity == 0)
    def _():
        step(ea_ref, eb_ref)

    @pl.when(parity == 1)
    def _():
        step(eb_ref, ea_ref)


def _attn_call(q, k, v, diag, batch, seq):
    n = q.shape[0]
    tiles = seq // ATTN_TQ
    n_prev = ATTN_PREV // ATTN_TQ
    tile_q = lambda t: jnp.minimum(t, tiles - 1)
    tile_v = lambda t: jnp.maximum(t - 1, 0)
    spec = lambda tile, d: pl.BlockSpec((ATTN_TQ, ATTN_WIDTH), lambda b, t: (b * tiles + jnp.maximum(tile(t) - d, 0), 0))
    assert n_prev == 2
    return pl.pallas_call(
        _attn_kernel,
        grid=(batch, tiles + 1),
        in_specs=[spec(tile_q, 0), spec(tile_q, 2), spec(tile_q, 1), spec(tile_q, 0),
                  spec(tile_v, 2), spec(tile_v, 1), spec(tile_v, 0), pl.BlockSpec(diag.shape, lambda b, t: (0, 0))],
        out_specs=spec(tile_v, 0),
        out_shape=jax.ShapeDtypeStruct((n, ATTN_WIDTH), F32),
        scratch_shapes=[pltpu.VMEM((n_prev + 1, ATTN_HEADS, ATTN_TQ, ATTN_TK), F32),
                        pltpu.VMEM((ATTN_WIDTH // LANES, 2 * ATTN_TQ, ATTN_TK), BF16),
                        pltpu.VMEM((ATTN_WIDTH // LANES, 2 * ATTN_TQ, ATTN_TK), BF16)],
        compiler_params=pltpu.CompilerParams(dimension_semantics=("arbitrary", "arbitrary"),
                                             vmem_limit_bytes=VMEM_LIMIT_BYTES),
        name="band_attn",
    )(q, k, k, k, v, v, v, diag)


def _attn_diag(rel_bias):
    width = ATTN_TQ + ATTN_TK
    n = jnp.arange(width)
    offset = jnp.where(n < ATTN_TK, n, n - width)
    rel = jnp.clip(ATTN_PREV - offset, -MAX_REL, MAX_REL) + MAX_REL
    return jnp.take(rel_bias.astype(F32), rel, axis=1)


def _router_gates(lt):
    tm = lt.shape[1]
    row = lax.broadcasted_iota(jnp.int32, (ROUTER_SLAB, tm), 0)
    gl = lt[0:ROUTER_SLAB]
    gmax = jnp.max(gl, axis=0, keepdims=True)
    p_g = 1.0 / jnp.sum(jnp.exp(gl - gmax), axis=0, keepdims=True)
    g_top = jnp.min(jnp.where(gl == gmax, row, ROUTER_SLAB), axis=0, keepdims=True)
    el = jnp.zeros((ROUTER_SLAB, tm), F32)
    for g in range(N_EXPERT_GROUPS):
        el = el + jnp.where(g_top == g, lt[(g + 1) * ROUTER_SLAB:(g + 2) * ROUTER_SLAB], 0.0)
    ee = jnp.exp(el - jnp.max(el, axis=0, keepdims=True))
    ep = ee / jnp.sum(ee, axis=0, keepdims=True)
    p1 = jnp.max(ep, axis=0, keepdims=True)
    i1 = jnp.min(jnp.where(ep == p1, row, ROUTER_SLAB), axis=0, keepdims=True)
    rest = jnp.where(row == i1, -1.0, ep)
    p2 = jnp.max(rest, axis=0, keepdims=True)
    i2 = jnp.min(jnp.where(rest == p2, row, ROUTER_SLAB), axis=0, keepdims=True)
    tot = p1 + p2
    w = (jnp.where(row == i1, p1 / tot, 0.0) + jnp.where(row == i2, p2 / tot, 0.0)) * p_g
    return [jnp.where(g_top == g, w, 0.0) for g in range(N_EXPERT_GROUPS)], g_top


def _sort_matrix(slot, n_slots, slot_axis):
    shape = (n_slots, slot.shape[1]) if slot_axis == 0 else (slot.shape[0], n_slots)
    return jnp.where(lax.broadcasted_iota(jnp.int32, shape, slot_axis) == slot, 1.0, 0.0).astype(BF16)


def _out_kernel(oa_ref, y0_ref, y1_ref, u0_ref, u1_ref, oc_ref, x_ref, d_ref, gluw_ref, glub_ref, og_ref, woutf_ref,
                fg_ref, wr_ref, br_ref, bd_ref, x1_ref, hs_ref, tok_ref, cnt_ref, wout_ref,
                hba_ref, gta_ref, slota_ref, hbb_ref, gtb_ref, slotb_ref):
    tm = x_ref.shape[0]
    n_slots = hs_ref.shape[0]

    @pl.when(pl.program_id(0) == 0)
    def _():
        wout_ref[...] = woutf_ref[0].astype(BF16)
        hbb_ref[...] = jnp.zeros_like(hbb_ref)
        gtb_ref[...] = jnp.zeros_like(gtb_ref)
        slotb_ref[...] = jnp.zeros_like(slotb_ref)

    def step(hb_ref, gt_ref, slot_ref, hbp_ref, gtp_ref, slotp_ref):
        y = jnp.concatenate([y0_ref[...], y1_ref[...]], axis=-1)
        u = jnp.concatenate([u0_ref[...], u1_ref[...]], axis=-1)
        y = jax.nn.gelu(y + d_ref[...] * u)
        ob = y * jax.nn.sigmoid(_dot(y.astype(BF16), gluw_ref[...]) + glub_ref[...])
        o = jnp.concatenate([oa_ref[...], ob, oc_ref[...]], axis=-1)
        on = o * lax.rsqrt(_group_sumsq(o, bd_ref) * (1.0 / OUT_NORM_GROUP) + EPS) * og_ref[...]
        x1 = x_ref[...] + _dot(on.astype(BF16), wout_ref[...])
        x1_ref[...] = x1
        ms = jnp.mean(x1 * x1, axis=-1, keepdims=True)
        hb = (x1 * lax.rsqrt(ms + EPS) * fg_ref[...]).astype(BF16)
        hb_ref[...] = hb
        lt = _dot(hb, wr_ref[...]).T + br_ref[...]
        slabs, g_top = _router_gates(lt)

        row = lax.broadcasted_iota(jnp.int32, (ROUTER_SLAB, tm), 0)
        lane = lax.broadcasted_iota(jnp.int32, (ROUTER_SLAB, tm), 1)
        member = jnp.where(row == g_top, 1.0, 0.0)
        cum = member
        shift = 1
        while shift < tm:
            cum = cum + jnp.where(lane >= shift, pltpu.roll(cum, shift, 1), 0.0)
            shift *= 2
        count = cum[:, tm - 1:tm]
        padded = jnp.floor((count + (MOE_UNIT - 1)) * (1.0 / MOE_UNIT)) * MOE_UNIT
        group_row = lax.broadcasted_iota(jnp.int32, (ROUTER_SLAB, 1), 0)
        start = jnp.zeros((ROUTER_SLAB, 1), F32)
        for g in range(1, N_EXPERT_GROUPS):
            start = start + jnp.where(group_row >= g, padded[g - 1:g, :], 0.0)
        slot = jnp.sum(member * (start + cum - 1.0), axis=0, keepdims=True)
        cnt_ref[...] = jnp.broadcast_to(count, cnt_ref.shape)
        slot_ref[...] = jnp.broadcast_to(slot, slot_ref.shape)

        pad = jnp.zeros((LANES - SLOT_LANE - ROUTER_SLAB, tm), F32)
        gates = jnp.concatenate(slabs + [jnp.broadcast_to(slot, (ROUTER_SLAB, tm)), pad], axis=0).T
        tok_ref[...] = gates
        gt_ref[...] = gates

        gates_p = gtp_ref[...]
        perm = _sort_matrix(slotp_ref[0:1, :].astype(jnp.int32), n_slots, 0)
        hs_ref[:, :D_MODEL] = _dot(perm, hbp_ref[...])
        g_hi = gates_p.astype(BF16)
        g_mid = (gates_p - g_hi.astype(F32)).astype(BF16)
        g_lo = (gates_p - g_hi.astype(F32) - g_mid.astype(F32)).astype(BF16)
        gs = _dot(perm, jnp.concatenate([g_hi, g_mid, g_lo], axis=-1))
        hs_ref[:, D_MODEL:] = gs[:, :LANES] + gs[:, LANES:2 * LANES] + gs[:, 2 * LANES:]

    parity = pl.program_id(0) % 2

    @pl.when(parity == 0)
    def _():
        step(hba_ref, gta_ref, slota_ref, hbb_ref, gtb_ref, slotb_ref)

    @pl.when(parity == 1)
    def _():
        step(hbb_ref, gtb_ref, slotb_ref, hba_ref, gta_ref, slota_ref)


def _out_call(oa, y0, y1, u0, u1, oc, x, d, gluw, glub, og, wout, layer, fg, wr, br, bd, tm):
    n = x.shape[0]
    n_tiles = n // tm
    n_slots = tm + MOE_PAD_ROWS
    cur = lambda i: (jnp.minimum(i, n_tiles - 1), 0)
    row = lambda c: pl.BlockSpec((tm, c), cur)
    full = lambda a: pl.BlockSpec(a.shape, lambda i: (0,) * a.ndim)
    return pl.pallas_call(
        _out_kernel,
        grid=(n_tiles + 1,),
        in_specs=[row(SGU_WIDTH), row(LANES), row(LANES), row(LANES), row(LANES), row(ATTN_WIDTH), row(D_MODEL),
                  full(d), full(gluw),
                  full(glub), full(og), pl.BlockSpec((1,) + wout.shape[1:], lambda i: (layer, 0, 0), pipeline_mode=pl.Buffered(1)),
                  full(fg), full(wr), full(br), full(bd)],
        out_specs=[row(D_MODEL), pl.BlockSpec((n_slots, MOE_ROW), lambda i: (jnp.maximum(i - 1, 0), 0)), row(LANES),
                   pl.BlockSpec((ROUTER_SLAB, LANES), cur)],
        out_shape=[jax.ShapeDtypeStruct((n, D_MODEL), F32), jax.ShapeDtypeStruct((n_tiles * n_slots, MOE_ROW), F32),
                   jax.ShapeDtypeStruct((n, LANES), F32),
                   jax.ShapeDtypeStruct((n_tiles * ROUTER_SLAB, LANES), F32)],
        scratch_shapes=[pltpu.VMEM(wout.shape[1:], BF16)] + 2 * [pltpu.VMEM((tm, D_MODEL), BF16), pltpu.VMEM((tm, LANES), F32),
                                                                 pltpu.VMEM((ROUTER_SLAB, tm), F32)],
        compiler_params=pltpu.CompilerParams(dimension_semantics=("arbitrary",), vmem_limit_bytes=VMEM_LIMIT_BYTES),
        name="out_proj",
    )(oa, y0, y1, u0, u1, oc, x, d, gluw, glub, og, wout, fg, wr, br, bd)


def _router_tables(wg, bg, we, be):
    w = jnp.zeros((D_MODEL, LANES), F32)
    b = jnp.full((LANES,), NEG_INF, F32)
    w = w.at[:, 0:N_EXPERT_GROUPS].set(wg.astype(F32))
    b = b.at[0:N_EXPERT_GROUPS].set(bg.astype(F32))
    for g in range(N_EXPERT_GROUPS):
        c0 = (g + 1) * ROUTER_SLAB
        w = w.at[:, c0:c0 + EXPERTS_PER_GROUP].set(we[g].astype(F32))
        b = b.at[c0:c0 + EXPERTS_PER_GROUP].set(be[g].astype(F32))
    return w.astype(BF16), b[:, None]


def _route_plan(counts, n, tm):
    n_tiles = n // tm
    units_per_tile = (tm + MOE_PAD_ROWS) // MOE_UNIT
    units_per_mtile = tm // MOE_UNIT
    cnt = counts.reshape(n_tiles, ROUTER_SLAB, LANES)[:, :N_EXPERT_GROUPS, 0].astype(jnp.int32)
    seg = (cnt + MOE_UNIT - 1) // MOE_UNIT
    seg_start = jnp.cumsum(seg, axis=1) - seg
    before = jnp.cumsum(seg, axis=0) - seg
    total = jnp.sum(seg, axis=0)
    mtiles = (total + units_per_mtile - 1) // units_per_mtile
    mtile_end = jnp.cumsum(mtiles)
    group_start = (mtile_end - mtiles) * units_per_mtile
    n_mtiles = n_tiles + N_EXPERT_GROUPS + -(-n_tiles * MOE_PAD_ROWS // tm)
    tile_group = jnp.minimum(jnp.sum(jnp.arange(n_mtiles)[:, None] >= mtile_end[None, :], axis=1),
                             N_EXPERT_GROUPS - 1)
    meta = jnp.concatenate([tile_group, mtile_end[-1:]]).astype(jnp.int32)
    groups = jnp.arange(N_EXPERT_GROUPS)
    pick = lambda table, g: jnp.sum(jnp.where(g[..., None] == groups, table, 0), axis=-1)

    q = jnp.arange(n_mtiles * units_per_mtile)
    g = jnp.repeat(tile_group, units_per_mtile)
    ql = q - pick(group_start, g)
    ends_g = pick((before + seg)[None], g[:, None])
    tile_of = jnp.minimum(jnp.sum(ql[:, None] >= ends_g, axis=1), n_tiles - 1)
    offset_g = pick((seg_start - before)[None], g[:, None])
    local = ql + jnp.sum(jnp.where(tile_of[:, None] == jnp.arange(n_tiles), offset_g, 0), axis=1)
    valid = (ql < pick(total, g)) & (q // units_per_mtile < mtile_end[-1])
    src = jnp.where(valid, tile_of * units_per_tile + local, units_per_tile - 1).astype(jnp.int32)

    ul = jnp.arange(units_per_tile)[None, :, None]
    seg_of = jnp.sum(ul >= (seg_start + seg)[:, None, :], axis=-1)
    gi = jnp.minimum(seg_of, N_EXPERT_GROUPS - 1)
    glob = pick((group_start + before - seg_start)[:, None, :], gi) + ul[..., 0]
    dst = jnp.where(seg_of < N_EXPERT_GROUPS, glob, 0).astype(jnp.int32).reshape(-1)
    return meta, src, dst, n_mtiles


def _unit_gather(table_ref, first, src_ref, dst_ref, sem, wait):
    n_rows = dst_ref.shape[0]
    if wait:
        pltpu.make_async_copy(src_ref.at[pl.ds(0, n_rows)], dst_ref, sem).wait()
        return

    def body(k, c):
        u = pl.multiple_of(table_ref[first + k] * MOE_UNIT, MOE_UNIT)
        pltpu.make_async_copy(src_ref.at[pl.ds(u, MOE_UNIT)],
                              dst_ref.at[pl.ds(pl.multiple_of(k * MOE_UNIT, MOE_UNIT), MOE_UNIT)], sem).start()
        return c
    lax.fori_loop(0, n_rows // MOE_UNIT, body, 0, unroll=4)


def _moe_group_kernel(meta_ref, src_ref, hs_ref, wg_ref, wu_ref, wd_ref, o_ref, wgb_ref, wub_ref, wdb_ref,
                      buf_ref, sem):
    j = pl.program_id(0)
    n_steps = pl.num_programs(0)
    n_used = meta_ref[n_steps]
    group = meta_ref[j]
    n_units = o_ref.shape[0] // MOE_UNIT
    slot = j % 2

    @pl.when(j == 0)
    def _():
        _unit_gather(src_ref, 0, hs_ref, buf_ref.at[0], sem.at[0], wait=False)

    @pl.when(j + 1 < n_steps)
    def _():
        _unit_gather(src_ref, (j + 1) * n_units, hs_ref, buf_ref.at[1 - slot], sem.at[1 - slot], wait=False)

    @pl.when((j == 0) | (group != meta_ref[jnp.maximum(j - 1, 0)]))
    def _():
        wgb_ref[...] = wg_ref[...].astype(BF16)
        wub_ref[...] = wu_ref[...].astype(BF16)
        wdb_ref[...] = wd_ref[...].astype(BF16)

    _unit_gather(src_ref, j * n_units, hs_ref, buf_ref.at[slot], sem.at[slot], wait=True)

    @pl.when(j < n_used)
    def _():
        h = buf_ref[slot, :, :D_MODEL].astype(BF16)
        gates = buf_ref[slot, :, D_MODEL:]
        lane = lax.broadcasted_iota(jnp.int32, (1, LANES), 1)
        out = None
        for e in range(EXPERTS_PER_GROUP):
            gt = _dot(h, wgb_ref[e])
            up = _dot(h, wub_ref[e])
            gate = jnp.sum(jnp.where(lane == group * ROUTER_SLAB + e, gates, 0.0), axis=-1, keepdims=True)
            a = (gt * jax.nn.sigmoid(gt)) * up * gate
            part = _dot(a.astype(BF16), wdb_ref[e])
            out = part if out is None else out + part
        o_ref[...] = out

    @pl.when(j >= n_used)
    def _():
        o_ref[...] = jnp.zeros_like(o_ref)


def _moe_group_call(meta, src, hs, wg, wu, wd, layer, n_mtiles, tm):
    wg, wu, wd = (a.reshape((-1,) + a.shape[2:]) for a in (wg, wu, wd))
    by_group = lambda a: pl.BlockSpec((EXPERTS_PER_GROUP,) + a.shape[1:],
                                      lambda j, meta, src: (layer * N_EXPERT_GROUPS + meta[j], 0, 0))
    return pl.pallas_call(
        _moe_group_kernel,
        grid_spec=pltpu.PrefetchScalarGridSpec(
            num_scalar_prefetch=2,
            grid=(n_mtiles,),
            in_specs=[pl.BlockSpec(memory_space=pl.ANY), by_group(wg), by_group(wu), by_group(wd)],
            out_specs=pl.BlockSpec((tm, D_MODEL), lambda j, meta, src: (j, 0)),
            scratch_shapes=[pltpu.VMEM((EXPERTS_PER_GROUP, D_MODEL, D_EXPERT), BF16),
                            pltpu.VMEM((EXPERTS_PER_GROUP, D_MODEL, D_EXPERT), BF16),
                            pltpu.VMEM((EXPERTS_PER_GROUP, D_EXPERT, D_MODEL), BF16),
                            pltpu.VMEM((2, tm, MOE_ROW), F32), pltpu.SemaphoreType.DMA((2,))],
        ),
        out_shape=jax.ShapeDtypeStruct((n_mtiles * tm, D_MODEL), F32),
        compiler_params=pltpu.CompilerParams(dimension_semantics=("arbitrary",), vmem_limit_bytes=VMEM_LIMIT_BYTES),
        name="moe_experts",
    )(meta, src, hs, wg, wu, wd)


def _combine_kernel(dst_ref, x1_ref, tok_ref, ys_ref, o_ref, buf_ref, sem):
    i = pl.program_id(0)
    n_slots = buf_ref.shape[1]
    n_units = n_slots // MOE_UNIT
    slot = i % 2

    @pl.when(i == 0)
    def _():
        _unit_gather(dst_ref, 0, ys_ref, buf_ref.at[0], sem.at[0], wait=False)

    @pl.when(i + 1 < pl.num_programs(0))
    def _():
        _unit_gather(dst_ref, (i + 1) * n_units, ys_ref, buf_ref.at[1 - slot], sem.at[1 - slot], wait=False)

    _unit_gather(dst_ref, i * n_units, ys_ref, buf_ref.at[slot], sem.at[slot], wait=True)

    y = buf_ref[slot]
    y_hi = y.astype(BF16)
    y_lo = (y - y_hi.astype(F32)).astype(BF16)
    unsort = _sort_matrix(tok_ref[:, SLOT_LANE:SLOT_LANE + 1].astype(jnp.int32), n_slots, 1)
    o_ref[...] = x1_ref[...] + _dot(unsort, y_hi) + _dot(unsort, y_lo)


def _combine_call(dst, x1, tok, ys, tm):
    n = x1.shape[0]
    n_slots = tm + MOE_PAD_ROWS
    return pl.pallas_call(
        _combine_kernel,
        grid_spec=pltpu.PrefetchScalarGridSpec(
            num_scalar_prefetch=1,
            grid=(n // tm,),
            in_specs=[pl.BlockSpec((tm, D_MODEL), lambda i, dst: (i, 0)),
                      pl.BlockSpec((tm, LANES), lambda i, dst: (i, 0)),
                      pl.BlockSpec(memory_space=pl.ANY)],
            out_specs=pl.BlockSpec((tm, D_MODEL), lambda i, dst: (i, 0)),
            scratch_shapes=[pltpu.VMEM((2, n_slots, D_MODEL), F32), pltpu.SemaphoreType.DMA((2,))],
        ),
        out_shape=jax.ShapeDtypeStruct((n, D_MODEL), F32),
        compiler_params=pltpu.CompilerParams(dimension_semantics=("arbitrary",), vmem_limit_bytes=VMEM_LIMIT_BYTES),
        name="moe_combine",
    )(dst, x1, tok, ys)


def kernel(x, norm_mix, w_in, sgu_norm, sgu_w, sgu_b, s5_lambda_re, s5_lambda_im, s5_log_dt, s5_b_re, s5_b_im,
           s5_c_re, s5_c_im, s5_d, s5_glu_w, s5_glu_b, q_norm, k_norm, rel_bias, out_norm, w_out, norm_ffn,
           router_group_w, router_group_b, router_expert_w, router_expert_b, w_gate, w_up, w_down):
    batch, seq, _ = x.shape
    n = batch * seq
    depth = w_in.shape[0]
    t = _tiles(n, seq)
    assert seq % ATTN_TQ == 0 and seq % (S5_T * t["s5_rows"]) == 0
    row_vec = lambda a: a.astype(F32)[None, :]

    lane_group = jnp.arange(LANES) // HEAD_DIM
    bd = (lane_group[:, None] == lane_group[None, :]).astype(BF16)
    block_chunk = jnp.arange(SGU_BLOCK) // CHUNK
    sgu_mask = block_chunk[None, :] <= block_chunk[:, None]

    xf = x.reshape(n, D_MODEL)
    for l in range(depth):
        sguw = jnp.where(sgu_mask[None], sgu_w[l], 0).astype(BF16)
        sgub = jnp.repeat(sgu_b[l].astype(F32).T, HEAD_DIM, axis=1)
        qg = row_vec(jnp.tile(q_norm[l], ATTN_HEADS)) * (HEAD_DIM ** -0.5)
        kg = row_vec(jnp.tile(k_norm[l], ATTN_HEADS))
        oa, u0, u1, q, k, v = _in_call(xf, row_vec(norm_mix[l]), w_in, l, row_vec(sgu_norm[l]), sguw, sgub,
                                      qg, kg, bd, t["tm_in"])

        wv, kb, wc, apow = _s5_tables(s5_lambda_re[l], s5_lambda_im[l], s5_log_dt[l], s5_b_re[l], s5_b_im[l],
                                        s5_c_re[l], s5_c_im[l], t["s5_rows"])
        y0, y1 = _s5_call(u0, u1, wv, kb, wc, apow, batch, t["s5_rows"])

        oc = _attn_call(q, k, v, _attn_diag(rel_bias[l]), batch, seq)

        wr, br = _router_tables(router_group_w[l], router_group_b[l], router_expert_w[l], router_expert_b[l])
        x1, hs, tok, counts = _out_call(oa, y0, y1, u0, u1, oc, xf, row_vec(s5_d[l]), s5_glu_w[l].astype(BF16), row_vec(s5_glu_b[l]),
                                 row_vec(out_norm[l]), w_out, l, row_vec(norm_ffn[l]), wr, br, bd,
                                 t["tm_out"])

        meta, src, dst, n_mtiles = _route_plan(counts, n, t["tm_out"])
        ys = _moe_group_call(meta, src, hs, w_gate, w_up, w_down, l, n_mtiles, t["tm_out"])
        xf = _combine_call(dst, x1, tok, ys, t["tm_out"])
    return xf.reshape(batch, seq, D_MODEL)
```

```python
import jax
import jax.numpy as jnp
from jax import lax
from jax.experimental import pallas as pl
from jax.experimental.pallas import tpu as pltpu

F32 = jnp.float32
BF16 = jnp.bfloat16

D_MODEL = 1024
CHUNK = 64
HEAD_DIM = 64
SGU_WIDTH = 256
SGU_HEADS = 4
SGU_BLOCK = 128
S5_WIDTH = 256
S5_GROUP = 16
S5_N_GROUPS = 16
S5_STATE = 64
ATTN_WIDTH = 512
ATTN_HEADS = 8
BAND_CHUNKS = 9
MAX_REL = 256
IN_COLS = 2 * SGU_WIDTH + S5_WIDTH + 3 * ATTN_WIDTH
OUT_NORM_GROUP = 64
N_EXPERT_GROUPS = 4
EXPERTS_PER_GROUP = 4
N_EXPERTS = 16
D_EXPERT = 256
EPS = 1e-6
NEG_INF = -1e30

LANES = 128
SUBLANES = 8
VMEM_LIMIT_BYTES = 56 * 1024 * 1024

S5_T = 16
S5_ROW = S5_T * S5_WIDTH
S5_NSTATE = S5_N_GROUPS * S5_STATE
S5_QUAD = 4
S5_SETS = 4
S5_QUADS = S5_T // S5_QUAD
S5_BLOCK = S5_QUAD * (S5_N_GROUPS // S5_SETS) * S5_GROUP
ATTN_TQ = 256
ATTN_PREV = (BAND_CHUNKS - 1) * CHUNK
ATTN_TK = ATTN_TQ + ATTN_PREV
ROUTER_SLAB = SUBLANES
MOE_ROW = D_MODEL + LANES
MOE_UNIT = SUBLANES
MOE_PAD_ROWS = N_EXPERT_GROUPS * MOE_UNIT
SLOT_LANE = N_EXPERT_GROUPS * ROUTER_SLAB


def _tiles(n_tokens, seq):
    def pick(pref, total):
        t = min(pref, total)
        assert total % t == 0
        return t
    return dict(
        tm_in=pick(512, seq),
        tm_out=pick(512, seq),
        s5_rows=pick(256, seq // S5_T),
    )


def _dot(a, b):
    return jnp.dot(a, b, preferred_element_type=F32)


def _dot_nt(a, b):
    return lax.dot_general(a, b, (((1,), (1,)), ((), ())), preferred_element_type=F32)


def _group_sumsq(x, bd_ref):
    x2 = (x * x).astype(BF16)
    parts = [_dot(x2[:, t * LANES:(t + 1) * LANES], bd_ref[...]) for t in range(x.shape[1] // LANES)]
    return jnp.concatenate(parts, axis=-1)


def _in_kernel(x_ref, g_ref, wf_ref, sgug_ref, sguw_ref, sgub_ref, qg_ref, kg_ref, bd_ref,
               oa_ref, u0_ref, u1_ref, q_ref, k_ref, v_ref, w_ref, ha_ref, hb_ref):
    tm = x_ref.shape[0]

    @pl.when(pl.program_id(0) == 0)
    def _():
        w_ref[...] = wf_ref[0].astype(BF16)
        hb_ref[...] = jnp.zeros_like(hb_ref)

    def project(h_ref):
        x = x_ref[...]
        ms = jnp.mean(x * x, axis=-1, keepdims=True)
        hn = (x * lax.rsqrt(ms + EPS) * g_ref[...]).astype(BF16)
        c0 = 0
        for width in (2 * SGU_WIDTH, S5_WIDTH, ATTN_WIDTH, ATTN_WIDTH, ATTN_WIDTH):
            h_ref[:, c0:c0 + width] = _dot(hn, w_ref[:, c0:c0 + width])
            c0 += width

    def mix(h_ref):
        z = jax.nn.gelu(h_ref[:, 0:2 * SGU_WIDTH])
        u = z[:, :SGU_WIDTH]
        v = z[:, SGU_WIDTH:]
        v = v * lax.rsqrt(jnp.mean(v * v, axis=-1, keepdims=True) + EPS) * sgug_ref[...]
        vb = v.astype(BF16)
        first_head = lax.broadcasted_iota(jnp.int32, (SGU_BLOCK, LANES), 1) < HEAD_DIM
        for r in range(tm // SGU_BLOCK):
            rows = slice(r * SGU_BLOCK, (r + 1) * SGU_BLOCK)
            for p in range(SGU_WIDTH // LANES):
                cols = slice(p * LANES, (p + 1) * LANES)
                vp = vb[rows, cols]
                mixed = jnp.where(first_head, _dot(sguw_ref[2 * p], vp), _dot(sguw_ref[2 * p + 1], vp))
                oa_ref[rows, cols] = u[rows, cols] * (mixed + sgub_ref[:, cols])

        c0 = 2 * SGU_WIDTH
        u0_ref[...] = h_ref[:, c0:c0 + LANES]
        u1_ref[...] = h_ref[:, c0 + LANES:c0 + 2 * LANES]

        c0 += S5_WIDTH
        q = h_ref[:, c0:c0 + ATTN_WIDTH]
        q_ref[...] = (q * lax.rsqrt(_group_sumsq(q, bd_ref) * (1.0 / HEAD_DIM) + EPS) * qg_ref[...]).astype(BF16)
        c0 += ATTN_WIDTH
        k = h_ref[:, c0:c0 + ATTN_WIDTH]
        k_ref[...] = (k * lax.rsqrt(_group_sumsq(k, bd_ref) * (1.0 / HEAD_DIM) + EPS) * kg_ref[...]).astype(BF16)
        c0 += ATTN_WIDTH
        v_ref[...] = h_ref[:, c0:c0 + ATTN_WIDTH].astype(BF16)

    parity = pl.program_id(0) % 2

    @pl.when(parity == 0)
    def _():
        project(ha_ref)
        mix(hb_ref)

    @pl.when(parity == 1)
    def _():
        project(hb_ref)
        mix(ha_ref)


def _in_call(x, g, w, layer, sgug, sguw, sgub, qg, kg, bd, tm):
    n = x.shape[0]
    n_tiles = n // tm
    row = lambda c: pl.BlockSpec((tm, c), lambda i: (jnp.maximum(i - 1, 0), 0))
    full = lambda a: pl.BlockSpec(a.shape, lambda i: (0,) * a.ndim)
    w_layer = pl.BlockSpec((1,) + w.shape[1:], lambda i: (layer, 0, 0), pipeline_mode=pl.Buffered(1))
    return pl.pallas_call(
        _in_kernel,
        grid=(n_tiles + 1,),
        in_specs=[pl.BlockSpec((tm, D_MODEL), lambda i: (jnp.minimum(i, n_tiles - 1), 0)), full(g), w_layer, full(sgug),
                  full(sguw), full(sgub), full(qg), full(kg), full(bd)],
        out_specs=[row(SGU_WIDTH), row(LANES), row(LANES), row(ATTN_WIDTH), row(ATTN_WIDTH), row(ATTN_WIDTH)],
        out_shape=[jax.ShapeDtypeStruct((n, SGU_WIDTH), F32), jax.ShapeDtypeStruct((n, LANES), F32),
                   jax.ShapeDtypeStruct((n, LANES), F32), jax.ShapeDtypeStruct((n, ATTN_WIDTH), BF16),
                   jax.ShapeDtypeStruct((n, ATTN_WIDTH), BF16), jax.ShapeDtypeStruct((n, ATTN_WIDTH), BF16)],
        scratch_shapes=[pltpu.VMEM(w.shape[1:], BF16), pltpu.VMEM((tm, w.shape[2]), F32), pltpu.VMEM((tm, w.shape[2]), F32)],
        compiler_params=pltpu.CompilerParams(dimension_semantics=("arbitrary",), vmem_limit_bytes=VMEM_LIMIT_BYTES),
        name="in_proj",
    )(x, g, w, sgug, sguw, sgub, qg, kg, bd)


def _s5_kernel(u0_ref, u1_ref, wv_ref, kb_ref, wc_ref, apow_ref, y0_ref, y1_ref, ur_ref, s_ref, sp_ref, yb_ref, carry_ref):
    rows = ur_ref.shape[0]
    n_steps = apow_ref.shape[0]
    half = LANES // 2
    low = lax.broadcasted_iota(jnp.int32, (1, LANES), 1) < half
    block = lambda tq, gs: slice((tq * S5_SETS + gs) * S5_BLOCK, (tq * S5_SETS + gs + 1) * S5_BLOCK)

    @pl.when(pl.program_id(1) == 0)
    def _():
        carry_ref[...] = jnp.zeros_like(carry_ref)

    for tq in range(S5_QUADS):
        for lt in range(S5_QUAD // 2):
            t0 = tq * S5_QUAD + 2 * lt
            for pair, u_ref in enumerate((u0_ref, u1_ref)):
                a = u_ref[pl.ds(t0, rows, stride=S5_T), :]
                b = u_ref[pl.ds(t0 + 1, rows, stride=S5_T), :]
                for par in range(2):
                    tile = jnp.where(low, a, pltpu.roll(b, half, 1)) if par == 0 else jnp.where(low, pltpu.roll(a, half, 1), b)
                    col = block(tq, 2 * pair + par).start + lt * LANES
                    ur_ref[:, col:col + LANES] = tile.astype(BF16)

    for gs in range(S5_SETS):
        v = None
        for tq in range(S5_QUADS):
            part = _dot(ur_ref[:, block(tq, gs)], wv_ref[tq * S5_SETS + gs])
            v = part if v is None else v + part
        s_ref[:, gs * S5_BLOCK:(gs + 1) * S5_BLOCK] = v[:, :S5_BLOCK]
        s_ref[:, S5_NSTATE + gs * S5_BLOCK:S5_NSTATE + (gs + 1) * S5_BLOCK] = v[:, S5_BLOCK:]

    row_id = lax.broadcasted_iota(jnp.int32, (rows, LANES), 0)
    for cb in range(S5_NSTATE // LANES):
        cre = slice(cb * LANES, (cb + 1) * LANES)
        cim = slice(S5_NSTATE + cb * LANES, S5_NSTATE + (cb + 1) * LANES)
        re = s_ref[:, cre]
        im = s_ref[:, cim]
        c_re = carry_ref[0:1, cre]
        c_im = carry_ref[0:1, cim]
        a_re = apow_ref[0, 0:1, cre]
        a_im = apow_ref[0, 1:2, cre]
        re = re + jnp.where(row_id == 0, a_re * c_re - a_im * c_im, 0.0)
        im = im + jnp.where(row_id == 0, a_re * c_im + a_im * c_re, 0.0)
        for k in range(n_steps):
            shift = 1 << k
            a_re = apow_ref[k, 0:1, cre]
            a_im = apow_ref[k, 1:2, cre]
            re_s = jnp.where(row_id >= shift, pltpu.roll(re, shift, 0), 0.0)
            im_s = jnp.where(row_id >= shift, pltpu.roll(im, shift, 0), 0.0)
            re, im = re + a_re * re_s - a_im * im_s, im + a_re * im_s + a_im * re_s
        sp_ref[:, cre] = jnp.where(row_id >= 1, pltpu.roll(re, 1, 0), c_re).astype(BF16)
        sp_ref[:, cim] = jnp.where(row_id >= 1, pltpu.roll(im, 1, 0), c_im).astype(BF16)
        carry_ref[0:1, cre] = re[rows - 1:rows, :]
        carry_ref[0:1, cim] = im[rows - 1:rows, :]

    for gs in range(S5_SETS):
        state = jnp.concatenate([sp_ref[:, gs * S5_BLOCK:(gs + 1) * S5_BLOCK],
                                 sp_ref[:, S5_NSTATE + gs * S5_BLOCK:S5_NSTATE + (gs + 1) * S5_BLOCK]], axis=1)
        for tq in range(S5_QUADS):
            y = _dot(state, wc_ref[tq * S5_SETS + gs])
            for tj in range(tq + 1):
                y = y + _dot(ur_ref[:, block(tj, gs)], kb_ref[(tq - tj) * S5_SETS + gs])
            yb_ref[:, block(tq, gs)] = y

    for tq in range(S5_QUADS):
        for lt in range(S5_QUAD // 2):
            t0 = tq * S5_QUAD + 2 * lt
            for pair, y_ref in enumerate((y0_ref, y1_ref)):
                a = yb_ref[:, block(tq, 2 * pair).start + lt * LANES:block(tq, 2 * pair).start + (lt + 1) * LANES]
                b = yb_ref[:, block(tq, 2 * pair + 1).start + lt * LANES:block(tq, 2 * pair + 1).start + (lt + 1) * LANES]
                y_ref[pl.ds(t0, rows, stride=S5_T), :] = jnp.where(low, a, pltpu.roll(b, half, 1))
                y_ref[pl.ds(t0 + 1, rows, stride=S5_T), :] = jnp.where(low, pltpu.roll(a, half, 1), b)


def _s5_call(u0, u1, wv, kb, wc, apow, batch, rows):
    n = u0.shape[0]
    tok = rows * S5_T
    tiles_per_seq = n // batch // tok
    const = lambda a: pl.BlockSpec(a.shape, lambda b, t: (0,) * a.ndim, pipeline_mode=pl.Buffered(1))
    blk = pl.BlockSpec((tok, LANES), lambda b, t: (b * tiles_per_seq + t, 0))
    half = jax.ShapeDtypeStruct((n, LANES), F32)
    return pl.pallas_call(
        _s5_kernel,
        grid=(batch, tiles_per_seq),
        in_specs=[blk, blk, const(wv), const(kb), const(wc), const(apow)],
        out_specs=[blk, blk],
        out_shape=[half, half],
        scratch_shapes=[pltpu.VMEM((rows, S5_ROW), BF16), pltpu.VMEM((rows, 2 * S5_NSTATE), F32),
                        pltpu.VMEM((rows, 2 * S5_NSTATE), BF16), pltpu.VMEM((rows, S5_ROW), F32),
                        pltpu.VMEM((SUBLANES, 2 * S5_NSTATE), F32)],
        compiler_params=pltpu.CompilerParams(dimension_semantics=("arbitrary", "arbitrary"),
                                             vmem_limit_bytes=VMEM_LIMIT_BYTES),
        name="s5_scan",
    )(u0, u1, wv, kb, wc, apow)


def _s5_tables(lam_re, lam_im, log_dt, b_re, b_im, c_re, c_im, rows):
    p_, h_ = S5_STATE, S5_GROUP
    q_, s_, m_ = S5_QUADS, S5_SETS, S5_N_GROUPS // S5_SETS
    hi = lax.Precision.HIGHEST
    lam_re, lam_im = lam_re.astype(F32), lam_im.astype(F32)
    dt = jnp.exp(log_dt.astype(F32))[:, None]
    tau = jnp.arange(S5_T + 1, dtype=F32)[:, None, None]
    mag = jnp.exp(tau * (lam_re * dt)[None])
    ap_re = mag * jnp.cos(tau * (lam_im * dt)[None])
    ap_im = mag * jnp.sin(tau * (lam_im * dt)[None])
    n_re, n_im = ap_re[1] - 1.0, ap_im[1]
    den = lam_re * lam_re + lam_im * lam_im
    k_re = ((n_re * lam_re + n_im * lam_im) / den)[..., None]
    k_im = ((n_im * lam_re - n_re * lam_im) / den)[..., None]
    b_re, b_im = b_re.astype(F32), b_im.astype(F32)
    bb_re = k_re * b_re - k_im * b_im
    bb_im = k_re * b_im + k_im * b_re
    c_re, c_im = c_re.astype(F32), c_im.astype(F32)
    ab_re = ap_re[:S5_T, :, :, None] * bb_re[None] - ap_im[:S5_T, :, :, None] * bb_im[None]
    ab_im = ap_re[:S5_T, :, :, None] * bb_im[None] + ap_im[:S5_T, :, :, None] * bb_re[None]

    def spread(base, col_of_base):
        cols = jnp.arange(S5_BLOCK)
        expand = (col_of_base(cols)[None, :] == jnp.arange(base.shape[1])[:, None]).astype(F32)
        return jnp.dot(base, expand, precision=hi)
    state_col = lambda cols: cols % p_
    chan_col = lambda cols: cols // (m_ * h_) * h_ + cols % h_
    row_group = jnp.arange(S5_BLOCK) // h_ % m_
    keep_state = (row_group[:, None] == (jnp.arange(S5_BLOCK) // p_)[None, :])[None]
    keep_chan = (row_group[:, None] == (jnp.arange(S5_BLOCK) // h_ % m_)[None, :])[None]
    keep_out = ((jnp.arange(S5_BLOCK) // p_)[:, None] == (jnp.arange(S5_BLOCK) // h_ % m_)[None, :])[None]

    def wv_part(ab):
        base = ab[::-1].reshape(q_, S5_QUAD, s_, m_, p_, h_).transpose(0, 2, 1, 3, 5, 4).reshape(-1, p_)
        return jnp.where(keep_state, spread(base, state_col).reshape(q_ * s_, S5_BLOCK, S5_BLOCK), 0.0)
    wv = jnp.concatenate([wv_part(ab_re), wv_part(ab_im)], axis=-1)

    k_tau = (jnp.einsum('gop,tgpi->tgio', c_re, ab_re, precision=hi)
             - jnp.einsum('gop,tgpi->tgio', c_im, ab_im, precision=hi))
    zero = jnp.zeros_like(k_tau[0])
    taps = jnp.stack([jnp.stack([jnp.stack([k_tau[S5_QUAD * dq + ti - tj] if S5_QUAD * dq + ti - tj >= 0 else zero
                                            for ti in range(S5_QUAD)]) for tj in range(S5_QUAD)])
                      for dq in range(q_)])
    base = taps.reshape(q_, S5_QUAD, S5_QUAD, s_, m_, h_, h_).transpose(0, 3, 1, 4, 5, 2, 6).reshape(-1, S5_QUAD * h_)
    kb = jnp.where(keep_chan, spread(base, chan_col).reshape(q_ * s_, S5_BLOCK, S5_BLOCK), 0.0)

    cp = lambda c: c.transpose(0, 2, 1)[:, :, None, :]
    w_re = cp(c_re) * ap_re[1:].transpose(1, 2, 0)[..., None] - cp(c_im) * ap_im[1:].transpose(1, 2, 0)[..., None]
    w_im = cp(c_re) * ap_im[1:].transpose(1, 2, 0)[..., None] + cp(c_im) * ap_re[1:].transpose(1, 2, 0)[..., None]
    def wc_part(w):
        base = w.reshape(s_, m_, p_, q_, S5_QUAD, h_).transpose(3, 0, 1, 2, 4, 5).reshape(-1, S5_QUAD * h_)
        return jnp.where(keep_out, spread(base, chan_col).reshape(q_ * s_, S5_BLOCK, S5_BLOCK), 0.0)
    wc = jnp.concatenate([wc_part(w_re), wc_part(-w_im)], axis=1)

    n_steps = max(1, (rows - 1).bit_length())
    pows = [(ap_re[S5_T].reshape(S5_NSTATE), ap_im[S5_T].reshape(S5_NSTATE))]
    for _ in range(n_steps - 1):
        r, i = pows[-1]
        pows.append((r * r - i * i, 2.0 * r * i))
    apow = jnp.stack([jnp.stack(p) for p in pows])
    return wv.astype(BF16), kb.astype(BF16), wc.astype(BF16), apow


def _attn_fill_bias(diag_ref, bias_ref):
    width = diag_ref.shape[1]
    qi = lax.broadcasted_iota(jnp.int32, (ATTN_TQ, ATTN_TK), 0)
    kj = lax.broadcasted_iota(jnp.int32, (ATTN_TQ, ATTN_TK), 1)
    q_chunk = qi // CHUNK + (BAND_CHUNKS - 1)
    k_chunk = kj // CHUNK
    in_band = (k_chunk <= q_chunk) & (k_chunk >= q_chunk - (BAND_CHUNKS - 1))
    for h in range(ATTN_HEADS):
        rows = jnp.broadcast_to(diag_ref[h:h + 1, :], (ATTN_TQ, width))
        base = pltpu.roll(rows, 0, 1, stride=1, stride_axis=0)[:, :ATTN_TK]
        for t in range(bias_ref.shape[0]):
            bias_ref[t, h] = jnp.where(in_band & (kj >= ATTN_PREV - t * ATTN_TQ), base, NEG_INF)


def _attn_kernel(q_ref, k0_ref, k1_ref, k2_ref, v0_ref, v1_ref, v2_ref, diag_ref, o_ref, bias_ref):
    @pl.when((pl.program_id(0) == 0) & (pl.program_id(1) == 0))
    def _():
        _attn_fill_bias(diag_ref, bias_ref)

    table = jnp.minimum(pl.program_id(1), bias_ref.shape[0] - 1)
    lane = lax.broadcasted_iota(jnp.int32, (1, LANES), 1)
    for p in range(ATTN_WIDTH // LANES):
        cols = slice(p * LANES, (p + 1) * LANES)
        qp = q_ref[:, cols]
        kcat = jnp.concatenate([k0_ref[:, cols], k1_ref[:, cols], k2_ref[:, cols]], axis=0)
        vcat = jnp.concatenate([v0_ref[:, cols], v1_ref[:, cols], v2_ref[:, cols]], axis=0)
        first = lane < HEAD_DIM
        zero = jnp.zeros_like(qp)
        qm = jnp.concatenate([jnp.where(first, qp, zero), jnp.where(first, zero, qp)], axis=0)
        bias = jnp.concatenate([bias_ref[table, 2 * p], bias_ref[table, 2 * p + 1]], axis=0)
        s = _dot_nt(qm, kcat) + bias
        e = jnp.exp(s - jnp.max(s, axis=-1, keepdims=True)).astype(BF16)
        o = _dot(e, jnp.concatenate([vcat, jnp.ones_like(vcat)], axis=1))
        r = o[:, :LANES] / o[:, LANES:]
        o_ref[:, cols] = jnp.where(first, r[:ATTN_TQ], r[ATTN_TQ:])


def _attn_call(q, k, v, diag, batch, seq):
    n = q.shape[0]
    tiles = seq // ATTN_TQ
    n_prev = ATTN_PREV // ATTN_TQ
    cur = pl.BlockSpec((ATTN_TQ, ATTN_WIDTH), lambda b, t: (b * tiles + t, 0))
    prev = lambda d: pl.BlockSpec((ATTN_TQ, ATTN_WIDTH), lambda b, t: (b * tiles + jnp.maximum(t - d, 0), 0))
    assert n_prev == 2
    return pl.pallas_call(
        _attn_kernel,
        grid=(batch, tiles),
        in_specs=[cur, prev(2), prev(1), cur, prev(2), prev(1), cur, pl.BlockSpec(diag.shape, lambda b, t: (0, 0))],
        out_specs=cur,
        out_shape=jax.ShapeDtypeStruct((n, ATTN_WIDTH), F32),
        scratch_shapes=[pltpu.VMEM((n_prev + 1, ATTN_HEADS, ATTN_TQ, ATTN_TK), F32)],
        compiler_params=pltpu.CompilerParams(dimension_semantics=("arbitrary", "arbitrary"),
                                             vmem_limit_bytes=VMEM_LIMIT_BYTES),
        name="band_attn",
    )(q, k, k, k, v, v, v, diag)


def _attn_diag(rel_bias):
    width = ATTN_TQ + ATTN_TK
    n = jnp.arange(width)
    offset = jnp.where(n < ATTN_TK, n, n - width)
    rel = jnp.clip(ATTN_PREV - offset, -MAX_REL, MAX_REL) + MAX_REL
    return jnp.take(rel_bias.astype(F32), rel, axis=1)


def _router_gates(lt):
    tm = lt.shape[1]
    row = lax.broadcasted_iota(jnp.int32, (ROUTER_SLAB, tm), 0)
    gl = lt[0:ROUTER_SLAB]
    gmax = jnp.max(gl, axis=0, keepdims=True)
    p_g = 1.0 / jnp.sum(jnp.exp(gl - gmax), axis=0, keepdims=True)
    g_top = jnp.min(jnp.where(gl == gmax, row, ROUTER_SLAB), axis=0, keepdims=True)
    el = jnp.zeros((ROUTER_SLAB, tm), F32)
    for g in range(N_EXPERT_GROUPS):
        el = el + jnp.where(g_top == g, lt[(g + 1) * ROUTER_SLAB:(g + 2) * ROUTER_SLAB], 0.0)
    ee = jnp.exp(el - jnp.max(el, axis=0, keepdims=True))
    ep = ee / jnp.sum(ee, axis=0, keepdims=True)
    p1 = jnp.max(ep, axis=0, keepdims=True)
    i1 = jnp.min(jnp.where(ep == p1, row, ROUTER_SLAB), axis=0, keepdims=True)
    rest = jnp.where(row == i1, -1.0, ep)
    p2 = jnp.max(rest, axis=0, keepdims=True)
    i2 = jnp.min(jnp.where(rest == p2, row, ROUTER_SLAB), axis=0, keepdims=True)
    tot = p1 + p2
    w = (jnp.where(row == i1, p1 / tot, 0.0) + jnp.where(row == i2, p2 / tot, 0.0)) * p_g
    return [jnp.where(g_top == g, w, 0.0) for g in range(N_EXPERT_GROUPS)], g_top


def _sort_matrix(slot, n_slots, slot_axis):
    shape = (n_slots, slot.shape[1]) if slot_axis == 0 else (slot.shape[0], n_slots)
    return jnp.where(lax.broadcasted_iota(jnp.int32, shape, slot_axis) == slot, 1.0, 0.0).astype(BF16)


def _out_kernel(oa_ref, y0_ref, y1_ref, u0_ref, u1_ref, oc_ref, x_ref, d_ref, gluw_ref, glub_ref, og_ref, woutf_ref,
                fg_ref, wr_ref, br_ref, bd_ref, x1_ref, hs_ref, tok_ref, cnt_ref, wout_ref, hb_ref, gt_ref, slot_ref):
    tm = x_ref.shape[0]
    n_slots = hs_ref.shape[0]

    @pl.when(pl.program_id(0) == 0)
    def _():
        wout_ref[...] = woutf_ref[0].astype(BF16)
        hb_ref[...] = jnp.zeros_like(hb_ref)
        gt_ref[...] = jnp.zeros_like(gt_ref)
        slot_ref[...] = jnp.zeros_like(slot_ref)


    y = jnp.concatenate([y0_ref[...], y1_ref[...]], axis=-1)
    u = jnp.concatenate([u0_ref[...], u1_ref[...]], axis=-1)
    y = jax.nn.gelu(y + d_ref[...] * u)
    ob = y * jax.nn.sigmoid(_dot(y.astype(BF16), gluw_ref[...]) + glub_ref[...])
    o = jnp.concatenate([oa_ref[...], ob, oc_ref[...]], axis=-1)
    on = o * lax.rsqrt(_group_sumsq(o, bd_ref) * (1.0 / OUT_NORM_GROUP) + EPS) * og_ref[...]
    x1 = x_ref[...] + _dot(on.astype(BF16), wout_ref[...])
    x1_ref[...] = x1
    ms = jnp.mean(x1 * x1, axis=-1, keepdims=True)
    hb = (x1 * lax.rsqrt(ms + EPS) * fg_ref[...]).astype(BF16)
    lt = _dot(hb, wr_ref[...]).T + br_ref[...]
    slabs, g_top = _router_gates(lt)

    row = lax.broadcasted_iota(jnp.int32, (ROUTER_SLAB, tm), 0)
    lane = lax.broadcasted_iota(jnp.int32, (ROUTER_SLAB, tm), 1)
    member = jnp.where(row == g_top, 1.0, 0.0)
    cum = member
    shift = 1
    while shift < tm:
        cum = cum + jnp.where(lane >= shift, pltpu.roll(cum, shift, 1), 0.0)
        shift *= 2
    count = cum[:, tm - 1:tm]
    padded = jnp.floor((count + (MOE_UNIT - 1)) * (1.0 / MOE_UNIT)) * MOE_UNIT
    group_row = lax.broadcasted_iota(jnp.int32, (ROUTER_SLAB, 1), 0)
    start = jnp.zeros((ROUTER_SLAB, 1), F32)
    for g in range(1, N_EXPERT_GROUPS):
        start = start + jnp.where(group_row >= g, padded[g - 1:g, :], 0.0)
    slot = jnp.sum(member * (start + cum - 1.0), axis=0, keepdims=True)
    cnt_ref[...] = jnp.broadcast_to(count, cnt_ref.shape)

    pad = jnp.zeros((LANES - SLOT_LANE - ROUTER_SLAB, tm), F32)
    gates = jnp.concatenate(slabs + [jnp.broadcast_to(slot, (ROUTER_SLAB, tm)), pad], axis=0).T
    tok_ref[...] = gates

    gates_p = gt_ref[...]
    perm = _sort_matrix(slot_ref[0:1, :].astype(jnp.int32), n_slots, 0)
    hs_ref[:, :D_MODEL] = _dot(perm, hb_ref[...])
    g_hi = gates_p.astype(BF16)
    g_mid = (gates_p - g_hi.astype(F32)).astype(BF16)
    g_lo = (gates_p - g_hi.astype(F32) - g_mid.astype(F32)).astype(BF16)
    gs = _dot(perm, jnp.concatenate([g_hi, g_mid, g_lo], axis=-1))
    hs_ref[:, D_MODEL:] = gs[:, :LANES] + gs[:, LANES:2 * LANES] + gs[:, 2 * LANES:]

    hb_ref[...] = hb
    gt_ref[...] = gates
    slot_ref[...] = jnp.broadcast_to(slot, slot_ref.shape)


def _out_call(oa, y0, y1, u0, u1, oc, x, d, gluw, glub, og, wout, layer, fg, wr, br, bd, tm):
    n = x.shape[0]
    n_tiles = n // tm
    n_slots = tm + MOE_PAD_ROWS
    cur = lambda i: (jnp.minimum(i, n_tiles - 1), 0)
    row = lambda c: pl.BlockSpec((tm, c), cur)
    full = lambda a: pl.BlockSpec(a.shape, lambda i: (0,) * a.ndim)
    return pl.pallas_call(
        _out_kernel,
        grid=(n_tiles + 1,),
        in_specs=[row(SGU_WIDTH), row(LANES), row(LANES), row(LANES), row(LANES), row(ATTN_WIDTH), row(D_MODEL),
                  full(d), full(gluw),
                  full(glub), full(og), pl.BlockSpec((1,) + wout.shape[1:], lambda i: (layer, 0, 0), pipeline_mode=pl.Buffered(1)),
                  full(fg), full(wr), full(br), full(bd)],
        out_specs=[row(D_MODEL), pl.BlockSpec((n_slots, MOE_ROW), lambda i: (jnp.maximum(i - 1, 0), 0)), row(LANES),
                   pl.BlockSpec((ROUTER_SLAB, LANES), cur)],
        out_shape=[jax.ShapeDtypeStruct((n, D_MODEL), F32), jax.ShapeDtypeStruct((n_tiles * n_slots, MOE_ROW), F32),
                   jax.ShapeDtypeStruct((n, LANES), F32),
                   jax.ShapeDtypeStruct((n_tiles * ROUTER_SLAB, LANES), F32)],
        scratch_shapes=[pltpu.VMEM(wout.shape[1:], BF16), pltpu.VMEM((tm, D_MODEL), BF16), pltpu.VMEM((tm, LANES), F32),
                        pltpu.VMEM((ROUTER_SLAB, tm), F32)],
        compiler_params=pltpu.CompilerParams(dimension_semantics=("arbitrary",), vmem_limit_bytes=VMEM_LIMIT_BYTES),
        name="out_proj",
    )(oa, y0, y1, u0, u1, oc, x, d, gluw, glub, og, wout, fg, wr, br, bd)


def _router_tables(wg, bg, we, be):
    w = jnp.zeros((D_MODEL, LANES), F32)
    b = jnp.full((LANES,), NEG_INF, F32)
    w = w.at[:, 0:N_EXPERT_GROUPS].set(wg.astype(F32))
    b = b.at[0:N_EXPERT_GROUPS].set(bg.astype(F32))
    for g in range(N_EXPERT_GROUPS):
        c0 = (g + 1) * ROUTER_SLAB
        w = w.at[:, c0:c0 + EXPERTS_PER_GROUP].set(we[g].astype(F32))
        b = b.at[c0:c0 + EXPERTS_PER_GROUP].set(be[g].astype(F32))
    return w.astype(BF16), b[:, None]


def _route_plan(counts, n, tm):
    n_tiles = n // tm
    units_per_tile = (tm + MOE_PAD_ROWS) // MOE_UNIT
    units_per_mtile = tm // MOE_UNIT
    cnt = counts.reshape(n_tiles, ROUTER_SLAB, LANES)[:, :N_EXPERT_GROUPS, 0].astype(jnp.int32)
    seg = (cnt + MOE_UNIT - 1) // MOE_UNIT
    seg_start = jnp.cumsum(seg, axis=1) - seg
    before = jnp.cumsum(seg, axis=0) - seg
    total = jnp.sum(seg, axis=0)
    mtiles = (total + units_per_mtile - 1) // units_per_mtile
    mtile_end = jnp.cumsum(mtiles)
    group_start = (mtile_end - mtiles) * units_per_mtile
    n_mtiles = n_tiles + N_EXPERT_GROUPS + -(-n_tiles * MOE_PAD_ROWS // tm)
    tile_group = jnp.minimum(jnp.sum(jnp.arange(n_mtiles)[:, None] >= mtile_end[None, :], axis=1),
                             N_EXPERT_GROUPS - 1)
    meta = jnp.concatenate([tile_group, mtile_end[-1:]]).astype(jnp.int32)
    groups = jnp.arange(N_EXPERT_GROUPS)
    pick = lambda table, g: jnp.sum(jnp.where(g[..., None] == groups, table, 0), axis=-1)

    q = jnp.arange(n_mtiles * units_per_mtile)
    g = jnp.repeat(tile_group, units_per_mtile)
    ql = q - pick(group_start, g)
    ends_g = pick((before + seg)[None], g[:, None])
    tile_of = jnp.minimum(jnp.sum(ql[:, None] >= ends_g, axis=1), n_tiles - 1)
    offset_g = pick((seg_start - before)[None], g[:, None])
    local = ql + jnp.sum(jnp.where(tile_of[:, None] == jnp.arange(n_tiles), offset_g, 0), axis=1)
    valid = (ql < pick(total, g)) & (q // units_per_mtile < mtile_end[-1])
    src = jnp.where(valid, tile_of * units_per_tile + local, units_per_tile - 1).astype(jnp.int32)

    ul = jnp.arange(units_per_tile)[None, :, None]
    seg_of = jnp.sum(ul >= (seg_start + seg)[:, None, :], axis=-1)
    gi = jnp.minimum(seg_of, N_EXPERT_GROUPS - 1)
    glob = pick((group_start + before - seg_start)[:, None, :], gi) + ul[..., 0]
    dst = jnp.where(seg_of < N_EXPERT_GROUPS, glob, 0).astype(jnp.int32).reshape(-1)
    return meta, src, dst, n_mtiles


def _unit_gather(table_ref, first, src_ref, dst_ref, sem, wait):
    n_rows = dst_ref.shape[0]
    if wait:
        pltpu.make_async_copy(src_ref.at[pl.ds(0, n_rows)], dst_ref, sem).wait()
        return

    def body(k, c):
        u = pl.multiple_of(table_ref[first + k] * MOE_UNIT, MOE_UNIT)
        pltpu.make_async_copy(src_ref.at[pl.ds(u, MOE_UNIT)],
                              dst_ref.at[pl.ds(pl.multiple_of(k * MOE_UNIT, MOE_UNIT), MOE_UNIT)], sem).start()
        return c
    lax.fori_loop(0, n_rows // MOE_UNIT, body, 0, unroll=4)


def _moe_group_kernel(meta_ref, src_ref, hs_ref, wg_ref, wu_ref, wd_ref, o_ref, wgb_ref, wub_ref, wdb_ref,
                      buf_ref, sem):
    j = pl.program_id(0)
    n_steps = pl.num_programs(0)
    n_used = meta_ref[n_steps]
    group = meta_ref[j]
    n_units = o_ref.shape[0] // MOE_UNIT
    slot = j % 2

    @pl.when(j == 0)
    def _():
        _unit_gather(src_ref, 0, hs_ref, buf_ref.at[0], sem.at[0], wait=False)

    @pl.when(j + 1 < n_steps)
    def _():
        _unit_gather(src_ref, (j + 1) * n_units, hs_ref, buf_ref.at[1 - slot], sem.at[1 - slot], wait=False)

    @pl.when((j == 0) | (group != meta_ref[jnp.maximum(j - 1, 0)]))
    def _():
        wgb_ref[...] = wg_ref[...].astype(BF16)
        wub_ref[...] = wu_ref[...].astype(BF16)
        wdb_ref[...] = wd_ref[...].astype(BF16)

    _unit_gather(src_ref, j * n_units, hs_ref, buf_ref.at[slot], sem.at[slot], wait=True)

    @pl.when(j < n_used)
    def _():
        h = buf_ref[slot, :, :D_MODEL].astype(BF16)
        gates = buf_ref[slot, :, D_MODEL:]
        lane = lax.broadcasted_iota(jnp.int32, (1, LANES), 1)
        out = None
        for e in range(EXPERTS_PER_GROUP):
            gt = _dot(h, wgb_ref[e])
            up = _dot(h, wub_ref[e])
            gate = jnp.sum(jnp.where(lane == group * ROUTER_SLAB + e, gates, 0.0), axis=-1, keepdims=True)
            a = (gt * jax.nn.sigmoid(gt)) * up * gate
            part = _dot(a.astype(BF16), wdb_ref[e])
            out = part if out is None else out + part
        o_ref[...] = out

    @pl.when(j >= n_used)
    def _():
        o_ref[...] = jnp.zeros_like(o_ref)


def _moe_group_call(meta, src, hs, wg, wu, wd, layer, n_mtiles, tm):
    wg, wu, wd = (a.reshape((-1,) + a.shape[2:]) for a in (wg, wu, wd))
    by_group = lambda a: pl.BlockSpec((EXPERTS_PER_GROUP,) + a.shape[1:],
                                      lambda j, meta, src: (layer * N_EXPERT_GROUPS + meta[j], 0, 0))
    return pl.pallas_call(
        _moe_group_kernel,
        grid_spec=pltpu.PrefetchScalarGridSpec(
            num_scalar_prefetch=2,
            grid=(n_mtiles,),
            in_specs=[pl.BlockSpec(memory_space=pl.ANY), by_group(wg), by_group(wu), by_group(wd)],
            out_specs=pl.BlockSpec((tm, D_MODEL), lambda j, meta, src: (j, 0)),
            scratch_shapes=[pltpu.VMEM((EXPERTS_PER_GROUP, D_MODEL, D_EXPERT), BF16),
                            pltpu.VMEM((EXPERTS_PER_GROUP, D_MODEL, D_EXPERT), BF16),
                            pltpu.VMEM((EXPERTS_PER_GROUP, D_EXPERT, D_MODEL), BF16),
                            pltpu.VMEM((2, tm, MOE_ROW), F32), pltpu.SemaphoreType.DMA((2,))],
        ),
        out_shape=jax.ShapeDtypeStruct((n_mtiles * tm, D_MODEL), F32),
        compiler_params=pltpu.CompilerParams(dimension_semantics=("arbitrary",), vmem_limit_bytes=VMEM_LIMIT_BYTES),
        name="moe_experts",
    )(meta, src, hs, wg, wu, wd)


def _combine_kernel(dst_ref, tok_ref, x1_ref, ys_ref, o_ref, buf_ref, xbuf_ref, sem, xsem):
    i = pl.program_id(0)
    n_steps = pl.num_programs(0)
    tm = o_ref.shape[0]
    n_slots = buf_ref.shape[1]
    n_units = n_slots // MOE_UNIT
    n_ring = buf_ref.shape[0]

    def residual_copy(step, slot):
        rows = pl.ds(pl.multiple_of(step * tm, tm), tm)
        return pltpu.make_async_copy(x1_ref.at[rows], xbuf_ref.at[slot], xsem.at[slot])

    def fetch(step):
        slot = step % n_ring
        _unit_gather(dst_ref, step * n_units, ys_ref, buf_ref.at[slot], sem.at[slot], wait=False)
        residual_copy(step, slot).start()

    @pl.when(i == 0)
    def _():
        fetch(0)

    @pl.when((i == 0) & (n_steps > 1))
    def _():
        fetch(1)

    @pl.when(i + 2 < n_steps)
    def _():
        fetch(i + 2)

    slot = i % n_ring
    _unit_gather(dst_ref, i * n_units, ys_ref, buf_ref.at[slot], sem.at[slot], wait=True)
    residual_copy(i, slot).wait()

    y = buf_ref[slot]
    y_hi = y.astype(BF16)
    y_lo = (y - y_hi.astype(F32)).astype(BF16)
    unsort = _sort_matrix(tok_ref[:, SLOT_LANE:SLOT_LANE + 1].astype(jnp.int32), n_slots, 1)
    o_ref[...] = xbuf_ref[slot] + _dot(unsort, y_hi) + _dot(unsort, y_lo)


def _combine_call(dst, x1, tok, ys, tm):
    n = x1.shape[0]
    n_slots = tm + MOE_PAD_ROWS
    n_ring = 3
    return pl.pallas_call(
        _combine_kernel,
        grid_spec=pltpu.PrefetchScalarGridSpec(
            num_scalar_prefetch=1,
            grid=(n // tm,),
            in_specs=[pl.BlockSpec((tm, LANES), lambda i, dst: (i, 0)),
                      pl.BlockSpec(memory_space=pl.ANY),
                      pl.BlockSpec(memory_space=pl.ANY)],
            out_specs=pl.BlockSpec((tm, D_MODEL), lambda i, dst: (i, 0)),
            scratch_shapes=[pltpu.VMEM((n_ring, n_slots, D_MODEL), F32), pltpu.VMEM((n_ring, tm, D_MODEL), F32),
                            pltpu.SemaphoreType.DMA((n_ring,)), pltpu.SemaphoreType.DMA((n_ring,))],
        ),
        out_shape=jax.ShapeDtypeStruct((n, D_MODEL), F32),
        compiler_params=pltpu.CompilerParams(dimension_semantics=("arbitrary",), vmem_limit_bytes=VMEM_LIMIT_BYTES),
        name="moe_combine",
    )(dst, tok, x1, ys)


def kernel(x, norm_mix, w_in, sgu_norm, sgu_w, sgu_b, s5_lambda_re, s5_lambda_im, s5_log_dt, s5_b_re, s5_b_im,
           s5_c_re, s5_c_im, s5_d, s5_glu_w, s5_glu_b, q_norm, k_norm, rel_bias, out_norm, w_out, norm_ffn,
           router_group_w, router_group_b, router_expert_w, router_expert_b, w_gate, w_up, w_down):
    batch, seq, _ = x.shape
    n = batch * seq
    depth = w_in.shape[0]
    t = _tiles(n, seq)
    assert seq % ATTN_TQ == 0 and seq % (S5_T * t["s5_rows"]) == 0
    row_vec = lambda a: a.astype(F32)[None, :]

    lane_group = jnp.arange(LANES) // HEAD_DIM
    bd = (lane_group[:, None] == lane_group[None, :]).astype(BF16)
    block_chunk = jnp.arange(SGU_BLOCK) // CHUNK
    sgu_mask = block_chunk[None, :] <= block_chunk[:, None]

    xf = x.reshape(n, D_MODEL)
    for l in range(depth):
        sguw = jnp.where(sgu_mask[None], sgu_w[l], 0).astype(BF16)
        sgub = jnp.repeat(sgu_b[l].astype(F32).T, HEAD_DIM, axis=1)
        qg = row_vec(jnp.tile(q_norm[l], ATTN_HEADS)) * (HEAD_DIM ** -0.5)
        kg = row_vec(jnp.tile(k_norm[l], ATTN_HEADS))
        oa, u0, u1, q, k, v = _in_call(xf, row_vec(norm_mix[l]), w_in, l, row_vec(sgu_norm[l]), sguw, sgub,
                                      qg, kg, bd, t["tm_in"])

        wv, kb, wc, apow = _s5_tables(s5_lambda_re[l], s5_lambda_im[l], s5_log_dt[l], s5_b_re[l], s5_b_im[l],
                                        s5_c_re[l], s5_c_im[l], t["s5_rows"])
        y0, y1 = _s5_call(u0, u1, wv, kb, wc, apow, batch, t["s5_rows"])

        oc = _attn_call(q, k, v, _attn_diag(rel_bias[l]), batch, seq)

        wr, br = _router_tables(router_group_w[l], router_group_b[l], router_expert_w[l], router_expert_b[l])
        x1, hs, tok, counts = _out_call(oa, y0, y1, u0, u1, oc, xf, row_vec(s5_d[l]), s5_glu_w[l].astype(BF16), row_vec(s5_glu_b[l]),
                                 row_vec(out_norm[l]), w_out, l, row_vec(norm_ffn[l]), wr, br, bd,
                                 t["tm_out"])

        meta, src, dst, n_mtiles = _route_plan(counts, n, t["tm_out"])
        ys = _moe_group_call(meta, src, hs, w_gate, w_up, w_down, l, n_mtiles, t["tm_out"])
        xf = _combine_call(dst, x1, tok, ys, t["tm_out"])
    return xf.reshape(batch, seq, D_MODEL)
```

```python
import jax
import jax.numpy as jnp
from jax import lax
from jax.experimental import pallas as pl
from jax.experimental.pallas import tpu as pltpu

F32 = jnp.float32
BF16 = jnp.bfloat16

D_MODEL = 1024
CHUNK = 64
HEAD_DIM = 64
SGU_WIDTH = 256
SGU_HEADS = 4
SGU_BLOCK = 128
S5_WIDTH = 256
S5_GROUP = 16
S5_N_GROUPS = 16
S5_STATE = 64
ATTN_WIDTH = 512
ATTN_HEADS = 8
BAND_CHUNKS = 9
MAX_REL = 256
IN_COLS = 2 * SGU_WIDTH + S5_WIDTH + 3 * ATTN_WIDTH
OUT_NORM_GROUP = 64
N_EXPERT_GROUPS = 4
EXPERTS_PER_GROUP = 4
N_EXPERTS = 16
D_EXPERT = 256
EPS = 1e-6
NEG_INF = -1e30

LANES = 128
SUBLANES = 8
VMEM_LIMIT_BYTES = 56 * 1024 * 1024

S5_T = 16
S5_ROW = S5_T * S5_WIDTH
S5_NSTATE = S5_N_GROUPS * S5_STATE
S5_QUAD = 4
S5_SETS = 4
S5_QUADS = S5_T // S5_QUAD
S5_BLOCK = S5_QUAD * (S5_N_GROUPS // S5_SETS) * S5_GROUP
ATTN_TQ = 256
ATTN_PREV = (BAND_CHUNKS - 1) * CHUNK
ATTN_TK = ATTN_TQ + ATTN_PREV
ROUTER_SLAB = SUBLANES
MOE_ROW = D_MODEL + LANES
MOE_UNIT = SUBLANES
MOE_PAD_ROWS = N_EXPERT_GROUPS * MOE_UNIT
SLOT_LANE = N_EXPERT_GROUPS * ROUTER_SLAB


def _tiles(n_tokens, seq):
    def pick(pref, total):
        t = min(pref, total)
        assert total % t == 0
        return t
    return dict(
        tm_in=pick(512, seq),
        tm_out=pick(512, seq),
        s5_rows=pick(256, seq // S5_T),
    )


def _dot(a, b):
    return jnp.dot(a, b, preferred_element_type=F32)


def _dot_nt(a, b):
    return lax.dot_general(a, b, (((1,), (1,)), ((), ())), preferred_element_type=F32)


def _group_sumsq(x, bd_ref):
    x2 = (x * x).astype(BF16)
    parts = [_dot(x2[:, t * LANES:(t + 1) * LANES], bd_ref[...]) for t in range(x.shape[1] // LANES)]
    return jnp.concatenate(parts, axis=-1)


def _in_kernel(x_ref, g_ref, wf_ref, sgug_ref, sguw_ref, sgub_ref, qg_ref, kg_ref, bd_ref,
               oa_ref, u0_ref, u1_ref, q_ref, k_ref, v_ref, w_ref, ha_ref, hb_ref):
    tm = x_ref.shape[0]

    @pl.when(pl.program_id(0) == 0)
    def _():
        w_ref[...] = wf_ref[0].astype(BF16)
        hb_ref[...] = jnp.zeros_like(hb_ref)

    def project(h_ref):
        x = x_ref[...]
        ms = jnp.mean(x * x, axis=-1, keepdims=True)
        hn = (x * lax.rsqrt(ms + EPS) * g_ref[...]).astype(BF16)
        c0 = 0
        for width in (2 * SGU_WIDTH, S5_WIDTH, ATTN_WIDTH, ATTN_WIDTH, ATTN_WIDTH):
            h_ref[:, c0:c0 + width] = _dot(hn, w_ref[:, c0:c0 + width])
            c0 += width

    def mix(h_ref):
        z = jax.nn.gelu(h_ref[:, 0:2 * SGU_WIDTH])
        u = z[:, :SGU_WIDTH]
        v = z[:, SGU_WIDTH:]
        v = v * lax.rsqrt(jnp.mean(v * v, axis=-1, keepdims=True) + EPS) * sgug_ref[...]
        vb = v.astype(BF16)
        first_head = lax.broadcasted_iota(jnp.int32, (SGU_BLOCK, LANES), 1) < HEAD_DIM
        for r in range(tm // SGU_BLOCK):
            rows = slice(r * SGU_BLOCK, (r + 1) * SGU_BLOCK)
            for p in range(SGU_WIDTH // LANES):
                cols = slice(p * LANES, (p + 1) * LANES)
                vp = vb[rows, cols]
                mixed = jnp.where(first_head, _dot(sguw_ref[2 * p], vp), _dot(sguw_ref[2 * p + 1], vp))
                oa_ref[rows, cols] = u[rows, cols] * (mixed + sgub_ref[:, cols])

        c0 = 2 * SGU_WIDTH
        u0_ref[...] = h_ref[:, c0:c0 + LANES]
        u1_ref[...] = h_ref[:, c0 + LANES:c0 + 2 * LANES]

        c0 += S5_WIDTH
        q = h_ref[:, c0:c0 + ATTN_WIDTH]
        q_ref[...] = (q * lax.rsqrt(_group_sumsq(q, bd_ref) * (1.0 / HEAD_DIM) + EPS) * qg_ref[...]).astype(BF16)
        c0 += ATTN_WIDTH
        k = h_ref[:, c0:c0 + ATTN_WIDTH]
        k_ref[...] = (k * lax.rsqrt(_group_sumsq(k, bd_ref) * (1.0 / HEAD_DIM) + EPS) * kg_ref[...]).astype(BF16)
        c0 += ATTN_WIDTH
        v_ref[...] = h_ref[:, c0:c0 + ATTN_WIDTH].astype(BF16)

    parity = pl.program_id(0) % 2

    @pl.when(parity == 0)
    def _():
        project(ha_ref)
        mix(hb_ref)

    @pl.when(parity == 1)
    def _():
        project(hb_ref)
        mix(ha_ref)


def _in_call(x, g, w, layer, sgug, sguw, sgub, qg, kg, bd, tm):
    n = x.shape[0]
    n_tiles = n // tm
    row = lambda c: pl.BlockSpec((tm, c), lambda i: (jnp.maximum(i - 1, 0), 0))
    full = lambda a: pl.BlockSpec(a.shape, lambda i: (0,) * a.ndim)
    w_layer = pl.BlockSpec((1,) + w.shape[1:], lambda i: (layer, 0, 0), pipeline_mode=pl.Buffered(1))
    return pl.pallas_call(
        _in_kernel,
        grid=(n_tiles + 1,),
        in_specs=[pl.BlockSpec((tm, D_MODEL), lambda i: (jnp.minimum(i, n_tiles - 1), 0)), full(g), w_layer, full(sgug),
                  full(sguw), full(sgub), full(qg), full(kg), full(bd)],
        out_specs=[row(SGU_WIDTH), row(LANES), row(LANES), row(ATTN_WIDTH), row(ATTN_WIDTH), row(ATTN_WIDTH)],
        out_shape=[jax.ShapeDtypeStruct((n, SGU_WIDTH), F32), jax.ShapeDtypeStruct((n, LANES), F32),
                   jax.ShapeDtypeStruct((n, LANES), F32), jax.ShapeDtypeStruct((n, ATTN_WIDTH), BF16),
                   jax.ShapeDtypeStruct((n, ATTN_WIDTH), BF16), jax.ShapeDtypeStruct((n, ATTN_WIDTH), BF16)],
        scratch_shapes=[pltpu.VMEM(w.shape[1:], BF16), pltpu.VMEM((tm, w.shape[2]), F32), pltpu.VMEM((tm, w.shape[2]), F32)],
        compiler_params=pltpu.CompilerParams(dimension_semantics=("arbitrary",), vmem_limit_bytes=VMEM_LIMIT_BYTES),
        name="in_proj",
    )(x, g, w, sgug, sguw, sgub, qg, kg, bd)


def _s5_kernel(u0_ref, u1_ref, wv_ref, kb_ref, wc_ref, apow_ref, y0_ref, y1_ref, ur_ref, s_ref, sp_ref, yb_ref, carry_ref):
    rows = ur_ref.shape[0]
    n_steps = apow_ref.shape[0]
    half = LANES // 2
    low = lax.broadcasted_iota(jnp.int32, (1, LANES), 1) < half
    block = lambda tq, gs: slice((tq * S5_SETS + gs) * S5_BLOCK, (tq * S5_SETS + gs + 1) * S5_BLOCK)

    @pl.when(pl.program_id(1) == 0)
    def _():
        carry_ref[...] = jnp.zeros_like(carry_ref)

    for tq in range(S5_QUADS):
        for lt in range(S5_QUAD // 2):
            t0 = tq * S5_QUAD + 2 * lt
            for pair, u_ref in enumerate((u0_ref, u1_ref)):
                a = u_ref[pl.ds(t0, rows, stride=S5_T), :]
                b = u_ref[pl.ds(t0 + 1, rows, stride=S5_T), :]
                for par in range(2):
                    tile = jnp.where(low, a, pltpu.roll(b, half, 1)) if par == 0 else jnp.where(low, pltpu.roll(a, half, 1), b)
                    col = block(tq, 2 * pair + par).start + lt * LANES
                    ur_ref[:, col:col + LANES] = tile.astype(BF16)

    for gs in range(S5_SETS):
        v = None
        for tq in range(S5_QUADS):
            part = _dot(ur_ref[:, block(tq, gs)], wv_ref[tq * S5_SETS + gs])
            v = part if v is None else v + part
        s_ref[:, gs * S5_BLOCK:(gs + 1) * S5_BLOCK] = v[:, :S5_BLOCK]
        s_ref[:, S5_NSTATE + gs * S5_BLOCK:S5_NSTATE + (gs + 1) * S5_BLOCK] = v[:, S5_BLOCK:]

    row_id = lax.broadcasted_iota(jnp.int32, (rows, LANES), 0)
    for cb in range(S5_NSTATE // LANES):
        cre = slice(cb * LANES, (cb + 1) * LANES)
        cim = slice(S5_NSTATE + cb * LANES, S5_NSTATE + (cb + 1) * LANES)
        re = s_ref[:, cre]
        im = s_ref[:, cim]
        c_re = carry_ref[0:1, cre]
        c_im = carry_ref[0:1, cim]
        a_re = apow_ref[0, 0:1, cre]
        a_im = apow_ref[0, 1:2, cre]
        re = re + jnp.where(row_id == 0, a_re * c_re - a_im * c_im, 0.0)
        im = im + jnp.where(row_id == 0, a_re * c_im + a_im * c_re, 0.0)
        for k in range(n_steps):
            shift = 1 << k
            a_re = apow_ref[k, 0:1, cre]
            a_im = apow_ref[k, 1:2, cre]
            re_s = jnp.where(row_id >= shift, pltpu.roll(re, shift, 0), 0.0)
            im_s = jnp.where(row_id >= shift, pltpu.roll(im, shift, 0), 0.0)
            re, im = re + a_re * re_s - a_im * im_s, im + a_re * im_s + a_im * re_s
        sp_ref[:, cre] = jnp.where(row_id >= 1, pltpu.roll(re, 1, 0), c_re).astype(BF16)
        sp_ref[:, cim] = jnp.where(row_id >= 1, pltpu.roll(im, 1, 0), c_im).astype(BF16)
        carry_ref[0:1, cre] = re[rows - 1:rows, :]
        carry_ref[0:1, cim] = im[rows - 1:rows, :]

    for gs in range(S5_SETS):
        state = jnp.concatenate([sp_ref[:, gs * S5_BLOCK:(gs + 1) * S5_BLOCK],
                                 sp_ref[:, S5_NSTATE + gs * S5_BLOCK:S5_NSTATE + (gs + 1) * S5_BLOCK]], axis=1)
        for tq in range(S5_QUADS):
            y = _dot(state, wc_ref[tq * S5_SETS + gs])
            for tj in range(tq + 1):
                y = y + _dot(ur_ref[:, block(tj, gs)], kb_ref[(tq - tj) * S5_SETS + gs])
            yb_ref[:, block(tq, gs)] = y

    for tq in range(S5_QUADS):
        for lt in range(S5_QUAD // 2):
            t0 = tq * S5_QUAD + 2 * lt
            for pair, y_ref in enumerate((y0_ref, y1_ref)):
                a = yb_ref[:, block(tq, 2 * pair).start + lt * LANES:block(tq, 2 * pair).start + (lt + 1) * LANES]
                b = yb_ref[:, block(tq, 2 * pair + 1).start + lt * LANES:block(tq, 2 * pair + 1).start + (lt + 1) * LANES]
                y_ref[pl.ds(t0, rows, stride=S5_T), :] = jnp.where(low, a, pltpu.roll(b, half, 1))
                y_ref[pl.ds(t0 + 1, rows, stride=S5_T), :] = jnp.where(low, pltpu.roll(a, half, 1), b)


def _s5_call(u0, u1, wv, kb, wc, apow, batch, rows):
    n = u0.shape[0]
    tok = rows * S5_T
    tiles_per_seq = n // batch // tok
    const = lambda a: pl.BlockSpec(a.shape, lambda b, t: (0,) * a.ndim, pipeline_mode=pl.Buffered(1))
    blk = pl.BlockSpec((tok, LANES), lambda b, t: (b * tiles_per_seq + t, 0))
    half = jax.ShapeDtypeStruct((n, LANES), F32)
    return pl.pallas_call(
        _s5_kernel,
        grid=(batch, tiles_per_seq),
        in_specs=[blk, blk, const(wv), const(kb), const(wc), const(apow)],
        out_specs=[blk, blk],
        out_shape=[half, half],
        scratch_shapes=[pltpu.VMEM((rows, S5_ROW), BF16), pltpu.VMEM((rows, 2 * S5_NSTATE), F32),
                        pltpu.VMEM((rows, 2 * S5_NSTATE), BF16), pltpu.VMEM((rows, S5_ROW), F32),
                        pltpu.VMEM((SUBLANES, 2 * S5_NSTATE), F32)],
        compiler_params=pltpu.CompilerParams(dimension_semantics=("arbitrary", "arbitrary"),
                                             vmem_limit_bytes=VMEM_LIMIT_BYTES),
        name="s5_scan",
    )(u0, u1, wv, kb, wc, apow)


def _s5_tables(lam_re, lam_im, log_dt, b_re, b_im, c_re, c_im, rows):
    p_, h_ = S5_STATE, S5_GROUP
    q_, s_, m_ = S5_QUADS, S5_SETS, S5_N_GROUPS // S5_SETS
    hi = lax.Precision.HIGHEST
    lam_re, lam_im = lam_re.astype(F32), lam_im.astype(F32)
    dt = jnp.exp(log_dt.astype(F32))[:, None]
    tau = jnp.arange(S5_T + 1, dtype=F32)[:, None, None]
    mag = jnp.exp(tau * (lam_re * dt)[None])
    ap_re = mag * jnp.cos(tau * (lam_im * dt)[None])
    ap_im = mag * jnp.sin(tau * (lam_im * dt)[None])
    n_re, n_im = ap_re[1] - 1.0, ap_im[1]
    den = lam_re * lam_re + lam_im * lam_im
    k_re = ((n_re * lam_re + n_im * lam_im) / den)[..., None]
    k_im = ((n_im * lam_re - n_re * lam_im) / den)[..., None]
    b_re, b_im = b_re.astype(F32), b_im.astype(F32)
    bb_re = k_re * b_re - k_im * b_im
    bb_im = k_re * b_im + k_im * b_re
    c_re, c_im = c_re.astype(F32), c_im.astype(F32)
    ab_re = ap_re[:S5_T, :, :, None] * bb_re[None] - ap_im[:S5_T, :, :, None] * bb_im[None]
    ab_im = ap_re[:S5_T, :, :, None] * bb_im[None] + ap_im[:S5_T, :, :, None] * bb_re[None]

    def spread(base, col_of_base):
        cols = jnp.arange(S5_BLOCK)
        expand = (col_of_base(cols)[None, :] == jnp.arange(base.shape[1])[:, None]).astype(F32)
        return jnp.dot(base, expand, precision=hi)
    state_col = lambda cols: cols % p_
    chan_col = lambda cols: cols // (m_ * h_) * h_ + cols % h_
    row_group = jnp.arange(S5_BLOCK) // h_ % m_
    keep_state = (row_group[:, None] == (jnp.arange(S5_BLOCK) // p_)[None, :])[None]
    keep_chan = (row_group[:, None] == (jnp.arange(S5_BLOCK) // h_ % m_)[None, :])[None]
    keep_out = ((jnp.arange(S5_BLOCK) // p_)[:, None] == (jnp.arange(S5_BLOCK) // h_ % m_)[None, :])[None]

    def wv_part(ab):
        base = ab[::-1].reshape(q_, S5_QUAD, s_, m_, p_, h_).transpose(0, 2, 1, 3, 5, 4).reshape(-1, p_)
        return jnp.where(keep_state, spread(base, state_col).reshape(q_ * s_, S5_BLOCK, S5_BLOCK), 0.0)
    wv = jnp.concatenate([wv_part(ab_re), wv_part(ab_im)], axis=-1)

    k_tau = (jnp.einsum('gop,tgpi->tgio', c_re, ab_re, precision=hi)
             - jnp.einsum('gop,tgpi->tgio', c_im, ab_im, precision=hi))
    zero = jnp.zeros_like(k_tau[0])
    taps = jnp.stack([jnp.stack([jnp.stack([k_tau[S5_QUAD * dq + ti - tj] if S5_QUAD * dq + ti - tj >= 0 else zero
                                            for ti in range(S5_QUAD)]) for tj in range(S5_QUAD)])
                      for dq in range(q_)])
    base = taps.reshape(q_, S5_QUAD, S5_QUAD, s_, m_, h_, h_).transpose(0, 3, 1, 4, 5, 2, 6).reshape(-1, S5_QUAD * h_)
    kb = jnp.where(keep_chan, spread(base, chan_col).reshape(q_ * s_, S5_BLOCK, S5_BLOCK), 0.0)

    cp = lambda c: c.transpose(0, 2, 1)[:, :, None, :]
    w_re = cp(c_re) * ap_re[1:].transpose(1, 2, 0)[..., None] - cp(c_im) * ap_im[1:].transpose(1, 2, 0)[..., None]
    w_im = cp(c_re) * ap_im[1:].transpose(1, 2, 0)[..., None] + cp(c_im) * ap_re[1:].transpose(1, 2, 0)[..., None]
    def wc_part(w):
        base = w.reshape(s_, m_, p_, q_, S5_QUAD, h_).transpose(3, 0, 1, 2, 4, 5).reshape(-1, S5_QUAD * h_)
        return jnp.where(keep_out, spread(base, chan_col).reshape(q_ * s_, S5_BLOCK, S5_BLOCK), 0.0)
    wc = jnp.concatenate([wc_part(w_re), wc_part(-w_im)], axis=1)

    n_steps = max(1, (rows - 1).bit_length())
    pows = [(ap_re[S5_T].reshape(S5_NSTATE), ap_im[S5_T].reshape(S5_NSTATE))]
    for _ in range(n_steps - 1):
        r, i = pows[-1]
        pows.append((r * r - i * i, 2.0 * r * i))
    apow = jnp.stack([jnp.stack(p) for p in pows])
    return wv.astype(BF16), kb.astype(BF16), wc.astype(BF16), apow


def _attn_fill_bias(diag_ref, bias_ref):
    width = diag_ref.shape[1]
    qi = lax.broadcasted_iota(jnp.int32, (ATTN_TQ, ATTN_TK), 0)
    kj = lax.broadcasted_iota(jnp.int32, (ATTN_TQ, ATTN_TK), 1)
    q_chunk = qi // CHUNK + (BAND_CHUNKS - 1)
    k_chunk = kj // CHUNK
    in_band = (k_chunk <= q_chunk) & (k_chunk >= q_chunk - (BAND_CHUNKS - 1))
    for h in range(ATTN_HEADS):
        rows = jnp.broadcast_to(diag_ref[h:h + 1, :], (ATTN_TQ, width))
        base = pltpu.roll(rows, 0, 1, stride=1, stride_axis=0)[:, :ATTN_TK]
        for t in range(bias_ref.shape[0]):
            bias_ref[t, h] = jnp.where(in_band & (kj >= ATTN_PREV - t * ATTN_TQ), base, NEG_INF)


def _attn_kernel(q_ref, k0_ref, k1_ref, k2_ref, v0_ref, v1_ref, v2_ref, diag_ref, o_ref, bias_ref):
    @pl.when((pl.program_id(0) == 0) & (pl.program_id(1) == 0))
    def _():
        _attn_fill_bias(diag_ref, bias_ref)

    table = jnp.minimum(pl.program_id(1), bias_ref.shape[0] - 1)
    lane = lax.broadcasted_iota(jnp.int32, (1, LANES), 1)
    for p in range(ATTN_WIDTH // LANES):
        cols = slice(p * LANES, (p + 1) * LANES)
        qp = q_ref[:, cols]
        kcat = jnp.concatenate([k0_ref[:, cols], k1_ref[:, cols], k2_ref[:, cols]], axis=0)
        vcat = jnp.concatenate([v0_ref[:, cols], v1_ref[:, cols], v2_ref[:, cols]], axis=0)
        first = lane < HEAD_DIM
        zero = jnp.zeros_like(qp)
        qm = jnp.concatenate([jnp.where(first, qp, zero), jnp.where(first, zero, qp)], axis=0)
        bias = jnp.concatenate([bias_ref[table, 2 * p], bias_ref[table, 2 * p + 1]], axis=0)
        s = _dot_nt(qm, kcat) + bias
        e = jnp.exp(s - jnp.max(s, axis=-1, keepdims=True)).astype(BF16)
        o = _dot(e, jnp.concatenate([vcat, jnp.ones_like(vcat)], axis=1))
        r = o[:, :LANES] / o[:, LANES:]
        o_ref[:, cols] = jnp.where(first, r[:ATTN_TQ], r[ATTN_TQ:])


def _attn_call(q, k, v, diag, batch, seq):
    n = q.shape[0]
    tiles = seq // ATTN_TQ
    n_prev = ATTN_PREV // ATTN_TQ
    cur = pl.BlockSpec((ATTN_TQ, ATTN_WIDTH), lambda b, t: (b * tiles + t, 0))
    prev = lambda d: pl.BlockSpec((ATTN_TQ, ATTN_WIDTH), lambda b, t: (b * tiles + jnp.maximum(t - d, 0), 0))
    assert n_prev == 2
    return pl.pallas_call(
        _attn_kernel,
        grid=(batch, tiles),
        in_specs=[cur, prev(2), prev(1), cur, prev(2), prev(1), cur, pl.BlockSpec(diag.shape, lambda b, t: (0, 0))],
        out_specs=cur,
        out_shape=jax.ShapeDtypeStruct((n, ATTN_WIDTH), F32),
        scratch_shapes=[pltpu.VMEM((n_prev + 1, ATTN_HEADS, ATTN_TQ, ATTN_TK), F32)],
        compiler_params=pltpu.CompilerParams(dimension_semantics=("arbitrary", "arbitrary"),
                                             vmem_limit_bytes=VMEM_LIMIT_BYTES),
        name="band_attn",
    )(q, k, k, k, v, v, v, diag)


def _attn_diag(rel_bias):
    width = ATTN_TQ + ATTN_TK
    n = jnp.arange(width)
    offset = jnp.where(n < ATTN_TK, n, n - width)
    rel = jnp.clip(ATTN_PREV - offset, -MAX_REL, MAX_REL) + MAX_REL
    return jnp.take(rel_bias.astype(F32), rel, axis=1)


def _router_gates(lt):
    tm = lt.shape[1]
    row = lax.broadcasted_iota(jnp.int32, (ROUTER_SLAB, tm), 0)
    gl = lt[0:ROUTER_SLAB]
    gmax = jnp.max(gl, axis=0, keepdims=True)
    p_g = 1.0 / jnp.sum(jnp.exp(gl - gmax), axis=0, keepdims=True)
    g_top = jnp.min(jnp.where(gl == gmax, row, ROUTER_SLAB), axis=0, keepdims=True)
    el = jnp.zeros((ROUTER_SLAB, tm), F32)
    for g in range(N_EXPERT_GROUPS):
        el = el + jnp.where(g_top == g, lt[(g + 1) * ROUTER_SLAB:(g + 2) * ROUTER_SLAB], 0.0)
    ee = jnp.exp(el - jnp.max(el, axis=0, keepdims=True))
    ep = ee / jnp.sum(ee, axis=0, keepdims=True)
    p1 = jnp.max(ep, axis=0, keepdims=True)
    i1 = jnp.min(jnp.where(ep == p1, row, ROUTER_SLAB), axis=0, keepdims=True)
    rest = jnp.where(row == i1, -1.0, ep)
    p2 = jnp.max(rest, axis=0, keepdims=True)
    i2 = jnp.min(jnp.where(rest == p2, row, ROUTER_SLAB), axis=0, keepdims=True)
    tot = p1 + p2
    w = (jnp.where(row == i1, p1 / tot, 0.0) + jnp.where(row == i2, p2 / tot, 0.0)) * p_g
    return [jnp.where(g_top == g, w, 0.0) for g in range(N_EXPERT_GROUPS)], g_top


def _sort_matrix(slot, n_slots, slot_axis):
    shape = (n_slots, slot.shape[1]) if slot_axis == 0 else (slot.shape[0], n_slots)
    return jnp.where(lax.broadcasted_iota(jnp.int32, shape, slot_axis) == slot, 1.0, 0.0).astype(BF16)


def _out_kernel(oa_ref, y0_ref, y1_ref, u0_ref, u1_ref, oc_ref, x_ref, d_ref, gluw_ref, glub_ref, og_ref, woutf_ref,
                fg_ref, wr_ref, br_ref, bd_ref, x1_ref, hs_ref, tok_ref, cnt_ref, wout_ref, hb_ref, gt_ref, slot_ref,
                ocb_ref, xb_ref, ring_sem):
    tm = x1_ref.shape[0]
    n_slots = hs_ref.shape[0]
    step_id = pl.program_id(0)
    n_steps = pl.num_programs(0)
    n_ring = xb_ref.shape[0]

    def ring_copies(step):
        s = step % n_ring
        rows = pl.ds(pl.multiple_of(jnp.minimum(step, n_steps - 2) * tm, tm), tm)
        return (pltpu.make_async_copy(oc_ref.at[rows], ocb_ref.at[s], ring_sem.at[0, s]),
                pltpu.make_async_copy(x_ref.at[rows], xb_ref.at[s], ring_sem.at[1, s]))

    def fetch(step):
        for c in ring_copies(step):
            c.start()

    @pl.when(step_id == 0)
    def _():
        fetch(0)
        fetch(1)
        wout_ref[...] = woutf_ref[0].astype(BF16)
        hb_ref[...] = jnp.zeros_like(hb_ref)
        gt_ref[...] = jnp.zeros_like(gt_ref)
        slot_ref[...] = jnp.zeros_like(slot_ref)

    @pl.when(step_id + 2 < n_steps)
    def _():
        fetch(step_id + 2)

    for c in ring_copies(step_id):
        c.wait()
    ring_slot = step_id % n_ring

    y = jnp.concatenate([y0_ref[...], y1_ref[...]], axis=-1)
    u = jnp.concatenate([u0_ref[...], u1_ref[...]], axis=-1)
    y = jax.nn.gelu(y + d_ref[...] * u)
    ob = y * jax.nn.sigmoid(_dot(y.astype(BF16), gluw_ref[...]) + glub_ref[...])
    o = jnp.concatenate([oa_ref[...], ob, ocb_ref[ring_slot]], axis=-1)
    on = o * lax.rsqrt(_group_sumsq(o, bd_ref) * (1.0 / OUT_NORM_GROUP) + EPS) * og_ref[...]
    x1 = xb_ref[ring_slot] + _dot(on.astype(BF16), wout_ref[...])
    x1_ref[...] = x1
    ms = jnp.mean(x1 * x1, axis=-1, keepdims=True)
    hb = (x1 * lax.rsqrt(ms + EPS) * fg_ref[...]).astype(BF16)
    lt = _dot(hb, wr_ref[...]).T + br_ref[...]
    slabs, g_top = _router_gates(lt)

    row = lax.broadcasted_iota(jnp.int32, (ROUTER_SLAB, tm), 0)
    lane = lax.broadcasted_iota(jnp.int32, (ROUTER_SLAB, tm), 1)
    member = jnp.where(row == g_top, 1.0, 0.0)
    cum = member
    shift = 1
    while shift < tm:
        cum = cum + jnp.where(lane >= shift, pltpu.roll(cum, shift, 1), 0.0)
        shift *= 2
    count = cum[:, tm - 1:tm]
    padded = jnp.floor((count + (MOE_UNIT - 1)) * (1.0 / MOE_UNIT)) * MOE_UNIT
    group_row = lax.broadcasted_iota(jnp.int32, (ROUTER_SLAB, 1), 0)
    start = jnp.zeros((ROUTER_SLAB, 1), F32)
    for g in range(1, N_EXPERT_GROUPS):
        start = start + jnp.where(group_row >= g, padded[g - 1:g, :], 0.0)
    slot = jnp.sum(member * (start + cum - 1.0), axis=0, keepdims=True)
    cnt_ref[...] = jnp.broadcast_to(count, cnt_ref.shape)

    pad = jnp.zeros((LANES - SLOT_LANE - ROUTER_SLAB, tm), F32)
    gates = jnp.concatenate(slabs + [jnp.broadcast_to(slot, (ROUTER_SLAB, tm)), pad], axis=0).T
    tok_ref[...] = gates

    gates_p = gt_ref[...]
    perm = _sort_matrix(slot_ref[0:1, :].astype(jnp.int32), n_slots, 0)
    hs_ref[:, :D_MODEL] = _dot(perm, hb_ref[...])
    g_hi = gates_p.astype(BF16)
    g_mid = (gates_p - g_hi.astype(F32)).astype(BF16)
    g_lo = (gates_p - g_hi.astype(F32) - g_mid.astype(F32)).astype(BF16)
    gs = _dot(perm, jnp.concatenate([g_hi, g_mid, g_lo], axis=-1))
    hs_ref[:, D_MODEL:] = gs[:, :LANES] + gs[:, LANES:2 * LANES] + gs[:, 2 * LANES:]

    hb_ref[...] = hb
    gt_ref[...] = gates
    slot_ref[...] = jnp.broadcast_to(slot, slot_ref.shape)


def _out_call(oa, y0, y1, u0, u1, oc, x, d, gluw, glub, og, wout, layer, fg, wr, br, bd, tm):
    n = x.shape[0]
    n_tiles = n // tm
    n_slots = tm + MOE_PAD_ROWS
    cur = lambda i: (jnp.minimum(i, n_tiles - 1), 0)
    row = lambda c: pl.BlockSpec((tm, c), cur)
    full = lambda a: pl.BlockSpec(a.shape, lambda i: (0,) * a.ndim)
    return pl.pallas_call(
        _out_kernel,
        grid=(n_tiles + 1,),
        in_specs=[row(SGU_WIDTH), row(LANES), row(LANES), row(LANES), row(LANES), pl.BlockSpec(memory_space=pl.ANY),
                  pl.BlockSpec(memory_space=pl.ANY), full(d), full(gluw),
                  full(glub), full(og), pl.BlockSpec((1,) + wout.shape[1:], lambda i: (layer, 0, 0), pipeline_mode=pl.Buffered(1)),
                  full(fg), full(wr), full(br), full(bd)],
        out_specs=[row(D_MODEL), pl.BlockSpec((n_slots, MOE_ROW), lambda i: (jnp.maximum(i - 1, 0), 0)), row(LANES),
                   pl.BlockSpec((ROUTER_SLAB, LANES), cur)],
        out_shape=[jax.ShapeDtypeStruct((n, D_MODEL), F32), jax.ShapeDtypeStruct((n_tiles * n_slots, MOE_ROW), F32),
                   jax.ShapeDtypeStruct((n, LANES), F32),
                   jax.ShapeDtypeStruct((n_tiles * ROUTER_SLAB, LANES), F32)],
        scratch_shapes=[pltpu.VMEM(wout.shape[1:], BF16), pltpu.VMEM((tm, D_MODEL), BF16), pltpu.VMEM((tm, LANES), F32),
                        pltpu.VMEM((ROUTER_SLAB, tm), F32), pltpu.VMEM((3, tm, ATTN_WIDTH), F32),
                        pltpu.VMEM((3, tm, D_MODEL), F32), pltpu.SemaphoreType.DMA((2, 3))],
        compiler_params=pltpu.CompilerParams(dimension_semantics=("arbitrary",), vmem_limit_bytes=VMEM_LIMIT_BYTES),
        name="out_proj",
    )(oa, y0, y1, u0, u1, oc, x, d, gluw, glub, og, wout, fg, wr, br, bd)


def _router_tables(wg, bg, we, be):
    w = jnp.zeros((D_MODEL, LANES), F32)
    b = jnp.full((LANES,), NEG_INF, F32)
    w = w.at[:, 0:N_EXPERT_GROUPS].set(wg.astype(F32))
    b = b.at[0:N_EXPERT_GROUPS].set(bg.astype(F32))
    for g in range(N_EXPERT_GROUPS):
        c0 = (g + 1) * ROUTER_SLAB
        w = w.at[:, c0:c0 + EXPERTS_PER_GROUP].set(we[g].astype(F32))
        b = b.at[c0:c0 + EXPERTS_PER_GROUP].set(be[g].astype(F32))
    return w.astype(BF16), b[:, None]


def _route_plan(counts, n, tm):
    n_tiles = n // tm
    units_per_tile = (tm + MOE_PAD_ROWS) // MOE_UNIT
    units_per_mtile = tm // MOE_UNIT
    cnt = counts.reshape(n_tiles, ROUTER_SLAB, LANES)[:, :N_EXPERT_GROUPS, 0].astype(jnp.int32)
    seg = (cnt + MOE_UNIT - 1) // MOE_UNIT
    seg_start = jnp.cumsum(seg, axis=1) - seg
    before = jnp.cumsum(seg, axis=0) - seg
    total = jnp.sum(seg, axis=0)
    mtiles = (total + units_per_mtile - 1) // units_per_mtile
    mtile_end = jnp.cumsum(mtiles)
    group_start = (mtile_end - mtiles) * units_per_mtile
    n_mtiles = n_tiles + N_EXPERT_GROUPS + -(-n_tiles * MOE_PAD_ROWS // tm)
    tile_group = jnp.minimum(jnp.sum(jnp.arange(n_mtiles)[:, None] >= mtile_end[None, :], axis=1),
                             N_EXPERT_GROUPS - 1)
    meta = jnp.concatenate([tile_group, mtile_end[-1:]]).astype(jnp.int32)
    groups = jnp.arange(N_EXPERT_GROUPS)
    pick = lambda table, g: jnp.sum(jnp.where(g[..., None] == groups, table, 0), axis=-1)

    q = jnp.arange(n_mtiles * units_per_mtile)
    g = jnp.repeat(tile_group, units_per_mtile)
    ql = q - pick(group_start, g)
    ends_g = pick((before + seg)[None], g[:, None])
    tile_of = jnp.minimum(jnp.sum(ql[:, None] >= ends_g, axis=1), n_tiles - 1)
    offset_g = pick((seg_start - before)[None], g[:, None])
    local = ql + jnp.sum(jnp.where(tile_of[:, None] == jnp.arange(n_tiles), offset_g, 0), axis=1)
    valid = (ql < pick(total, g)) & (q // units_per_mtile < mtile_end[-1])
    src = jnp.where(valid, tile_of * units_per_tile + local, units_per_tile - 1).astype(jnp.int32)

    ul = jnp.arange(units_per_tile)[None, :, None]
    seg_of = jnp.sum(ul >= (seg_start + seg)[:, None, :], axis=-1)
    gi = jnp.minimum(seg_of, N_EXPERT_GROUPS - 1)
    glob = pick((group_start + before - seg_start)[:, None, :], gi) + ul[..., 0]
    dst = jnp.where(seg_of < N_EXPERT_GROUPS, glob, 0).astype(jnp.int32).reshape(-1)
    return meta, src, dst, n_mtiles


def _unit_gather(table_ref, first, src_ref, dst_ref, sem, wait):
    n_rows = dst_ref.shape[0]
    if wait:
        pltpu.make_async_copy(src_ref.at[pl.ds(0, n_rows)], dst_ref, sem).wait()
        return

    def body(k, c):
        u = pl.multiple_of(table_ref[first + k] * MOE_UNIT, MOE_UNIT)
        pltpu.make_async_copy(src_ref.at[pl.ds(u, MOE_UNIT)],
                              dst_ref.at[pl.ds(pl.multiple_of(k * MOE_UNIT, MOE_UNIT), MOE_UNIT)], sem).start()
        return c
    lax.fori_loop(0, n_rows // MOE_UNIT, body, 0, unroll=4)


def _moe_group_kernel(meta_ref, src_ref, hs_ref, wg_ref, wu_ref, wd_ref, o_ref, wgb_ref, wub_ref, wdb_ref,
                      buf_ref, sem):
    j = pl.program_id(0)
    n_steps = pl.num_programs(0)
    n_used = meta_ref[n_steps]
    group = meta_ref[j]
    n_units = o_ref.shape[0] // MOE_UNIT
    n_ring = buf_ref.shape[0]
    slot = j % n_ring

    def fetch(step):
        s = step % n_ring
        _unit_gather(src_ref, step * n_units, hs_ref, buf_ref.at[s], sem.at[s], wait=False)

    @pl.when(j == 0)
    def _():
        fetch(0)

    @pl.when((j == 0) & (n_steps > 1))
    def _():
        fetch(1)

    @pl.when(j + 2 < n_steps)
    def _():
        fetch(j + 2)

    @pl.when((j == 0) | (group != meta_ref[jnp.maximum(j - 1, 0)]))
    def _():
        wgb_ref[...] = wg_ref[...].astype(BF16)
        wub_ref[...] = wu_ref[...].astype(BF16)
        wdb_ref[...] = wd_ref[...].astype(BF16)

    _unit_gather(src_ref, j * n_units, hs_ref, buf_ref.at[slot], sem.at[slot], wait=True)

    @pl.when(j < n_used)
    def _():
        h = buf_ref[slot, :, :D_MODEL].astype(BF16)
        gates = buf_ref[slot, :, D_MODEL:]
        lane = lax.broadcasted_iota(jnp.int32, (1, LANES), 1)
        out = None
        for e in range(EXPERTS_PER_GROUP):
            gt = _dot(h, wgb_ref[e])
            up = _dot(h, wub_ref[e])
            gate = jnp.sum(jnp.where(lane == group * ROUTER_SLAB + e, gates, 0.0), axis=-1, keepdims=True)
            a = (gt * jax.nn.sigmoid(gt)) * up * gate
            part = _dot(a.astype(BF16), wdb_ref[e])
            out = part if out is None else out + part
        o_ref[...] = out

    @pl.when(j >= n_used)
    def _():
        o_ref[...] = jnp.zeros_like(o_ref)


def _moe_group_call(meta, src, hs, wg, wu, wd, layer, n_mtiles, tm):
    wg, wu, wd = (a.reshape((-1,) + a.shape[2:]) for a in (wg, wu, wd))
    by_group = lambda a: pl.BlockSpec((EXPERTS_PER_GROUP,) + a.shape[1:],
                                      lambda j, meta, src: (layer * N_EXPERT_GROUPS + meta[j], 0, 0))
    return pl.pallas_call(
        _moe_group_kernel,
        grid_spec=pltpu.PrefetchScalarGridSpec(
            num_scalar_prefetch=2,
            grid=(n_mtiles,),
            in_specs=[pl.BlockSpec(memory_space=pl.ANY), by_group(wg), by_group(wu), by_group(wd)],
            out_specs=pl.BlockSpec((tm, D_MODEL), lambda j, meta, src: (j, 0)),
            scratch_shapes=[pltpu.VMEM((EXPERTS_PER_GROUP, D_MODEL, D_EXPERT), BF16),
                            pltpu.VMEM((EXPERTS_PER_GROUP, D_MODEL, D_EXPERT), BF16),
                            pltpu.VMEM((EXPERTS_PER_GROUP, D_EXPERT, D_MODEL), BF16),
                            pltpu.VMEM((3, tm, MOE_ROW), F32), pltpu.SemaphoreType.DMA((3,))],
        ),
        out_shape=jax.ShapeDtypeStruct((n_mtiles * tm, D_MODEL), F32),
        compiler_params=pltpu.CompilerParams(dimension_semantics=("arbitrary",), vmem_limit_bytes=VMEM_LIMIT_BYTES),
        name="moe_experts",
    )(meta, src, hs, wg, wu, wd)


def _combine_kernel(dst_ref, tok_ref, x1_ref, ys_ref, o_ref, buf_ref, xbuf_ref, sem, xsem):
    i = pl.program_id(0)
    n_steps = pl.num_programs(0)
    tm = o_ref.shape[0]
    n_slots = buf_ref.shape[1]
    n_units = n_slots // MOE_UNIT
    n_ring = buf_ref.shape[0]

    def residual_copy(step, slot):
        rows = pl.ds(pl.multiple_of(step * tm, tm), tm)
        return pltpu.make_async_copy(x1_ref.at[rows], xbuf_ref.at[slot], xsem.at[slot])

    def fetch(step):
        slot = step % n_ring
        _unit_gather(dst_ref, step * n_units, ys_ref, buf_ref.at[slot], sem.at[slot], wait=False)
        residual_copy(step, slot).start()

    @pl.when(i == 0)
    def _():
        fetch(0)

    @pl.when((i == 0) & (n_steps > 1))
    def _():
        fetch(1)

    @pl.when(i + 2 < n_steps)
    def _():
        fetch(i + 2)

    slot = i % n_ring
    _unit_gather(dst_ref, i * n_units, ys_ref, buf_ref.at[slot], sem.at[slot], wait=True)
    residual_copy(i, slot).wait()

    y = buf_ref[slot]
    y_hi = y.astype(BF16)
    y_lo = (y - y_hi.astype(F32)).astype(BF16)
    unsort = _sort_matrix(tok_ref[:, SLOT_LANE:SLOT_LANE + 1].astype(jnp.int32), n_slots, 1)
    o_ref[...] = xbuf_ref[slot] + _dot(unsort, y_hi) + _dot(unsort, y_lo)


def _combine_call(dst, x1, tok, ys, tm):
    n = x1.shape[0]
    n_slots = tm + MOE_PAD_ROWS
    n_ring = 3
    return pl.pallas_call(
        _combine_kernel,
        grid_spec=pltpu.PrefetchScalarGridSpec(
            num_scalar_prefetch=1,
            grid=(n // tm,),
            in_specs=[pl.BlockSpec((tm, LANES), lambda i, dst: (i, 0)),
                      pl.BlockSpec(memory_space=pl.ANY),
                      pl.BlockSpec(memory_space=pl.ANY)],
            out_specs=pl.BlockSpec((tm, D_MODEL), lambda i, dst: (i, 0)),
            scratch_shapes=[pltpu.VMEM((n_ring, n_slots, D_MODEL), F32), pltpu.VMEM((n_ring, tm, D_MODEL), F32),
                            pltpu.SemaphoreType.DMA((n_ring,)), pltpu.SemaphoreType.DMA((n_ring,))],
        ),
        out_shape=jax.ShapeDtypeStruct((n, D_MODEL), F32),
        compiler_params=pltpu.CompilerParams(dimension_semantics=("arbitrary",), vmem_limit_bytes=VMEM_LIMIT_BYTES),
        name="moe_combine",
    )(dst, tok, x1, ys)


def kernel(x, norm_mix, w_in, sgu_norm, sgu_w, sgu_b, s5_lambda_re, s5_lambda_im, s5_log_dt, s5_b_re, s5_b_im,
           s5_c_re, s5_c_im, s5_d, s5_glu_w, s5_glu_b, q_norm, k_norm, rel_bias, out_norm, w_out, norm_ffn,
           router_group_w, router_group_b, router_expert_w, router_expert_b, w_gate, w_up, w_down):
    batch, seq, _ = x.shape
    n = batch * seq
    depth = w_in.shape[0]
    t = _tiles(n, seq)
    assert seq % ATTN_TQ == 0 and seq % (S5_T * t["s5_rows"]) == 0
    row_vec = lambda a: a.astype(F32)[None, :]

    lane_group = jnp.arange(LANES) // HEAD_DIM
    bd = (lane_group[:, None] == lane_group[None, :]).astype(BF16)
    block_chunk = jnp.arange(SGU_BLOCK) // CHUNK
    sgu_mask = block_chunk[None, :] <= block_chunk[:, None]

    xf = x.reshape(n, D_MODEL)
    for l in range(depth):
        sguw = jnp.where(sgu_mask[None], sgu_w[l], 0).astype(BF16)
        sgub = jnp.repeat(sgu_b[l].astype(F32).T, HEAD_DIM, axis=1)
        qg = row_vec(jnp.tile(q_norm[l], ATTN_HEADS)) * (HEAD_DIM ** -0.5)
        kg = row_vec(jnp.tile(k_norm[l], ATTN_HEADS))
        oa, u0, u1, q, k, v = _in_call(xf, row_vec(norm_mix[l]), w_in, l, row_vec(sgu_norm[l]), sguw, sgub,
                                      qg, kg, bd, t["tm_in"])

        wv, kb, wc, apow = _s5_tables(s5_lambda_re[l], s5_lambda_im[l], s5_log_dt[l], s5_b_re[l], s5_b_im[l],
                                        s5_c_re[l], s5_c_im[l], t["s5_rows"])
        y0, y1 = _s5_call(u0, u1, wv, kb, wc, apow, batch, t["s5_rows"])

        oc = _attn_call(q, k, v, _attn_diag(rel_bias[l]), batch, seq)

        wr, br = _router_tables(router_group_w[l], router_group_b[l], router_expert_w[l], router_expert_b[l])
        x1, hs, tok, counts = _out_call(oa, y0, y1, u0, u1, oc, xf, row_vec(s5_d[l]), s5_glu_w[l].astype(BF16), row_vec(s5_glu_b[l]),
                                 row_vec(out_norm[l]), w_out, l, row_vec(norm_ffn[l]), wr, br, bd,
                                 t["tm_out"])

        meta, src, dst, n_mtiles = _route_plan(counts, n, t["tm_out"])
        ys = _moe_group_call(meta, src, hs, w_gate, w_up, w_down, l, n_mtiles, t["tm_out"])
        xf = _combine_call(dst, x1, tok, ys, t["tm_out"])
    return xf.reshape(batch, seq, D_MODEL)
```

```python
import jax
import jax.numpy as jnp
from jax import lax
from jax.experimental import pallas as pl
from jax.experimental.pallas import tpu as pltpu

F32 = jnp.float32
BF16 = jnp.bfloat16

D_MODEL = 1024
CHUNK = 64
HEAD_DIM = 64
SGU_WIDTH = 256
SGU_HEADS = 4
SGU_BLOCK = 128
S5_WIDTH = 256
S5_GROUP = 16
S5_N_GROUPS = 16
S5_STATE = 64
ATTN_WIDTH = 512
ATTN_HEADS = 8
BAND_CHUNKS = 9
MAX_REL = 256
IN_COLS = 2 * SGU_WIDTH + S5_WIDTH + 3 * ATTN_WIDTH
OUT_NORM_GROUP = 64
N_EXPERT_GROUPS = 4
EXPERTS_PER_GROUP = 4
N_EXPERTS = 16
D_EXPERT = 256
EPS = 1e-6
NEG_INF = -1e30

LANES = 128
SUBLANES = 8
VMEM_LIMIT_BYTES = 56 * 1024 * 1024

S5_T = 16
S5_ROW = S5_T * S5_WIDTH
S5_NSTATE = S5_N_GROUPS * S5_STATE
S5_QUAD = 4
S5_SETS = 4
S5_QUADS = S5_T // S5_QUAD
S5_BLOCK = S5_QUAD * (S5_N_GROUPS // S5_SETS) * S5_GROUP
ATTN_TQ = 256
ATTN_PREV = (BAND_CHUNKS - 1) * CHUNK
ATTN_TK = ATTN_TQ + ATTN_PREV
ROUTER_SLAB = SUBLANES
MOE_ROW = D_MODEL + LANES
MOE_UNIT = SUBLANES
MOE_PAD_ROWS = N_EXPERT_GROUPS * MOE_UNIT
SLOT_LANE = N_EXPERT_GROUPS * ROUTER_SLAB


def _tiles(n_tokens, seq):
    def pick(pref, total):
        t = min(pref, total)
        assert total % t == 0
        return t
    return dict(
        tm_in=pick(512, seq),
        tm_out=pick(512, seq),
        s5_rows=pick(256, seq // S5_T),
    )


def _dot(a, b):
    return jnp.dot(a, b, preferred_element_type=F32)


def _dot_nt(a, b):
    return lax.dot_general(a, b, (((1,), (1,)), ((), ())), preferred_element_type=F32)


def _group_sumsq(x, bd_ref):
    x2 = (x * x).astype(BF16)
    parts = [_dot(x2[:, t * LANES:(t + 1) * LANES], bd_ref[...]) for t in range(x.shape[1] // LANES)]
    return jnp.concatenate(parts, axis=-1)


def _in_kernel(x_ref, g_ref, wf_ref, sgug_ref, sguw_ref, sgub_ref, qg_ref, kg_ref, bd_ref,
               oa_ref, u0_ref, u1_ref, q_ref, k_ref, v_ref, w_ref, ha_ref, hb_ref, xb_ref, ring_sem):
    tm = oa_ref.shape[0]
    step_id = pl.program_id(0)
    n_steps = pl.num_programs(0)
    n_ring = xb_ref.shape[0]

    def ring_copy(step):
        s = step % n_ring
        rows = pl.ds(pl.multiple_of(jnp.minimum(step, n_steps - 2) * tm, tm), tm)
        return pltpu.make_async_copy(x_ref.at[rows], xb_ref.at[s], ring_sem.at[s])

    @pl.when(step_id == 0)
    def _():
        ring_copy(0).start()
        ring_copy(1).start()
        w_ref[...] = wf_ref[0].astype(BF16)
        hb_ref[...] = jnp.zeros_like(hb_ref)

    @pl.when(step_id + 2 < n_steps)
    def _():
        ring_copy(step_id + 2).start()

    ring_copy(step_id).wait()
    ring_slot = step_id % n_ring

    def project(h_ref):
        x = xb_ref[ring_slot]
        ms = jnp.mean(x * x, axis=-1, keepdims=True)
        hn = (x * lax.rsqrt(ms + EPS) * g_ref[...]).astype(BF16)
        c0 = 0
        for width in (2 * SGU_WIDTH, S5_WIDTH, ATTN_WIDTH, ATTN_WIDTH, ATTN_WIDTH):
            h_ref[:, c0:c0 + width] = _dot(hn, w_ref[:, c0:c0 + width])
            c0 += width

    def mix(h_ref):
        z = jax.nn.gelu(h_ref[:, 0:2 * SGU_WIDTH])
        u = z[:, :SGU_WIDTH]
        v = z[:, SGU_WIDTH:]
        v = v * lax.rsqrt(jnp.mean(v * v, axis=-1, keepdims=True) + EPS) * sgug_ref[...]
        vb = v.astype(BF16)
        first_head = lax.broadcasted_iota(jnp.int32, (SGU_BLOCK, LANES), 1) < HEAD_DIM
        for r in range(tm // SGU_BLOCK):
            rows = slice(r * SGU_BLOCK, (r + 1) * SGU_BLOCK)
            for p in range(SGU_WIDTH // LANES):
                cols = slice(p * LANES, (p + 1) * LANES)
                vp = vb[rows, cols]
                mixed = jnp.where(first_head, _dot(sguw_ref[2 * p], vp), _dot(sguw_ref[2 * p + 1], vp))
                oa_ref[rows, cols] = u[rows, cols] * (mixed + sgub_ref[:, cols])

        c0 = 2 * SGU_WIDTH
        u0_ref[...] = h_ref[:, c0:c0 + LANES]
        u1_ref[...] = h_ref[:, c0 + LANES:c0 + 2 * LANES]

        c0 += S5_WIDTH
        q = h_ref[:, c0:c0 + ATTN_WIDTH]
        q_ref[...] = (q * lax.rsqrt(_group_sumsq(q, bd_ref) * (1.0 / HEAD_DIM) + EPS) * qg_ref[...]).astype(BF16)
        c0 += ATTN_WIDTH
        k = h_ref[:, c0:c0 + ATTN_WIDTH]
        k_ref[...] = (k * lax.rsqrt(_group_sumsq(k, bd_ref) * (1.0 / HEAD_DIM) + EPS) * kg_ref[...]).astype(BF16)
        c0 += ATTN_WIDTH
        v_ref[...] = h_ref[:, c0:c0 + ATTN_WIDTH].astype(BF16)

    parity = pl.program_id(0) % 2

    @pl.when(parity == 0)
    def _():
        project(ha_ref)
        mix(hb_ref)

    @pl.when(parity == 1)
    def _():
        project(hb_ref)
        mix(ha_ref)


def _in_call(x, g, w, layer, sgug, sguw, sgub, qg, kg, bd, tm):
    n = x.shape[0]
    n_tiles = n // tm
    row = lambda c: pl.BlockSpec((tm, c), lambda i: (jnp.maximum(i - 1, 0), 0))
    full = lambda a: pl.BlockSpec(a.shape, lambda i: (0,) * a.ndim)
    w_layer = pl.BlockSpec((1,) + w.shape[1:], lambda i: (layer, 0, 0), pipeline_mode=pl.Buffered(1))
    return pl.pallas_call(
        _in_kernel,
        grid=(n_tiles + 1,),
        in_specs=[pl.BlockSpec(memory_space=pl.ANY), full(g), w_layer, full(sgug),
                  full(sguw), full(sgub), full(qg), full(kg), full(bd)],
        out_specs=[row(SGU_WIDTH), row(LANES), row(LANES), row(ATTN_WIDTH), row(ATTN_WIDTH), row(ATTN_WIDTH)],
        out_shape=[jax.ShapeDtypeStruct((n, SGU_WIDTH), F32), jax.ShapeDtypeStruct((n, LANES), F32),
                   jax.ShapeDtypeStruct((n, LANES), F32), jax.ShapeDtypeStruct((n, ATTN_WIDTH), BF16),
                   jax.ShapeDtypeStruct((n, ATTN_WIDTH), BF16), jax.ShapeDtypeStruct((n, ATTN_WIDTH), BF16)],
        scratch_shapes=[pltpu.VMEM(w.shape[1:], BF16), pltpu.VMEM((tm, w.shape[2]), F32), pltpu.VMEM((tm, w.shape[2]), F32),
                        pltpu.VMEM((3, tm, D_MODEL), F32), pltpu.SemaphoreType.DMA((3,))],
        compiler_params=pltpu.CompilerParams(dimension_semantics=("arbitrary",), vmem_limit_bytes=VMEM_LIMIT_BYTES),
        name="in_proj",
    )(x, g, w, sgug, sguw, sgub, qg, kg, bd)


def _s5_kernel(u0_ref, u1_ref, wv_ref, kb_ref, wc_ref, apow_ref, y0_ref, y1_ref, ur_ref, s_ref, sp_ref, yb_ref, carry_ref):
    rows = ur_ref.shape[0]
    n_steps = apow_ref.shape[0]
    half = LANES // 2
    low = lax.broadcasted_iota(jnp.int32, (1, LANES), 1) < half
    block = lambda tq, gs: slice((tq * S5_SETS + gs) * S5_BLOCK, (tq * S5_SETS + gs + 1) * S5_BLOCK)

    @pl.when(pl.program_id(1) == 0)
    def _():
        carry_ref[...] = jnp.zeros_like(carry_ref)

    for tq in range(S5_QUADS):
        for lt in range(S5_QUAD // 2):
            t0 = tq * S5_QUAD + 2 * lt
            for pair, u_ref in enumerate((u0_ref, u1_ref)):
                a = u_ref[pl.ds(t0, rows, stride=S5_T), :]
                b = u_ref[pl.ds(t0 + 1, rows, stride=S5_T), :]
                for par in range(2):
                    tile = jnp.where(low, a, pltpu.roll(b, half, 1)) if par == 0 else jnp.where(low, pltpu.roll(a, half, 1), b)
                    col = block(tq, 2 * pair + par).start + lt * LANES
                    ur_ref[:, col:col + LANES] = tile.astype(BF16)

    for gs in range(S5_SETS):
        v = None
        for tq in range(S5_QUADS):
            part = _dot(ur_ref[:, block(tq, gs)], wv_ref[tq * S5_SETS + gs])
            v = part if v is None else v + part
        s_ref[:, gs * S5_BLOCK:(gs + 1) * S5_BLOCK] = v[:, :S5_BLOCK]
        s_ref[:, S5_NSTATE + gs * S5_BLOCK:S5_NSTATE + (gs + 1) * S5_BLOCK] = v[:, S5_BLOCK:]

    row_id = lax.broadcasted_iota(jnp.int32, (rows, LANES), 0)
    for cb in range(S5_NSTATE // LANES):
        cre = slice(cb * LANES, (cb + 1) * LANES)
        cim = slice(S5_NSTATE + cb * LANES, S5_NSTATE + (cb + 1) * LANES)
        re = s_ref[:, cre]
        im = s_ref[:, cim]
        c_re = carry_ref[0:1, cre]
        c_im = carry_ref[0:1, cim]
        a_re = apow_ref[0, 0:1, cre]
        a_im = apow_ref[0, 1:2, cre]
        re = re + jnp.where(row_id == 0, a_re * c_re - a_im * c_im, 0.0)
        im = im + jnp.where(row_id == 0, a_re * c_im + a_im * c_re, 0.0)
        for k in range(n_steps):
            shift = 1 << k
            a_re = apow_ref[k, 0:1, cre]
            a_im = apow_ref[k, 1:2, cre]
            re_s = jnp.where(row_id >= shift, pltpu.roll(re, shift, 0), 0.0)
            im_s = jnp.where(row_id >= shift, pltpu.roll(im, shift, 0), 0.0)
            re, im = re + a_re * re_s - a_im * im_s, im + a_re * im_s + a_im * re_s
        sp_ref[:, cre] = jnp.where(row_id >= 1, pltpu.roll(re, 1, 0), c_re).astype(BF16)
        sp_ref[:, cim] = jnp.where(row_id >= 1, pltpu.roll(im, 1, 0), c_im).astype(BF16)
        carry_ref[0:1, cre] = re[rows - 1:rows, :]
        carry_ref[0:1, cim] = im[rows - 1:rows, :]

    for gs in range(S5_SETS):
        state = jnp.concatenate([sp_ref[:, gs * S5_BLOCK:(gs + 1) * S5_BLOCK],
                                 sp_ref[:, S5_NSTATE + gs * S5_BLOCK:S5_NSTATE + (gs + 1) * S5_BLOCK]], axis=1)
        for tq in range(S5_QUADS):
            y = _dot(state, wc_ref[tq * S5_SETS + gs])
            for tj in range(tq + 1):
                y = y + _dot(ur_ref[:, block(tj, gs)], kb_ref[(tq - tj) * S5_SETS + gs])
            yb_ref[:, block(tq, gs)] = y

    for tq in range(S5_QUADS):
        for lt in range(S5_QUAD // 2):
            t0 = tq * S5_QUAD + 2 * lt
            for pair, y_ref in enumerate((y0_ref, y1_ref)):
                a = yb_ref[:, block(tq, 2 * pair).start + lt * LANES:block(tq, 2 * pair).start + (lt + 1) * LANES]
                b = yb_ref[:, block(tq, 2 * pair + 1).start + lt * LANES:block(tq, 2 * pair + 1).start + (lt + 1) * LANES]
                y_ref[pl.ds(t0, rows, stride=S5_T), :] = jnp.where(low, a, pltpu.roll(b, half, 1))
                y_ref[pl.ds(t0 + 1, rows, stride=S5_T), :] = jnp.where(low, pltpu.roll(a, half, 1), b)


def _s5_call(u0, u1, wv, kb, wc, apow, batch, rows):
    n = u0.shape[0]
    tok = rows * S5_T
    tiles_per_seq = n // batch // tok
    const = lambda a: pl.BlockSpec(a.shape, lambda b, t: (0,) * a.ndim, pipeline_mode=pl.Buffered(1))
    blk = pl.BlockSpec((tok, LANES), lambda b, t: (b * tiles_per_seq + t, 0))
    half = jax.ShapeDtypeStruct((n, LANES), F32)
    return pl.pallas_call(
        _s5_kernel,
        grid=(batch, tiles_per_seq),
        in_specs=[blk, blk, const(wv), const(kb), const(wc), const(apow)],
        out_specs=[blk, blk],
        out_shape=[half, half],
        scratch_shapes=[pltpu.VMEM((rows, S5_ROW), BF16), pltpu.VMEM((rows, 2 * S5_NSTATE), F32),
                        pltpu.VMEM((rows, 2 * S5_NSTATE), BF16), pltpu.VMEM((rows, S5_ROW), F32),
                        pltpu.VMEM((SUBLANES, 2 * S5_NSTATE), F32)],
        compiler_params=pltpu.CompilerParams(dimension_semantics=("arbitrary", "arbitrary"),
                                             vmem_limit_bytes=VMEM_LIMIT_BYTES),
        name="s5_scan",
    )(u0, u1, wv, kb, wc, apow)


def _s5_tables(lam_re, lam_im, log_dt, b_re, b_im, c_re, c_im, rows):
    p_, h_ = S5_STATE, S5_GROUP
    q_, s_, m_ = S5_QUADS, S5_SETS, S5_N_GROUPS // S5_SETS
    hi = lax.Precision.HIGHEST
    lam_re, lam_im = lam_re.astype(F32), lam_im.astype(F32)
    dt = jnp.exp(log_dt.astype(F32))[:, None]
    tau = jnp.arange(S5_T + 1, dtype=F32)[:, None, None]
    mag = jnp.exp(tau * (lam_re * dt)[None])
    ap_re = mag * jnp.cos(tau * (lam_im * dt)[None])
    ap_im = mag * jnp.sin(tau * (lam_im * dt)[None])
    n_re, n_im = ap_re[1] - 1.0, ap_im[1]
    den = lam_re * lam_re + lam_im * lam_im
    k_re = ((n_re * lam_re + n_im * lam_im) / den)[..., None]
    k_im = ((n_im * lam_re - n_re * lam_im) / den)[..., None]
    b_re, b_im = b_re.astype(F32), b_im.astype(F32)
    bb_re = k_re * b_re - k_im * b_im
    bb_im = k_re * b_im + k_im * b_re
    c_re, c_im = c_re.astype(F32), c_im.astype(F32)
    ab_re = ap_re[:S5_T, :, :, None] * bb_re[None] - ap_im[:S5_T, :, :, None] * bb_im[None]
    ab_im = ap_re[:S5_T, :, :, None] * bb_im[None] + ap_im[:S5_T, :, :, None] * bb_re[None]

    def spread(base, col_of_base):
        cols = jnp.arange(S5_BLOCK)
        expand = (col_of_base(cols)[None, :] == jnp.arange(base.shape[1])[:, None]).astype(F32)
        return jnp.dot(base, expand, precision=hi)
    state_col = lambda cols: cols % p_
    chan_col = lambda cols: cols // (m_ * h_) * h_ + cols % h_
    row_group = jnp.arange(S5_BLOCK) // h_ % m_
    keep_state = (row_group[:, None] == (jnp.arange(S5_BLOCK) // p_)[None, :])[None]
    keep_chan = (row_group[:, None] == (jnp.arange(S5_BLOCK) // h_ % m_)[None, :])[None]
    keep_out = ((jnp.arange(S5_BLOCK) // p_)[:, None] == (jnp.arange(S5_BLOCK) // h_ % m_)[None, :])[None]

    def wv_part(ab):
        base = ab[::-1].reshape(q_, S5_QUAD, s_, m_, p_, h_).transpose(0, 2, 1, 3, 5, 4).reshape(-1, p_)
        return jnp.where(keep_state, spread(base, state_col).reshape(q_ * s_, S5_BLOCK, S5_BLOCK), 0.0)
    wv = jnp.concatenate([wv_part(ab_re), wv_part(ab_im)], axis=-1)

    k_tau = (jnp.einsum('gop,tgpi->tgio', c_re, ab_re, precision=hi)
             - jnp.einsum('gop,tgpi->tgio', c_im, ab_im, precision=hi))
    zero = jnp.zeros_like(k_tau[0])
    taps = jnp.stack([jnp.stack([jnp.stack([k_tau[S5_QUAD * dq + ti - tj] if S5_QUAD * dq + ti - tj >= 0 else zero
                                            for ti in range(S5_QUAD)]) for tj in range(S5_QUAD)])
                      for dq in range(q_)])
    base = taps.reshape(q_, S5_QUAD, S5_QUAD, s_, m_, h_, h_).transpose(0, 3, 1, 4, 5, 2, 6).reshape(-1, S5_QUAD * h_)
    kb = jnp.where(keep_chan, spread(base, chan_col).reshape(q_ * s_, S5_BLOCK, S5_BLOCK), 0.0)

    cp = lambda c: c.transpose(0, 2, 1)[:, :, None, :]
    w_re = cp(c_re) * ap_re[1:].transpose(1, 2, 0)[..., None] - cp(c_im) * ap_im[1:].transpose(1, 2, 0)[..., None]
    w_im = cp(c_re) * ap_im[1:].transpose(1, 2, 0)[..., None] + cp(c_im) * ap_re[1:].transpose(1, 2, 0)[..., None]
    def wc_part(w):
        base = w.reshape(s_, m_, p_, q_, S5_QUAD, h_).transpose(3, 0, 1, 2, 4, 5).reshape(-1, S5_QUAD * h_)
        return jnp.where(keep_out, spread(base, chan_col).reshape(q_ * s_, S5_BLOCK, S5_BLOCK), 0.0)
    wc = jnp.concatenate([wc_part(w_re), wc_part(-w_im)], axis=1)

    n_steps = max(1, (rows - 1).bit_length())
    pows = [(ap_re[S5_T].reshape(S5_NSTATE), ap_im[S5_T].reshape(S5_NSTATE))]
    for _ in range(n_steps - 1):
        r, i = pows[-1]
        pows.append((r * r - i * i, 2.0 * r * i))
    apow = jnp.stack([jnp.stack(p) for p in pows])
    return wv.astype(BF16), kb.astype(BF16), wc.astype(BF16), apow


def _attn_fill_bias(diag_ref, bias_ref):
    width = diag_ref.shape[1]
    qi = lax.broadcasted_iota(jnp.int32, (ATTN_TQ, ATTN_TK), 0)
    kj = lax.broadcasted_iota(jnp.int32, (ATTN_TQ, ATTN_TK), 1)
    q_chunk = qi // CHUNK + (BAND_CHUNKS - 1)
    k_chunk = kj // CHUNK
    in_band = (k_chunk <= q_chunk) & (k_chunk >= q_chunk - (BAND_CHUNKS - 1))
    for h in range(ATTN_HEADS):
        rows = jnp.broadcast_to(diag_ref[h:h + 1, :], (ATTN_TQ, width))
        base = pltpu.roll(rows, 0, 1, stride=1, stride_axis=0)[:, :ATTN_TK]
        for t in range(bias_ref.shape[0]):
            bias_ref[t, h] = jnp.where(in_band & (kj >= ATTN_PREV - t * ATTN_TQ), base, NEG_INF)


def _attn_kernel(q_ref, k0_ref, k1_ref, k2_ref, v0_ref, v1_ref, v2_ref, diag_ref, o_ref, bias_ref):
    @pl.when((pl.program_id(0) == 0) & (pl.program_id(1) == 0))
    def _():
        _attn_fill_bias(diag_ref, bias_ref)

    table = jnp.minimum(pl.program_id(1), bias_ref.shape[0] - 1)
    lane = lax.broadcasted_iota(jnp.int32, (1, LANES), 1)
    for p in range(ATTN_WIDTH // LANES):
        cols = slice(p * LANES, (p + 1) * LANES)
        qp = q_ref[:, cols]
        kcat = jnp.concatenate([k0_ref[:, cols], k1_ref[:, cols], k2_ref[:, cols]], axis=0)
        vcat = jnp.concatenate([v0_ref[:, cols], v1_ref[:, cols], v2_ref[:, cols]], axis=0)
        first = lane < HEAD_DIM
        zero = jnp.zeros_like(qp)
        qm = jnp.concatenate([jnp.where(first, qp, zero), jnp.where(first, zero, qp)], axis=0)
        bias = jnp.concatenate([bias_ref[table, 2 * p], bias_ref[table, 2 * p + 1]], axis=0)
        s = _dot_nt(qm, kcat) + bias
        e = jnp.exp(s - jnp.max(s, axis=-1, keepdims=True)).astype(BF16)
        o = _dot(e, jnp.concatenate([vcat, jnp.ones_like(vcat)], axis=1))
        r = o[:, :LANES] / o[:, LANES:]
        o_ref[:, cols] = jnp.where(first, r[:ATTN_TQ], r[ATTN_TQ:])


def _attn_call(q, k, v, diag, batch, seq):
    n = q.shape[0]
    tiles = seq // ATTN_TQ
    n_prev = ATTN_PREV // ATTN_TQ
    cur = pl.BlockSpec((ATTN_TQ, ATTN_WIDTH), lambda b, t: (b * tiles + t, 0))
    prev = lambda d: pl.BlockSpec((ATTN_TQ, ATTN_WIDTH), lambda b, t: (b * tiles + jnp.maximum(t - d, 0), 0))
    assert n_prev == 2
    return pl.pallas_call(
        _attn_kernel,
        grid=(batch, tiles),
        in_specs=[cur, prev(2), prev(1), cur, prev(2), prev(1), cur, pl.BlockSpec(diag.shape, lambda b, t: (0, 0))],
        out_specs=cur,
        out_shape=jax.ShapeDtypeStruct((n, ATTN_WIDTH), F32),
        scratch_shapes=[pltpu.VMEM((n_prev + 1, ATTN_HEADS, ATTN_TQ, ATTN_TK), F32)],
        compiler_params=pltpu.CompilerParams(dimension_semantics=("arbitrary", "arbitrary"),
                                             vmem_limit_bytes=VMEM_LIMIT_BYTES),
        name="band_attn",
    )(q, k, k, k, v, v, v, diag)


def _attn_diag(rel_bias):
    width = ATTN_TQ + ATTN_TK
    n = jnp.arange(width)
    offset = jnp.where(n < ATTN_TK, n, n - width)
    rel = jnp.clip(ATTN_PREV - offset, -MAX_REL, MAX_REL) + MAX_REL
    return jnp.take(rel_bias.astype(F32), rel, axis=1)


def _router_gates(lt):
    tm = lt.shape[1]
    row = lax.broadcasted_iota(jnp.int32, (ROUTER_SLAB, tm), 0)
    gl = lt[0:ROUTER_SLAB]
    gmax = jnp.max(gl, axis=0, keepdims=True)
    p_g = 1.0 / jnp.sum(jnp.exp(gl - gmax), axis=0, keepdims=True)
    g_top = jnp.min(jnp.where(gl == gmax, row, ROUTER_SLAB), axis=0, keepdims=True)
    el = jnp.zeros((ROUTER_SLAB, tm), F32)
    for g in range(N_EXPERT_GROUPS):
        el = el + jnp.where(g_top == g, lt[(g + 1) * ROUTER_SLAB:(g + 2) * ROUTER_SLAB], 0.0)
    ee = jnp.exp(el - jnp.max(el, axis=0, keepdims=True))
    ep = ee / jnp.sum(ee, axis=0, keepdims=True)
    p1 = jnp.max(ep, axis=0, keepdims=True)
    i1 = jnp.min(jnp.where(ep == p1, row, ROUTER_SLAB), axis=0, keepdims=True)
    rest = jnp.where(row == i1, -1.0, ep)
    p2 = jnp.max(rest, axis=0, keepdims=True)
    i2 = jnp.min(jnp.where(rest == p2, row, ROUTER_SLAB), axis=0, keepdims=True)
    tot = p1 + p2
    w = (jnp.where(row == i1, p1 / tot, 0.0) + jnp.where(row == i2, p2 / tot, 0.0)) * p_g
    return [jnp.where(g_top == g, w, 0.0) for g in range(N_EXPERT_GROUPS)], g_top


def _sort_matrix(slot, n_slots, slot_axis):
    shape = (n_slots, slot.shape[1]) if slot_axis == 0 else (slot.shape[0], n_slots)
    return jnp.where(lax.broadcasted_iota(jnp.int32, shape, slot_axis) == slot, 1.0, 0.0).astype(BF16)


def _out_kernel(oa_ref, y0_ref, y1_ref, u0_ref, u1_ref, oc_ref, x_ref, d_ref, gluw_ref, glub_ref, og_ref, woutf_ref,
                fg_ref, wr_ref, br_ref, bd_ref, x1_ref, hs_ref, tok_ref, cnt_ref, wout_ref, hb_ref, gt_ref, slot_ref,
                ocb_ref, xb_ref, ring_sem):
    tm = x1_ref.shape[0]
    n_slots = hs_ref.shape[0]
    step_id = pl.program_id(0)
    n_steps = pl.num_programs(0)
    n_ring = xb_ref.shape[0]

    def ring_copies(step):
        s = step % n_ring
        rows = pl.ds(pl.multiple_of(jnp.minimum(step, n_steps - 2) * tm, tm), tm)
        return (pltpu.make_async_copy(oc_ref.at[rows], ocb_ref.at[s], ring_sem.at[0, s]),
                pltpu.make_async_copy(x_ref.at[rows], xb_ref.at[s], ring_sem.at[1, s]))

    def fetch(step):
        for c in ring_copies(step):
            c.start()

    @pl.when(step_id == 0)
    def _():
        fetch(0)
        fetch(1)
        wout_ref[...] = woutf_ref[0].astype(BF16)
        hb_ref[...] = jnp.zeros_like(hb_ref)
        gt_ref[...] = jnp.zeros_like(gt_ref)
        slot_ref[...] = jnp.zeros_like(slot_ref)

    @pl.when(step_id + 2 < n_steps)
    def _():
        fetch(step_id + 2)

    for c in ring_copies(step_id):
        c.wait()
    ring_slot = step_id % n_ring

    y = jnp.concatenate([y0_ref[...], y1_ref[...]], axis=-1)
    u = jnp.concatenate([u0_ref[...], u1_ref[...]], axis=-1)
    y = jax.nn.gelu(y + d_ref[...] * u)
    ob = y * jax.nn.sigmoid(_dot(y.astype(BF16), gluw_ref[...]) + glub_ref[...])
    o = jnp.concatenate([oa_ref[...], ob, ocb_ref[ring_slot]], axis=-1)
    on = o * lax.rsqrt(_group_sumsq(o, bd_ref) * (1.0 / OUT_NORM_GROUP) + EPS) * og_ref[...]
    x1 = xb_ref[ring_slot] + _dot(on.astype(BF16), wout_ref[...])
    x1_ref[...] = x1
    ms = jnp.mean(x1 * x1, axis=-1, keepdims=True)
    hb = (x1 * lax.rsqrt(ms + EPS) * fg_ref[...]).astype(BF16)
    lt = _dot(hb, wr_ref[...]).T + br_ref[...]
    slabs, g_top = _router_gates(lt)

    row = lax.broadcasted_iota(jnp.int32, (ROUTER_SLAB, tm), 0)
    lane = lax.broadcasted_iota(jnp.int32, (ROUTER_SLAB, tm), 1)
    member = jnp.where(row == g_top, 1.0, 0.0)
    cum = member
    shift = 1
    while shift < tm:
        cum = cum + jnp.where(lane >= shift, pltpu.roll(cum, shift, 1), 0.0)
        shift *= 2
    count = cum[:, tm - 1:tm]
    padded = jnp.floor((count + (MOE_UNIT - 1)) * (1.0 / MOE_UNIT)) * MOE_UNIT
    group_row = lax.broadcasted_iota(jnp.int32, (ROUTER_SLAB, 1), 0)
    start = jnp.zeros((ROUTER_SLAB, 1), F32)
    for g in range(1, N_EXPERT_GROUPS):
        start = start + jnp.where(group_row >= g, padded[g - 1:g, :], 0.0)
    slot = jnp.sum(member * (start + cum - 1.0), axis=0, keepdims=True)
    cnt_ref[...] = jnp.broadcast_to(count, cnt_ref.shape)

    pad = jnp.zeros((LANES - SLOT_LANE - ROUTER_SLAB, tm), F32)
    gates = jnp.concatenate(slabs + [jnp.broadcast_to(slot, (ROUTER_SLAB, tm)), pad], axis=0).T
    tok_ref[...] = gates

    gates_p = gt_ref[...]
    perm = _sort_matrix(slot_ref[0:1, :].astype(jnp.int32), n_slots, 0)
    hs_ref[:, :D_MODEL] = _dot(perm, hb_ref[...])
    g_hi = gates_p.astype(BF16)
    g_mid = (gates_p - g_hi.astype(F32)).astype(BF16)
    g_lo = (gates_p - g_hi.astype(F32) - g_mid.astype(F32)).astype(BF16)
    gs = _dot(perm, jnp.concatenate([g_hi, g_mid, g_lo], axis=-1))
    hs_ref[:, D_MODEL:] = gs[:, :LANES] + gs[:, LANES:2 * LANES] + gs[:, 2 * LANES:]

    hb_ref[...] = hb
    gt_ref[...] = gates
    slot_ref[...] = jnp.broadcast_to(slot, slot_ref.shape)


def _out_call(oa, y0, y1, u0, u1, oc, x, d, gluw, glub, og, wout, layer, fg, wr, br, bd, tm):
    n = x.shape[0]
    n_tiles = n // tm
    n_slots = tm + MOE_PAD_ROWS
    cur = lambda i: (jnp.minimum(i, n_tiles - 1), 0)
    row = lambda c: pl.BlockSpec((tm, c), cur)
    full = lambda a: pl.BlockSpec(a.shape, lambda i: (0,) * a.ndim)
    return pl.pallas_call(
        _out_kernel,
        grid=(n_tiles + 1,),
        in_specs=[row(SGU_WIDTH), row(LANES), row(LANES), row(LANES), row(LANES), pl.BlockSpec(memory_space=pl.ANY),
                  pl.BlockSpec(memory_space=pl.ANY), full(d), full(gluw),
                  full(glub), full(og), pl.BlockSpec((1,) + wout.shape[1:], lambda i: (layer, 0, 0), pipeline_mode=pl.Buffered(1)),
                  full(fg), full(wr), full(br), full(bd)],
        out_specs=[row(D_MODEL), pl.BlockSpec((n_slots, MOE_ROW), lambda i: (jnp.maximum(i - 1, 0), 0)), row(LANES),
                   pl.BlockSpec((ROUTER_SLAB, LANES), cur)],
        out_shape=[jax.ShapeDtypeStruct((n, D_MODEL), F32), jax.ShapeDtypeStruct((n_tiles * n_slots, MOE_ROW), F32),
                   jax.ShapeDtypeStruct((n, LANES), F32),
                   jax.ShapeDtypeStruct((n_tiles * ROUTER_SLAB, LANES), F32)],
        scratch_shapes=[pltpu.VMEM(wout.shape[1:], BF16), pltpu.VMEM((tm, D_MODEL), BF16), pltpu.VMEM((tm, LANES), F32),
                        pltpu.VMEM((ROUTER_SLAB, tm), F32), pltpu.VMEM((3, tm, ATTN_WIDTH), F32),
                        pltpu.VMEM((3, tm, D_MODEL), F32), pltpu.SemaphoreType.DMA((2, 3))],
        compiler_params=pltpu.CompilerParams(dimension_semantics=("arbitrary",), vmem_limit_bytes=VMEM_LIMIT_BYTES),
        name="out_proj",
    )(oa, y0, y1, u0, u1, oc, x, d, gluw, glub, og, wout, fg, wr, br, bd)


def _router_tables(wg, bg, we, be):
    w = jnp.zeros((D_MODEL, LANES), F32)
    b = jnp.full((LANES,), NEG_INF, F32)
    w = w.at[:, 0:N_EXPERT_GROUPS].set(wg.astype(F32))
    b = b.at[0:N_EXPERT_GROUPS].set(bg.astype(F32))
    for g in range(N_EXPERT_GROUPS):
        c0 = (g + 1) * ROUTER_SLAB
        w = w.at[:, c0:c0 + EXPERTS_PER_GROUP].set(we[g].astype(F32))
        b = b.at[c0:c0 + EXPERTS_PER_GROUP].set(be[g].astype(F32))
    return w.astype(BF16), b[:, None]


def _route_plan(counts, n, tm):
    n_tiles = n // tm
    units_per_tile = (tm + MOE_PAD_ROWS) // MOE_UNIT
    units_per_mtile = tm // MOE_UNIT
    cnt = counts.reshape(n_tiles, ROUTER_SLAB, LANES)[:, :N_EXPERT_GROUPS, 0].astype(jnp.int32)
    seg = (cnt + MOE_UNIT - 1) // MOE_UNIT
    seg_start = jnp.cumsum(seg, axis=1) - seg
    before = jnp.cumsum(seg, axis=0) - seg
    total = jnp.sum(seg, axis=0)
    mtiles = (total + units_per_mtile - 1) // units_per_mtile
    mtile_end = jnp.cumsum(mtiles)
    group_start = (mtile_end - mtiles) * units_per_mtile
    n_mtiles = n_tiles + N_EXPERT_GROUPS + -(-n_tiles * MOE_PAD_ROWS // tm)
    tile_group = jnp.minimum(jnp.sum(jnp.arange(n_mtiles)[:, None] >= mtile_end[None, :], axis=1),
                             N_EXPERT_GROUPS - 1)
    meta = jnp.concatenate([tile_group, mtile_end[-1:]]).astype(jnp.int32)
    groups = jnp.arange(N_EXPERT_GROUPS)
    pick = lambda table, g: jnp.sum(jnp.where(g[..., None] == groups, table, 0), axis=-1)

    q = jnp.arange(n_mtiles * units_per_mtile)
    g = jnp.repeat(tile_group, units_per_mtile)
    ql = q - pick(group_start, g)
    ends_g = pick((before + seg)[None], g[:, None])
    tile_of = jnp.minimum(jnp.sum(ql[:, None] >= ends_g, axis=1), n_tiles - 1)
    offset_g = pick((seg_start - before)[None], g[:, None])
    local = ql + jnp.sum(jnp.where(tile_of[:, None] == jnp.arange(n_tiles), offset_g, 0), axis=1)
    valid = (ql < pick(total, g)) & (q // units_per_mtile < mtile_end[-1])
    src = jnp.where(valid, tile_of * units_per_tile + local, units_per_tile - 1).astype(jnp.int32)

    ul = jnp.arange(units_per_tile)[None, :, None]
    seg_of = jnp.sum(ul >= (seg_start + seg)[:, None, :], axis=-1)
    gi = jnp.minimum(seg_of, N_EXPERT_GROUPS - 1)
    glob = pick((group_start + before - seg_start)[:, None, :], gi) + ul[..., 0]
    dst = jnp.where(seg_of < N_EXPERT_GROUPS, glob, 0).astype(jnp.int32).reshape(-1)
    return meta, src, dst, n_mtiles


def _unit_gather(table_ref, first, src_ref, dst_ref, sem, wait):
    n_rows = dst_ref.shape[0]
    if wait:
        pltpu.make_async_copy(src_ref.at[pl.ds(0, n_rows)], dst_ref, sem).wait()
        return

    def body(k, c):
        u = pl.multiple_of(table_ref[first + k] * MOE_UNIT, MOE_UNIT)
        pltpu.make_async_copy(src_ref.at[pl.ds(u, MOE_UNIT)],
                              dst_ref.at[pl.ds(pl.multiple_of(k * MOE_UNIT, MOE_UNIT), MOE_UNIT)], sem).start()
        return c
    lax.fori_loop(0, n_rows // MOE_UNIT, body, 0, unroll=4)


def _moe_group_kernel(meta_ref, src_ref, hs_ref, wg_ref, wu_ref, wd_ref, o_ref, wgb_ref, wub_ref, wdb_ref,
                      buf_ref, sem):
    j = pl.program_id(0)
    n_steps = pl.num_programs(0)
    n_used = meta_ref[n_steps]
    group = meta_ref[j]
    n_units = o_ref.shape[0] // MOE_UNIT
    n_ring = buf_ref.shape[0]
    slot = j % n_ring

    def fetch(step):
        s = step % n_ring
        _unit_gather(src_ref, step * n_units, hs_ref, buf_ref.at[s], sem.at[s], wait=False)

    @pl.when(j == 0)
    def _():
        fetch(0)

    @pl.when((j == 0) & (n_steps > 1))
    def _():
        fetch(1)

    @pl.when(j + 2 < n_steps)
    def _():
        fetch(j + 2)

    @pl.when((j == 0) | (group != meta_ref[jnp.maximum(j - 1, 0)]))
    def _():
        wgb_ref[...] = wg_ref[...].astype(BF16)
        wub_ref[...] = wu_ref[...].astype(BF16)
        wdb_ref[...] = wd_ref[...].astype(BF16)

    _unit_gather(src_ref, j * n_units, hs_ref, buf_ref.at[slot], sem.at[slot], wait=True)

    @pl.when(j < n_used)
    def _():
        h = buf_ref[slot, :, :D_MODEL].astype(BF16)
        gates = buf_ref[slot, :, D_MODEL:]
        lane = lax.broadcasted_iota(jnp.int32, (1, LANES), 1)
        out = None
        for e in range(EXPERTS_PER_GROUP):
            gt = _dot(h, wgb_ref[e])
            up = _dot(h, wub_ref[e])
            gate = jnp.sum(jnp.where(lane == group * ROUTER_SLAB + e, gates, 0.0), axis=-1, keepdims=True)
            a = (gt * jax.nn.sigmoid(gt)) * up * gate
            part = _dot(a.astype(BF16), wdb_ref[e])
            out = part if out is None else out + part
        o_ref[...] = out

    @pl.when(j >= n_used)
    def _():
        o_ref[...] = jnp.zeros_like(o_ref)


def _moe_group_call(meta, src, hs, wg, wu, wd, layer, n_mtiles, tm):
    wg, wu, wd = (a.reshape((-1,) + a.shape[2:]) for a in (wg, wu, wd))
    by_group = lambda a: pl.BlockSpec((EXPERTS_PER_GROUP,) + a.shape[1:],
                                      lambda j, meta, src: (layer * N_EXPERT_GROUPS + meta[j], 0, 0))
    return pl.pallas_call(
        _moe_group_kernel,
        grid_spec=pltpu.PrefetchScalarGridSpec(
            num_scalar_prefetch=2,
            grid=(n_mtiles,),
            in_specs=[pl.BlockSpec(memory_space=pl.ANY), by_group(wg), by_group(wu), by_group(wd)],
            out_specs=pl.BlockSpec((tm, D_MODEL), lambda j, meta, src: (j, 0)),
            scratch_shapes=[pltpu.VMEM((EXPERTS_PER_GROUP, D_MODEL, D_EXPERT), BF16),
                            pltpu.VMEM((EXPERTS_PER_GROUP, D_MODEL, D_EXPERT), BF16),
                            pltpu.VMEM((EXPERTS_PER_GROUP, D_EXPERT, D_MODEL), BF16),
                            pltpu.VMEM((3, tm, MOE_ROW), F32), pltpu.SemaphoreType.DMA((3,))],
        ),
        out_shape=jax.ShapeDtypeStruct((n_mtiles * tm, D_MODEL), F32),
        compiler_params=pltpu.CompilerParams(dimension_semantics=("arbitrary",), vmem_limit_bytes=VMEM_LIMIT_BYTES),
        name="moe_experts",
    )(meta, src, hs, wg, wu, wd)


def _combine_kernel(dst_ref, tok_ref, x1_ref, ys_ref, o_ref, buf_ref, xbuf_ref, sem, xsem):
    i = pl.program_id(0)
    n_steps = pl.num_programs(0)
    tm = o_ref.shape[0]
    n_slots = buf_ref.shape[1]
    n_units = n_slots // MOE_UNIT
    n_ring = buf_ref.shape[0]

    def residual_copy(step, slot):
        rows = pl.ds(pl.multiple_of(step * tm, tm), tm)
        return pltpu.make_async_copy(x1_ref.at[rows], xbuf_ref.at[slot], xsem.at[slot])

    def fetch(step):
        slot = step % n_ring
        _unit_gather(dst_ref, step * n_units, ys_ref, buf_ref.at[slot], sem.at[slot], wait=False)
        residual_copy(step, slot).start()

    @pl.when(i == 0)
    def _():
        fetch(0)

    @pl.when((i == 0) & (n_steps > 1))
    def _():
        fetch(1)

    @pl.when(i + 2 < n_steps)
    def _():
        fetch(i + 2)

    slot = i % n_ring
    _unit_gather(dst_ref, i * n_units, ys_ref, buf_ref.at[slot], sem.at[slot], wait=True)
    residual_copy(i, slot).wait()

    y = buf_ref[slot]
    y_hi = y.astype(BF16)
    y_lo = (y - y_hi.astype(F32)).astype(BF16)
    unsort = _sort_matrix(tok_ref[:, SLOT_LANE:SLOT_LANE + 1].astype(jnp.int32), n_slots, 1)
    o_ref[...] = xbuf_ref[slot] + _dot(unsort, y_hi) + _dot(unsort, y_lo)


def _combine_call(dst, x1, tok, ys, tm):
    n = x1.shape[0]
    n_slots = tm + MOE_PAD_ROWS
    n_ring = 3
    return pl.pallas_call(
        _combine_kernel,
        grid_spec=pltpu.PrefetchScalarGridSpec(
            num_scalar_prefetch=1,
            grid=(n // tm,),
            in_specs=[pl.BlockSpec((tm, LANES), lambda i, dst: (i, 0)),
                      pl.BlockSpec(memory_space=pl.ANY),
                      pl.BlockSpec(memory_space=pl.ANY)],
            out_specs=pl.BlockSpec((tm, D_MODEL), lambda i, dst: (i, 0)),
            scratch_shapes=[pltpu.VMEM((n_ring, n_slots, D_MODEL), F32), pltpu.VMEM((n_ring, tm, D_MODEL), F32),
                            pltpu.SemaphoreType.DMA((n_ring,)), pltpu.SemaphoreType.DMA((n_ring,))],
        ),
        out_shape=jax.ShapeDtypeStruct((n, D_MODEL), F32),
        compiler_params=pltpu.CompilerParams(dimension_semantics=("arbitrary",), vmem_limit_bytes=VMEM_LIMIT_BYTES),
        name="moe_combine",
    )(dst, tok, x1, ys)


def kernel(x, norm_mix, w_in, sgu_norm, sgu_w, sgu_b, s5_lambda_re, s5_lambda_im, s5_log_dt, s5_b_re, s5_b_im,
           s5_c_re, s5_c_im, s5_d, s5_glu_w, s5_glu_b, q_norm, k_norm, rel_bias, out_norm, w_out, norm_ffn,
           router_group_w, router_group_b, router_expert_w, router_expert_b, w_gate, w_up, w_down):
    batch, seq, _ = x.shape
    n = batch * seq
    depth = w_in.shape[0]
    t = _tiles(n, seq)
    assert seq % ATTN_TQ == 0 and seq % (S5_T * t["s5_rows"]) == 0
    row_vec = lambda a: a.astype(F32)[None, :]

    lane_group = jnp.arange(LANES) // HEAD_DIM
    bd = (lane_group[:, None] == lane_group[None, :]).astype(BF16)
    block_chunk = jnp.arange(SGU_BLOCK) // CHUNK
    sgu_mask = block_chunk[None, :] <= block_chunk[:, None]

    xf = x.reshape(n, D_MODEL)
    for l in range(depth):
        sguw = jnp.where(sgu_mask[None], sgu_w[l], 0).astype(BF16)
        sgub = jnp.repeat(sgu_b[l].astype(F32).T, HEAD_DIM, axis=1)
        qg = row_vec(jnp.tile(q_norm[l], ATTN_HEADS)) * (HEAD_DIM ** -0.5)
        kg = row_vec(jnp.tile(k_norm[l], ATTN_HEADS))
        oa, u0, u1, q, k, v = _in_call(xf, row_vec(norm_mix[l]), w_in, l, row_vec(sgu_norm[l]), sguw, sgub,
                                      qg, kg, bd, t["tm_in"])

        wv, kb, wc, apow = _s5_tables(s5_lambda_re[l], s5_lambda_im[l], s5_log_dt[l], s5_b_re[l], s5_b_im[l],
                                        s5_c_re[l], s5_c_im[l], t["s5_rows"])
        y0, y1 = _s5_call(u0, u1, wv, kb, wc, apow, batch, t["s5_rows"])

        oc = _attn_call(q, k, v, _attn_diag(rel_bias[l]), batch, seq)

        wr, br = _router_tables(router_group_w[l], router_group_b[l], router_expert_w[l], router_expert_b[l])
        x1, hs, tok, counts = _out_call(oa, y0, y1, u0, u1, oc, xf, row_vec(s5_d[l]), s5_glu_w[l].astype(BF16), row_vec(s5_glu_b[l]),
                                 row_vec(out_norm[l]), w_out, l, row_vec(norm_ffn[l]), wr, br, bd,
                                 t["tm_out"])

        meta, src, dst, n_mtiles = _route_plan(counts, n, t["tm_out"])
        ys = _moe_group_call(meta, src, hs, w_gate, w_up, w_down, l, n_mtiles, t["tm_out"])
        xf = _combine_call(dst, x1, tok, ys, t["tm_out"])
    return xf.reshape(batch, seq, D_MODEL)
```
